```python
import math
import jax, jax.numpy as jnp
from jax import lax
import numpy as np

D_MODEL = 1024
BATCH = 32
SEQ = 256
DEPTH = 2
DEC_BATCH = 4
DEC_SEQ = 4096
PAST_LEN = 256

GRID_W = 64
HEAD_DIM = 64
A_HEADS = 8
A_KV_HEADS = 2
A_GROUP = A_HEADS // A_KV_HEADS
B_HEADS = 4
WINDOW = 128
BLOCK = 128
N_BAND = -(-WINDOW // BLOCK)
ROPE_BASE = 10000.0
ATT_SIZES = [A_HEADS * HEAD_DIM, A_KV_HEADS * HEAD_DIM, A_KV_HEADS * HEAD_DIM,
             B_HEADS * 2 * HEAD_DIM, B_HEADS * 2 * HEAD_DIM, B_HEADS * 2 * HEAD_DIM]
ATT_IN = sum(ATT_SIZES)
ATT_OUT = A_HEADS * HEAD_DIM + B_HEADS * 2 * HEAD_DIM
D_RNN = 1280
RNN_BLOCKS = 10
RNN_BW = D_RNN // RNN_BLOCKS
CONV_W = 4
CONV_LEFT = (CONV_W - 1) // 2
RGLRU_C = 8.0
D_FF = 4 * D_MODEL
N_ATT = (DEPTH + 1) // 2
N_REC = DEPTH // 2
EPS = 1e-6
SCALE = HEAD_DIM ** -0.5
NEG = -1e30

kernel_name = 'hybrid_diffusion_prefix_step'

F32 = jnp.float32


def rmsnorm(x, g):
    xf = x.astype(F32)
    y = xf * lax.rsqrt(jnp.mean(xf * xf, axis=-1, keepdims=True) + EPS)
    return (y * g.astype(F32)).astype(x.dtype)


def ada_mod(cvec, w, b):
    m = jax.nn.silu(cvec) @ w + b
    return jnp.split(m[:, None, :], 6, axis=-1)


def modulate(h, shift, scale):
    return h * (1.0 + scale) + shift


def axial_rope_tables(n_tokens):
    rows = n_tokens // GRID_W
    r, cl = jnp.meshgrid(jnp.arange(rows, dtype=F32), jnp.arange(GRID_W, dtype=F32), indexing='ij')
    quarter = HEAD_DIM // 4
    inv = ROPE_BASE ** (-jnp.arange(quarter, dtype=F32) / quarter)
    ang = jnp.stack([r.reshape(-1)[:, None] * inv, cl.reshape(-1)[:, None] * inv], axis=1)
    return jnp.cos(ang)[:, None], jnp.sin(ang)[:, None]


def apply_rope(x, cos, sin):
    B, T, H, _ = x.shape
    xf = x.astype(F32).reshape(B, T, H, 2, 2, HEAD_DIM // 4)
    x1, x2 = xf[..., 0, :], xf[..., 1, :]
    out = jnp.stack([x1 * cos - x2 * sin, x1 * sin + x2 * cos], axis=-2)
    return out.reshape(B, T, H, HEAD_DIM).astype(x.dtype)


def rope_pairs(x, cos, sin):
    B, T, H, E = x.shape
    return apply_rope(x.reshape(B, T, 2 * H, HEAD_DIM), cos, sin).reshape(B, T, H, E)


def att_project(h, w_in):
    B, T, _ = h.shape
    idx = np.cumsum(ATT_SIZES)[:-1].tolist()
    qa, ka, va, qb, kb, vb = jnp.split(h @ w_in, idx, axis=-1)
    return (qa.reshape(B, T, A_HEADS, HEAD_DIM), ka.reshape(B, T, A_KV_HEADS, HEAD_DIM),
            va.reshape(B, T, A_KV_HEADS, HEAD_DIM), qb.reshape(B, T, B_HEADS, 2 * HEAD_DIM),
            kb.reshape(B, T, B_HEADS, 2 * HEAD_DIM), vb.reshape(B, T, B_HEADS, 2 * HEAD_DIM))


def sink_attention_dense(q, k, v, sink):
    B, Q, H, d = q.shape
    nq = Q // BLOCK
    qb = jnp.moveaxis(q.reshape(B, nq, BLOCK, A_KV_HEADS, A_GROUP, d), 1, 0)
    sink_l = sink.reshape(A_KV_HEADS, A_GROUP).astype(F32)

    def one(qi):
        s = jnp.einsum('bqkgd,bskd->bkgqs', qi, k).astype(F32) * SCALE
        sk = jnp.broadcast_to(sink_l[None, :, :, None, None], s.shape[:-1] + (1,))
        p = jax.nn.softmax(jnp.concatenate([s, sk], axis=-1), axis=-1)[..., :-1]
        return jnp.einsum('bkgqs,bskd->bqkgd', p.astype(v.dtype), v)

    o = lax.map(one, qb)
    return jnp.moveaxis(o, 0, 1).reshape(B, Q, H, d)


def banded_sink_attention(q, k, v, kc, vc, sink):
    B, T, H, d = q.shape
    nb = T // BLOCK
    L = kc.shape[1]
    qb = q.reshape(B, nb, BLOCK, A_KV_HEADS, A_GROUP, d)

    def band(t):
        tp = jnp.pad(t, ((0, 0), (N_BAND * BLOCK, N_BAND * BLOCK), (0, 0), (0, 0)))
        tp = tp.reshape(B, nb + 2 * N_BAND, BLOCK, A_KV_HEADS, d)
        return jnp.concatenate([tp[:, j:j + nb] for j in range(2 * N_BAND + 1)], axis=2)

    kb, vb = band(k), band(v)
    S = (2 * N_BAND + 1) * BLOCK
    start = jnp.arange(nb)[:, None] * BLOCK
    qpos = start + jnp.arange(BLOCK)[None]
    kpos = start + jnp.arange(S)[None] - N_BAND * BLOCK
    mask = ((jnp.abs(qpos[:, :, None] - kpos[:, None, :]) <= WINDOW)
            & (kpos[:, None, :] >= 0) & (kpos[:, None, :] < T))
    s_band = jnp.einsum('bnqkgd,bnskd->bnkgqs', qb, kb).astype(F32) * SCALE
    s_band = jnp.where(mask[None, :, None, None], s_band, NEG)
    s_ctx = jnp.einsum('bnqkgd,bskd->bnkgqs', qb, kc).astype(F32) * SCALE
    sk = jnp.broadcast_to(sink.reshape(A_KV_HEADS, A_GROUP).astype(F32)[None, None, :, :, None, None],
                          s_ctx.shape[:-1] + (1,))
    p = jax.nn.softmax(jnp.concatenate([s_ctx, s_band, sk], axis=-1), axis=-1)
    p_ctx = p[..., :L].astype(v.dtype)
    p_band = p[..., L:L + S].astype(v.dtype)
    o = (jnp.einsum('bnkgqs,bskd->bnqkgd', p_ctx, vc)
         + jnp.einsum('bnkgqs,bnskd->bnqkgd', p_band, vb))
    return o.reshape(B, T, H, d)


def diff_lambda(lam_qk, lambda_init):
    lq = lam_qk.astype(F32)
    return jnp.exp(jnp.sum(lq[0] * lq[1])) - jnp.exp(jnp.sum(lq[2] * lq[3])) + lambda_init


def diff_attention(q, k, v, lam):
    B, Q, H, E = q.shape
    d = E // 2
    k1, k2 = k[..., :d], k[..., d:]
    nq = Q // BLOCK
    qb = jnp.moveaxis(q.reshape(B, nq, BLOCK, H, E), 1, 0)

    def one(qi):
        s1 = jnp.einsum('bqhd,bshd->bhqs', qi[..., :d], k1).astype(F32) * SCALE
        s2 = jnp.einsum('bqhd,bshd->bhqs', qi[..., d:], k2).astype(F32) * SCALE
        w = jax.nn.softmax(s1, axis=-1) - lam * jax.nn.softmax(s2, axis=-1)
        return jnp.einsum('bhqs,bshe->bqhe', w.astype(v.dtype), v)

    o = lax.map(one, qb)
    return jnp.moveaxis(o, 0, 1).reshape(B, Q, H, E)


def att_merge(oa, ob, subln, lambda_init, w_out):
    B, T = oa.shape[:2]
    ob = rmsnorm(ob, subln) * (1.0 - lambda_init)
    return jnp.concatenate([oa.reshape(B, T, -1), ob.reshape(B, T, -1)], axis=-1) @ w_out


def att_mixer_context(h, w_in, w_out, sink, lam_qk, subln, lambda_init):
    qa, ka, va, qb, kb, vb = att_project(h, w_in)
    oa = sink_attention_dense(qa, ka, va, sink)
    ob = diff_attention(qb, kb, vb, diff_lambda(lam_qk, lambda_init))
    return att_merge(oa, ob, subln, lambda_init, w_out), (ka, va, kb, vb)


def att_mixer_latent(h, ck_a, cv_a, ck_b, cv_b, w_in, w_out, sink, lam_qk, subln, lambda_init):
    cos, sin = axial_rope_tables(h.shape[1])
    qa, ka, va, qb, kb, vb = att_project(h, w_in)
    qa, ka = apply_rope(qa, cos, sin), apply_rope(ka, cos, sin)
    qb, kb = rope_pairs(qb, cos, sin), rope_pairs(kb, cos, sin)
    oa = banded_sink_attention(qa, ka, va, ck_a.astype(ka.dtype), cv_a.astype(va.dtype), sink)
    k_all = jnp.concatenate([ck_b.astype(kb.dtype), kb], axis=1)
    v_all = jnp.concatenate([cv_b.astype(vb.dtype), vb], axis=1)
    ob = diff_attention(qb, k_all, v_all, diff_lambda(lam_qk, lambda_init))
    return att_merge(oa, ob, subln, lambda_init, w_out)


def centred_dwconv(x, w, b):
    T = x.shape[1]
    xp = jnp.pad(x, ((0, 0), (CONV_LEFT, CONV_W - 1 - CONV_LEFT), (0, 0)))
    y = b + xp[:, 0:T] * w[0]
    for j in range(1, CONV_W):
        y = y + xp[:, j:j + T] * w[j]
    return y


def block_diag(x, w, b):
    B, T, _ = x.shape
    return jnp.einsum('btnc,ncd->btnd', x.reshape(B, T, RNN_BLOCKS, RNN_BW), w).reshape(B, T, D_RNN) + b


def rglru_scan(x, w_a, b_a, w_x, b_x, lam, h0, reverse):
    r = jax.nn.sigmoid(block_diag(x, w_a, b_a).astype(F32))
    i = jax.nn.sigmoid(block_diag(x, w_x, b_x).astype(F32))
    log_a = -RGLRU_C * r * jax.nn.softplus(-lam.astype(F32))
    a = jnp.exp(log_a)
    u = jnp.sqrt(-jnp.expm1(2.0 * log_a)) * (i * x.astype(F32))

    def step(hc, au):
        hc = au[0] * hc + au[1]
        return hc, hc

    hT, hs = lax.scan(step, h0.astype(F32), (jnp.moveaxis(a, 1, 0), jnp.moveaxis(u, 1, 0)), reverse=reverse)
    return jnp.moveaxis(hs, 0, 1), hT


def rec_mixer(h, h0_f, h0_b, w_in, conv_w, conv_b, w_a, b_a, w_x, b_x, lam, w_out):
    gate, xr = jnp.split(h @ w_in, 2, axis=-1)
    xr = centred_dwconv(xr, conv_w, conv_b)
    hf, sf = rglru_scan(xr, w_a[0], b_a[0], w_x[0], b_x[0], lam[0], h0_f, False)
    hb, sb = rglru_scan(xr, w_a[1], b_a[1], w_x[1], b_x[1], lam[1], h0_b, True)
    y = (hf + hb).astype(h.dtype) * jax.nn.gelu(gate)
    return y @ w_out, sf, sb


def sq_relu_mlp(h, w1, w2):
    return jnp.square(jax.nn.relu(h @ w1)) @ w2


def setup_inputs(seed: int = 0) -> dict:
    key = jax.random.key(seed)
    ks = jax.random.split(key, 32)

    def nrm(k, shape, scale=1.0):
        return jax.random.normal(k, shape, F32) * scale

    a0 = jax.random.uniform(ks[29], (N_REC, 2, D_RNN), F32, minval=0.9, maxval=0.999)
    a1 = a0 ** (1.0 / RGLRU_C)
    return {
        'x_prompt': nrm(ks[0], (BATCH, SEQ, D_MODEL)),
        'x_sample': nrm(ks[1], (DEC_BATCH, DEC_SEQ, D_MODEL)),
        'cache_a_k': nrm(ks[2], (DEC_BATCH, N_ATT, PAST_LEN, A_KV_HEADS, HEAD_DIM)),
        'cache_a_v': nrm(ks[3], (DEC_BATCH, N_ATT, PAST_LEN, A_KV_HEADS, HEAD_DIM)),
        'cache_b_k': nrm(ks[4], (DEC_BATCH, N_ATT, PAST_LEN, B_HEADS, 2 * HEAD_DIM)),
        'cache_b_v': nrm(ks[5], (DEC_BATCH, N_ATT, PAST_LEN, B_HEADS, 2 * HEAD_DIM)),
        'state_fwd': nrm(ks[6], (DEC_BATCH, N_REC, D_RNN), 0.5),
        'state_bwd': nrm(ks[7], (DEC_BATCH, N_REC, D_RNN), 0.5),
        'c': nrm(ks[8], (DEC_BATCH, D_MODEL)),
        'c_ctx': nrm(ks[9], (D_MODEL,)),
        'norm1': 1.0 + nrm(ks[10], (DEPTH, D_MODEL), 0.02),
        'norm2': 1.0 + nrm(ks[11], (DEPTH, D_MODEL), 0.02),
        'w_ada': nrm(ks[12], (DEPTH, D_MODEL, 6 * D_MODEL), 0.5 * D_MODEL ** -0.5),
        'b_ada': nrm(ks[13], (DEPTH, 6 * D_MODEL), 0.02),
        'w_mlp1': nrm(ks[14], (DEPTH, D_MODEL, D_FF), D_MODEL ** -0.5),
        'w_mlp2': nrm(ks[15], (DEPTH, D_FF, D_MODEL), D_FF ** -0.5),
        'att_w_in': nrm(ks[16], (N_ATT, D_MODEL, ATT_IN), D_MODEL ** -0.5),
        'att_w_out': nrm(ks[17], (N_ATT, ATT_OUT, D_MODEL), ATT_OUT ** -0.5),
        'att_sink': nrm(ks[18], (N_ATT, A_HEADS), 0.5),
        'att_lam_qk': nrm(ks[19], (N_ATT, 4, HEAD_DIM), 0.1),
        'att_subln': 1.0 + nrm(ks[20], (N_ATT, 2 * HEAD_DIM), 0.02),
        'rec_w_in': nrm(ks[21], (N_REC, D_MODEL, 2 * D_RNN), D_MODEL ** -0.5),
        'rec_conv_w': nrm(ks[22], (N_REC, CONV_W, D_RNN), CONV_W ** -0.5),
        'rec_conv_b': nrm(ks[23], (N_REC, D_RNN), 0.02),
        'rec_w_a': nrm(ks[24], (N_REC, 2, RNN_BLOCKS, RNN_BW, RNN_BW), RNN_BW ** -0.5),
        'rec_b_a': nrm(ks[25], (N_REC, 2, D_RNN), 0.02),
        'rec_w_x': nrm(ks[26], (N_REC, 2, RNN_BLOCKS, RNN_BW, RNN_BW), RNN_BW ** -0.5),
        'rec_b_x': nrm(ks[27], (N_REC, 2, D_RNN), 0.02),
        'rec_lam': jnp.log(a1) - jnp.log1p(-a1),
        'rec_w_out': nrm(ks[28], (N_REC, D_RNN, D_MODEL), D_RNN ** -0.5),
        'final_norm': 1.0 + nrm(ks[30], (D_MODEL,), 0.02),
    }


def reference(x_prompt, x_sample, cache_a_k, cache_a_v, cache_b_k, cache_b_v, state_fwd, state_bwd,
              c, c_ctx, norm1, norm2, w_ada, b_ada, w_mlp1, w_mlp2, att_w_in, att_w_out, att_sink,
              att_lam_qk, att_subln, rec_w_in, rec_conv_w, rec_conv_b, rec_w_a, rec_b_a, rec_w_x,
              rec_b_x, rec_lam, rec_w_out, final_norm):
    xp, xs = x_prompt, x_sample
    new_ak, new_av, new_bk, new_bv, new_sf, new_sb = [], [], [], [], [], []
    for layer in range(DEPTH):
        j = layer // 2
        mp = ada_mod(c_ctx[None], w_ada[layer], b_ada[layer])
        ms = ada_mod(c, w_ada[layer], b_ada[layer])
        hp = modulate(rmsnorm(xp, norm1[layer]), mp[0], mp[1])
        hs = modulate(rmsnorm(xs, norm1[layer]), ms[0], ms[1])
        if layer % 2 == 0:
            lam_init = 0.8 - 0.6 * math.exp(-0.3 * layer)
            op, (ka, va, kb, vb) = att_mixer_context(hp, att_w_in[j], att_w_out[j], att_sink[j],
                                                     att_lam_qk[j], att_subln[j], lam_init)
            os_ = att_mixer_latent(hs, cache_a_k[:, j], cache_a_v[:, j], cache_b_k[:, j], cache_b_v[:, j],
                                   att_w_in[j], att_w_out[j], att_sink[j], att_lam_qk[j], att_subln[j],
                                   lam_init)
            new_ak.append(ka)
            new_av.append(va)
            new_bk.append(kb)
            new_bv.append(vb)
        else:
            zeros = jnp.zeros((xp.shape[0], D_RNN), F32)
            op, sf, sb = rec_mixer(hp, zeros, zeros, rec_w_in[j], rec_conv_w[j], rec_conv_b[j], rec_w_a[j],
                                   rec_b_a[j], rec_w_x[j], rec_b_x[j], rec_lam[j], rec_w_out[j])
            os_, _, _ = rec_mixer(hs, state_fwd[:, j], state_bwd[:, j], rec_w_in[j], rec_conv_w[j],
                                  rec_conv_b[j], rec_w_a[j], rec_b_a[j], rec_w_x[j], rec_b_x[j], rec_lam[j],
                                  rec_w_out[j])
            new_sf.append(sf)
            new_sb.append(sb)
        xp = xp + mp[2] * op
        xs = xs + ms[2] * os_
        hp = modulate(rmsnorm(xp, norm2[layer]), mp[3], mp[4])
        hs = modulate(rmsnorm(xs, norm2[layer]), ms[3], ms[4])
        xp = xp + mp[5] * sq_relu_mlp(hp, w_mlp1[layer], w_mlp2[layer])
        xs = xs + ms[5] * sq_relu_mlp(hs, w_mlp1[layer], w_mlp2[layer])
    y_prompt = rmsnorm(xp, final_norm)
    y_sample = rmsnorm(xs, final_norm)
    new_a_k = jnp.stack(new_ak, axis=1)
    new_a_v = jnp.stack(new_av, axis=1)
    new_b_k = jnp.stack(new_bk, axis=1)
    new_b_v = jnp.stack(new_bv, axis=1)
    new_state_fwd = jnp.stack(new_sf, axis=1)
    new_state_bwd = jnp.stack(new_sb, axis=1)
    return (y_prompt, y_sample, new_a_k, new_a_v, new_b_k, new_b_v, new_state_fwd, new_state_bwd)
```

```python
import functools
import math

import jax
import jax.numpy as jnp
import numpy as np
from jax import lax
from jax.experimental import pallas as pl
from jax.experimental.pallas import tpu as pltpu

F32 = jnp.float32
BF16 = jnp.bfloat16

LANES = 128
SUBLANES = 8
VMEM_LIMIT_BYTES = 56 * 1024 * 1024

D_MODEL = 1024
DEPTH = 2
GRID_W = 64
HEAD_DIM = 64
A_HEADS = 8
A_KV_HEADS = 2
A_GROUP = A_HEADS // A_KV_HEADS
B_HEADS = 4
WINDOW = 128
ROPE_BASE = 10000.0
D_RNN = 1280
RNN_BLOCKS = 10
RNN_BW = D_RNN // RNN_BLOCKS
CONV_W = 4
CONV_LEFT = (CONV_W - 1) // 2
RGLRU_C = 8.0
D_FF = 4 * D_MODEL
EPS = 1e-6
SCALE = HEAD_DIM ** -0.5
NEG = -1e30

QA_W = A_HEADS * LANES
KA_W = A_KV_HEADS * HEAD_DIM
B_W = B_HEADS * 2 * HEAD_DIM
MOD_ROWS = 8
CTX_ROW = 4


def _cparams(*semantics):
    return pltpu.CompilerParams(dimension_semantics=semantics, vmem_limit_bytes=VMEM_LIMIT_BYTES)


def _dot(a, b):
    return jnp.dot(a, b, preferred_element_type=F32)


def _dot_nt(a, b):
    return lax.dot_general(a, b, (((1,), (1,)), ((), ())), preferred_element_type=F32)


def _rms(x, g):
    return x * lax.rsqrt(jnp.mean(x * x, axis=-1, keepdims=True) + EPS) * g


def _ada_kernel(c_ref, w_ref, b_ref, o_ref):
    c = c_ref[...]
    s = c * jax.nn.sigmoid(c)
    o_ref[...] = _dot(s.astype(BF16), w_ref[...].astype(BF16)) + b_ref[...]


def _ada_mod(cvec, w_ada, b_ada):
    tn = 1536
    out = pl.pallas_call(
        _ada_kernel,
        grid=(DEPTH, 6 * D_MODEL // tn),
        in_specs=[
            pl.BlockSpec((MOD_ROWS, D_MODEL), lambda l, j: (0, 0)),
            pl.BlockSpec((None, D_MODEL, tn), lambda l, j: (l, 0, j)),
            pl.BlockSpec((None, 1, tn), lambda l, j: (l, 0, j)),
        ],
        out_specs=pl.BlockSpec((None, MOD_ROWS, tn), lambda l, j: (l, 0, j)),
        out_shape=jax.ShapeDtypeStruct((DEPTH, MOD_ROWS, 6 * D_MODEL), F32),
        compiler_params=_cparams("parallel", "parallel"),
        name="ada_mod",
    )(cvec, w_ada, b_ada.reshape(DEPTH, 1, 6 * D_MODEL))
    return out.reshape(DEPTH, MOD_ROWS, 6, D_MODEL)


def _proj_kernel(*refs, segs, rope):
    if rope:
        x_ref, mod_ref, g_ref, w_ref, cos_ref, sin_ref, *outs = refs
    else:
        x_ref, mod_ref, g_ref, w_ref, *outs = refs
    x = x_ref[...]
    h = _rms(x, g_ref[...])
    h = h * (1.0 + mod_ref[1:2, :]) + mod_ref[0:1, :]
    hb = h.astype(BF16)
    tm = x.shape[0]
    if rope:
        lane = lax.broadcasted_iota(jnp.int32, (tm, LANES), 1)
        first = (lane & 16) == 0
        cos = cos_ref[...]
        sin = sin_ref[...]
    for (c0, width, do_rope, scale), o_ref in zip(segs, outs):
        y = _dot(hb, w_ref[:, c0:c0 + width])
        for t in range(width // LANES):
            yt = y[:, t * LANES:(t + 1) * LANES]
            if do_rope:
                sw = jnp.where(first, pltpu.roll(yt, LANES - 16, 1), pltpu.roll(yt, 16, 1))
                yt = yt * cos + sw * sin
            if scale != 1.0:
                yt = yt * scale
            o_ref[:, t * LANES:(t + 1) * LANES] = yt.astype(o_ref.dtype)


def _proj(x, mod, g, w, segs, out_dtypes, *, tm, ctx, rope_tabs=None, name):
    B, T, D = x.shape
    row = (lambda b, i: (CTX_ROW, 0, 0)) if ctx else (lambda b, i: (b, 0, 0))
    in_specs = [
        pl.BlockSpec((None, tm, D), lambda b, i: (b, i, 0)),
        pl.BlockSpec((None, 6, D), row),
        pl.BlockSpec((1, D), lambda b, i: (0, 0)),
        pl.BlockSpec(w.shape, lambda b, i: (0, 0)),
    ]
    args = [x, mod, g.reshape(1, D), w]
    if rope_tabs is not None:
        in_specs += [pl.BlockSpec((tm, LANES), lambda b, i: (i, 0))] * 2
        args += list(rope_tabs)
    out_specs = [pl.BlockSpec((None, tm, s[1]), lambda b, i: (b, i, 0)) for s in segs]
    out_shape = [jax.ShapeDtypeStruct((B, T, s[1]), dt) for s, dt in zip(segs, out_dtypes)]
    return pl.pallas_call(
        functools.partial(_proj_kernel, segs=tuple(segs), rope=rope_tabs is not None),
        grid=(B, T // tm),
        in_specs=in_specs,
        out_specs=out_specs,
        out_shape=out_shape,
        compiler_params=_cparams("parallel", "parallel"),
        name=name,
    )(*args)


def _diff_lambda(lq, lam_init):
    s1 = jnp.sum(lq[0:1, :] * lq[1:2, :], axis=1, keepdims=True)
    s2 = jnp.sum(lq[2:3, :] * lq[3:4, :], axis=1, keepdims=True)
    return jnp.exp(s1) - jnp.exp(s2) + lam_init


def _stack_group_queries(qa_ref, g):
    return jnp.concatenate(
        [qa_ref[:, (A_GROUP * g + hh) * LANES:(A_GROUP * g + hh + 1) * LANES] for hh in range(A_GROUP)], axis=0)


def _sink_column(sink_ref, g, tq):
    return jnp.concatenate(
        [jnp.full((tq, 1), sink_ref[A_GROUP * g + hh], F32) for hh in range(A_GROUP)], axis=0)


def _store_a_heads(o_ref, heads, lo):
    for j in range(A_HEADS // 2):
        a, b = heads[2 * j], heads[2 * j + 1]
        if (2 * j) // A_GROUP == 0:
            tile = jnp.where(lo, a, pltpu.roll(b, HEAD_DIM, 1))
        else:
            tile = jnp.where(lo, pltpu.roll(a, HEAD_DIM, 1), b)
        o_ref[:, j * LANES:(j + 1) * LANES] = tile.astype(o_ref.dtype)


def _subln(o, subln, lam_init):
    return _rms(o, subln) * (1.0 - lam_init)


def _ctx_attn_kernel(sink_ref, qa_ref, ka_ref, va_ref, qb_ref, kb_ref, vb_ref, lamqk_ref, subln_ref, o_ref, *,
                     lam_init):
    T = qa_ref.shape[0]
    lane = lax.broadcasted_iota(jnp.int32, (T, LANES), 1)
    lo = lane < HEAD_DIM
    ka = ka_ref[...].astype(BF16)
    va = va_ref[...].astype(BF16)
    heads = []
    for g in range(A_KV_HEADS):
        qg = _stack_group_queries(qa_ref, g)
        s = _dot_nt(qg, ka)
        sink = _sink_column(sink_ref, g, T)
        m = jnp.maximum(jnp.max(s, axis=1, keepdims=True), sink)
        p = jnp.exp(s - m)
        l = jnp.sum(p, axis=1, keepdims=True) + jnp.exp(sink - m)
        o = _dot(p.astype(BF16), va) / l
        heads += [o[hh * T:(hh + 1) * T] for hh in range(A_GROUP)]
    _store_a_heads(o_ref, heads, lo)

    lam = _diff_lambda(lamqk_ref[...], lam_init)
    for h in range(B_HEADS):
        sl = slice(h * LANES, (h + 1) * LANES)
        q = qb_ref[:, sl]
        zero = jnp.zeros_like(q)
        qs = jnp.concatenate([jnp.where(lo, q, zero), jnp.where(lo, zero, q)], axis=0)
        s = _dot_nt(qs, kb_ref[:, sl].astype(BF16))
        e = jnp.exp(s - jnp.max(s, axis=1, keepdims=True))
        pn = e / jnp.sum(e, axis=1, keepdims=True)
        w = pn[:T] - lam * pn[T:]
        o = _dot(w.astype(BF16), vb_ref[:, sl].astype(BF16))
        o_ref[:, A_HEADS * HEAD_DIM + h * LANES:A_HEADS * HEAD_DIM + (h + 1) * LANES] = (
            _subln(o, subln_ref[...], lam_init).astype(o_ref.dtype))


def _ctx_attention(qa, ka, va, qb, kb, vb, sink, lam_qk, subln, lam_init):
    B, T, _ = qa.shape
    blk = lambda w: pl.BlockSpec((None, T, w), lambda b: (b, 0, 0))
    full = lambda a: pl.BlockSpec(a.shape, lambda b: (0,) * a.ndim)
    return pl.pallas_call(
        functools.partial(_ctx_attn_kernel, lam_init=lam_init),
        grid=(B,),
        in_specs=[pl.BlockSpec(memory_space=pltpu.SMEM), blk(QA_W), blk(KA_W), blk(KA_W), blk(B_W), blk(B_W),
                  blk(B_W), full(lam_qk), full(subln)],
        out_specs=blk(A_HEADS * HEAD_DIM + B_W),
        out_shape=jax.ShapeDtypeStruct((B, T, A_HEADS * HEAD_DIM + B_W), BF16),
        compiler_params=_cparams("parallel"),
        name="ctx_attention",
    )(sink, qa, ka, va, qb, kb, vb, lam_qk, subln)


def _lat_a_kernel(sink_ref, qa_ref, k_ref, v_ref, ck_ref, cv_ref, o_ref, *, tq, band):
    T = k_ref.shape[0]
    qi = pl.program_id(1)
    start = jnp.clip(qi * tq - WINDOW, 0, T - band)
    start = pl.multiple_of(start, WINDOW)
    kb = k_ref[pl.ds(start, band), :]
    vb = v_ref[pl.ds(start, band), :]
    rows = A_GROUP * tq
    qpos = qi * tq + lax.broadcasted_iota(jnp.int32, (rows, band), 0) % tq
    kpos = start + lax.broadcasted_iota(jnp.int32, (rows, band), 1)
    keep = jnp.abs(qpos - kpos) <= WINDOW
    lo = lax.broadcasted_iota(jnp.int32, (tq, LANES), 1) < HEAD_DIM
    heads = []
    for g in range(A_KV_HEADS):
        qg = _stack_group_queries(qa_ref, g)
        s_c = _dot_nt(qg, ck_ref[...])
        s_b = jnp.where(keep, _dot_nt(qg, kb), NEG)
        sink = _sink_column(sink_ref, g, tq)
        m = jnp.maximum(jnp.maximum(jnp.max(s_c, axis=1, keepdims=True), jnp.max(s_b, axis=1, keepdims=True)), sink)
        p_c = jnp.exp(s_c - m)
        p_b = jnp.exp(s_b - m)
        l = jnp.sum(p_c, axis=1, keepdims=True) + jnp.sum(p_b, axis=1, keepdims=True) + jnp.exp(sink - m)
        o = (_dot(p_c.astype(BF16), cv_ref[...]) + _dot(p_b.astype(BF16), vb)) / l
        heads += [o[hh * tq:(hh + 1) * tq] for hh in range(A_GROUP)]
    _store_a_heads(o_ref, heads, lo)


def _lat_a_attention(qa, ka, va, cka, cva, sink):
    B, T, _ = qa.shape
    L = cka.shape[1]
    tq = WINDOW
    band = 3 * WINDOW
    return pl.pallas_call(
        functools.partial(_lat_a_kernel, tq=tq, band=band),
        grid=(B, T // tq),
        in_specs=[
            pl.BlockSpec(memory_space=pltpu.SMEM),
            pl.BlockSpec((None, tq, QA_W), lambda b, i: (b, i, 0)),
            pl.BlockSpec((None, T, KA_W), lambda b, i: (b, 0, 0)),
            pl.BlockSpec((None, T, KA_W), lambda b, i: (b, 0, 0)),
            pl.BlockSpec((None, L, KA_W), lambda b, i: (b, 0, 0)),
            pl.BlockSpec((None, L, KA_W), lambda b, i: (b, 0, 0)),
        ],
        out_specs=pl.BlockSpec((None, tq, A_HEADS * HEAD_DIM), lambda b, i: (b, i, 0)),
        out_shape=jax.ShapeDtypeStruct((B, T, A_HEADS * HEAD_DIM), BF16),
        compiler_params=_cparams("parallel", "parallel"),
        name="lat_a_attention",
    )(sink, qa, ka, va, cka, cva)


def _lat_b_kernel(q_ref, k_ref, v_ref, ck_ref, cv_ref, lamqk_ref, subln_ref, o_ref, *, tk, lam_init):
    tq = q_ref.shape[0]
    T = k_ref.shape[0]
    lo = lax.broadcasted_iota(jnp.int32, (tq, LANES), 1) < HEAD_DIM
    q = q_ref[...]
    zero = jnp.zeros_like(q)
    qs = jnp.concatenate([jnp.where(lo, q, zero), jnp.where(lo, zero, q)], axis=0)

    def step(k, v, carry):
        m, l, acc = carry
        s = _dot_nt(qs, k)
        m_new = jnp.maximum(m, jnp.max(s, axis=1, keepdims=True))
        alpha = jnp.exp(m - m_new)
        p = jnp.exp(s - m_new)
        l = alpha * l + jnp.sum(p, axis=1, keepdims=True)
        acc = alpha * acc + _dot(p.astype(BF16), v)
        return m_new, l, acc

    carry = (jnp.full((2 * tq, 1), -jnp.inf, F32), jnp.zeros((2 * tq, 1), F32), jnp.zeros((2 * tq, LANES), F32))
    carry = step(ck_ref[...], cv_ref[...], carry)

    def body(j, carry):
        start = pl.multiple_of(j * tk, tk)
        return step(k_ref[pl.ds(start, tk), :], v_ref[pl.ds(start, tk), :], carry)

    _, l, acc = lax.fori_loop(0, T // tk, body, carry)
    o = acc / l
    lam = _diff_lambda(lamqk_ref[...], lam_init)
    out = o[:tq] - lam * o[tq:]
    o_ref[...] = _subln(out, subln_ref[...], lam_init).astype(o_ref.dtype)


def _lat_b_attention(qb, kb, vb, ckb, cvb, lam_qk, subln, lam_init):
    B, T, _ = qb.shape
    L = ckb.shape[1]
    tq = 512
    tk = 512
    full = lambda a: pl.BlockSpec(a.shape, lambda b, h, i: (0,) * a.ndim)
    return pl.pallas_call(
        functools.partial(_lat_b_kernel, tk=tk, lam_init=lam_init),
        grid=(B, B_HEADS, T // tq),
        in_specs=[
            pl.BlockSpec((None, tq, LANES), lambda b, h, i: (b, i, h)),
            pl.BlockSpec((None, T, LANES), lambda b, h, i: (b, 0, h)),
            pl.BlockSpec((None, T, LANES), lambda b, h, i: (b, 0, h)),
            pl.BlockSpec((None, L, LANES), lambda b, h, i: (b, 0, h)),
            pl.BlockSpec((None, L, LANES), lambda b, h, i: (b, 0, h)),
            full(lam_qk), full(subln),
        ],
        out_specs=pl.BlockSpec((None, tq, LANES), lambda b, h, i: (b, i, h)),
        out_shape=jax.ShapeDtypeStruct((B, T, B_W), BF16),
        compiler_params=_cparams("parallel", "parallel", "parallel"),
        name="lat_b_attention",
    )(qb, kb, vb, ckb, cvb, lam_qk, subln)


def _scan8(a, u, rowid, reverse):
    for d in (1, 2, 4):
        if reverse:
            edge = rowid >= SUBLANES - d
            shift = SUBLANES - d
        else:
            edge = rowid < d
            shift = d
        a_s = jnp.where(edge, 1.0, pltpu.roll(a, shift, 0))
        u_s = jnp.where(edge, 0.0, pltpu.roll(u, shift, 0))
        u = u + a * u_s
        a = a * a_s
    return a, u


def _rec_kernel(gate_ref, xr_ref, cw_ref, cb_ref, wbd_ref, bbd_ref, lam_ref, h0f_ref, h0b_ref,
                y_ref, sf_ref, sb_ref, xp, af, uf, ab, ub, *, tc, group):
    T = xr_ref.shape[0]
    pad = SUBLANES
    xp[0:pad, :] = jnp.zeros((pad, LANES), F32)
    xp[T + pad:T + 2 * pad, :] = jnp.zeros((pad, LANES), F32)
    xp[pad:T + pad, :] = xr_ref[...]
    cw = cw_ref[...]
    cb = cb_ref[...]
    nl = -lam_ref[...]
    softplus = jnp.maximum(nl, 0.0) + jnp.log1p(jnp.exp(-jnp.abs(nl)))
    c8 = -RGLRU_C * softplus

    def gates(ci, _):
        t0 = pl.multiple_of(ci * tc, tc)
        w = xp[pl.ds(t0, tc + 2 * pad), :]
        y = cb
        for j in range(CONV_W):
            off = pad - CONV_LEFT + j
            y = y + w[off:off + tc, :] * cw[j:j + 1, :]
        z = _dot(y.astype(BF16), wbd_ref[...]) + bbd_ref[...]
        for d, (a_s, u_s) in enumerate(((af, uf), (ab, ub))):
            r = jax.nn.sigmoid(z[:, 2 * d * LANES:(2 * d + 1) * LANES])
            i = jax.nn.sigmoid(z[:, (2 * d + 1) * LANES:(2 * d + 2) * LANES])
            log_a = c8[d:d + 1, :] * r
            a = jnp.exp(log_a)
            u = jnp.sqrt(-jnp.tanh(log_a) * (a * a + 1.0)) * (i * y)
            a_s[pl.ds(t0, tc), :] = a
            u_s[pl.ds(t0, tc), :] = u
        return 0

    lax.fori_loop(0, T // tc, gates, 0)

    rowid = lax.broadcasted_iota(jnp.int32, (SUBLANES, LANES), 0)
    n_it = T // (SUBLANES * group)

    def fwd(it, carry):
        for q in range(group):
            r0 = pl.multiple_of((it * group + q) * SUBLANES, SUBLANES)
            a, u = _scan8(af[pl.ds(r0, SUBLANES), :], uf[pl.ds(r0, SUBLANES), :], rowid, False)
            h = u + a * carry
            uf[pl.ds(r0, SUBLANES), :] = h
            carry = jnp.broadcast_to(h[SUBLANES - 1:SUBLANES, :], (SUBLANES, LANES))
        return carry

    hf_last = lax.fori_loop(0, n_it, fwd, jnp.broadcast_to(h0f_ref[...], (SUBLANES, LANES)))
    sf_ref[...] = hf_last[0:1, :]

    def bwd(it, carry):
        for q in range(group):
            r0 = pl.multiple_of((n_it * group - 1 - (it * group + q)) * SUBLANES, SUBLANES)
            a, u = _scan8(ab[pl.ds(r0, SUBLANES), :], ub[pl.ds(r0, SUBLANES), :], rowid, True)
            h = u + a * carry
            ub[pl.ds(r0, SUBLANES), :] = h
            carry = jnp.broadcast_to(h[0:1, :], (SUBLANES, LANES))
        return carry

    hb_first = lax.fori_loop(0, n_it, bwd, jnp.broadcast_to(h0b_ref[...], (SUBLANES, LANES)))
    sb_ref[...] = hb_first[0:1, :]

    k_gelu = math.sqrt(2.0 / math.pi)

    def combine(ci, _):
        t0 = pl.multiple_of(ci * tc, tc)
        g = gate_ref[pl.ds(t0, tc), :]
        gelu = g * (0.5 * (1.0 + jnp.tanh(k_gelu * (g + 0.044715 * (g * g * g)))))
        y_ref[pl.ds(t0, tc), :] = ((uf[pl.ds(t0, tc), :] + ub[pl.ds(t0, tc), :]) * gelu).astype(y_ref.dtype)
        return 0

    lax.fori_loop(0, T // tc, combine, 0)


def _rec_mixer(gate, xr, h0f, h0b, conv_w, conv_b, w_bd, b_bd, lam, *, name):
    B, T, _ = xr.shape
    tc = 256
    col = lambda rows: pl.BlockSpec((rows, LANES), lambda b, n: (0, n))
    seq = pl.BlockSpec((None, T, LANES), lambda b, n: (b, 0, n))
    st = pl.BlockSpec((None, 1, LANES), lambda b, n: (b, 0, n))
    return pl.pallas_call(
        functools.partial(_rec_kernel, tc=tc, group=4),
        grid=(B, RNN_BLOCKS),
        in_specs=[seq, seq, col(CONV_W), col(1),
                  pl.BlockSpec((None, LANES, 4 * LANES), lambda b, n: (n, 0, 0)),
                  pl.BlockSpec((None, 1, 4 * LANES), lambda b, n: (n, 0, 0)),
                  col(2), st, st],
        out_specs=[seq, st, st],
        out_shape=[jax.ShapeDtypeStruct((B, T, D_RNN), BF16),
                   jax.ShapeDtypeStruct((B, 1, D_RNN), F32),
                   jax.ShapeDtypeStruct((B, 1, D_RNN), F32)],
        scratch_shapes=[pltpu.VMEM((T + 2 * SUBLANES, LANES), F32)] + [pltpu.VMEM((T, LANES), F32)] * 4,
        compiler_params=_cparams("parallel", "parallel"),
        name=name,
    )(gate, xr, conv_w, conv_b.reshape(1, D_RNN), w_bd, b_bd, lam, h0f, h0b)


def _post_kernel(*refs, n_mix, final, fc):
    x_ref, mod_ref, g2_ref = refs[:3]
    mix = refs[3:3 + 2 * n_mix]
    w1_ref, w2_ref = refs[3 + 2 * n_mix:5 + 2 * n_mix]
    rest = refs[5 + 2 * n_mix:]
    if final:
        gf_ref, o_ref = rest
    else:
        (o_ref,) = rest
    mixed = _dot(mix[0][...], mix[1][...])
    for i in range(1, n_mix):
        mixed = mixed + _dot(mix[2 * i][...], mix[2 * i + 1][...])
    x1 = x_ref[...] + mod_ref[2:3, :] * mixed
    h = _rms(x1, g2_ref[...])
    hb = (h * (1.0 + mod_ref[4:5, :]) + mod_ref[3:4, :]).astype(BF16)
    acc = None
    for c in range(D_FF // fc):
        a = _dot(hb, w1_ref[:, c * fc:(c + 1) * fc])
        a = jnp.square(jnp.maximum(a, 0.0)).astype(BF16)
        part = _dot(a, w2_ref[c * fc:(c + 1) * fc, :])
        acc = part if acc is None else acc + part
    x2 = x1 + mod_ref[5:6, :] * acc
    if final:
        x2 = _rms(x2, gf_ref[...])
    o_ref[...] = x2


def _post(x, mod, g2, mixes, w1, w2, *, tm, ctx, final_g=None, name):
    B, T, D = x.shape
    row = (lambda b, i: (CTX_ROW, 0, 0)) if ctx else (lambda b, i: (b, 0, 0))
    const = lambda a: pl.BlockSpec(a.shape, lambda b, i: (0,) * a.ndim, pipeline_mode=pl.Buffered(1))
    in_specs = [
        pl.BlockSpec((None, tm, D), lambda b, i: (b, i, 0)),
        pl.BlockSpec((None, 6, D), row),
        pl.BlockSpec((1, D), lambda b, i: (0, 0)),
    ]
    args = [x, mod, g2.reshape(1, D)]
    for o, w in mixes:
        in_specs += [pl.BlockSpec((None, tm, o.shape[-1]), lambda b, i: (b, i, 0)), const(w)]
        args += [o, w]
    in_specs += [const(w1), const(w2)]
    args += [w1, w2]
    if final_g is not None:
        in_specs.append(pl.BlockSpec((1, D), lambda b, i: (0, 0)))
        args.append(final_g.reshape(1, D))
    return pl.pallas_call(
        functools.partial(_post_kernel, n_mix=len(mixes), final=final_g is not None, fc=1024),
        grid=(B, T // tm),
        in_specs=in_specs,
        out_specs=pl.BlockSpec((None, tm, D), lambda b, i: (b, i, 0)),
        out_shape=jax.ShapeDtypeStruct((B, T, D), F32),
        compiler_params=_cparams("parallel", "parallel"),
        name=name,
    )(*args)


def _rope_tables(n_tokens):
    rows = n_tokens // GRID_W
    r, cl = jnp.meshgrid(jnp.arange(rows, dtype=F32), jnp.arange(GRID_W, dtype=F32), indexing='ij')
    quarter = HEAD_DIM // 4
    inv = ROPE_BASE ** (-jnp.arange(quarter, dtype=F32) / quarter)
    ang = jnp.stack([r.reshape(-1)[:, None] * inv, cl.reshape(-1)[:, None] * inv], axis=1)
    cos, sin = jnp.cos(ang), jnp.sin(ang)
    cos64 = jnp.concatenate([cos[:, 0], cos[:, 0], cos[:, 1], cos[:, 1]], axis=-1)
    sin64 = jnp.concatenate([-sin[:, 0], sin[:, 0], -sin[:, 1], sin[:, 1]], axis=-1)
    return jnp.tile(cos64, (1, LANES // HEAD_DIM)), jnp.tile(sin64, (1, LANES // HEAD_DIM))


def _att_in_weights(w_in):
    d = w_in.shape[0]
    nq = A_HEADS * HEAD_DIM
    wq = w_in[:, :nq].reshape(d, A_HEADS, HEAD_DIM)
    z = jnp.zeros_like(wq)
    in_first = (jnp.arange(A_HEADS) // A_GROUP == 0)[None, :, None]
    wq = jnp.where(in_first, jnp.concatenate([wq, z], axis=-1), jnp.concatenate([z, wq], axis=-1))
    return jnp.concatenate([wq.reshape(d, QA_W), w_in[:, nq:]], axis=1).astype(BF16)


def _block_diag_weights(w_a, b_a, w_x, b_x):
    w = jnp.concatenate([w_a[0], w_x[0], w_a[1], w_x[1]], axis=-1).astype(BF16)
    b = jnp.concatenate([v.reshape(RNN_BLOCKS, 1, RNN_BW) for v in (b_a[0], b_x[0], b_a[1], b_x[1])], axis=-1)
    return w, b


def kernel(x_prompt, x_sample, cache_a_k, cache_a_v, cache_b_k, cache_b_v, state_fwd, state_bwd, c, c_ctx, norm1, norm2, w_ada, b_ada, w_mlp1, w_mlp2, att_w_in, att_w_out, att_sink, att_lam_qk, att_subln, rec_w_in, rec_conv_w, rec_conv_b, rec_w_a, rec_b_a, rec_w_x, rec_b_x, rec_lam, rec_w_out, final_norm):
    nb, n_seq, _ = x_prompt.shape
    nd, d_seq, _ = x_sample.shape
    past = cache_a_k.shape[2]
    assert nd <= CTX_ROW and DEPTH == 2
    cvec = jnp.concatenate([c, jnp.zeros((CTX_ROW - nd, D_MODEL), F32), c_ctx[None],
                            jnp.zeros((MOD_ROWS - CTX_ROW - 1, D_MODEL), F32)], axis=0)
    mod = _ada_mod(cvec, w_ada, b_ada)
    w1 = w_mlp1.astype(BF16)
    w2 = w_mlp2.astype(BF16)
    tm_ctx, tm_lat = n_seq, 512

    lam_init = 0.8 - 0.6 * math.exp(-0.3 * 0)
    w_in = _att_in_weights(att_w_in[0])
    w_out = att_w_out[0].astype(BF16)
    nqa = A_HEADS * HEAD_DIM
    c0 = [0, QA_W, QA_W + KA_W, QA_W + 2 * KA_W, QA_W + 2 * KA_W + B_W, QA_W + 2 * KA_W + 2 * B_W]
    widths = [QA_W, KA_W, KA_W, B_W, B_W, B_W]
    scales = [SCALE, 1.0, 1.0, SCALE, 1.0, 1.0]
    roped = [True, True, False, True, True, False]
    sink = att_sink[0]
    lam_qk = att_lam_qk[0]
    subln = att_subln[0].reshape(1, 2 * HEAD_DIM)

    segs_ctx = [(c0[i], widths[i], False, scales[i]) for i in range(6)]
    qa, ka, va, qb, kb, vb = _proj(x_prompt, mod[0], norm1[0], w_in, segs_ctx, [BF16, F32, F32, BF16, F32, F32],
                                   tm=tm_ctx, ctx=True, name="proj_att_ctx")
    o_ctx = _ctx_attention(qa, ka, va, qb, kb, vb, sink, lam_qk, subln, lam_init)
    xp = _post(x_prompt, mod[0], norm2[0], [(o_ctx, w_out)], w1[0], w2[0], tm=tm_ctx, ctx=True, name="post_att_ctx")
    new_a_k = ka.reshape(nb, 1, n_seq, A_KV_HEADS, HEAD_DIM)
    new_a_v = va.reshape(nb, 1, n_seq, A_KV_HEADS, HEAD_DIM)
    new_b_k = kb.reshape(nb, 1, n_seq, B_HEADS, 2 * HEAD_DIM)
    new_b_v = vb.reshape(nb, 1, n_seq, B_HEADS, 2 * HEAD_DIM)

    segs_lat = [(c0[i], widths[i], roped[i], scales[i]) for i in range(6)]
    qa, ka, va, qb, kb, vb = _proj(x_sample, mod[0], norm1[0], w_in, segs_lat, [BF16] * 6, tm=tm_lat, ctx=False,
                                   rope_tabs=_rope_tables(d_seq), name="proj_att_lat")
    cka = cache_a_k[:, 0].reshape(nd, past, KA_W).astype(BF16)
    cva = cache_a_v[:, 0].reshape(nd, past, KA_W).astype(BF16)
    ckb = cache_b_k[:, 0].reshape(nd, past, B_W).astype(BF16)
    cvb = cache_b_v[:, 0].reshape(nd, past, B_W).astype(BF16)
    oa = _lat_a_attention(qa, ka, va, cka, cva, sink)
    ob = _lat_b_attention(qb, kb, vb, ckb, cvb, lam_qk, subln, lam_init)
    xs = _post(x_sample, mod[0], norm2[0], [(oa, w_out[:nqa]), (ob, w_out[nqa:])], w1[0], w2[0], tm=tm_lat,
               ctx=False, name="post_att_lat")

    w_rin = rec_w_in[0].astype(BF16)
    w_rout = rec_w_out[0].astype(BF16)
    w_bd, b_bd = _block_diag_weights(rec_w_a[0], rec_b_a[0], rec_w_x[0], rec_b_x[0])
    segs_rec = [(0, D_RNN, False, 1.0), (D_RNN, D_RNN, False, 1.0)]
    zeros = jnp.zeros((nb, 1, D_RNN), F32)

    gate, xr = _proj(xp, mod[1], norm1[1], w_rin, segs_rec, [F32, F32], tm=tm_ctx, ctx=True, name="proj_rec_ctx")
    y, sf, sb = _rec_mixer(gate, xr, zeros, zeros, rec_conv_w[0], rec_conv_b[0], w_bd, b_bd, rec_lam[0],
                           name="rec_mixer_ctx")
    y_prompt = _post(xp, mod[1], norm2[1], [(y, w_rout)], w1[1], w2[1], tm=tm_ctx, ctx=True, final_g=final_norm,
                     name="post_rec_ctx")

    gate, xr = _proj(xs, mod[1], norm1[1], w_rin, segs_rec, [F32, F32], tm=tm_lat, ctx=False, name="proj_rec_lat")
    y, _, _ = _rec_mixer(gate, xr, state_fwd[:, 0:1], state_bwd[:, 0:1], rec_conv_w[0], rec_conv_b[0], w_bd, b_bd,
                         rec_lam[0], name="rec_mixer_lat")
    y_sample = _post(xs, mod[1], norm2[1], [(y, w_rout)], w1[1], w2[1], tm=tm_lat, ctx=False, final_g=final_norm,
                     name="post_rec_lat")

    return (y_prompt, y_sample, new_a_k, new_a_v, new_b_k, new_b_v, sf, sb)
```

```python
import functools
import math

import jax
import jax.numpy as jnp
import numpy as np
from jax import lax
from jax.experimental import pallas as pl
from jax.experimental.pallas import tpu as pltpu

F32 = jnp.float32
BF16 = jnp.bfloat16

LANES = 128
SUBLANES = 8
VMEM_LIMIT_BYTES = 56 * 1024 * 1024

D_MODEL = 1024
DEPTH = 2
GRID_W = 64
HEAD_DIM = 64
A_HEADS = 8
A_KV_HEADS = 2
A_GROUP = A_HEADS // A_KV_HEADS
B_HEADS = 4
WINDOW = 128
ROPE_BASE = 10000.0
D_RNN = 1280
RNN_BLOCKS = 10
RNN_BW = D_RNN // RNN_BLOCKS
CONV_W = 4
CONV_LEFT = (CONV_W - 1) // 2
RGLRU_C = 8.0
D_FF = 4 * D_MODEL
EPS = 1e-6
SCALE = HEAD_DIM ** -0.5
NEG = -1e30

QA_W = A_HEADS * LANES
KA_W = A_KV_HEADS * HEAD_DIM
B_W = B_HEADS * 2 * HEAD_DIM
MOD_ROWS = 8
CTX_ROW = 4
LOG2E = math.log2(math.e)
ONES_ROWS = 16


def _cparams(*semantics):
    return pltpu.CompilerParams(dimension_semantics=semantics, vmem_limit_bytes=VMEM_LIMIT_BYTES)


def _dot(a, b):
    return jnp.dot(a, b, preferred_element_type=F32)


def _dot_nt(a, b):
    return lax.dot_general(a, b, (((1,), (1,)), ((), ())), preferred_element_type=F32)


def _rms(x, g):
    return x * lax.rsqrt(jnp.mean(x * x, axis=-1, keepdims=True) + EPS) * g


def _ada_kernel(c_ref, w_ref, b_ref, o_ref):
    c = c_ref[...]
    s = c * jax.nn.sigmoid(c)
    o_ref[...] = _dot(s.astype(BF16), w_ref[...].astype(BF16)) + b_ref[...]


def _ada_mod(cvec, w_ada, b_ada):
    tn = 1536
    out = pl.pallas_call(
        _ada_kernel,
        grid=(DEPTH, 6 * D_MODEL // tn),
        in_specs=[
            pl.BlockSpec((MOD_ROWS, D_MODEL), lambda l, j: (0, 0)),
            pl.BlockSpec((None, D_MODEL, tn), lambda l, j: (l, 0, j)),
            pl.BlockSpec((None, 1, tn), lambda l, j: (l, 0, j)),
        ],
        out_specs=pl.BlockSpec((None, MOD_ROWS, tn), lambda l, j: (l, 0, j)),
        out_shape=jax.ShapeDtypeStruct((DEPTH, MOD_ROWS, 6 * D_MODEL), F32),
        compiler_params=_cparams("parallel", "parallel"),
        name="ada_mod",
    )(cvec, w_ada, b_ada.reshape(DEPTH, 1, 6 * D_MODEL))
    return out.reshape(DEPTH, MOD_ROWS, 6, D_MODEL)


def _proj_kernel(*refs, segs, rope):
    if rope:
        x_ref, mod_ref, g_ref, w_ref, cos_ref, sin_ref, *outs = refs
    else:
        x_ref, mod_ref, g_ref, w_ref, *outs = refs
    x = x_ref[...]
    h = _rms(x, g_ref[...])
    h = h * (1.0 + mod_ref[1:2, :]) + mod_ref[0:1, :]
    hb = h.astype(BF16)
    tm = x.shape[0]
    if rope:
        lane = lax.broadcasted_iota(jnp.int32, (tm, LANES), 1)
        first = (lane & 16) == 0
        cos = cos_ref[...]
        sin = sin_ref[...]
    for (c0, width, do_rope, scale, transposed), o_ref in zip(segs, outs):
        y = _dot(hb, w_ref[:, c0:c0 + width])
        for t in range(width // LANES):
            yt = y[:, t * LANES:(t + 1) * LANES]
            if do_rope:
                sw = jnp.where(first, pltpu.roll(yt, LANES - 16, 1), pltpu.roll(yt, 16, 1))
                yt = yt * cos + sw * sin
            if scale != 1.0:
                yt = yt * scale
            if transposed:
                o_ref[t * LANES:(t + 1) * LANES, :] = yt.T.astype(o_ref.dtype)
            else:
                o_ref[:, t * LANES:(t + 1) * LANES] = yt.astype(o_ref.dtype)


def _proj(x, mod, g, w, segs, out_dtypes, *, tm, ctx, rope_tabs=None, name):
    B, T, D = x.shape
    row = (lambda b, i: (CTX_ROW, 0, 0)) if ctx else (lambda b, i: (b, 0, 0))
    in_specs = [
        pl.BlockSpec((None, tm, D), lambda b, i: (b, i, 0)),
        pl.BlockSpec((None, 6, D), row),
        pl.BlockSpec((1, D), lambda b, i: (0, 0)),
        pl.BlockSpec(w.shape, lambda b, i: (0, 0)),
    ]
    args = [x, mod, g.reshape(1, D), w]
    if rope_tabs is not None:
        in_specs += [pl.BlockSpec((tm, LANES), lambda b, i: (i, 0))] * 2
        args += list(rope_tabs)
    out_specs = [pl.BlockSpec((None, s[1], tm), lambda b, i: (b, 0, i)) if s[4] else
                 pl.BlockSpec((None, tm, s[1]), lambda b, i: (b, i, 0)) for s in segs]
    out_shape = [jax.ShapeDtypeStruct((B, s[1], T) if s[4] else (B, T, s[1]), dt)
                 for s, dt in zip(segs, out_dtypes)]
    return pl.pallas_call(
        functools.partial(_proj_kernel, segs=tuple(segs), rope=rope_tabs is not None),
        grid=(B, T // tm),
        in_specs=in_specs,
        out_specs=out_specs,
        out_shape=out_shape,
        compiler_params=_cparams("parallel", "parallel"),
        name=name,
    )(*args)


def _diff_lambda(lq, lam_init):
    s1 = jnp.sum(lq[0:1, :] * lq[1:2, :], axis=1, keepdims=True)
    s2 = jnp.sum(lq[2:3, :] * lq[3:4, :], axis=1, keepdims=True)
    return jnp.exp(s1) - jnp.exp(s2) + lam_init


def _stack_group_queries(qa_ref, g):
    return jnp.concatenate(
        [qa_ref[:, (A_GROUP * g + hh) * LANES:(A_GROUP * g + hh + 1) * LANES] for hh in range(A_GROUP)], axis=0)


def _sink_column(sink_ref, g, tq):
    return jnp.concatenate(
        [jnp.full((tq, 1), sink_ref[A_GROUP * g + hh] * LOG2E, F32) for hh in range(A_GROUP)], axis=0)


def _store_a_heads(o_ref, heads, lo):
    for j in range(A_HEADS // 2):
        a, b = heads[2 * j], heads[2 * j + 1]
        if (2 * j) // A_GROUP == 0:
            tile = jnp.where(lo, a, pltpu.roll(b, HEAD_DIM, 1))
        else:
            tile = jnp.where(lo, pltpu.roll(a, HEAD_DIM, 1), b)
        o_ref[:, j * LANES:(j + 1) * LANES] = tile.astype(o_ref.dtype)


def _subln(o, subln, lam_init):
    return _rms(o, subln) * (1.0 - lam_init)


def _ctx_attn_kernel(sink_ref, qa_ref, ka_ref, va_ref, qb_ref, kb_ref, vb_ref, lamqk_ref, subln_ref, o_ref, *,
                     lam_init):
    T = qa_ref.shape[0]
    lane = lax.broadcasted_iota(jnp.int32, (T, LANES), 1)
    lo = lane < HEAD_DIM
    ka = ka_ref[...].astype(BF16)
    va = va_ref[...].astype(BF16)
    heads = []
    for g in range(A_KV_HEADS):
        qg = _stack_group_queries(qa_ref, g)
        s = _dot_nt(qg, ka)
        sink = _sink_column(sink_ref, g, T)
        m = jnp.maximum(jnp.max(s, axis=1, keepdims=True), sink)
        p = jnp.exp2(s - m)
        l = jnp.sum(p, axis=1, keepdims=True) + jnp.exp2(sink - m)
        o = _dot(p.astype(BF16), va) / l
        heads += [o[hh * T:(hh + 1) * T] for hh in range(A_GROUP)]
    _store_a_heads(o_ref, heads, lo)

    lam = _diff_lambda(lamqk_ref[...], lam_init)
    for h in range(B_HEADS):
        sl = slice(h * LANES, (h + 1) * LANES)
        q = qb_ref[:, sl]
        zero = jnp.zeros_like(q)
        qs = jnp.concatenate([jnp.where(lo, q, zero), jnp.where(lo, zero, q)], axis=0)
        s = _dot_nt(qs, kb_ref[:, sl].astype(BF16))
        e = jnp.exp2(s - jnp.max(s, axis=1, keepdims=True))
        pn = e / jnp.sum(e, axis=1, keepdims=True)
        w = pn[:T] - lam * pn[T:]
        o = _dot(w.astype(BF16), vb_ref[:, sl].astype(BF16))
        o_ref[:, A_HEADS * HEAD_DIM + h * LANES:A_HEADS * HEAD_DIM + (h + 1) * LANES] = (
            _subln(o, subln_ref[...], lam_init).astype(o_ref.dtype))


def _ctx_attention(qa, ka, va, qb, kb, vb, sink, lam_qk, subln, lam_init):
    B, T, _ = qa.shape
    blk = lambda w: pl.BlockSpec((None, T, w), lambda b: (b, 0, 0))
    full = lambda a: pl.BlockSpec(a.shape, lambda b: (0,) * a.ndim)
    return pl.pallas_call(
        functools.partial(_ctx_attn_kernel, lam_init=lam_init),
        grid=(B,),
        in_specs=[pl.BlockSpec(memory_space=pltpu.SMEM), blk(QA_W), blk(KA_W), blk(KA_W), blk(B_W), blk(B_W),
                  blk(B_W), full(lam_qk), full(subln)],
        out_specs=blk(A_HEADS * HEAD_DIM + B_W),
        out_shape=jax.ShapeDtypeStruct((B, T, A_HEADS * HEAD_DIM + B_W), BF16),
        compiler_params=_cparams("parallel"),
        name="ctx_attention",
    )(sink, qa, ka, va, qb, kb, vb, lam_qk, subln)


def _lat_a_kernel(sink_ref, qa_ref, k_ref, vt_ref, ck_ref, cvt_ref, o_ref, *, tq, band):
    T = k_ref.shape[0]
    qi = pl.program_id(1)
    start = jnp.clip(qi * tq - WINDOW, 0, T - band)
    start = pl.multiple_of(start, WINDOW)
    kb = k_ref[pl.ds(start, band), :]
    vbt = vt_ref[:, pl.ds(start, band)]
    cols = A_GROUP * tq
    kpos = start + lax.broadcasted_iota(jnp.int32, (band, cols), 0)
    qpos = qi * tq + lax.broadcasted_iota(jnp.int32, (band, cols), 1) % tq
    keep = jnp.abs(qpos - kpos) <= WINDOW
    cvt1 = jnp.concatenate([cvt_ref[...], jnp.ones((ONES_ROWS, cvt_ref.shape[1]), BF16)], axis=0)
    vbt1 = jnp.concatenate([vbt, jnp.ones((ONES_ROWS, band), BF16)], axis=0)
    heads = []
    for g in range(A_KV_HEADS):
        qg = _stack_group_queries(qa_ref, g)
        s_c = _dot_nt(ck_ref[...], qg)
        s_b = jnp.where(keep, _dot_nt(kb, qg), NEG)
        sink = jnp.concatenate(
            [jnp.full((1, tq), sink_ref[A_GROUP * g + hh] * LOG2E, F32) for hh in range(A_GROUP)], axis=1)
        m = jnp.maximum(jnp.maximum(jnp.max(s_c, axis=0, keepdims=True), jnp.max(s_b, axis=0, keepdims=True)), sink)
        p_c = jnp.exp2(s_c - m).astype(BF16)
        p_b = jnp.exp2(s_b - m).astype(BF16)
        ot = _dot(cvt1, p_c) + _dot(vbt1, p_b)
        ot = ot[:LANES] / (ot[LANES:LANES + 1] + jnp.exp2(sink - m))
        heads += [ot[g * HEAD_DIM:(g + 1) * HEAD_DIM, hh * tq:(hh + 1) * tq] for hh in range(A_GROUP)]
    o_ref[...] = jnp.concatenate(heads, axis=0).T.astype(o_ref.dtype)


def _lat_a_attention(qa, ka, vat, cka, cvat, sink):
    B, T, _ = qa.shape
    L = cka.shape[1]
    tq = WINDOW
    band = 3 * WINDOW
    return pl.pallas_call(
        functools.partial(_lat_a_kernel, tq=tq, band=band),
        grid=(B, T // tq),
        in_specs=[
            pl.BlockSpec(memory_space=pltpu.SMEM),
            pl.BlockSpec((None, tq, QA_W), lambda b, i: (b, i, 0)),
            pl.BlockSpec((None, T, KA_W), lambda b, i: (b, 0, 0)),
            pl.BlockSpec((None, KA_W, T), lambda b, i: (b, 0, 0)),
            pl.BlockSpec((None, L, KA_W), lambda b, i: (b, 0, 0)),
            pl.BlockSpec((None, KA_W, L), lambda b, i: (b, 0, 0)),
        ],
        out_specs=pl.BlockSpec((None, tq, A_HEADS * HEAD_DIM), lambda b, i: (b, i, 0)),
        out_shape=jax.ShapeDtypeStruct((B, T, A_HEADS * HEAD_DIM), BF16),
        compiler_params=_cparams("parallel", "parallel"),
        name="lat_a_attention",
    )(sink, qa, ka, vat, cka, cvat)


def _lat_b_kernel(q_ref, k_ref, vt_ref, ck_ref, cvt_ref, lamqk_ref, subln_ref, o_ref, *, tk, lam_init):
    tq = q_ref.shape[0]
    T = k_ref.shape[0]
    lo = lax.broadcasted_iota(jnp.int32, (tq, LANES), 1) < HEAD_DIM
    q = q_ref[...]
    zero = jnp.zeros_like(q)
    qs = jnp.concatenate([jnp.where(lo, q, zero), jnp.where(lo, zero, q)], axis=0)

    def step(k, vt, carry):
        m, acc = carry
        s = _dot_nt(k, qs)
        m_new = jnp.maximum(m, jnp.max(s, axis=0, keepdims=True))
        p = jnp.exp2(s - m_new).astype(BF16)
        vt1 = jnp.concatenate([vt, jnp.ones((ONES_ROWS, vt.shape[1]), BF16)], axis=0)
        acc = jnp.exp2(m - m_new) * acc + _dot(vt1, p)
        return m_new, acc

    carry = (jnp.full((1, 2 * tq), -jnp.inf, F32), jnp.zeros((LANES + ONES_ROWS, 2 * tq), F32))
    carry = step(ck_ref[...], cvt_ref[...], carry)

    for j in range(T // tk):
        carry = step(k_ref[j * tk:(j + 1) * tk, :], vt_ref[:, j * tk:(j + 1) * tk], carry)
    _, acc = carry
    ot = acc[:LANES] / acc[LANES:LANES + 1]
    lam = _diff_lambda(lamqk_ref[...], lam_init)
    out = (ot[:, :tq] - lam * ot[:, tq:]).T
    o_ref[...] = _subln(out, subln_ref[...], lam_init).astype(o_ref.dtype)


def _lat_b_attention(qb, kb, vbt, ckb, cvbt, lam_qk, subln, lam_init):
    B, T, _ = qb.shape
    L = ckb.shape[1]
    tq = 512
    tk = 512
    full = lambda a: pl.BlockSpec(a.shape, lambda b, h, i: (0,) * a.ndim)
    return pl.pallas_call(
        functools.partial(_lat_b_kernel, tk=tk, lam_init=lam_init),
        grid=(B, B_HEADS, T // tq),
        in_specs=[
            pl.BlockSpec((None, tq, LANES), lambda b, h, i: (b, i, h)),
            pl.BlockSpec((None, T, LANES), lambda b, h, i: (b, 0, h)),
            pl.BlockSpec((None, LANES, T), lambda b, h, i: (b, h, 0)),
            pl.BlockSpec((None, L, LANES), lambda b, h, i: (b, 0, h)),
            pl.BlockSpec((None, LANES, L), lambda b, h, i: (b, h, 0)),
            full(lam_qk), full(subln),
        ],
        out_specs=pl.BlockSpec((None, tq, LANES), lambda b, h, i: (b, i, h)),
        out_shape=jax.ShapeDtypeStruct((B, T, B_W), BF16),
        compiler_params=_cparams("parallel", "parallel", "parallel"),
        name="lat_b_attention",
    )(qb, kb, vbt, ckb, cvbt, lam_qk, subln)


def _scan8(a, u, rowid, reverse):
    for d in (1, 2, 4):
        if reverse:
            edge = rowid >= SUBLANES - d
            shift = SUBLANES - d
        else:
            edge = rowid < d
            shift = d
        a_s = jnp.where(edge, 1.0, pltpu.roll(a, shift, 0))
        u_s = jnp.where(edge, 0.0, pltpu.roll(u, shift, 0))
        u = u + a * u_s
        a = a * a_s
    return a, u


def _rec_kernel(gate_ref, xr_ref, cw_ref, cb_ref, wbd_ref, bbd_ref, lam_ref, h0f_ref, h0b_ref,
                y_ref, sf_ref, sb_ref, xp, af, uf, ab, ub, *, tc, group):
    T = xr_ref.shape[0]
    pad = SUBLANES
    xp[0:pad, :] = jnp.zeros((pad, LANES), F32)
    xp[T + pad:T + 2 * pad, :] = jnp.zeros((pad, LANES), F32)
    xp[pad:T + pad, :] = xr_ref[...]
    cw = cw_ref[...]
    cb = cb_ref[...]
    nl = -lam_ref[...]
    softplus = jnp.maximum(nl, 0.0) + jnp.log1p(jnp.exp(-jnp.abs(nl)))
    c8 = -RGLRU_C * softplus

    def gates(ci, _):
        t0 = pl.multiple_of(ci * tc, tc)
        w = xp[pl.ds(t0, tc + 2 * pad), :]
        y = cb
        for j in range(CONV_W):
            off = pad - CONV_LEFT + j
            y = y + w[off:off + tc, :] * cw[j:j + 1, :]
        z = _dot(y.astype(BF16), wbd_ref[...]) + bbd_ref[...]
        for d, (a_s, u_s) in enumerate(((af, uf), (ab, ub))):
            r = jax.nn.sigmoid(z[:, 2 * d * LANES:(2 * d + 1) * LANES])
            i = jax.nn.sigmoid(z[:, (2 * d + 1) * LANES:(2 * d + 2) * LANES])
            log_a = c8[d:d + 1, :] * r
            a = jnp.exp(log_a)
            u = jnp.sqrt(-jnp.tanh(log_a) * (a * a + 1.0)) * (i * y)
            a_s[pl.ds(t0, tc), :] = a
            u_s[pl.ds(t0, tc), :] = u
        return 0

    lax.fori_loop(0, T // tc, gates, 0)

    rowid = lax.broadcasted_iota(jnp.int32, (SUBLANES, LANES), 0)
    n_it = T // (SUBLANES * group)

    def fwd(it, carry):
        for q in range(group):
            r0 = pl.multiple_of((it * group + q) * SUBLANES, SUBLANES)
            a, u = _scan8(af[pl.ds(r0, SUBLANES), :], uf[pl.ds(r0, SUBLANES), :], rowid, False)
            h = u + a * carry
            uf[pl.ds(r0, SUBLANES), :] = h
            carry = jnp.broadcast_to(h[SUBLANES - 1:SUBLANES, :], (SUBLANES, LANES))
        return carry

    hf_last = lax.fori_loop(0, n_it, fwd, jnp.broadcast_to(h0f_ref[...], (SUBLANES, LANES)))
    sf_ref[...] = hf_last[0:1, :]

    def bwd(it, carry):
        for q in range(group):
            r0 = pl.multiple_of((n_it * group - 1 - (it * group + q)) * SUBLANES, SUBLANES)
            a, u = _scan8(ab[pl.ds(r0, SUBLANES), :], ub[pl.ds(r0, SUBLANES), :], rowid, True)
            h = u + a * carry
            ub[pl.ds(r0, SUBLANES), :] = h
            carry = jnp.broadcast_to(h[0:1, :], (SUBLANES, LANES))
        return carry

    hb_first = lax.fori_loop(0, n_it, bwd, jnp.broadcast_to(h0b_ref[...], (SUBLANES, LANES)))
    sb_ref[...] = hb_first[0:1, :]

    k_gelu = math.sqrt(2.0 / math.pi)

    def combine(ci, _):
        t0 = pl.multiple_of(ci * tc, tc)
        g = gate_ref[pl.ds(t0, tc), :]
        gelu = g * (0.5 * (1.0 + jnp.tanh(k_gelu * (g + 0.044715 * (g * g * g)))))
        y_ref[pl.ds(t0, tc), :] = ((uf[pl.ds(t0, tc), :] + ub[pl.ds(t0, tc), :]) * gelu).astype(y_ref.dtype)
        return 0

    lax.fori_loop(0, T // tc, combine, 0)


def _rec_mixer(gate, xr, h0f, h0b, conv_w, conv_b, w_bd, b_bd, lam, *, name):
    B, T, _ = xr.shape
    tc = 256
    col = lambda rows: pl.BlockSpec((rows, LANES), lambda b, n: (0, n))
    seq = pl.BlockSpec((None, T, LANES), lambda b, n: (b, 0, n))
    st = pl.BlockSpec((None, 1, LANES), lambda b, n: (b, 0, n))
    return pl.pallas_call(
        functools.partial(_rec_kernel, tc=tc, group=4),
        grid=(B, RNN_BLOCKS),
        in_specs=[seq, seq, col(CONV_W), col(1),
                  pl.BlockSpec((None, LANES, 4 * LANES), lambda b, n: (n, 0, 0)),
                  pl.BlockSpec((None, 1, 4 * LANES), lambda b, n: (n, 0, 0)),
                  col(2), st, st],
        out_specs=[seq, st, st],
        out_shape=[jax.ShapeDtypeStruct((B, T, D_RNN), BF16),
                   jax.ShapeDtypeStruct((B, 1, D_RNN), F32),
                   jax.ShapeDtypeStruct((B, 1, D_RNN), F32)],
        scratch_shapes=[pltpu.VMEM((T + 2 * SUBLANES, LANES), F32)] + [pltpu.VMEM((T, LANES), F32)] * 4,
        compiler_params=_cparams("parallel", "parallel"),
        name=name,
    )(gate, xr, conv_w, conv_b.reshape(1, D_RNN), w_bd, b_bd, lam, h0f, h0b)


def _post_kernel(*refs, n_mix, final, fc):
    x_ref, mod_ref, g2_ref = refs[:3]
    mix = refs[3:3 + 2 * n_mix]
    w1_ref, w2_ref = refs[3 + 2 * n_mix:5 + 2 * n_mix]
    rest = refs[5 + 2 * n_mix:]
    if final:
        gf_ref, o_ref = rest
    else:
        (o_ref,) = rest
    mixed = _dot(mix[0][...], mix[1][...])
    for i in range(1, n_mix):
        mixed = mixed + _dot(mix[2 * i][...], mix[2 * i + 1][...])
    x1 = x_ref[...] + mod_ref[2:3, :] * mixed
    h = _rms(x1, g2_ref[...])
    hb = (h * (1.0 + mod_ref[4:5, :]) + mod_ref[3:4, :]).astype(BF16)
    acc = None
    for c in range(D_FF // fc):
        a = _dot(hb, w1_ref[:, c * fc:(c + 1) * fc])
        a = jnp.square(jnp.maximum(a, 0.0)).astype(BF16)
        part = _dot(a, w2_ref[c * fc:(c + 1) * fc, :])
        acc = part if acc is None else acc + part
    x2 = x1 + mod_ref[5:6, :] * acc
    if final:
        x2 = _rms(x2, gf_ref[...])
    o_ref[...] = x2


def _post(x, mod, g2, mixes, w1, w2, *, tm, ctx, final_g=None, name):
    B, T, D = x.shape
    row = (lambda b, i: (CTX_ROW, 0, 0)) if ctx else (lambda b, i: (b, 0, 0))
    const = lambda a: pl.BlockSpec(a.shape, lambda b, i: (0,) * a.ndim, pipeline_mode=pl.Buffered(1))
    in_specs = [
        pl.BlockSpec((None, tm, D), lambda b, i: (b, i, 0)),
        pl.BlockSpec((None, 6, D), row),
        pl.BlockSpec((1, D), lambda b, i: (0, 0)),
    ]
    args = [x, mod, g2.reshape(1, D)]
    for o, w in mixes:
        in_specs += [pl.BlockSpec((None, tm, o.shape[-1]), lambda b, i: (b, i, 0)), const(w)]
        args += [o, w]
    in_specs += [const(w1), const(w2)]
    args += [w1, w2]
    if final_g is not None:
        in_specs.append(pl.BlockSpec((1, D), lambda b, i: (0, 0)))
        args.append(final_g.reshape(1, D))
    return pl.pallas_call(
        functools.partial(_post_kernel, n_mix=len(mixes), final=final_g is not None, fc=1024),
        grid=(B, T // tm),
        in_specs=in_specs,
        out_specs=pl.BlockSpec((None, tm, D), lambda b, i: (b, i, 0)),
        out_shape=jax.ShapeDtypeStruct((B, T, D), F32),
        compiler_params=_cparams("parallel", "parallel"),
        name=name,
    )(*args)


def _rope_tables(n_tokens):
    rows = n_tokens // GRID_W
    r, cl = jnp.meshgrid(jnp.arange(rows, dtype=F32), jnp.arange(GRID_W, dtype=F32), indexing='ij')
    quarter = HEAD_DIM // 4
    inv = ROPE_BASE ** (-jnp.arange(quarter, dtype=F32) / quarter)
    ang = jnp.stack([r.reshape(-1)[:, None] * inv, cl.reshape(-1)[:, None] * inv], axis=1)
    cos, sin = jnp.cos(ang), jnp.sin(ang)
    cos64 = jnp.concatenate([cos[:, 0], cos[:, 0], cos[:, 1], cos[:, 1]], axis=-1)
    sin64 = jnp.concatenate([-sin[:, 0], sin[:, 0], -sin[:, 1], sin[:, 1]], axis=-1)
    return jnp.tile(cos64, (1, LANES // HEAD_DIM)), jnp.tile(sin64, (1, LANES // HEAD_DIM))


def _att_in_weights(w_in):
    d = w_in.shape[0]
    nq = A_HEADS * HEAD_DIM
    wq = w_in[:, :nq].reshape(d, A_HEADS, HEAD_DIM)
    z = jnp.zeros_like(wq)
    in_first = (jnp.arange(A_HEADS) // A_GROUP == 0)[None, :, None]
    wq = jnp.where(in_first, jnp.concatenate([wq, z], axis=-1), jnp.concatenate([z, wq], axis=-1))
    return jnp.concatenate([wq.reshape(d, QA_W), w_in[:, nq:]], axis=1).astype(BF16)


def _block_diag_weights(w_a, b_a, w_x, b_x):
    w = jnp.concatenate([w_a[0], w_x[0], w_a[1], w_x[1]], axis=-1).astype(BF16)
    b = jnp.concatenate([v.reshape(RNN_BLOCKS, 1, RNN_BW) for v in (b_a[0], b_x[0], b_a[1], b_x[1])], axis=-1)
    return w, b


def kernel(x_prompt, x_sample, cache_a_k, cache_a_v, cache_b_k, cache_b_v, state_fwd, state_bwd, c, c_ctx, norm1, norm2, w_ada, b_ada, w_mlp1, w_mlp2, att_w_in, att_w_out, att_sink, att_lam_qk, att_subln, rec_w_in, rec_conv_w, rec_conv_b, rec_w_a, rec_b_a, rec_w_x, rec_b_x, rec_lam, rec_w_out, final_norm):
    nb, n_seq, _ = x_prompt.shape
    nd, d_seq, _ = x_sample.shape
    past = cache_a_k.shape[2]
    assert nd <= CTX_ROW and DEPTH == 2
    cvec = jnp.concatenate([c, jnp.zeros((CTX_ROW - nd, D_MODEL), F32), c_ctx[None],
                            jnp.zeros((MOD_ROWS - CTX_ROW - 1, D_MODEL), F32)], axis=0)
    mod = _ada_mod(cvec, w_ada, b_ada)
    w1 = w_mlp1.astype(BF16)
    w2 = w_mlp2.astype(BF16)
    tm_ctx, tm_lat = n_seq, 512

    lam_init = 0.8 - 0.6 * math.exp(-0.3 * 0)
    w_in = _att_in_weights(att_w_in[0])
    w_out = att_w_out[0].astype(BF16)
    nqa = A_HEADS * HEAD_DIM
    c0 = [0, QA_W, QA_W + KA_W, QA_W + 2 * KA_W, QA_W + 2 * KA_W + B_W, QA_W + 2 * KA_W + 2 * B_W]
    widths = [QA_W, KA_W, KA_W, B_W, B_W, B_W]
    scales = [SCALE * LOG2E, 1.0, 1.0, SCALE * LOG2E, 1.0, 1.0]
    roped = [True, True, False, True, True, False]
    sink = att_sink[0]
    lam_qk = att_lam_qk[0]
    subln = att_subln[0].reshape(1, 2 * HEAD_DIM)

    segs_ctx = [(c0[i], widths[i], False, scales[i], False) for i in range(6)]
    qa, ka, va, qb, kb, vb = _proj(x_prompt, mod[0], norm1[0], w_in, segs_ctx, [BF16, F32, F32, BF16, F32, F32],
                                   tm=tm_ctx, ctx=True, name="proj_att_ctx")
    o_ctx = _ctx_attention(qa, ka, va, qb, kb, vb, sink, lam_qk, subln, lam_init)
    xp = _post(x_prompt, mod[0], norm2[0], [(o_ctx, w_out)], w1[0], w2[0], tm=tm_ctx, ctx=True, name="post_att_ctx")
    new_a_k = ka.reshape(nb, 1, n_seq, A_KV_HEADS, HEAD_DIM)
    new_a_v = va.reshape(nb, 1, n_seq, A_KV_HEADS, HEAD_DIM)
    new_b_k = kb.reshape(nb, 1, n_seq, B_HEADS, 2 * HEAD_DIM)
    new_b_v = vb.reshape(nb, 1, n_seq, B_HEADS, 2 * HEAD_DIM)

    segs_lat = [(c0[i], widths[i], roped[i], scales[i], i in (2, 5)) for i in range(6)]
    qa, ka, vat, qb, kb, vbt = _proj(x_sample, mod[0], norm1[0], w_in, segs_lat, [BF16] * 6, tm=tm_lat, ctx=False,
                                     rope_tabs=_rope_tables(d_seq), name="proj_att_lat")
    cka = cache_a_k[:, 0].reshape(nd, past, KA_W).astype(BF16)
    cvat = jnp.swapaxes(cache_a_v[:, 0].reshape(nd, past, KA_W), 1, 2).astype(BF16)
    ckb = cache_b_k[:, 0].reshape(nd, past, B_W).astype(BF16)
    cvbt = jnp.swapaxes(cache_b_v[:, 0].reshape(nd, past, B_W), 1, 2).astype(BF16)
    oa = _lat_a_attention(qa, ka, vat, cka, cvat, sink)
    ob = _lat_b_attention(qb, kb, vbt, ckb, cvbt, lam_qk, subln, lam_init)
    xs = _post(x_sample, mod[0], norm2[0], [(oa, w_out[:nqa]), (ob, w_out[nqa:])], w1[0], w2[0], tm=tm_lat,
               ctx=False, name="post_att_lat")

    w_rin = rec_w_in[0].astype(BF16)
    w_rout = rec_w_out[0].astype(BF16)
    w_bd, b_bd = _block_diag_weights(rec_w_a[0], rec_b_a[0], rec_w_x[0], rec_b_x[0])
    segs_rec = [(0, D_RNN, False, 1.0, False), (D_RNN, D_RNN, False, 1.0, False)]
    zeros = jnp.zeros((nb, 1, D_RNN), F32)

    gate, xr = _proj(xp, mod[1], norm1[1], w_rin, segs_rec, [F32, F32], tm=tm_ctx, ctx=True, name="proj_rec_ctx")
    y, sf, sb = _rec_mixer(gate, xr, zeros, zeros, rec_conv_w[0], rec_conv_b[0], w_bd, b_bd, rec_lam[0],
                           name="rec_mixer_ctx")
    y_prompt = _post(xp, mod[1], norm2[1], [(y, w_rout)], w1[1], w2[1], tm=tm_ctx, ctx=True, final_g=final_norm,
                     name="post_rec_ctx")

    gate, xr = _proj(xs, mod[1], norm1[1], w_rin, segs_rec, [F32, F32], tm=tm_lat, ctx=False, name="proj_rec_lat")
    y, _, _ = _rec_mixer(gate, xr, state_fwd[:, 0:1], state_bwd[:, 0:1], rec_conv_w[0], rec_conv_b[0], w_bd, b_bd,
                         rec_lam[0], name="rec_mixer_lat")
    y_sample = _post(xs, mod[1], norm2[1], [(y, w_rout)], w1[1], w2[1], tm=tm_lat, ctx=False, final_g=final_norm,
                     name="post_rec_lat")

    return (y_prompt, y_sample, new_a_k, new_a_v, new_b_k, new_b_v, sf, sb)
```

```python
import functools
import math

import jax
import jax.numpy as jnp
import numpy as np
from jax import lax
from jax.experimental import pallas as pl
from jax.experimental.pallas import tpu as pltpu

F32 = jnp.float32
BF16 = jnp.bfloat16

LANES = 128
SUBLANES = 8
VMEM_LIMIT_BYTES = 56 * 1024 * 1024

D_MODEL = 1024
DEPTH = 2
GRID_W = 64
HEAD_DIM = 64
A_HEADS = 8
A_KV_HEADS = 2
A_GROUP = A_HEADS // A_KV_HEADS
B_HEADS = 4
WINDOW = 128
ROPE_BASE = 10000.0
D_RNN = 1280
RNN_BLOCKS = 10
RNN_BW = D_RNN // RNN_BLOCKS
CONV_W = 4
CONV_LEFT = (CONV_W - 1) // 2
RGLRU_C = 8.0
D_FF = 4 * D_MODEL
EPS = 1e-6
SCALE = HEAD_DIM ** -0.5
NEG = -1e30

QA_W = A_HEADS * LANES
KA_W = A_KV_HEADS * HEAD_DIM
B_W = B_HEADS * 2 * HEAD_DIM
MOD_ROWS = 8
CTX_ROW = 4
LOG2E = math.log2(math.e)
ONES_ROWS = 16


def _cparams(*semantics):
    return pltpu.CompilerParams(dimension_semantics=semantics, vmem_limit_bytes=VMEM_LIMIT_BYTES)


def _dot(a, b):
    return jnp.dot(a, b, preferred_element_type=F32)


def _dot_nt(a, b):
    return lax.dot_general(a, b, (((1,), (1,)), ((), ())), preferred_element_type=F32)


def _rms(x, g):
    return x * lax.rsqrt(jnp.mean(x * x, axis=-1, keepdims=True) + EPS) * g


def _ada_kernel(c_ref, w_ref, b_ref, o_ref):
    c = c_ref[...]
    s = c * jax.nn.sigmoid(c)
    o_ref[...] = _dot(s.astype(BF16), w_ref[...].astype(BF16)) + b_ref[...]


def _ada_mod(cvec, w_ada, b_ada):
    tn = 1536
    out = pl.pallas_call(
        _ada_kernel,
        grid=(DEPTH, 6 * D_MODEL // tn),
        in_specs=[
            pl.BlockSpec((MOD_ROWS, D_MODEL), lambda l, j: (0, 0)),
            pl.BlockSpec((None, D_MODEL, tn), lambda l, j: (l, 0, j)),
            pl.BlockSpec((None, 1, tn), lambda l, j: (l, 0, j)),
        ],
        out_specs=pl.BlockSpec((None, MOD_ROWS, tn), lambda l, j: (l, 0, j)),
        out_shape=jax.ShapeDtypeStruct((DEPTH, MOD_ROWS, 6 * D_MODEL), F32),
        compiler_params=_cparams("parallel", "parallel"),
        name="ada_mod",
    )(cvec, w_ada, b_ada.reshape(DEPTH, 1, 6 * D_MODEL))
    return out.reshape(DEPTH, MOD_ROWS, 6, D_MODEL)


def _proj_kernel(*refs, segs, rope):
    if rope:
        x_ref, mod_ref, g_ref, w_ref, cos_ref, sin_ref, *outs = refs
    else:
        x_ref, mod_ref, g_ref, w_ref, *outs = refs
    x = x_ref[...]
    h = _rms(x, g_ref[...])
    h = h * (1.0 + mod_ref[1:2, :]) + mod_ref[0:1, :]
    hb = h.astype(BF16)
    tm = x.shape[0]
    if rope:
        lane = lax.broadcasted_iota(jnp.int32, (tm, LANES), 1)
        first = (lane & 16) == 0
        cos = cos_ref[...]
        sin = sin_ref[...]
    for (c0, width, do_rope, scale, transposed), o_ref in zip(segs, outs):
        y = _dot(hb, w_ref[:, c0:c0 + width])
        for t in range(width // LANES):
            yt = y[:, t * LANES:(t + 1) * LANES]
            if do_rope:
                sw = jnp.where(first, pltpu.roll(yt, LANES - 16, 1), pltpu.roll(yt, 16, 1))
                yt = yt * cos + sw * sin
            if scale != 1.0:
                yt = yt * scale
            if transposed:
                o_ref[t * LANES:(t + 1) * LANES, :] = yt.T.astype(o_ref.dtype)
            else:
                o_ref[:, t * LANES:(t + 1) * LANES] = yt.astype(o_ref.dtype)


def _proj(x, mod, g, w, segs, out_dtypes, *, tm, ctx, rope_tabs=None, name):
    B, T, D = x.shape
    row = (lambda b, i: (CTX_ROW, 0, 0)) if ctx else (lambda b, i: (b, 0, 0))
    in_specs = [
        pl.BlockSpec((None, tm, D), lambda b, i: (b, i, 0)),
        pl.BlockSpec((None, 6, D), row),
        pl.BlockSpec((1, D), lambda b, i: (0, 0)),
        pl.BlockSpec(w.shape, lambda b, i: (0, 0)),
    ]
    args = [x, mod, g.reshape(1, D), w]
    if rope_tabs is not None:
        in_specs += [pl.BlockSpec((tm, LANES), lambda b, i: (i, 0))] * 2
        args += list(rope_tabs)
    out_specs = [pl.BlockSpec((None, s[1], tm), lambda b, i: (b, 0, i)) if s[4] else
                 pl.BlockSpec((None, tm, s[1]), lambda b, i: (b, i, 0)) for s in segs]
    out_shape = [jax.ShapeDtypeStruct((B, s[1], T) if s[4] else (B, T, s[1]), dt)
                 for s, dt in zip(segs, out_dtypes)]
    return pl.pallas_call(
        functools.partial(_proj_kernel, segs=tuple(segs), rope=rope_tabs is not None),
        grid=(B, T // tm),
        in_specs=in_specs,
        out_specs=out_specs,
        out_shape=out_shape,
        compiler_params=_cparams("parallel", "parallel"),
        name=name,
    )(*args)


def _diff_lambda(lq, lam_init):
    s1 = jnp.sum(lq[0:1, :] * lq[1:2, :], axis=1, keepdims=True)
    s2 = jnp.sum(lq[2:3, :] * lq[3:4, :], axis=1, keepdims=True)
    return jnp.exp(s1) - jnp.exp(s2) + lam_init


def _stack_group_queries(qa_ref, g):
    return jnp.concatenate(
        [qa_ref[:, (A_GROUP * g + hh) * LANES:(A_GROUP * g + hh + 1) * LANES] for hh in range(A_GROUP)], axis=0)


def _sink_column(sink_ref, g, tq):
    return jnp.concatenate(
        [jnp.full((tq, 1), sink_ref[A_GROUP * g + hh] * LOG2E, F32) for hh in range(A_GROUP)], axis=0)


def _store_a_heads(o_ref, heads, lo):
    for j in range(A_HEADS // 2):
        a, b = heads[2 * j], heads[2 * j + 1]
        if (2 * j) // A_GROUP == 0:
            tile = jnp.where(lo, a, pltpu.roll(b, HEAD_DIM, 1))
        else:
            tile = jnp.where(lo, pltpu.roll(a, HEAD_DIM, 1), b)
        o_ref[:, j * LANES:(j + 1) * LANES] = tile.astype(o_ref.dtype)


def _subln(o, subln, lam_init):
    return _rms(o, subln) * (1.0 - lam_init)


def _ctx_attn_kernel(sink_ref, qa_ref, ka_ref, va_ref, qb_ref, kb_ref, vb_ref, lamqk_ref, subln_ref, o_ref, *,
                     lam_init):
    T = qa_ref.shape[0]
    lane = lax.broadcasted_iota(jnp.int32, (T, LANES), 1)
    lo = lane < HEAD_DIM
    ka = ka_ref[...].astype(BF16)
    va = va_ref[...].astype(BF16)
    heads = []
    for g in range(A_KV_HEADS):
        qg = _stack_group_queries(qa_ref, g)
        s = _dot_nt(qg, ka)
        sink = _sink_column(sink_ref, g, T)
        m = jnp.maximum(jnp.max(s, axis=1, keepdims=True), sink)
        p = jnp.exp2(s - m)
        l = jnp.sum(p, axis=1, keepdims=True) + jnp.exp2(sink - m)
        o = _dot(p.astype(BF16), va) / l
        heads += [o[hh * T:(hh + 1) * T] for hh in range(A_GROUP)]
    _store_a_heads(o_ref, heads, lo)

    lam = _diff_lambda(lamqk_ref[...], lam_init)
    for h in range(B_HEADS):
        sl = slice(h * LANES, (h + 1) * LANES)
        q = qb_ref[:, sl]
        zero = jnp.zeros_like(q)
        qs = jnp.concatenate([jnp.where(lo, q, zero), jnp.where(lo, zero, q)], axis=0)
        s = _dot_nt(qs, kb_ref[:, sl].astype(BF16))
        e = jnp.exp2(s - jnp.max(s, axis=1, keepdims=True))
        pn = e / jnp.sum(e, axis=1, keepdims=True)
        w = pn[:T] - lam * pn[T:]
        o = _dot(w.astype(BF16), vb_ref[:, sl].astype(BF16))
        o_ref[:, A_HEADS * HEAD_DIM + h * LANES:A_HEADS * HEAD_DIM + (h + 1) * LANES] = (
            _subln(o, subln_ref[...], lam_init).astype(o_ref.dtype))


def _ctx_attention(qa, ka, va, qb, kb, vb, sink, lam_qk, subln, lam_init):
    B, T, _ = qa.shape
    blk = lambda w: pl.BlockSpec((None, T, w), lambda b: (b, 0, 0))
    full = lambda a: pl.BlockSpec(a.shape, lambda b: (0,) * a.ndim)
    return pl.pallas_call(
        functools.partial(_ctx_attn_kernel, lam_init=lam_init),
        grid=(B,),
        in_specs=[pl.BlockSpec(memory_space=pltpu.SMEM), blk(QA_W), blk(KA_W), blk(KA_W), blk(B_W), blk(B_W),
                  blk(B_W), full(lam_qk), full(subln)],
        out_specs=blk(A_HEADS * HEAD_DIM + B_W),
        out_shape=jax.ShapeDtypeStruct((B, T, A_HEADS * HEAD_DIM + B_W), BF16),
        compiler_params=_cparams("parallel"),
        name="ctx_attention",
    )(sink, qa, ka, va, qb, kb, vb, lam_qk, subln)


def _lat_a_kernel(sink_ref, qa_ref, k_ref, vt_ref, ck_ref, cvt_ref, o_ref, *, tq, band):
    T = k_ref.shape[0]
    qi = pl.program_id(1)
    start = jnp.clip(qi * tq - WINDOW, 0, T - band)
    start = pl.multiple_of(start, WINDOW)
    kb = k_ref[pl.ds(start, band), :]
    vbt = vt_ref[:, pl.ds(start, band)]
    cols = A_GROUP * tq
    kpos = start + lax.broadcasted_iota(jnp.int32, (band, cols), 0)
    qpos = qi * tq + lax.broadcasted_iota(jnp.int32, (band, cols), 1) % tq
    keep = jnp.abs(qpos - kpos) <= WINDOW
    cvt1 = jnp.concatenate([cvt_ref[...], jnp.ones((ONES_ROWS, cvt_ref.shape[1]), BF16)], axis=0)
    vbt1 = jnp.concatenate([vbt, jnp.ones((ONES_ROWS, band), BF16)], axis=0)
    heads = []
    for g in range(A_KV_HEADS):
        qg = _stack_group_queries(qa_ref, g)
        s_c = _dot_nt(ck_ref[...], qg)
        s_b = jnp.where(keep, _dot_nt(kb, qg), NEG)
        sink = jnp.concatenate(
            [jnp.full((1, tq), sink_ref[A_GROUP * g + hh] * LOG2E, F32) for hh in range(A_GROUP)], axis=1)
        m = jnp.maximum(jnp.maximum(jnp.max(s_c, axis=0, keepdims=True), jnp.max(s_b, axis=0, keepdims=True)), sink)
        p_c = jnp.exp2(s_c - m).astype(BF16)
        p_b = jnp.exp2(s_b - m).astype(BF16)
        ot = _dot(cvt1, p_c) + _dot(vbt1, p_b)
        ot = ot[:LANES] / (ot[LANES:LANES + 1] + jnp.exp2(sink - m))
        heads += [ot[g * HEAD_DIM:(g + 1) * HEAD_DIM, hh * tq:(hh + 1) * tq] for hh in range(A_GROUP)]
    o_ref[...] = jnp.concatenate(heads, axis=0).T.astype(o_ref.dtype)


def _lat_a_attention(qa, ka, vat, cka, cvat, sink):
    B, T, _ = qa.shape
    L = cka.shape[1]
    tq = WINDOW
    band = 3 * WINDOW
    return pl.pallas_call(
        functools.partial(_lat_a_kernel, tq=tq, band=band),
        grid=(B, T // tq),
        in_specs=[
            pl.BlockSpec(memory_space=pltpu.SMEM),
            pl.BlockSpec((None, tq, QA_W), lambda b, i: (b, i, 0)),
            pl.BlockSpec((None, T, KA_W), lambda b, i: (b, 0, 0)),
            pl.BlockSpec((None, KA_W, T), lambda b, i: (b, 0, 0)),
            pl.BlockSpec((None, L, KA_W), lambda b, i: (b, 0, 0)),
            pl.BlockSpec((None, KA_W, L), lambda b, i: (b, 0, 0)),
        ],
        out_specs=pl.BlockSpec((None, tq, A_HEADS * HEAD_DIM), lambda b, i: (b, i, 0)),
        out_shape=jax.ShapeDtypeStruct((B, T, A_HEADS * HEAD_DIM), BF16),
        compiler_params=_cparams("parallel", "parallel"),
        name="lat_a_attention",
    )(sink, qa, ka, vat, cka, cvat)


def _lat_b_kernel(q_ref, k_ref, vt_ref, ck_ref, cvt_ref, lamqk_ref, subln_ref, o_ref, *, tk, lam_init):
    tq = q_ref.shape[0]
    T = k_ref.shape[0]
    lo = lax.broadcasted_iota(jnp.int32, (tq, LANES), 1) < HEAD_DIM
    q = q_ref[...]
    zero = jnp.zeros_like(q)
    qs = jnp.concatenate([jnp.where(lo, q, zero), jnp.where(lo, zero, q)], axis=0)

    def step(k, vt, carry):
        m, acc = carry
        s = _dot_nt(k, qs)
        m_new = jnp.maximum(m, jnp.max(s, axis=0, keepdims=True))
        p = jnp.exp2(s - m_new).astype(BF16)
        vt1 = jnp.concatenate([vt, jnp.ones((ONES_ROWS, vt.shape[1]), BF16)], axis=0)
        acc = jnp.exp2(m - m_new) * acc + _dot(vt1, p)
        return m_new, acc

    carry = (jnp.full((1, 2 * tq), -jnp.inf, F32), jnp.zeros((LANES + ONES_ROWS, 2 * tq), F32))
    carry = step(ck_ref[...], cvt_ref[...], carry)

    for j in range(T // tk):
        carry = step(k_ref[j * tk:(j + 1) * tk, :], vt_ref[:, j * tk:(j + 1) * tk], carry)
    _, acc = carry
    ot = acc[:LANES] / acc[LANES:LANES + 1]
    lam = _diff_lambda(lamqk_ref[...], lam_init)
    out = (ot[:, :tq] - lam * ot[:, tq:]).T
    o_ref[...] = _subln(out, subln_ref[...], lam_init).astype(o_ref.dtype)


def _lat_b_attention(qb, kb, vbt, ckb, cvbt, lam_qk, subln, lam_init):
    B, T, _ = qb.shape
    L = ckb.shape[1]
    tq = 512
    tk = 512
    full = lambda a: pl.BlockSpec(a.shape, lambda b, h, i: (0,) * a.ndim)
    return pl.pallas_call(
        functools.partial(_lat_b_kernel, tk=tk, lam_init=lam_init),
        grid=(B, B_HEADS, T // tq),
        in_specs=[
            pl.BlockSpec((None, tq, LANES), lambda b, h, i: (b, i, h)),
            pl.BlockSpec((None, T, LANES), lambda b, h, i: (b, 0, h)),
            pl.BlockSpec((None, LANES, T), lambda b, h, i: (b, h, 0)),
            pl.BlockSpec((None, L, LANES), lambda b, h, i: (b, 0, h)),
            pl.BlockSpec((None, LANES, L), lambda b, h, i: (b, h, 0)),
            full(lam_qk), full(subln),
        ],
        out_specs=pl.BlockSpec((None, tq, LANES), lambda b, h, i: (b, i, h)),
        out_shape=jax.ShapeDtypeStruct((B, T, B_W), BF16),
        compiler_params=_cparams("parallel", "parallel", "parallel"),
        name="lat_b_attention",
    )(qb, kb, vbt, ckb, cvbt, lam_qk, subln)


def _scan_chunk_len(n_steps):
    chunk = -(-n_steps // SUBLANES)
    while chunk % 8 != 4:
        chunk += 1
    return chunk


def _rec_kernel(gate_ref, xr_ref, cw_ref, cb_ref, wbd_ref, bbd_ref, lam_ref, h0f_ref, h0b_ref,
                y_ref, sf_ref, sb_ref, xp, af, uf, ab, ub, pf_s, hf_s, pb_s, hb_s, *, nblk, **kw):
    for n in range(nblk):
        sl = slice(n * LANES, (n + 1) * LANES)
        _rec_block(gate_ref.at[:, sl], xr_ref.at[:, sl], cw_ref.at[:, sl], cb_ref.at[:, sl], wbd_ref.at[n],
                   bbd_ref.at[n], lam_ref.at[:, sl], h0f_ref.at[:, sl], h0b_ref.at[:, sl], y_ref.at[:, sl],
                   sf_ref.at[:, sl], sb_ref.at[:, sl], xp.at[n], af.at[n], uf.at[n], ab.at[n], ub.at[n],
                   pf_s.at[n], hf_s.at[n], pb_s.at[n], hb_s.at[n], **kw)


def _rec_block(gate_ref, xr_ref, cw_ref, cb_ref, wbd_ref, bbd_ref, lam_ref, h0f_ref, h0b_ref,
               y_ref, sf_ref, sb_ref, xp, af, uf, ab, ub, pf_s, hf_s, pb_s, hb_s, *, tc, gates_unroll, chunk, unroll):
    T = xr_ref.shape[0]
    pad = SUBLANES
    xp[0:pad, :] = jnp.zeros((pad, LANES), F32)
    xp[T + pad:T + 2 * pad, :] = jnp.zeros((pad, LANES), F32)
    xp[pad:T + pad, :] = xr_ref[...]
    tail = SUBLANES * chunk - T
    for a_s, u_s in ((af, uf), (ab, ub)):
        a_s[T:T + tail, :] = jnp.ones((tail, LANES), F32)
        u_s[T:T + tail, :] = jnp.zeros((tail, LANES), F32)
    cw = cw_ref[...]
    cb = cb_ref[...]
    nl = -lam_ref[...]
    softplus = jnp.maximum(nl, 0.0) + jnp.log1p(jnp.exp(-jnp.abs(nl)))
    cp = (0.5 * RGLRU_C) * softplus

    def gates(it, _):
        for q in range(gates_unroll):
            gate_chunk(it * gates_unroll + q)
        return 0

    def gate_chunk(ci):
        t0 = pl.multiple_of(ci * tc, tc)
        y = cb
        for j in range(CONV_W):
            y = y + xp[pl.ds(t0 + (pad - CONV_LEFT + j), tc), :] * cw[j:j + 1, :]
        t = jnp.tanh(_dot(y.astype(BF16), wbd_ref[...]) + bbd_ref[...])
        yh = 0.5 * y
        for d, (a_s, u_s) in enumerate(((af, uf), (ab, ub))):
            t_r = t[:, 2 * d * LANES:(2 * d + 1) * LANES]
            t_i = t[:, (2 * d + 1) * LANES:(2 * d + 2) * LANES]
            neg_log_a = cp[d:d + 1, :] * t_r + cp[d:d + 1, :]
            a = jnp.exp2(neg_log_a * (-LOG2E))
            w = jnp.tanh(neg_log_a) * (a * a + 1.0)
            sqrt_w = jnp.where(w > 0.0, w * lax.rsqrt(w), 0.0)
            a_s[pl.ds(t0, tc), :] = a
            u_s[pl.ds(t0, tc), :] = (t_i + 1.0) * (yh * sqrt_w)

    lax.fori_loop(0, T // (tc * gates_unroll), gates, 0)

    def rows(k):
        return pl.ds(k, SUBLANES, stride=chunk)

    def local_scan(it, carry):
        hf, pf, hb, pb = carry
        for q in range(unroll):
            k = it * unroll + q
            a = af[rows(k), :]
            hf = a * hf + uf[rows(k), :]
            pf = a * pf
            hf_s[rows(k), :] = hf
            pf_s[rows(k), :] = pf
            kb = chunk - 1 - k
            a = ab[rows(kb), :]
            hb = a * hb + ub[rows(kb), :]
            pb = a * pb
            hb_s[rows(kb), :] = hb
            pb_s[rows(kb), :] = pb
        return hf, pf, hb, pb

    zero = jnp.zeros((SUBLANES, LANES), F32)
    one = jnp.ones((SUBLANES, LANES), F32)
    hf, pf, hb, pb = lax.fori_loop(0, chunk // unroll, local_scan, (zero, one, zero, one))

    cf = [h0f_ref[...]]
    for r in range(SUBLANES - 1):
        cf.append(hf[r:r + 1, :] + pf[r:r + 1, :] * cf[r])
    cb_rev = [h0b_ref[...]]
    for r in range(SUBLANES - 1, 0, -1):
        cb_rev.append(hb[r:r + 1, :] + pb[r:r + 1, :] * cb_rev[-1])
    carry_f = jnp.concatenate(cf, axis=0)
    carry_b = jnp.concatenate(cb_rev[::-1], axis=0)

    def apply_carry(it, _):
        for q in range(unroll):
            k = it * unroll + q
            uf[rows(k), :] = hf_s[rows(k), :] + pf_s[rows(k), :] * carry_f
            ub[rows(k), :] = hb_s[rows(k), :] + pb_s[rows(k), :] * carry_b
        return 0

    lax.fori_loop(0, chunk // unroll, apply_carry, 0)
    sf_ref[...] = uf[T - 1:T, :]
    sb_ref[...] = ub[0:1, :]

    k_gelu = math.sqrt(2.0 / math.pi)

    def combine(ci, _):
        t0 = pl.multiple_of(ci * tc, tc)
        g = gate_ref[pl.ds(t0, tc), :]
        gelu = g * (0.5 * (1.0 + jnp.tanh(k_gelu * (g + 0.044715 * (g * g * g)))))
        y_ref[pl.ds(t0, tc), :] = ((uf[pl.ds(t0, tc), :] + ub[pl.ds(t0, tc), :]) * gelu).astype(y_ref.dtype)
        return 0

    lax.fori_loop(0, T // tc, combine, 0)


def _rec_mixer(gate, xr, h0f, h0b, conv_w, conv_b, w_bd, b_bd, lam, *, nblk, tc, name):
    B, T, _ = xr.shape
    chunk = _scan_chunk_len(T)
    wid = nblk * LANES
    col = lambda rows: pl.BlockSpec((rows, wid), lambda b, n: (0, n))
    seq = pl.BlockSpec((None, T, wid), lambda b, n: (b, 0, n))
    st = pl.BlockSpec((None, 1, wid), lambda b, n: (b, 0, n))
    return pl.pallas_call(
        functools.partial(_rec_kernel, nblk=nblk, tc=tc, gates_unroll=2, chunk=chunk, unroll=12),
        grid=(B, RNN_BLOCKS // nblk),
        in_specs=[seq, seq, col(CONV_W), col(1),
                  pl.BlockSpec((nblk, LANES, 4 * LANES), lambda b, n: (n, 0, 0)),
                  pl.BlockSpec((nblk, 1, 4 * LANES), lambda b, n: (n, 0, 0)),
                  col(2), st, st],
        out_specs=[seq, st, st],
        out_shape=[jax.ShapeDtypeStruct((B, T, D_RNN), BF16),
                   jax.ShapeDtypeStruct((B, 1, D_RNN), F32),
                   jax.ShapeDtypeStruct((B, 1, D_RNN), F32)],
        scratch_shapes=[pltpu.VMEM((nblk, T + 2 * SUBLANES, LANES), F32)]
        + [pltpu.VMEM((nblk, SUBLANES * chunk, LANES), F32)] * 8,
        compiler_params=_cparams("parallel", "parallel"),
        name=name,
    )(gate, xr, conv_w, conv_b.reshape(1, D_RNN), w_bd, b_bd, lam, h0f, h0b)


def _post_kernel(*refs, n_mix, final, fc):
    x_ref, mod_ref, g2_ref = refs[:3]
    mix = refs[3:3 + 2 * n_mix]
    w1_ref, w2_ref = refs[3 + 2 * n_mix:5 + 2 * n_mix]
    rest = refs[5 + 2 * n_mix:]
    if final:
        gf_ref, o_ref = rest
    else:
        (o_ref,) = rest
    mixed = _dot(mix[0][...], mix[1][...])
    for i in range(1, n_mix):
        mixed = mixed + _dot(mix[2 * i][...], mix[2 * i + 1][...])
    x1 = x_ref[...] + mod_ref[2:3, :] * mixed
    h = _rms(x1, g2_ref[...])
    hb = (h * (1.0 + mod_ref[4:5, :]) + mod_ref[3:4, :]).astype(BF16)
    acc = None
    for c in range(D_FF // fc):
        a = _dot(hb, w1_ref[:, c * fc:(c + 1) * fc])
        a = jnp.square(jnp.maximum(a, 0.0)).astype(BF16)
        part = _dot(a, w2_ref[c * fc:(c + 1) * fc, :])
        acc = part if acc is None else acc + part
    x2 = x1 + mod_ref[5:6, :] * acc
    if final:
        x2 = _rms(x2, gf_ref[...])
    o_ref[...] = x2


def _post(x, mod, g2, mixes, w1, w2, *, tm, ctx, final_g=None, name):
    B, T, D = x.shape
    row = (lambda b, i: (CTX_ROW, 0, 0)) if ctx else (lambda b, i: (b, 0, 0))
    const = lambda a: pl.BlockSpec(a.shape, lambda b, i: (0,) * a.ndim, pipeline_mode=pl.Buffered(1))
    in_specs = [
        pl.BlockSpec((None, tm, D), lambda b, i: (b, i, 0)),
        pl.BlockSpec((None, 6, D), row),
        pl.BlockSpec((1, D), lambda b, i: (0, 0)),
    ]
    args = [x, mod, g2.reshape(1, D)]
    for o, w in mixes:
        in_specs += [pl.BlockSpec((None, tm, o.shape[-1]), lambda b, i: (b, i, 0)), const(w)]
        args += [o, w]
    in_specs += [const(w1), const(w2)]
    args += [w1, w2]
    if final_g is not None:
        in_specs.append(pl.BlockSpec((1, D), lambda b, i: (0, 0)))
        args.append(final_g.reshape(1, D))
    return pl.pallas_call(
        functools.partial(_post_kernel, n_mix=len(mixes), final=final_g is not None, fc=1024),
        grid=(B, T // tm),
        in_specs=in_specs,
        out_specs=pl.BlockSpec((None, tm, D), lambda b, i: (b, i, 0)),
        out_shape=jax.ShapeDtypeStruct((B, T, D), F32),
        compiler_params=_cparams("parallel", "parallel"),
        name=name,
    )(*args)


def _rope_tables(n_tokens):
    rows = n_tokens // GRID_W
    r, cl = jnp.meshgrid(jnp.arange(rows, dtype=F32), jnp.arange(GRID_W, dtype=F32), indexing='ij')
    quarter = HEAD_DIM // 4
    inv = ROPE_BASE ** (-jnp.arange(quarter, dtype=F32) / quarter)
    ang = jnp.stack([r.reshape(-1)[:, None] * inv, cl.reshape(-1)[:, None] * inv], axis=1)
    cos, sin = jnp.cos(ang), jnp.sin(ang)
    cos64 = jnp.concatenate([cos[:, 0], cos[:, 0], cos[:, 1], cos[:, 1]], axis=-1)
    sin64 = jnp.concatenate([-sin[:, 0], sin[:, 0], -sin[:, 1], sin[:, 1]], axis=-1)
    return jnp.tile(cos64, (1, LANES // HEAD_DIM)), jnp.tile(sin64, (1, LANES // HEAD_DIM))


def _att_in_weights(w_in):
    d = w_in.shape[0]
    nq = A_HEADS * HEAD_DIM
    wq = w_in[:, :nq].reshape(d, A_HEADS, HEAD_DIM)
    z = jnp.zeros_like(wq)
    in_first = (jnp.arange(A_HEADS) // A_GROUP == 0)[None, :, None]
    wq = jnp.where(in_first, jnp.concatenate([wq, z], axis=-1), jnp.concatenate([z, wq], axis=-1))
    return jnp.concatenate([wq.reshape(d, QA_W), w_in[:, nq:]], axis=1).astype(BF16)


def _block_diag_weights(w_a, b_a, w_x, b_x):
    w = jnp.concatenate([w_a[0], w_x[0], w_a[1], w_x[1]], axis=-1)
    b = jnp.concatenate([v.reshape(RNN_BLOCKS, 1, RNN_BW) for v in (b_a[0], b_x[0], b_a[1], b_x[1])], axis=-1)
    return (0.5 * w).astype(BF16), 0.5 * b


def kernel(x_prompt, x_sample, cache_a_k, cache_a_v, cache_b_k, cache_b_v, state_fwd, state_bwd, c, c_ctx, norm1, norm2, w_ada, b_ada, w_mlp1, w_mlp2, att_w_in, att_w_out, att_sink, att_lam_qk, att_subln, rec_w_in, rec_conv_w, rec_conv_b, rec_w_a, rec_b_a, rec_w_x, rec_b_x, rec_lam, rec_w_out, final_norm):
    nb, n_seq, _ = x_prompt.shape
    nd, d_seq, _ = x_sample.shape
    past = cache_a_k.shape[2]
    assert nd <= CTX_ROW and DEPTH == 2
    cvec = jnp.concatenate([c, jnp.zeros((CTX_ROW - nd, D_MODEL), F32), c_ctx[None],
                            jnp.zeros((MOD_ROWS - CTX_ROW - 1, D_MODEL), F32)], axis=0)
    mod = _ada_mod(cvec, w_ada, b_ada)
    w1 = w_mlp1.astype(BF16)
    w2 = w_mlp2.astype(BF16)
    tm_ctx, tm_lat = n_seq, 512

    lam_init = 0.8 - 0.6 * math.exp(-0.3 * 0)
    w_in = _att_in_weights(att_w_in[0])
    w_out = att_w_out[0].astype(BF16)
    nqa = A_HEADS * HEAD_DIM
    c0 = [0, QA_W, QA_W + KA_W, QA_W + 2 * KA_W, QA_W + 2 * KA_W + B_W, QA_W + 2 * KA_W + 2 * B_W]
    widths = [QA_W, KA_W, KA_W, B_W, B_W, B_W]
    scales = [SCALE * LOG2E, 1.0, 1.0, SCALE * LOG2E, 1.0, 1.0]
    roped = [True, True, False, True, True, False]
    sink = att_sink[0]
    lam_qk = att_lam_qk[0]
    subln = att_subln[0].reshape(1, 2 * HEAD_DIM)

    segs_ctx = [(c0[i], widths[i], False, scales[i], False) for i in range(6)]
    qa, ka, va, qb, kb, vb = _proj(x_prompt, mod[0], norm1[0], w_in, segs_ctx, [BF16, F32, F32, BF16, F32, F32],
                                   tm=tm_ctx, ctx=True, name="proj_att_ctx")
    o_ctx = _ctx_attention(qa, ka, va, qb, kb, vb, sink, lam_qk, subln, lam_init)
    xp = _post(x_prompt, mod[0], norm2[0], [(o_ctx, w_out)], w1[0], w2[0], tm=tm_ctx, ctx=True, name="post_att_ctx")
    new_a_k = ka.reshape(nb, 1, n_seq, A_KV_HEADS, HEAD_DIM)
    new_a_v = va.reshape(nb, 1, n_seq, A_KV_HEADS, HEAD_DIM)
    new_b_k = kb.reshape(nb, 1, n_seq, B_HEADS, 2 * HEAD_DIM)
    new_b_v = vb.reshape(nb, 1, n_seq, B_HEADS, 2 * HEAD_DIM)

    segs_lat = [(c0[i], widths[i], roped[i], scales[i], i in (2, 5)) for i in range(6)]
    qa, ka, vat, qb, kb, vbt = _proj(x_sample, mod[0], norm1[0], w_in, segs_lat, [BF16] * 6, tm=tm_lat, ctx=False,
                                     rope_tabs=_rope_tables(d_seq), name="proj_att_lat")
    cka = cache_a_k[:, 0].reshape(nd, past, KA_W).astype(BF16)
    cvat = jnp.swapaxes(cache_a_v[:, 0].reshape(nd, past, KA_W), 1, 2).astype(BF16)
    ckb = cache_b_k[:, 0].reshape(nd, past, B_W).astype(BF16)
    cvbt = jnp.swapaxes(cache_b_v[:, 0].reshape(nd, past, B_W), 1, 2).astype(BF16)
    oa = _lat_a_attention(qa, ka, vat, cka, cvat, sink)
    ob = _lat_b_attention(qb, kb, vbt, ckb, cvbt, lam_qk, subln, lam_init)
    xs = _post(x_sample, mod[0], norm2[0], [(oa, w_out[:nqa]), (ob, w_out[nqa:])], w1[0], w2[0], tm=tm_lat,
               ctx=False, name="post_att_lat")

    w_rin = rec_w_in[0].astype(BF16)
    w_rout = rec_w_out[0].astype(BF16)
    w_bd, b_bd = _block_diag_weights(rec_w_a[0], rec_b_a[0], rec_w_x[0], rec_b_x[0])
    segs_rec = [(0, D_RNN, False, 1.0, False), (D_RNN, D_RNN, False, 1.0, False)]
    zeros = jnp.zeros((nb, 1, D_RNN), F32)

    gate, xr = _proj(xp, mod[1], norm1[1], w_rin, segs_rec, [F32, F32], tm=tm_ctx, ctx=True, name="proj_rec_ctx")
    y, sf, sb = _rec_mixer(gate, xr, zeros, zeros, rec_conv_w[0], rec_conv_b[0], w_bd, b_bd, rec_lam[0],
                           nblk=RNN_BLOCKS, tc=128, name="rec_mixer_ctx")
    y_prompt = _post(xp, mod[1], norm2[1], [(y, w_rout)], w1[1], w2[1], tm=tm_ctx, ctx=True, final_g=final_norm,
                     name="post_rec_ctx")

    gate, xr = _proj(xs, mod[1], norm1[1], w_rin, segs_rec, [F32, F32], tm=tm_lat, ctx=False, name="proj_rec_lat")
    y, _, _ = _rec_mixer(gate, xr, state_fwd[:, 0:1], state_bwd[:, 0:1], rec_conv_w[0], rec_conv_b[0], w_bd, b_bd,
                         rec_lam[0], nblk=1, tc=256, name="rec_mixer_lat")
    y_sample = _post(xs, mod[1], norm2[1], [(y, w_rout)], w1[1], w2[1], tm=tm_lat, ctx=False, final_g=final_norm,
                     name="post_rec_lat")

    return (y_prompt, y_sample, new_a_k, new_a_v, new_b_k, new_b_v, sf, sb)
```

```python
import functools
import math

import jax
import jax.numpy as jnp
import numpy as np
from jax import lax
from jax.experimental import pallas as pl
from jax.experimental.pallas import tpu as pltpu

F32 = jnp.float32
BF16 = jnp.bfloat16

LANES = 128
SUBLANES = 8
VMEM_LIMIT_BYTES = 56 * 1024 * 1024

D_MODEL = 1024
DEPTH = 2
GRID_W = 64
HEAD_DIM = 64
A_HEADS = 8
A_KV_HEADS = 2
A_GROUP = A_HEADS // A_KV_HEADS
B_HEADS = 4
WINDOW = 128
ROPE_BASE = 10000.0
D_RNN = 1280
RNN_BLOCKS = 10
RNN_BW = D_RNN // RNN_BLOCKS
CONV_W = 4
CONV_LEFT = (CONV_W - 1) // 2
RGLRU_C = 8.0
D_FF = 4 * D_MODEL
EPS = 1e-6
SCALE = HEAD_DIM ** -0.5
NEG = -1e30

QA_W = A_HEADS * LANES
KA_W = A_KV_HEADS * HEAD_DIM
B_W = B_HEADS * 2 * HEAD_DIM
MOD_ROWS = 8
CTX_ROW = 4
LOG2E = math.log2(math.e)
ONES_ROWS = 16


def _cparams(*semantics):
    return pltpu.CompilerParams(dimension_semantics=semantics, vmem_limit_bytes=VMEM_LIMIT_BYTES)


def _dot(a, b):
    return jnp.dot(a, b, preferred_element_type=F32)


def _dot_nt(a, b):
    return lax.dot_general(a, b, (((1,), (1,)), ((), ())), preferred_element_type=F32)


def _rms(x, g):
    return x * lax.rsqrt(jnp.mean(x * x, axis=-1, keepdims=True) + EPS) * g


def _ada_kernel(c_ref, w_ref, b_ref, o_ref):
    c = c_ref[...]
    s = c * jax.nn.sigmoid(c)
    o_ref[...] = _dot(s.astype(BF16), w_ref[...].astype(BF16)) + b_ref[...]


def _ada_mod(cvec, w_ada, b_ada):
    tn = 1536
    out = pl.pallas_call(
        _ada_kernel,
        grid=(DEPTH, 6 * D_MODEL // tn),
        in_specs=[
            pl.BlockSpec((MOD_ROWS, D_MODEL), lambda l, j: (0, 0)),
            pl.BlockSpec((None, D_MODEL, tn), lambda l, j: (l, 0, j)),
            pl.BlockSpec((None, 1, tn), lambda l, j: (l, 0, j)),
        ],
        out_specs=pl.BlockSpec((None, MOD_ROWS, tn), lambda l, j: (l, 0, j)),
        out_shape=jax.ShapeDtypeStruct((DEPTH, MOD_ROWS, 6 * D_MODEL), F32),
        compiler_params=_cparams("parallel", "parallel"),
        name="ada_mod",
    )(cvec, w_ada, b_ada.reshape(DEPTH, 1, 6 * D_MODEL))
    return out.reshape(DEPTH, MOD_ROWS, 6, D_MODEL)


def _proj_kernel(*refs, segs, rope):
    if rope:
        x_ref, mod_ref, g_ref, w_ref, cos_ref, sin_ref, *outs = refs
    else:
        x_ref, mod_ref, g_ref, w_ref, *outs = refs
    x = x_ref[...]
    h = _rms(x, g_ref[...])
    h = h * (1.0 + mod_ref[1:2, :]) + mod_ref[0:1, :]
    hb = h.astype(BF16)
    tm = x.shape[0]
    if rope:
        lane = lax.broadcasted_iota(jnp.int32, (tm, LANES), 1)
        first = (lane & 16) == 0
        cos = cos_ref[...]
        sin = sin_ref[...]
    for (c0, width, do_rope, scale, transposed), o_ref in zip(segs, outs):
        y = _dot(hb, w_ref[:, c0:c0 + width])
        for t in range(width // LANES):
            yt = y[:, t * LANES:(t + 1) * LANES]
            if do_rope:
                sw = jnp.where(first, pltpu.roll(yt, LANES - 16, 1), pltpu.roll(yt, 16, 1))
                yt = yt * cos + sw * sin
            if scale != 1.0:
                yt = yt * scale
            if transposed:
                o_ref[t * LANES:(t + 1) * LANES, :] = yt.T.astype(o_ref.dtype)
            else:
                o_ref[:, t * LANES:(t + 1) * LANES] = yt.astype(o_ref.dtype)


def _proj(x, mod, g, w, segs, out_dtypes, *, tm, ctx, rope_tabs=None, name):
    B, T, D = x.shape
    row = (lambda b, i: (CTX_ROW, 0, 0)) if ctx else (lambda b, i: (b, 0, 0))
    in_specs = [
        pl.BlockSpec((None, tm, D), lambda b, i: (b, i, 0)),
        pl.BlockSpec((None, 6, D), row),
        pl.BlockSpec((1, D), lambda b, i: (0, 0)),
        pl.BlockSpec(w.shape, lambda b, i: (0, 0)),
    ]
    args = [x, mod, g.reshape(1, D), w]
    if rope_tabs is not None:
        in_specs += [pl.BlockSpec((tm, LANES), lambda b, i: (i, 0))] * 2
        args += list(rope_tabs)
    out_specs = [pl.BlockSpec((None, s[1], tm), lambda b, i: (b, 0, i)) if s[4] else
                 pl.BlockSpec((None, tm, s[1]), lambda b, i: (b, i, 0)) for s in segs]
    out_shape = [jax.ShapeDtypeStruct((B, s[1], T) if s[4] else (B, T, s[1]), dt)
                 for s, dt in zip(segs, out_dtypes)]
    return pl.pallas_call(
        functools.partial(_proj_kernel, segs=tuple(segs), rope=rope_tabs is not None),
        grid=(B, T // tm),
        in_specs=in_specs,
        out_specs=out_specs,
        out_shape=out_shape,
        compiler_params=_cparams("parallel", "parallel"),
        name=name,
    )(*args)


def _diff_lambda(lq, lam_init):
    s1 = jnp.sum(lq[0:1, :] * lq[1:2, :], axis=1, keepdims=True)
    s2 = jnp.sum(lq[2:3, :] * lq[3:4, :], axis=1, keepdims=True)
    return jnp.exp(s1) - jnp.exp(s2) + lam_init


def _stack_group_queries(qa_ref, g):
    return jnp.concatenate(
        [qa_ref[:, (A_GROUP * g + hh) * LANES:(A_GROUP * g + hh + 1) * LANES] for hh in range(A_GROUP)], axis=0)


def _sink_column(sink_ref, g, tq):
    return jnp.concatenate(
        [jnp.full((tq, 1), sink_ref[A_GROUP * g + hh] * LOG2E, F32) for hh in range(A_GROUP)], axis=0)


def _store_a_heads(o_ref, heads, lo):
    for j in range(A_HEADS // 2):
        a, b = heads[2 * j], heads[2 * j + 1]
        if (2 * j) // A_GROUP == 0:
            tile = jnp.where(lo, a, pltpu.roll(b, HEAD_DIM, 1))
        else:
            tile = jnp.where(lo, pltpu.roll(a, HEAD_DIM, 1), b)
        o_ref[:, j * LANES:(j + 1) * LANES] = tile.astype(o_ref.dtype)


def _subln(o, subln, lam_init):
    return _rms(o, subln) * (1.0 - lam_init)


def _ctx_attn_kernel(sink_ref, qa_ref, ka_ref, va_ref, qb_ref, kb_ref, vb_ref, lamqk_ref, subln_ref, o_ref, *,
                     lam_init):
    T = qa_ref.shape[0]
    lane = lax.broadcasted_iota(jnp.int32, (T, LANES), 1)
    lo = lane < HEAD_DIM
    ka = ka_ref[...].astype(BF16)
    va = va_ref[...].astype(BF16)
    heads = []
    for g in range(A_KV_HEADS):
        qg = _stack_group_queries(qa_ref, g)
        s = _dot_nt(qg, ka)
        sink = _sink_column(sink_ref, g, T)
        m = jnp.maximum(jnp.max(s, axis=1, keepdims=True), sink)
        p = jnp.exp2(s - m)
        l = jnp.sum(p, axis=1, keepdims=True) + jnp.exp2(sink - m)
        o = _dot(p.astype(BF16), va) / l
        heads += [o[hh * T:(hh + 1) * T] for hh in range(A_GROUP)]
    _store_a_heads(o_ref, heads, lo)

    lam = _diff_lambda(lamqk_ref[...], lam_init)
    for h in range(B_HEADS):
        sl = slice(h * LANES, (h + 1) * LANES)
        q = qb_ref[:, sl]
        zero = jnp.zeros_like(q)
        qs = jnp.concatenate([jnp.where(lo, q, zero), jnp.where(lo, zero, q)], axis=0)
        s = _dot_nt(qs, kb_ref[:, sl].astype(BF16))
        e = jnp.exp2(s - jnp.max(s, axis=1, keepdims=True))
        pn = e / jnp.sum(e, axis=1, keepdims=True)
        w = pn[:T] - lam * pn[T:]
        o = _dot(w.astype(BF16), vb_ref[:, sl].astype(BF16))
        o_ref[:, A_HEADS * HEAD_DIM + h * LANES:A_HEADS * HEAD_DIM + (h + 1) * LANES] = (
            _subln(o, subln_ref[...], lam_init).astype(o_ref.dtype))


def _ctx_attention(qa, ka, va, qb, kb, vb, sink, lam_qk, subln, lam_init):
    B, T, _ = qa.shape
    blk = lambda w: pl.BlockSpec((None, T, w), lambda b: (b, 0, 0))
    full = lambda a: pl.BlockSpec(a.shape, lambda b: (0,) * a.ndim)
    return pl.pallas_call(
        functools.partial(_ctx_attn_kernel, lam_init=lam_init),
        grid=(B,),
        in_specs=[pl.BlockSpec(memory_space=pltpu.SMEM), blk(QA_W), blk(KA_W), blk(KA_W), blk(B_W), blk(B_W),
                  blk(B_W), full(lam_qk), full(subln)],
        out_specs=blk(A_HEADS * HEAD_DIM + B_W),
        out_shape=jax.ShapeDtypeStruct((B, T, A_HEADS * HEAD_DIM + B_W), BF16),
        compiler_params=_cparams("parallel"),
        name="ctx_attention",
    )(sink, qa, ka, va, qb, kb, vb, lam_qk, subln)


def _lat_a_kernel(sink_ref, qa_ref, k_ref, vt_ref, ck_ref, cvt_ref, o_ref, *, tq, band):
    T = k_ref.shape[0]
    qi = pl.program_id(1)
    start = jnp.clip(qi * tq - WINDOW, 0, T - band)
    start = pl.multiple_of(start, WINDOW)
    kb = k_ref[pl.ds(start, band), :]
    vbt = vt_ref[:, pl.ds(start, band)]
    cols = A_GROUP * tq
    kpos = start + lax.broadcasted_iota(jnp.int32, (band, cols), 0)
    qpos = qi * tq + lax.broadcasted_iota(jnp.int32, (band, cols), 1) % tq
    keep = jnp.abs(qpos - kpos) <= WINDOW
    cvt1 = jnp.concatenate([cvt_ref[...], jnp.ones((ONES_ROWS, cvt_ref.shape[1]), BF16)], axis=0)
    vbt1 = jnp.concatenate([vbt, jnp.ones((ONES_ROWS, band), BF16)], axis=0)
    heads = []
    scores = []
    for g in range(A_KV_HEADS):
        qg = _stack_group_queries(qa_ref, g)
        scores.append((_dot_nt(ck_ref[...], qg), _dot_nt(kb, qg)))
    for g in range(A_KV_HEADS):
        s_c = scores[g][0]
        s_b = jnp.where(keep, scores[g][1], NEG)
        sink = jnp.concatenate(
            [jnp.full((1, tq), sink_ref[A_GROUP * g + hh] * LOG2E, F32) for hh in range(A_GROUP)], axis=1)
        m = jnp.maximum(jnp.maximum(jnp.max(s_c, axis=0, keepdims=True), jnp.max(s_b, axis=0, keepdims=True)), sink)
        p_c = jnp.exp2(s_c - m).astype(BF16)
        p_b = jnp.exp2(s_b - m).astype(BF16)
        ot = _dot(cvt1, p_c) + _dot(vbt1, p_b)
        ot = ot[:LANES] / (ot[LANES:LANES + 1] + jnp.exp2(sink - m))
        heads += [ot[g * HEAD_DIM:(g + 1) * HEAD_DIM, hh * tq:(hh + 1) * tq] for hh in range(A_GROUP)]
    o_ref[...] = jnp.concatenate(heads, axis=0).T.astype(o_ref.dtype)


def _lat_a_attention(qa, ka, vat, cka, cvat, sink):
    B, T, _ = qa.shape
    L = cka.shape[1]
    tq = WINDOW
    band = 3 * WINDOW
    return pl.pallas_call(
        functools.partial(_lat_a_kernel, tq=tq, band=band),
        grid=(B, T // tq),
        in_specs=[
            pl.BlockSpec(memory_space=pltpu.SMEM),
            pl.BlockSpec((None, tq, QA_W), lambda b, i: (b, i, 0)),
            pl.BlockSpec((None, T, KA_W), lambda b, i: (b, 0, 0)),
            pl.BlockSpec((None, KA_W, T), lambda b, i: (b, 0, 0)),
            pl.BlockSpec((None, L, KA_W), lambda b, i: (b, 0, 0)),
            pl.BlockSpec((None, KA_W, L), lambda b, i: (b, 0, 0)),
        ],
        out_specs=pl.BlockSpec((None, tq, A_HEADS * HEAD_DIM), lambda b, i: (b, i, 0)),
        out_shape=jax.ShapeDtypeStruct((B, T, A_HEADS * HEAD_DIM), BF16),
        compiler_params=_cparams("parallel", "parallel"),
        name="lat_a_attention",
    )(sink, qa, ka, vat, cka, cvat)


def _lat_b_kernel(q_ref, k_ref, vt_ref, ck_ref, cvt_ref, lamqk_ref, subln_ref, o_ref, *, tk, lam_init):
    tq = q_ref.shape[0]
    T = k_ref.shape[0]
    lo = lax.broadcasted_iota(jnp.int32, (tq, LANES), 1) < HEAD_DIM
    q = q_ref[...]
    zero = jnp.zeros_like(q)
    qs = jnp.concatenate([jnp.where(lo, q, zero), jnp.where(lo, zero, q)], axis=0)

    blocks = [(ck_ref[...], cvt_ref[...])]
    blocks += [(k_ref[j * tk:(j + 1) * tk, :], vt_ref[:, j * tk:(j + 1) * tk]) for j in range(T // tk)]

    def scores(k):
        return _dot_nt(k, qs)

    def accumulate(s, vt, carry):
        m, acc = carry
        m_new = jnp.maximum(m, jnp.max(s, axis=0, keepdims=True))
        p = jnp.exp2(s - m_new).astype(BF16)
        vt1 = jnp.concatenate([vt, jnp.ones((ONES_ROWS, vt.shape[1]), BF16)], axis=0)
        acc = jnp.exp2(m - m_new) * acc + _dot(vt1, p)
        return m_new, acc

    carry = (jnp.full((1, 2 * tq), -jnp.inf, F32), jnp.zeros((LANES + ONES_ROWS, 2 * tq), F32))
    s = scores(blocks[0][0])
    for j in range(len(blocks)):
        s_next = scores(blocks[j + 1][0]) if j + 1 < len(blocks) else None
        carry = accumulate(s, blocks[j][1], carry)
        s = s_next
    _, acc = carry
    ot = acc[:LANES] / acc[LANES:LANES + 1]
    lam = _diff_lambda(lamqk_ref[...], lam_init)
    out = (ot[:, :tq] - lam * ot[:, tq:]).T
    o_ref[...] = _subln(out, subln_ref[...], lam_init).astype(o_ref.dtype)


def _lat_b_attention(qb, kb, vbt, ckb, cvbt, lam_qk, subln, lam_init):
    B, T, _ = qb.shape
    L = ckb.shape[1]
    tq = 512
    full = lambda a: pl.BlockSpec(a.shape, lambda b, h, i: (0,) * a.ndim)
    return pl.pallas_call(
        functools.partial(_lat_b_kernel, tk=512, lam_init=lam_init),
        grid=(B, B_HEADS, T // tq),
        in_specs=[
            pl.BlockSpec((None, tq, LANES), lambda b, h, i: (b, i, h)),
            pl.BlockSpec((None, T, LANES), lambda b, h, i: (b, 0, h)),
            pl.BlockSpec((None, LANES, T), lambda b, h, i: (b, h, 0)),
            pl.BlockSpec((None, L, LANES), lambda b, h, i: (b, 0, h)),
            pl.BlockSpec((None, LANES, L), lambda b, h, i: (b, h, 0)),
            full(lam_qk), full(subln),
        ],
        out_specs=pl.BlockSpec((None, tq, LANES), lambda b, h, i: (b, i, h)),
        out_shape=jax.ShapeDtypeStruct((B, T, B_W), BF16),
        compiler_params=_cparams("parallel", "parallel", "parallel"),
        name="lat_b_attention",
    )(qb, kb, vbt, ckb, cvbt, lam_qk, subln)


def _scan_chunk_len(n_steps):
    chunk = -(-n_steps // SUBLANES)
    while chunk % 8 != 4:
        chunk += 1
    return chunk


def _rec_kernel(gate_ref, xr_ref, cw_ref, cb_ref, wbd_ref, bbd_ref, lam_ref, h0f_ref, h0b_ref,
                y_ref, sf_ref, sb_ref, xp, af, uf, ab, ub, pf_s, hf_s, pb_s, hb_s, *, nblk, **kw):
    for n in range(nblk):
        sl = slice(n * LANES, (n + 1) * LANES)
        _rec_block(gate_ref.at[:, sl], xr_ref.at[:, sl], cw_ref.at[:, sl], cb_ref.at[:, sl], wbd_ref.at[n],
                   bbd_ref.at[n], lam_ref.at[:, sl], h0f_ref.at[:, sl], h0b_ref.at[:, sl], y_ref.at[:, sl],
                   sf_ref.at[:, sl], sb_ref.at[:, sl], xp.at[n], af.at[n], uf.at[n], ab.at[n], ub.at[n],
                   pf_s.at[n], hf_s.at[n], pb_s.at[n], hb_s.at[n], **kw)


def _rec_block(gate_ref, xr_ref, cw_ref, cb_ref, wbd_ref, bbd_ref, lam_ref, h0f_ref, h0b_ref,
               y_ref, sf_ref, sb_ref, xp, af, uf, ab, ub, pf_s, hf_s, pb_s, hb_s, *, tc, gates_unroll, chunk, unroll):
    T = xr_ref.shape[0]
    pad = SUBLANES
    xp[0:pad, :] = jnp.zeros((pad, LANES), F32)
    xp[T + pad:T + 2 * pad, :] = jnp.zeros((pad, LANES), F32)
    xp[pad:T + pad, :] = xr_ref[...]
    tail = SUBLANES * chunk - T
    for a_s, u_s in ((af, uf), (ab, ub)):
        a_s[T:T + tail, :] = jnp.ones((tail, LANES), F32)
        u_s[T:T + tail, :] = jnp.zeros((tail, LANES), F32)
    cw = cw_ref[...]
    cb = cb_ref[...]
    nl = -lam_ref[...]
    softplus = jnp.maximum(nl, 0.0) + jnp.log1p(jnp.exp(-jnp.abs(nl)))
    cp = (0.5 * RGLRU_C) * softplus

    def gates(it, _):
        for q in range(gates_unroll):
            gate_chunk(it * gates_unroll + q)
        return 0

    def gate_chunk(ci):
        t0 = pl.multiple_of(ci * tc, tc)
        y = cb
        for j in range(CONV_W):
            y = y + xp[pl.ds(t0 + (pad - CONV_LEFT + j), tc), :] * cw[j:j + 1, :]
        t = jnp.tanh(_dot(y.astype(BF16), wbd_ref[...]) + bbd_ref[...])
        yh = 0.5 * y
        for d, (a_s, u_s) in enumerate(((af, uf), (ab, ub))):
            t_r = t[:, 2 * d * LANES:(2 * d + 1) * LANES]
            t_i = t[:, (2 * d + 1) * LANES:(2 * d + 2) * LANES]
            neg_log_a = cp[d:d + 1, :] * t_r + cp[d:d + 1, :]
            a = jnp.exp2(neg_log_a * (-LOG2E))
            w = jnp.tanh(neg_log_a) * (a * a + 1.0)
            sqrt_w = jnp.where(w > 0.0, w * lax.rsqrt(w), 0.0)
            a_s[pl.ds(t0, tc), :] = a
            u_s[pl.ds(t0, tc), :] = (t_i + 1.0) * (yh * sqrt_w)

    lax.fori_loop(0, T // (tc * gates_unroll), gates, 0)

    def rows(k):
        return pl.ds(k, SUBLANES, stride=chunk)

    def local_scan(it, carry):
        hf, pf, hb, pb = carry
        for q in range(unroll):
            k = it * unroll + q
            a = af[rows(k), :]
            hf = a * hf + uf[rows(k), :]
            pf = a * pf
            hf_s[rows(k), :] = hf
            pf_s[rows(k), :] = pf
            kb = chunk - 1 - k
            a = ab[rows(kb), :]
            hb = a * hb + ub[rows(kb), :]
            pb = a * pb
            hb_s[rows(kb), :] = hb
            pb_s[rows(kb), :] = pb
        return hf, pf, hb, pb

    zero = jnp.zeros((SUBLANES, LANES), F32)
    one = jnp.ones((SUBLANES, LANES), F32)
    hf, pf, hb, pb = lax.fori_loop(0, chunk // unroll, local_scan, (zero, one, zero, one))

    cf = [h0f_ref[...]]
    for r in range(SUBLANES - 1):
        cf.append(hf[r:r + 1, :] + pf[r:r + 1, :] * cf[r])
    cb_rev = [h0b_ref[...]]
    for r in range(SUBLANES - 1, 0, -1):
        cb_rev.append(hb[r:r + 1, :] + pb[r:r + 1, :] * cb_rev[-1])
    carry_f = jnp.concatenate(cf, axis=0)
    carry_b = jnp.concatenate(cb_rev[::-1], axis=0)

    def apply_carry(it, _):
        for q in range(unroll):
            k = it * unroll + q
            uf[rows(k), :] = hf_s[rows(k), :] + pf_s[rows(k), :] * carry_f
            ub[rows(k), :] = hb_s[rows(k), :] + pb_s[rows(k), :] * carry_b
        return 0

    lax.fori_loop(0, chunk // unroll, apply_carry, 0)
    sf_ref[...] = uf[T - 1:T, :]
    sb_ref[...] = ub[0:1, :]

    k_gelu = math.sqrt(2.0 / math.pi)

    def combine(ci, _):
        t0 = pl.multiple_of(ci * tc, tc)
        g = gate_ref[pl.ds(t0, tc), :]
        gelu = g * (0.5 * (1.0 + jnp.tanh(k_gelu * (g + 0.044715 * (g * g * g)))))
        y_ref[pl.ds(t0, tc), :] = ((uf[pl.ds(t0, tc), :] + ub[pl.ds(t0, tc), :]) * gelu).astype(y_ref.dtype)
        return 0

    lax.fori_loop(0, T // tc, combine, 0)


def _rec_mixer(gate, xr, h0f, h0b, conv_w, conv_b, w_bd, b_bd, lam, *, nblk, tc, name):
    B, T, _ = xr.shape
    chunk = _scan_chunk_len(T)
    wid = nblk * LANES
    col = lambda rows: pl.BlockSpec((rows, wid), lambda b, n: (0, n))
    seq = pl.BlockSpec((None, T, wid), lambda b, n: (b, 0, n))
    st = pl.BlockSpec((None, 1, wid), lambda b, n: (b, 0, n))
    return pl.pallas_call(
        functools.partial(_rec_kernel, nblk=nblk, tc=tc, gates_unroll=2, chunk=chunk, unroll=12),
        grid=(B, RNN_BLOCKS // nblk),
        in_specs=[seq, seq, col(CONV_W), col(1),
                  pl.BlockSpec((nblk, LANES, 4 * LANES), lambda b, n: (n, 0, 0)),
                  pl.BlockSpec((nblk, 1, 4 * LANES), lambda b, n: (n, 0, 0)),
                  col(2), st, st],
        out_specs=[seq, st, st],
        out_shape=[jax.ShapeDtypeStruct((B, T, D_RNN), BF16),
                   jax.ShapeDtypeStruct((B, 1, D_RNN), F32),
                   jax.ShapeDtypeStruct((B, 1, D_RNN), F32)],
        scratch_shapes=[pltpu.VMEM((nblk, T + 2 * SUBLANES, LANES), F32)]
        + [pltpu.VMEM((nblk, SUBLANES * chunk, LANES), F32)] * 8,
        compiler_params=_cparams("parallel", "parallel"),
        name=name,
    )(gate, xr, conv_w, conv_b.reshape(1, D_RNN), w_bd, b_bd, lam, h0f, h0b)


def _post_kernel(*refs, n_mix, final, fc):
    x_ref, mod_ref, g2_ref = refs[:3]
    mix = refs[3:3 + 2 * n_mix]
    w1_ref, w2_ref = refs[3 + 2 * n_mix:5 + 2 * n_mix]
    rest = refs[5 + 2 * n_mix:]
    if final:
        gf_ref, o_ref = rest
    else:
        (o_ref,) = rest
    mixed = _dot(mix[0][...], mix[1][...])
    for i in range(1, n_mix):
        mixed = mixed + _dot(mix[2 * i][...], mix[2 * i + 1][...])
    x1 = x_ref[...] + mod_ref[2:3, :] * mixed
    h = _rms(x1, g2_ref[...])
    hb = (h * (1.0 + mod_ref[4:5, :]) + mod_ref[3:4, :]).astype(BF16)
    n_chunks = D_FF // fc
    up = lambda c: _dot(hb, w1_ref[:, c * fc:(c + 1) * fc])
    acc = None
    a = up(0)
    for c in range(n_chunks):
        a_next = up(c + 1) if c + 1 < n_chunks else None
        part = _dot(jnp.square(jnp.maximum(a, 0.0)).astype(BF16), w2_ref[c * fc:(c + 1) * fc, :])
        acc = part if acc is None else acc + part
        a = a_next
    x2 = x1 + mod_ref[5:6, :] * acc
    if final:
        x2 = _rms(x2, gf_ref[...])
    o_ref[...] = x2


def _post(x, mod, g2, mixes, w1, w2, *, tm, ctx, final_g=None, name):
    B, T, D = x.shape
    row = (lambda b, i: (CTX_ROW, 0, 0)) if ctx else (lambda b, i: (b, 0, 0))
    const = lambda a: pl.BlockSpec(a.shape, lambda b, i: (0,) * a.ndim, pipeline_mode=pl.Buffered(1))
    in_specs = [
        pl.BlockSpec((None, tm, D), lambda b, i: (b, i, 0)),
        pl.BlockSpec((None, 6, D), row),
        pl.BlockSpec((1, D), lambda b, i: (0, 0)),
    ]
    args = [x, mod, g2.reshape(1, D)]
    for o, w in mixes:
        in_specs += [pl.BlockSpec((None, tm, o.shape[-1]), lambda b, i: (b, i, 0)), const(w)]
        args += [o, w]
    in_specs += [const(w1), const(w2)]
    args += [w1, w2]
    if final_g is not None:
        in_specs.append(pl.BlockSpec((1, D), lambda b, i: (0, 0)))
        args.append(final_g.reshape(1, D))
    return pl.pallas_call(
        functools.partial(_post_kernel, n_mix=len(mixes), final=final_g is not None, fc=1024),
        grid=(B, T // tm),
        in_specs=in_specs,
        out_specs=pl.BlockSpec((None, tm, D), lambda b, i: (b, i, 0)),
        out_shape=jax.ShapeDtypeStruct((B, T, D), F32),
        compiler_params=_cparams("parallel", "parallel"),
        name=name,
    )(*args)


def _rope_tables(n_tokens):
    rows = n_tokens // GRID_W
    r, cl = jnp.meshgrid(jnp.arange(rows, dtype=F32), jnp.arange(GRID_W, dtype=F32), indexing='ij')
    quarter = HEAD_DIM // 4
    inv = ROPE_BASE ** (-jnp.arange(quarter, dtype=F32) / quarter)
    ang = jnp.stack([r.reshape(-1)[:, None] * inv, cl.reshape(-1)[:, None] * inv], axis=1)
    cos, sin = jnp.cos(ang), jnp.sin(ang)
    cos64 = jnp.concatenate([cos[:, 0], cos[:, 0], cos[:, 1], cos[:, 1]], axis=-1)
    sin64 = jnp.concatenate([-sin[:, 0], sin[:, 0], -sin[:, 1], sin[:, 1]], axis=-1)
    return jnp.tile(cos64, (1, LANES // HEAD_DIM)), jnp.tile(sin64, (1, LANES // HEAD_DIM))


def _att_in_weights(w_in):
    d = w_in.shape[0]
    nq = A_HEADS * HEAD_DIM
    wq = w_in[:, :nq].reshape(d, A_HEADS, HEAD_DIM)
    z = jnp.zeros_like(wq)
    in_first = (jnp.arange(A_HEADS) // A_GROUP == 0)[None, :, None]
    wq = jnp.where(in_first, jnp.concatenate([wq, z], axis=-1), jnp.concatenate([z, wq], axis=-1))
    return jnp.concatenate([wq.reshape(d, QA_W), w_in[:, nq:]], axis=1).astype(BF16)


def _block_diag_weights(w_a, b_a, w_x, b_x):
    w = jnp.concatenate([w_a[0], w_x[0], w_a[1], w_x[1]], axis=-1)
    b = jnp.concatenate([v.reshape(RNN_BLOCKS, 1, RNN_BW) for v in (b_a[0], b_x[0], b_a[1], b_x[1])], axis=-1)
    return (0.5 * w).astype(BF16), 0.5 * b


def kernel(x_prompt, x_sample, cache_a_k, cache_a_v, cache_b_k, cache_b_v, state_fwd, state_bwd, c, c_ctx, norm1, norm2, w_ada, b_ada, w_mlp1, w_mlp2, att_w_in, att_w_out, att_sink, att_lam_qk, att_subln, rec_w_in, rec_conv_w, rec_conv_b, rec_w_a, rec_b_a, rec_w_x, rec_b_x, rec_lam, rec_w_out, final_norm):
    nb, n_seq, _ = x_prompt.shape
    nd, d_seq, _ = x_sample.shape
    past = cache_a_k.shape[2]
    assert nd <= CTX_ROW and DEPTH == 2
    cvec = jnp.concatenate([c, jnp.zeros((CTX_ROW - nd, D_MODEL), F32), c_ctx[None],
                            jnp.zeros((MOD_ROWS - CTX_ROW - 1, D_MODEL), F32)], axis=0)
    mod = _ada_mod(cvec, w_ada, b_ada)
    w1 = w_mlp1.astype(BF16)
    w2 = w_mlp2.astype(BF16)
    tm_ctx, tm_lat = n_seq, 512

    lam_init = 0.8 - 0.6 * math.exp(-0.3 * 0)
    w_in = _att_in_weights(att_w_in[0])
    w_out = att_w_out[0].astype(BF16)
    nqa = A_HEADS * HEAD_DIM
    c0 = [0, QA_W, QA_W + KA_W, QA_W + 2 * KA_W, QA_W + 2 * KA_W + B_W, QA_W + 2 * KA_W + 2 * B_W]
    widths = [QA_W, KA_W, KA_W, B_W, B_W, B_W]
    scales = [SCALE * LOG2E, 1.0, 1.0, SCALE * LOG2E, 1.0, 1.0]
    roped = [True, True, False, True, True, False]
    sink = att_sink[0]
    lam_qk = att_lam_qk[0]
    subln = att_subln[0].reshape(1, 2 * HEAD_DIM)

    segs_ctx = [(c0[i], widths[i], False, scales[i], False) for i in range(6)]
    qa, ka, va, qb, kb, vb = _proj(x_prompt, mod[0], norm1[0], w_in, segs_ctx, [BF16, F32, F32, BF16, F32, F32],
                                   tm=tm_ctx, ctx=True, name="proj_att_ctx")
    o_ctx = _ctx_attention(qa, ka, va, qb, kb, vb, sink, lam_qk, subln, lam_init)
    xp = _post(x_prompt, mod[0], norm2[0], [(o_ctx, w_out)], w1[0], w2[0], tm=tm_ctx, ctx=True, name="post_att_ctx")
    new_a_k = ka.reshape(nb, 1, n_seq, A_KV_HEADS, HEAD_DIM)
    new_a_v = va.reshape(nb, 1, n_seq, A_KV_HEADS, HEAD_DIM)
    new_b_k = kb.reshape(nb, 1, n_seq, B_HEADS, 2 * HEAD_DIM)
    new_b_v = vb.reshape(nb, 1, n_seq, B_HEADS, 2 * HEAD_DIM)

    segs_lat = [(c0[i], widths[i], roped[i], scales[i], i in (2, 5)) for i in range(6)]
    qa, ka, vat, qb, kb, vbt = _proj(x_sample, mod[0], norm1[0], w_in, segs_lat, [BF16] * 6, tm=tm_lat, ctx=False,
                                     rope_tabs=_rope_tables(d_seq), name="proj_att_lat")
    cka = cache_a_k[:, 0].reshape(nd, past, KA_W).astype(BF16)
    cvat = jnp.swapaxes(cache_a_v[:, 0].reshape(nd, past, KA_W), 1, 2).astype(BF16)
    ckb = cache_b_k[:, 0].reshape(nd, past, B_W).astype(BF16)
    cvbt = jnp.swapaxes(cache_b_v[:, 0].reshape(nd, past, B_W), 1, 2).astype(BF16)
    oa = _lat_a_attention(qa, ka, vat, cka, cvat, sink)
    ob = _lat_b_attention(qb, kb, vbt, ckb, cvbt, lam_qk, subln, lam_init)
    xs = _post(x_sample, mod[0], norm2[0], [(oa, w_out[:nqa]), (ob, w_out[nqa:])], w1[0], w2[0], tm=tm_lat,
               ctx=False, name="post_att_lat")

    w_rin = rec_w_in[0].astype(BF16)
    w_rout = rec_w_out[0].astype(BF16)
    w_bd, b_bd = _block_diag_weights(rec_w_a[0], rec_b_a[0], rec_w_x[0], rec_b_x[0])
    segs_rec = [(0, D_RNN, False, 1.0, False), (D_RNN, D_RNN, False, 1.0, False)]
    zeros = jnp.zeros((nb, 1, D_RNN), F32)

    gate, xr = _proj(xp, mod[1], norm1[1], w_rin, segs_rec, [F32, F32], tm=tm_ctx, ctx=True, name="proj_rec_ctx")
    y, sf, sb = _rec_mixer(gate, xr, zeros, zeros, rec_conv_w[0], rec_conv_b[0], w_bd, b_bd, rec_lam[0],
                           nblk=RNN_BLOCKS, tc=128, name="rec_mixer_ctx")
    y_prompt = _post(xp, mod[1], norm2[1], [(y, w_rout)], w1[1], w2[1], tm=tm_ctx, ctx=True, final_g=final_norm,
                     name="post_rec_ctx")

    gate, xr = _proj(xs, mod[1], norm1[1], w_rin, segs_rec, [F32, F32], tm=tm_lat, ctx=False, name="proj_rec_lat")
    y, _, _ = _rec_mixer(gate, xr, state_fwd[:, 0:1], state_bwd[:, 0:1], rec_conv_w[0], rec_conv_b[0], w_bd, b_bd,
                         rec_lam[0], nblk=1, tc=256, name="rec_mixer_lat")
    y_sample = _post(xs, mod[1], norm2[1], [(y, w_rout)], w1[1], w2[1], tm=tm_lat, ctx=False, final_g=final_norm,
                     name="post_rec_lat")

    return (y_prompt, y_sample, new_a_k, new_a_v, new_b_k, new_b_v, sf, sb)
```

```python
import functools
import math

import jax
import jax.numpy as jnp
import numpy as np
from jax import lax
from jax.experimental import pallas as pl
from jax.experimental.pallas import tpu as pltpu

F32 = jnp.float32
BF16 = jnp.bfloat16

LANES = 128
SUBLANES = 8
VMEM_LIMIT_BYTES = 56 * 1024 * 1024

D_MODEL = 1024
DEPTH = 2
GRID_W = 64
HEAD_DIM = 64
A_HEADS = 8
A_KV_HEADS = 2
A_GROUP = A_HEADS // A_KV_HEADS
B_HEADS = 4
WINDOW = 128
ROPE_BASE = 10000.0
D_RNN = 1280
RNN_BLOCKS = 10
RNN_BW = D_RNN // RNN_BLOCKS
CONV_W = 4
CONV_LEFT = (CONV_W - 1) // 2
RGLRU_C = 8.0
D_FF = 4 * D_MODEL
EPS = 1e-6
SCALE = HEAD_DIM ** -0.5
NEG = -1e30

QA_W = A_HEADS * LANES
KA_W = A_KV_HEADS * HEAD_DIM
B_W = B_HEADS * 2 * HEAD_DIM
MOD_ROWS = 8
CTX_ROW = 4
LOG2E = math.log2(math.e)
ONES_ROWS = 16


def _cparams(*semantics):
    return pltpu.CompilerParams(dimension_semantics=semantics, vmem_limit_bytes=VMEM_LIMIT_BYTES)


def _dot(a, b):
    return jnp.dot(a, b, preferred_element_type=F32)


def _dot_nt(a, b):
    return lax.dot_general(a, b, (((1,), (1,)), ((), ())), preferred_element_type=F32)


def _rms(x, g):
    return x * lax.rsqrt(jnp.mean(x * x, axis=-1, keepdims=True) + EPS) * g


def _ada_kernel(c_ref, w_ref, b_ref, o_ref):
    c = c_ref[...]
    s = c * jax.nn.sigmoid(c)
    o_ref[...] = _dot(s.astype(BF16), w_ref[...].astype(BF16)) + b_ref[...]


def _ada_mod(cvec, w_ada, b_ada):
    tn = 1536
    out = pl.pallas_call(
        _ada_kernel,
        grid=(DEPTH, 6 * D_MODEL // tn),
        in_specs=[
            pl.BlockSpec((MOD_ROWS, D_MODEL), lambda l, j: (0, 0)),
            pl.BlockSpec((None, D_MODEL, tn), lambda l, j: (l, 0, j)),
            pl.BlockSpec((None, 1, tn), lambda l, j: (l, 0, j)),
        ],
        out_specs=pl.BlockSpec((None, MOD_ROWS, tn), lambda l, j: (l, 0, j)),
        out_shape=jax.ShapeDtypeStruct((DEPTH, MOD_ROWS, 6 * D_MODEL), F32),
        compiler_params=_cparams("parallel", "parallel"),
        name="ada_mod",
    )(cvec, w_ada, b_ada.reshape(DEPTH, 1, 6 * D_MODEL))
    return out.reshape(DEPTH, MOD_ROWS, 6, D_MODEL)


def _proj_kernel(*refs, segs, rope):
    if rope:
        x_ref, mod_ref, g_ref, w_ref, cos_ref, sin_ref, *outs = refs
    else:
        x_ref, mod_ref, g_ref, w_ref, *outs = refs
    x = x_ref[...]
    h = _rms(x, g_ref[...])
    h = h * (1.0 + mod_ref[1:2, :]) + mod_ref[0:1, :]
    hb = h.astype(BF16)
    tm = x.shape[0]
    if rope:
        lane = lax.broadcasted_iota(jnp.int32, (tm, LANES), 1)
        first = (lane & 16) == 0
        cos = cos_ref[...]
        sin = sin_ref[...]
    for (c0, width, do_rope, scale, transposed), o_ref in zip(segs, outs):
        y = _dot(hb, w_ref[:, c0:c0 + width])
        for t in range(width // LANES):
            yt = y[:, t * LANES:(t + 1) * LANES]
            if do_rope:
                sw = jnp.where(first, pltpu.roll(yt, LANES - 16, 1), pltpu.roll(yt, 16, 1))
                yt = yt * cos + sw * sin
            if scale != 1.0:
                yt = yt * scale
            if transposed:
                o_ref[t * LANES:(t + 1) * LANES, :] = yt.T.astype(o_ref.dtype)
            else:
                o_ref[:, t * LANES:(t + 1) * LANES] = yt.astype(o_ref.dtype)


def _proj(x, mod, g, w, segs, out_dtypes, *, tm, ctx, rope_tabs=None, name):
    B, T, D = x.shape
    row = (lambda b, i: (CTX_ROW, 0, 0)) if ctx else (lambda b, i: (b, 0, 0))
    in_specs = [
        pl.BlockSpec((None, tm, D), lambda b, i: (b, i, 0)),
        pl.BlockSpec((None, 6, D), row),
        pl.BlockSpec((1, D), lambda b, i: (0, 0)),
        pl.BlockSpec(w.shape, lambda b, i: (0, 0)),
    ]
    args = [x, mod, g.reshape(1, D), w]
    if rope_tabs is not None:
        in_specs += [pl.BlockSpec((tm, LANES), lambda b, i: (i, 0))] * 2
        args += list(rope_tabs)
    out_specs = [pl.BlockSpec((None, s[1], tm), lambda b, i: (b, 0, i)) if s[4] else
                 pl.BlockSpec((None, tm, s[1]), lambda b, i: (b, i, 0)) for s in segs]
    out_shape = [jax.ShapeDtypeStruct((B, s[1], T) if s[4] else (B, T, s[1]), dt)
                 for s, dt in zip(segs, out_dtypes)]
    return pl.pallas_call(
        functools.partial(_proj_kernel, segs=tuple(segs), rope=rope_tabs is not None),
        grid=(B, T // tm),
        in_specs=in_specs,
        out_specs=out_specs,
        out_shape=out_shape,
        compiler_params=_cparams("parallel", "parallel"),
        name=name,
    )(*args)


def _diff_lambda(lq, lam_init):
    s1 = jnp.sum(lq[0:1, :] * lq[1:2, :], axis=1, keepdims=True)
    s2 = jnp.sum(lq[2:3, :] * lq[3:4, :], axis=1, keepdims=True)
    return jnp.exp(s1) - jnp.exp(s2) + lam_init


def _stack_group_queries(qa_ref, g, rows):
    return jnp.concatenate(
        [qa_ref[rows, (A_GROUP * g + hh) * LANES:(A_GROUP * g + hh + 1) * LANES] for hh in range(A_GROUP)], axis=0)


def _stack_pair_queries(q):
    lo = lax.broadcasted_iota(jnp.int32, q.shape, 1) < HEAD_DIM
    zero = jnp.zeros_like(q)
    return jnp.concatenate([jnp.where(lo, q, zero), jnp.where(lo, zero, q)], axis=0)


def _with_ones_rows(vt):
    return jnp.concatenate([vt, jnp.ones((ONES_ROWS, vt.shape[1]), BF16)], axis=0)


def _sink_row(sink_ref, g, tq):
    return jnp.concatenate(
        [jnp.full((1, tq), sink_ref[A_GROUP * g + hh] * LOG2E, F32) for hh in range(A_GROUP)], axis=1)


def _softmax_values(parts, sink=None):
    m = functools.reduce(jnp.maximum, [jnp.max(s, axis=0, keepdims=True) for s, _ in parts])
    if sink is not None:
        m = jnp.maximum(m, sink)
    ot = sum(_dot(vt1, jnp.exp2(s - m).astype(BF16)) for s, vt1 in parts)
    den = ot[LANES:LANES + 1]
    if sink is not None:
        den = den + jnp.exp2(sink - m)
    return ot[:LANES] / den


def _a_heads(ot, g, tq):
    return [ot[g * HEAD_DIM:(g + 1) * HEAD_DIM, hh * tq:(hh + 1) * tq] for hh in range(A_GROUP)]


def _subln(o, subln, lam_init):
    return _rms(o, subln) * (1.0 - lam_init)


def _diff_combine(ot, lam, subln, lam_init):
    tq = ot.shape[1] // 2
    return _subln((ot[:, :tq] - lam * ot[:, tq:]).T, subln, lam_init)


def _ctx_attn_kernel(sink_ref, qa_ref, ka_ref, va_ref, qb_ref, kb_ref, vb_ref, lamqk_ref, subln_ref, o_ref, *,
                     lam_init):
    T = qa_ref.shape[0]
    ka = ka_ref[...].astype(BF16)
    vat1 = _with_ones_rows(va_ref[...].T.astype(BF16))
    scores_a = [_dot_nt(ka, _stack_group_queries(qa_ref, g, slice(None))) for g in range(A_KV_HEADS)]
    scores_b, vbt1 = [], []
    for h in range(B_HEADS):
        sl = slice(h * LANES, (h + 1) * LANES)
        scores_b.append(_dot_nt(kb_ref[:, sl].astype(BF16), _stack_pair_queries(qb_ref[:, sl])))
        vbt1.append(_with_ones_rows(vb_ref[:, sl].T.astype(BF16)))
    heads = []
    for g in range(A_KV_HEADS):
        heads += _a_heads(_softmax_values([(scores_a[g], vat1)], _sink_row(sink_ref, g, T)), g, T)
    nqa = A_HEADS * HEAD_DIM
    o_ref[:, :nqa] = jnp.concatenate(heads, axis=0).T.astype(o_ref.dtype)
    lam = _diff_lambda(lamqk_ref[...], lam_init)
    for h in range(B_HEADS):
        ot = _softmax_values([(scores_b[h], vbt1[h])])
        o_ref[:, nqa + h * LANES:nqa + (h + 1) * LANES] = (
            _diff_combine(ot, lam, subln_ref[...], lam_init).astype(o_ref.dtype))


def _ctx_attention(qa, ka, va, qb, kb, vb, sink, lam_qk, subln, lam_init):
    B, T, _ = qa.shape
    blk = lambda w: pl.BlockSpec((None, T, w), lambda b: (b, 0, 0))
    full = lambda a: pl.BlockSpec(a.shape, lambda b: (0,) * a.ndim)
    return pl.pallas_call(
        functools.partial(_ctx_attn_kernel, lam_init=lam_init),
        grid=(B,),
        in_specs=[pl.BlockSpec(memory_space=pltpu.SMEM), blk(QA_W), blk(KA_W), blk(KA_W), blk(B_W), blk(B_W),
                  blk(B_W), full(lam_qk), full(subln)],
        out_specs=blk(A_HEADS * HEAD_DIM + B_W),
        out_shape=jax.ShapeDtypeStruct((B, T, A_HEADS * HEAD_DIM + B_W), BF16),
        compiler_params=_cparams("parallel"),
        name="ctx_attention",
    )(sink, qa, ka, va, qb, kb, vb, lam_qk, subln)


def _lat_a_kernel(sink_ref, qa_ref, k_ref, vt_ref, ck_ref, cvt_ref, o_ref, *, tq, nq, band):
    T = k_ref.shape[0]
    cols = A_GROUP * tq
    cvt1 = _with_ones_rows(cvt_ref[...])
    jobs = []
    for i in range(nq):
        qi = pl.program_id(1) * nq + i
        start = pl.multiple_of(jnp.clip(qi * tq - WINDOW, 0, T - band), WINDOW)
        kb = k_ref[pl.ds(start, band), :]
        vbt1 = _with_ones_rows(vt_ref[:, pl.ds(start, band)])
        kpos = start + lax.broadcasted_iota(jnp.int32, (band, cols), 0)
        qpos = qi * tq + lax.broadcasted_iota(jnp.int32, (band, cols), 1) % tq
        keep = jnp.abs(qpos - kpos) <= WINDOW
        for g in range(A_KV_HEADS):
            qg = _stack_group_queries(qa_ref, g, slice(i * tq, (i + 1) * tq))
            jobs.append((_dot_nt(ck_ref[...], qg), _dot_nt(kb, qg), keep, vbt1))
    for i in range(nq):
        heads = []
        for g in range(A_KV_HEADS):
            s_c, s_b, keep, vbt1 = jobs[i * A_KV_HEADS + g]
            ot = _softmax_values([(s_c, cvt1), (jnp.where(keep, s_b, NEG), vbt1)], _sink_row(sink_ref, g, tq))
            heads += _a_heads(ot, g, tq)
        o_ref[i * tq:(i + 1) * tq, :] = jnp.concatenate(heads, axis=0).T.astype(o_ref.dtype)


def _lat_a_attention(qa, ka, vat, cka, cvat, sink):
    B, T, _ = qa.shape
    L = cka.shape[1]
    tq, nq = WINDOW, 4
    band = 3 * WINDOW
    return pl.pallas_call(
        functools.partial(_lat_a_kernel, tq=tq, nq=nq, band=band),
        grid=(B, T // (tq * nq)),
        in_specs=[
            pl.BlockSpec(memory_space=pltpu.SMEM),
            pl.BlockSpec((None, tq * nq, QA_W), lambda b, i: (b, i, 0)),
            pl.BlockSpec((None, T, KA_W), lambda b, i: (b, 0, 0)),
            pl.BlockSpec((None, KA_W, T), lambda b, i: (b, 0, 0)),
            pl.BlockSpec((None, L, KA_W), lambda b, i: (b, 0, 0)),
            pl.BlockSpec((None, KA_W, L), lambda b, i: (b, 0, 0)),
        ],
        out_specs=pl.BlockSpec((None, tq * nq, A_HEADS * HEAD_DIM), lambda b, i: (b, i, 0)),
        out_shape=jax.ShapeDtypeStruct((B, T, A_HEADS * HEAD_DIM), BF16),
        compiler_params=_cparams("parallel", "parallel"),
        name="lat_a_attention",
    )(sink, qa, ka, vat, cka, cvat)


def _lat_b_kernel(q_ref, k_ref, vt_ref, ck_ref, cvt_ref, lamqk_ref, subln_ref, o_ref, *, tk, lam_init):
    tq = q_ref.shape[0]
    T = k_ref.shape[0]
    qs = _stack_pair_queries(q_ref[...])
    blocks = [(ck_ref[...], cvt_ref[...])]
    blocks += [(k_ref[j * tk:(j + 1) * tk, :], vt_ref[:, j * tk:(j + 1) * tk]) for j in range(T // tk)]

    def scores(k):
        return _dot_nt(k, qs)

    def accumulate(s, vt, carry):
        m, acc = carry
        m_new = jnp.maximum(m, jnp.max(s, axis=0, keepdims=True))
        p = jnp.exp2(s - m_new).astype(BF16)
        acc = jnp.exp2(m - m_new) * acc + _dot(_with_ones_rows(vt), p)
        return m_new, acc

    carry = (jnp.full((1, 2 * tq), -jnp.inf, F32), jnp.zeros((LANES + ONES_ROWS, 2 * tq), F32))
    s = scores(blocks[0][0])
    for j in range(len(blocks)):
        s_next = scores(blocks[j + 1][0]) if j + 1 < len(blocks) else None
        carry = accumulate(s, blocks[j][1], carry)
        s = s_next
    _, acc = carry
    ot = acc[:LANES] / acc[LANES:LANES + 1]
    lam = _diff_lambda(lamqk_ref[...], lam_init)
    o_ref[...] = _diff_combine(ot, lam, subln_ref[...], lam_init).astype(o_ref.dtype)


def _lat_b_attention(qb, kb, vbt, ckb, cvbt, lam_qk, subln, lam_init):
    B, T, _ = qb.shape
    L = ckb.shape[1]
    tq = 512
    full = lambda a: pl.BlockSpec(a.shape, lambda b, h, i: (0,) * a.ndim)
    return pl.pallas_call(
        functools.partial(_lat_b_kernel, tk=512, lam_init=lam_init),
        grid=(B, B_HEADS, T // tq),
        in_specs=[
            pl.BlockSpec((None, tq, LANES), lambda b, h, i: (b, i, h)),
            pl.BlockSpec((None, T, LANES), lambda b, h, i: (b, 0, h)),
            pl.BlockSpec((None, LANES, T), lambda b, h, i: (b, h, 0)),
            pl.BlockSpec((None, L, LANES), lambda b, h, i: (b, 0, h)),
            pl.BlockSpec((None, LANES, L), lambda b, h, i: (b, h, 0)),
            full(lam_qk), full(subln),
        ],
        out_specs=pl.BlockSpec((None, tq, LANES), lambda b, h, i: (b, i, h)),
        out_shape=jax.ShapeDtypeStruct((B, T, B_W), BF16),
        compiler_params=_cparams("parallel", "parallel", "parallel"),
        name="lat_b_attention",
    )(qb, kb, vbt, ckb, cvbt, lam_qk, subln)


def _scan_chunk_len(n_steps):
    chunk = -(-n_steps // SUBLANES)
    while chunk % 8 != 4:
        chunk += 1
    return chunk


def _rec_kernel(gate_ref, xr_ref, cw_ref, cb_ref, wbd_ref, bbd_ref, lam_ref, h0f_ref, h0b_ref,
                y_ref, sf_ref, sb_ref, xp, af, uf, ab, ub, pf_s, hf_s, pb_s, hb_s, *, nblk, **kw):
    for n in range(nblk):
        sl = slice(n * LANES, (n + 1) * LANES)
        _rec_block(gate_ref.at[:, sl], xr_ref.at[:, sl], cw_ref.at[:, sl], cb_ref.at[:, sl], wbd_ref.at[n],
                   bbd_ref.at[n], lam_ref.at[:, sl], h0f_ref.at[:, sl], h0b_ref.at[:, sl], y_ref.at[:, sl],
                   sf_ref.at[:, sl], sb_ref.at[:, sl], xp.at[n], af.at[n], uf.at[n], ab.at[n], ub.at[n],
                   pf_s.at[n], hf_s.at[n], pb_s.at[n], hb_s.at[n], **kw)


def _rec_block(gate_ref, xr_ref, cw_ref, cb_ref, wbd_ref, bbd_ref, lam_ref, h0f_ref, h0b_ref,
               y_ref, sf_ref, sb_ref, xp, af, uf, ab, ub, pf_s, hf_s, pb_s, hb_s, *, tc, gates_unroll, chunk, unroll):
    T = xr_ref.shape[0]
    pad = SUBLANES
    xp[0:pad, :] = jnp.zeros((pad, LANES), F32)
    xp[T + pad:T + 2 * pad, :] = jnp.zeros((pad, LANES), F32)
    xp[pad:T + pad, :] = xr_ref[...]
    tail = SUBLANES * chunk - T
    for a_s, u_s in ((af, uf), (ab, ub)):
        a_s[T:T + tail, :] = jnp.ones((tail, LANES), F32)
        u_s[T:T + tail, :] = jnp.zeros((tail, LANES), F32)
    cw = cw_ref[...]
    cb = cb_ref[...]
    nl = -lam_ref[...]
    softplus = jnp.maximum(nl, 0.0) + jnp.log1p(jnp.exp(-jnp.abs(nl)))
    cp = (0.5 * RGLRU_C) * softplus

    def gates(it, _):
        for q in range(gates_unroll):
            gate_chunk(it * gates_unroll + q)
        return 0

    def gate_chunk(ci):
        t0 = pl.multiple_of(ci * tc, tc)
        y = cb
        for j in range(CONV_W):
            y = y + xp[pl.ds(t0 + (pad - CONV_LEFT + j), tc), :] * cw[j:j + 1, :]
        t = jnp.tanh(_dot(y.astype(BF16), wbd_ref[...]) + bbd_ref[...])
        yh = 0.5 * y
        for d, (a_s, u_s) in enumerate(((af, uf), (ab, ub))):
            t_r = t[:, 2 * d * LANES:(2 * d + 1) * LANES]
            t_i = t[:, (2 * d + 1) * LANES:(2 * d + 2) * LANES]
            neg_log_a = cp[d:d + 1, :] * t_r + cp[d:d + 1, :]
            a = jnp.exp2(neg_log_a * (-LOG2E))
            w = jnp.tanh(neg_log_a) * (a * a + 1.0)
            sqrt_w = jnp.where(w > 0.0, w * lax.rsqrt(w), 0.0)
            a_s[pl.ds(t0, tc), :] = a
            u_s[pl.ds(t0, tc), :] = (t_i + 1.0) * (yh * sqrt_w)

    lax.fori_loop(0, T // (tc * gates_unroll), gates, 0)

    def rows(k):
        return pl.ds(k, SUBLANES, stride=chunk)

    def local_scan(it, carry):
        hf, pf, hb, pb = carry
        for q in range(unroll):
            k = it * unroll + q
            a = af[rows(k), :]
            hf = a * hf + uf[rows(k), :]
            pf = a * pf
            hf_s[rows(k), :] = hf
            pf_s[rows(k), :] = pf
            kb = chunk - 1 - k
            a = ab[rows(kb), :]
            hb = a * hb + ub[rows(kb), :]
            pb = a * pb
            hb_s[rows(kb), :] = hb
            pb_s[rows(kb), :] = pb
        return hf, pf, hb, pb

    zero = jnp.zeros((SUBLANES, LANES), F32)
    one = jnp.ones((SUBLANES, LANES), F32)
    hf, pf, hb, pb = lax.fori_loop(0, chunk // unroll, local_scan, (zero, one, zero, one))

    cf = [h0f_ref[...]]
    for r in range(SUBLANES - 1):
        cf.append(hf[r:r + 1, :] + pf[r:r + 1, :] * cf[r])
    cb_rev = [h0b_ref[...]]
    for r in range(SUBLANES - 1, 0, -1):
        cb_rev.append(hb[r:r + 1, :] + pb[r:r + 1, :] * cb_rev[-1])
    carry_f = jnp.concatenate(cf, axis=0)
    carry_b = jnp.concatenate(cb_rev[::-1], axis=0)

    def apply_carry(it, _):
        for q in range(unroll):
            k = it * unroll + q
            uf[rows(k), :] = hf_s[rows(k), :] + pf_s[rows(k), :] * carry_f
            ub[rows(k), :] = hb_s[rows(k), :] + pb_s[rows(k), :] * carry_b
        return 0

    lax.fori_loop(0, chunk // unroll, apply_carry, 0)
    sf_ref[...] = uf[T - 1:T, :]
    sb_ref[...] = ub[0:1, :]

    k_gelu = math.sqrt(2.0 / math.pi)

    def combine(ci, _):
        t0 = pl.multiple_of(ci * tc, tc)
        g = gate_ref[pl.ds(t0, tc), :]
        gelu = g * (0.5 * (1.0 + jnp.tanh(k_gelu * (g + 0.044715 * (g * g * g)))))
        y_ref[pl.ds(t0, tc), :] = ((uf[pl.ds(t0, tc), :] + ub[pl.ds(t0, tc), :]) * gelu).astype(y_ref.dtype)
        return 0

    lax.fori_loop(0, T // tc, combine, 0)


def _rec_mixer(gate, xr, h0f, h0b, conv_w, conv_b, w_bd, b_bd, lam, *, nblk, tc, name):
    B, T, _ = xr.shape
    chunk = _scan_chunk_len(T)
    wid = nblk * LANES
    col = lambda rows: pl.BlockSpec((rows, wid), lambda b, n: (0, n))
    seq = pl.BlockSpec((None, T, wid), lambda b, n: (b, 0, n))
    st = pl.BlockSpec((None, 1, wid), lambda b, n: (b, 0, n))
    return pl.pallas_call(
        functools.partial(_rec_kernel, nblk=nblk, tc=tc, gates_unroll=2, chunk=chunk, unroll=12),
        grid=(B, RNN_BLOCKS // nblk),
        in_specs=[seq, seq, col(CONV_W), col(1),
                  pl.BlockSpec((nblk, LANES, 4 * LANES), lambda b, n: (n, 0, 0)),
                  pl.BlockSpec((nblk, 1, 4 * LANES), lambda b, n: (n, 0, 0)),
                  col(2), st, st],
        out_specs=[seq, st, st],
        out_shape=[jax.ShapeDtypeStruct((B, T, D_RNN), BF16),
                   jax.ShapeDtypeStruct((B, 1, D_RNN), F32),
                   jax.ShapeDtypeStruct((B, 1, D_RNN), F32)],
        scratch_shapes=[pltpu.VMEM((nblk, T + 2 * SUBLANES, LANES), F32)]
        + [pltpu.VMEM((nblk, SUBLANES * chunk, LANES), F32)] * 8,
        compiler_params=_cparams("parallel", "parallel"),
        name=name,
    )(gate, xr, conv_w, conv_b.reshape(1, D_RNN), w_bd, b_bd, lam, h0f, h0b)


def _post_kernel(*refs, n_mix, final, fc):
    x_ref, mod_ref, g2_ref = refs[:3]
    mix = refs[3:3 + 2 * n_mix]
    w1_ref, w2_ref = refs[3 + 2 * n_mix:5 + 2 * n_mix]
    rest = refs[5 + 2 * n_mix:]
    if final:
        gf_ref, o_ref = rest
    else:
        (o_ref,) = rest
    mixed = _dot(mix[0][...], mix[1][...])
    for i in range(1, n_mix):
        mixed = mixed + _dot(mix[2 * i][...], mix[2 * i + 1][...])
    x1 = x_ref[...] + mod_ref[2:3, :] * mixed
    h = _rms(x1, g2_ref[...])
    hb = (h * (1.0 + mod_ref[4:5, :]) + mod_ref[3:4, :]).astype(BF16)
    n_chunks = D_FF // fc
    up = lambda c: _dot(hb, w1_ref[:, c * fc:(c + 1) * fc])
    acc = None
    a = up(0)
    for c in range(n_chunks):
        a_next = up(c + 1) if c + 1 < n_chunks else None
        part = _dot(jnp.square(jnp.maximum(a, 0.0)).astype(BF16), w2_ref[c * fc:(c + 1) * fc, :])
        acc = part if acc is None else acc + part
        a = a_next
    x2 = x1 + mod_ref[5:6, :] * acc
    if final:
        x2 = _rms(x2, gf_ref[...])
    o_ref[...] = x2


def _post(x, mod, g2, mixes, w1, w2, *, tm, ctx, final_g=None, name):
    B, T, D = x.shape
    row = (lambda b, i: (CTX_ROW, 0, 0)) if ctx else (lambda b, i: (b, 0, 0))
    const = lambda a: pl.BlockSpec(a.shape, lambda b, i: (0,) * a.ndim, pipeline_mode=pl.Buffered(1))
    in_specs = [
        pl.BlockSpec((None, tm, D), lambda b, i: (b, i, 0)),
        pl.BlockSpec((None, 6, D), row),
        pl.BlockSpec((1, D), lambda b, i: (0, 0)),
    ]
    args = [x, mod, g2.reshape(1, D)]
    for o, w in mixes:
        in_specs += [pl.BlockSpec((None, tm, o.shape[-1]), lambda b, i: (b, i, 0)), const(w)]
        args += [o, w]
    in_specs += [const(w1), const(w2)]
    args += [w1, w2]
    if final_g is not None:
        in_specs.append(pl.BlockSpec((1, D), lambda b, i: (0, 0)))
        args.append(final_g.reshape(1, D))
    return pl.pallas_call(
        functools.partial(_post_kernel, n_mix=len(mixes), final=final_g is not None, fc=1024),
        grid=(B, T // tm),
        in_specs=in_specs,
        out_specs=pl.BlockSpec((None, tm, D), lambda b, i: (b, i, 0)),
        out_shape=jax.ShapeDtypeStruct((B, T, D), F32),
        compiler_params=_cparams("parallel", "parallel"),
        name=name,
    )(*args)


def _rope_tables(n_tokens):
    rows = n_tokens // GRID_W
    r, cl = jnp.meshgrid(jnp.arange(rows, dtype=F32), jnp.arange(GRID_W, dtype=F32), indexing='ij')
    quarter = HEAD_DIM // 4
    inv = ROPE_BASE ** (-jnp.arange(quarter, dtype=F32) / quarter)
    ang = jnp.stack([r.reshape(-1)[:, None] * inv, cl.reshape(-1)[:, None] * inv], axis=1)
    cos, sin = jnp.cos(ang), jnp.sin(ang)
    cos64 = jnp.concatenate([cos[:, 0], cos[:, 0], cos[:, 1], cos[:, 1]], axis=-1)
    sin64 = jnp.concatenate([-sin[:, 0], sin[:, 0], -sin[:, 1], sin[:, 1]], axis=-1)
    return jnp.tile(cos64, (1, LANES // HEAD_DIM)), jnp.tile(sin64, (1, LANES // HEAD_DIM))


def _att_in_weights(w_in):
    d = w_in.shape[0]
    nq = A_HEADS * HEAD_DIM
    wq = w_in[:, :nq].reshape(d, A_HEADS, HEAD_DIM)
    z = jnp.zeros_like(wq)
    in_first = (jnp.arange(A_HEADS) // A_GROUP == 0)[None, :, None]
    wq = jnp.where(in_first, jnp.concatenate([wq, z], axis=-1), jnp.concatenate([z, wq], axis=-1))
    return jnp.concatenate([wq.reshape(d, QA_W), w_in[:, nq:]], axis=1).astype(BF16)


def _block_diag_weights(w_a, b_a, w_x, b_x):
    w = jnp.concatenate([w_a[0], w_x[0], w_a[1], w_x[1]], axis=-1)
    b = jnp.concatenate([v.reshape(RNN_BLOCKS, 1, RNN_BW) for v in (b_a[0], b_x[0], b_a[1], b_x[1])], axis=-1)
    return (0.5 * w).astype(BF16), 0.5 * b


def kernel(x_prompt, x_sample, cache_a_k, cache_a_v, cache_b_k, cache_b_v, state_fwd, state_bwd, c, c_ctx, norm1, norm2, w_ada, b_ada, w_mlp1, w_mlp2, att_w_in, att_w_out, att_sink, att_lam_qk, att_subln, rec_w_in, rec_conv_w, rec_conv_b, rec_w_a, rec_b_a, rec_w_x, rec_b_x, rec_lam, rec_w_out, final_norm):
    nb, n_seq, _ = x_prompt.shape
    nd, d_seq, _ = x_sample.shape
    past = cache_a_k.shape[2]
    assert nd <= CTX_ROW and DEPTH == 2
    cvec = jnp.concatenate([c, jnp.zeros((CTX_ROW - nd, D_MODEL), F32), c_ctx[None],
                            jnp.zeros((MOD_ROWS - CTX_ROW - 1, D_MODEL), F32)], axis=0)
    mod = _ada_mod(cvec, w_ada, b_ada)
    w1 = w_mlp1.astype(BF16)
    w2 = w_mlp2.astype(BF16)
    tm_ctx, tm_lat = n_seq, 512

    lam_init = 0.8 - 0.6 * math.exp(-0.3 * 0)
    w_in = _att_in_weights(att_w_in[0])
    w_out = att_w_out[0].astype(BF16)
    nqa = A_HEADS * HEAD_DIM
    c0 = [0, QA_W, QA_W + KA_W, QA_W + 2 * KA_W, QA_W + 2 * KA_W + B_W, QA_W + 2 * KA_W + 2 * B_W]
    widths = [QA_W, KA_W, KA_W, B_W, B_W, B_W]
    scales = [SCALE * LOG2E, 1.0, 1.0, SCALE * LOG2E, 1.0, 1.0]
    roped = [True, True, False, True, True, False]
    sink = att_sink[0]
    lam_qk = att_lam_qk[0]
    subln = att_subln[0].reshape(1, 2 * HEAD_DIM)

    segs_ctx = [(c0[i], widths[i], False, scales[i], False) for i in range(6)]
    qa, ka, va, qb, kb, vb = _proj(x_prompt, mod[0], norm1[0], w_in, segs_ctx, [BF16, F32, F32, BF16, F32, F32],
                                   tm=tm_ctx, ctx=True, name="proj_att_ctx")
    o_ctx = _ctx_attention(qa, ka, va, qb, kb, vb, sink, lam_qk, subln, lam_init)
    xp = _post(x_prompt, mod[0], norm2[0], [(o_ctx, w_out)], w1[0], w2[0], tm=tm_ctx, ctx=True, name="post_att_ctx")
    new_a_k = ka.reshape(nb, 1, n_seq, A_KV_HEADS, HEAD_DIM)
    new_a_v = va.reshape(nb, 1, n_seq, A_KV_HEADS, HEAD_DIM)
    new_b_k = kb.reshape(nb, 1, n_seq, B_HEADS, 2 * HEAD_DIM)
    new_b_v = vb.reshape(nb, 1, n_seq, B_HEADS, 2 * HEAD_DIM)

    segs_lat = [(c0[i], widths[i], roped[i], scales[i], i in (2, 5)) for i in range(6)]
    qa, ka, vat, qb, kb, vbt = _proj(x_sample, mod[0], norm1[0], w_in, segs_lat, [BF16] * 6, tm=tm_lat, ctx=False,
                                     rope_tabs=_rope_tables(d_seq), name="proj_att_lat")
    cka = cache_a_k[:, 0].reshape(nd, past, KA_W).astype(BF16)
    cvat = jnp.swapaxes(cache_a_v[:, 0].reshape(nd, past, KA_W), 1, 2).astype(BF16)
    ckb = cache_b_k[:, 0].reshape(nd, past, B_W).astype(BF16)
    cvbt = jnp.swapaxes(cache_b_v[:, 0].reshape(nd, past, B_W), 1, 2).astype(BF16)
    oa = _lat_a_attention(qa, ka, vat, cka, cvat, sink)
    ob = _lat_b_attention(qb, kb, vbt, ckb, cvbt, lam_qk, subln, lam_init)
    xs = _post(x_sample, mod[0], norm2[0], [(oa, w_out[:nqa]), (ob, w_out[nqa:])], w1[0], w2[0], tm=tm_lat,
               ctx=False, name="post_att_lat")

    w_rin = rec_w_in[0].astype(BF16)
    w_rout = rec_w_out[0].astype(BF16)
    w_bd, b_bd = _block_diag_weights(rec_w_a[0], rec_b_a[0], rec_w_x[0], rec_b_x[0])
    segs_rec = [(0, D_RNN, False, 1.0, False), (D_RNN, D_RNN, False, 1.0, False)]
    zeros = jnp.zeros((nb, 1, D_RNN), F32)

    gate, xr = _proj(xp, mod[1], norm1[1], w_rin, segs_rec, [F32, F32], tm=tm_ctx, ctx=True, name="proj_rec_ctx")
    y, sf, sb = _rec_mixer(gate, xr, zeros, zeros, rec_conv_w[0], rec_conv_b[0], w_bd, b_bd, rec_lam[0],
                           nblk=RNN_BLOCKS, tc=128, name="rec_mixer_ctx")
    y_prompt = _post(xp, mod[1], norm2[1], [(y, w_rout)], w1[1], w2[1], tm=tm_ctx, ctx=True, final_g=final_norm,
                     name="post_rec_ctx")

    gate, xr = _proj(xs, mod[1], norm1[1], w_rin, segs_rec, [F32, F32], tm=tm_lat, ctx=False, name="proj_rec_lat")
    y, _, _ = _rec_mixer(gate, xr, state_fwd[:, 0:1], state_bwd[:, 0:1], rec_conv_w[0], rec_conv_b[0], w_bd, b_bd,
                         rec_lam[0], nblk=1, tc=256, name="rec_mixer_lat")
    y_sample = _post(xs, mod[1], norm2[1], [(y, w_rout)], w1[1], w2[1], tm=tm_lat, ctx=False, final_g=final_norm,
                     name="post_rec_lat")

    return (y_prompt, y_sample, new_a_k, new_a_v, new_b_k, new_b_v, sf, sb)
```

```python
import functools
import math

import jax
import jax.numpy as jnp
import numpy as np
from jax import lax
from jax.experimental import pallas as pl
from jax.experimental.pallas import tpu as pltpu

F32 = jnp.float32
BF16 = jnp.bfloat16

LANES = 128
SUBLANES = 8
VMEM_LIMIT_BYTES = 56 * 1024 * 1024

D_MODEL = 1024
DEPTH = 2
GRID_W = 64
HEAD_DIM = 64
A_HEADS = 8
A_KV_HEADS = 2
A_GROUP = A_HEADS // A_KV_HEADS
B_HEADS = 4
WINDOW = 128
ROPE_BASE = 10000.0
D_RNN = 1280
RNN_BLOCKS = 10
RNN_BW = D_RNN // RNN_BLOCKS
CONV_W = 4
CONV_LEFT = (CONV_W - 1) // 2
RGLRU_C = 8.0
D_FF = 4 * D_MODEL
EPS = 1e-6
SCALE = HEAD_DIM ** -0.5
NEG = -1e30

QA_W = A_HEADS * LANES
KA_W = A_KV_HEADS * HEAD_DIM
B_W = B_HEADS * 2 * HEAD_DIM
MOD_ROWS = 8
CTX_ROW = 4
LOG2E = math.log2(math.e)
ONES_ROWS = 16
SCAN_MAX_UNROLL = 48


def _cparams(*semantics):
    return pltpu.CompilerParams(dimension_semantics=semantics, vmem_limit_bytes=VMEM_LIMIT_BYTES)


def _dot(a, b):
    return jnp.dot(a, b, preferred_element_type=F32)


def _dot_nt(a, b):
    return lax.dot_general(a, b, (((1,), (1,)), ((), ())), preferred_element_type=F32)


def _rms(x, g):
    return x * lax.rsqrt(jnp.mean(x * x, axis=-1, keepdims=True) + EPS) * g


def _ada_kernel(c_ref, w_ref, b_ref, o_ref):
    c = c_ref[...]
    s = c * jax.nn.sigmoid(c)
    o_ref[...] = _dot(s.astype(BF16), w_ref[...].astype(BF16)) + b_ref[...]


def _ada_mod(cvec, w_ada, b_ada):
    tn = 1536
    out = pl.pallas_call(
        _ada_kernel,
        grid=(DEPTH, 6 * D_MODEL // tn),
        in_specs=[
            pl.BlockSpec((MOD_ROWS, D_MODEL), lambda l, j: (0, 0)),
            pl.BlockSpec((None, D_MODEL, tn), lambda l, j: (l, 0, j)),
            pl.BlockSpec((None, 1, tn), lambda l, j: (l, 0, j)),
        ],
        out_specs=pl.BlockSpec((None, MOD_ROWS, tn), lambda l, j: (l, 0, j)),
        out_shape=jax.ShapeDtypeStruct((DEPTH, MOD_ROWS, 6 * D_MODEL), F32),
        compiler_params=_cparams("parallel", "parallel"),
        name="ada_mod",
    )(cvec, w_ada, b_ada.reshape(DEPTH, 1, 6 * D_MODEL))
    return out.reshape(DEPTH, MOD_ROWS, 6, D_MODEL)


def _proj_kernel(*refs, segs, rope):
    if rope:
        x_ref, mod_ref, g_ref, w_ref, cos_ref, sin_ref, *outs = refs
    else:
        x_ref, mod_ref, g_ref, w_ref, *outs = refs
    x = x_ref[...]
    h = _rms(x, g_ref[...])
    h = h * (1.0 + mod_ref[1:2, :]) + mod_ref[0:1, :]
    hb = h.astype(BF16)
    tm = x.shape[0]
    if rope:
        lane = lax.broadcasted_iota(jnp.int32, (tm, LANES), 1)
        first = (lane & 16) == 0
        cos = cos_ref[...]
        sin = sin_ref[...]
    for (c0, width, do_rope, scale, transposed), o_ref in zip(segs, outs):
        y = _dot(hb, w_ref[:, c0:c0 + width])
        for t in range(width // LANES):
            yt = y[:, t * LANES:(t + 1) * LANES]
            if do_rope:
                sw = jnp.where(first, pltpu.roll(yt, LANES - 16, 1), pltpu.roll(yt, 16, 1))
                yt = yt * cos + sw * sin
            if scale != 1.0:
                yt = yt * scale
            if transposed:
                o_ref[t * LANES:(t + 1) * LANES, :] = yt.T.astype(o_ref.dtype)
            else:
                o_ref[:, t * LANES:(t + 1) * LANES] = yt.astype(o_ref.dtype)


def _proj(x, mod, g, w, segs, out_dtypes, *, tm, ctx, rope_tabs=None, name):
    B, T, D = x.shape
    row = (lambda b, i: (CTX_ROW, 0, 0)) if ctx else (lambda b, i: (b, 0, 0))
    in_specs = [
        pl.BlockSpec((None, tm, D), lambda b, i: (b, i, 0)),
        pl.BlockSpec((None, 6, D), row),
        pl.BlockSpec((1, D), lambda b, i: (0, 0)),
        pl.BlockSpec(w.shape, lambda b, i: (0, 0)),
    ]
    args = [x, mod, g.reshape(1, D), w]
    if rope_tabs is not None:
        in_specs += [pl.BlockSpec((tm, LANES), lambda b, i: (i, 0))] * 2
        args += list(rope_tabs)
    out_specs = [pl.BlockSpec((None, s[1], tm), lambda b, i: (b, 0, i)) if s[4] else
                 pl.BlockSpec((None, tm, s[1]), lambda b, i: (b, i, 0)) for s in segs]
    out_shape = [jax.ShapeDtypeStruct((B, s[1], T) if s[4] else (B, T, s[1]), dt)
                 for s, dt in zip(segs, out_dtypes)]
    return pl.pallas_call(
        functools.partial(_proj_kernel, segs=tuple(segs), rope=rope_tabs is not None),
        grid=(B, T // tm),
        in_specs=in_specs,
        out_specs=out_specs,
        out_shape=out_shape,
        compiler_params=_cparams("parallel", "parallel"),
        name=name,
    )(*args)


def _diff_lambda(lq, lam_init):
    s1 = jnp.sum(lq[0:1, :] * lq[1:2, :], axis=1, keepdims=True)
    s2 = jnp.sum(lq[2:3, :] * lq[3:4, :], axis=1, keepdims=True)
    return jnp.exp(s1) - jnp.exp(s2) + lam_init


def _stack_group_queries(qa_ref, g, rows):
    return jnp.concatenate(
        [qa_ref[rows, (A_GROUP * g + hh) * LANES:(A_GROUP * g + hh + 1) * LANES] for hh in range(A_GROUP)], axis=0)


def _stack_pair_queries(q):
    lo = lax.broadcasted_iota(jnp.int32, q.shape, 1) < HEAD_DIM
    zero = jnp.zeros_like(q)
    return jnp.concatenate([jnp.where(lo, q, zero), jnp.where(lo, zero, q)], axis=0)


def _with_ones_rows(vt):
    return jnp.concatenate([vt, jnp.ones((ONES_ROWS, vt.shape[1]), BF16)], axis=0)


def _sink_row(sink_ref, g, tq):
    return jnp.concatenate(
        [jnp.full((1, tq), sink_ref[A_GROUP * g + hh] * LOG2E, F32) for hh in range(A_GROUP)], axis=1)


def _softmax_values(parts, sink=None):
    m = functools.reduce(jnp.maximum, [jnp.max(s, axis=0, keepdims=True) for s, _ in parts])
    if sink is not None:
        m = jnp.maximum(m, sink)
    ot = sum(_dot(vt1, jnp.exp2(s - m).astype(BF16)) for s, vt1 in parts)
    den = ot[LANES:LANES + 1]
    if sink is not None:
        den = den + jnp.exp2(sink - m)
    return ot[:LANES] / den


def _a_heads(ot, g, tq):
    return [ot[g * HEAD_DIM:(g + 1) * HEAD_DIM, hh * tq:(hh + 1) * tq] for hh in range(A_GROUP)]


def _subln(o, subln, lam_init):
    return _rms(o, subln) * (1.0 - lam_init)


def _diff_combine(ot, lam, subln, lam_init):
    tq = ot.shape[1] // 2
    return _subln((ot[:, :tq] - lam * ot[:, tq:]).T, subln, lam_init)


def _ctx_attn_kernel(sink_ref, qa_ref, ka_ref, va_ref, qb_ref, kb_ref, vb_ref, lamqk_ref, subln_ref, o_ref, *,
                     lam_init):
    T = qa_ref.shape[0]
    ka = ka_ref[...].astype(BF16)
    vat1 = _with_ones_rows(va_ref[...].T.astype(BF16))
    scores_a = [_dot_nt(ka, _stack_group_queries(qa_ref, g, slice(None))) for g in range(A_KV_HEADS)]
    scores_b, vbt1 = [], []
    for h in range(B_HEADS):
        sl = slice(h * LANES, (h + 1) * LANES)
        scores_b.append(_dot_nt(kb_ref[:, sl].astype(BF16), _stack_pair_queries(qb_ref[:, sl])))
        vbt1.append(_with_ones_rows(vb_ref[:, sl].T.astype(BF16)))
    heads = []
    for g in range(A_KV_HEADS):
        heads += _a_heads(_softmax_values([(scores_a[g], vat1)], _sink_row(sink_ref, g, T)), g, T)
    nqa = A_HEADS * HEAD_DIM
    o_ref[:, :nqa] = jnp.concatenate(heads, axis=0).T.astype(o_ref.dtype)
    lam = _diff_lambda(lamqk_ref[...], lam_init)
    for h in range(B_HEADS):
        ot = _softmax_values([(scores_b[h], vbt1[h])])
        o_ref[:, nqa + h * LANES:nqa + (h + 1) * LANES] = (
            _diff_combine(ot, lam, subln_ref[...], lam_init).astype(o_ref.dtype))


def _ctx_attention(qa, ka, va, qb, kb, vb, sink, lam_qk, subln, lam_init):
    B, T, _ = qa.shape
    blk = lambda w: pl.BlockSpec((None, T, w), lambda b: (b, 0, 0))
    full = lambda a: pl.BlockSpec(a.shape, lambda b: (0,) * a.ndim)
    return pl.pallas_call(
        functools.partial(_ctx_attn_kernel, lam_init=lam_init),
        grid=(B,),
        in_specs=[pl.BlockSpec(memory_space=pltpu.SMEM), blk(QA_W), blk(KA_W), blk(KA_W), blk(B_W), blk(B_W),
                  blk(B_W), full(lam_qk), full(subln)],
        out_specs=blk(A_HEADS * HEAD_DIM + B_W),
        out_shape=jax.ShapeDtypeStruct((B, T, A_HEADS * HEAD_DIM + B_W), BF16),
        compiler_params=_cparams("parallel"),
        name="ctx_attention",
    )(sink, qa, ka, va, qb, kb, vb, lam_qk, subln)


def _lat_a_kernel(sink_ref, qa_ref, k_ref, vt_ref, ck_ref, cvt_ref, o_ref, *, tq, nq, band):
    T = k_ref.shape[0]
    cols = A_GROUP * tq
    cvt1 = _with_ones_rows(cvt_ref[...])
    jobs = []
    for i in range(nq):
        qi = pl.program_id(1) * nq + i
        start = pl.multiple_of(jnp.clip(qi * tq - WINDOW, 0, T - band), WINDOW)
        kb = k_ref[pl.ds(start, band), :]
        vbt1 = _with_ones_rows(vt_ref[:, pl.ds(start, band)])
        kpos = start + lax.broadcasted_iota(jnp.int32, (band, cols), 0)
        qpos = qi * tq + lax.broadcasted_iota(jnp.int32, (band, cols), 1) % tq
        keep = jnp.abs(qpos - kpos) <= WINDOW
        for g in range(A_KV_HEADS):
            qg = _stack_group_queries(qa_ref, g, slice(i * tq, (i + 1) * tq))
            jobs.append((_dot_nt(ck_ref[...], qg), _dot_nt(kb, qg), keep, vbt1))
    for i in range(nq):
        heads = []
        for g in range(A_KV_HEADS):
            s_c, s_b, keep, vbt1 = jobs[i * A_KV_HEADS + g]
            ot = _softmax_values([(s_c, cvt1), (jnp.where(keep, s_b, NEG), vbt1)], _sink_row(sink_ref, g, tq))
            heads += _a_heads(ot, g, tq)
        o_ref[i * tq:(i + 1) * tq, :] = jnp.concatenate(heads, axis=0).T.astype(o_ref.dtype)


def _lat_a_attention(qa, ka, vat, cka, cvat, sink):
    B, T, _ = qa.shape
    L = cka.shape[1]
    tq, nq = WINDOW, 4
    band = 3 * WINDOW
    return pl.pallas_call(
        functools.partial(_lat_a_kernel, tq=tq, nq=nq, band=band),
        grid=(B, T // (tq * nq)),
        in_specs=[
            pl.BlockSpec(memory_space=pltpu.SMEM),
            pl.BlockSpec((None, tq * nq, QA_W), lambda b, i: (b, i, 0)),
            pl.BlockSpec((None, T, KA_W), lambda b, i: (b, 0, 0)),
            pl.BlockSpec((None, KA_W, T), lambda b, i: (b, 0, 0)),
            pl.BlockSpec((None, L, KA_W), lambda b, i: (b, 0, 0)),
            pl.BlockSpec((None, KA_W, L), lambda b, i: (b, 0, 0)),
        ],
        out_specs=pl.BlockSpec((None, tq * nq, A_HEADS * HEAD_DIM), lambda b, i: (b, i, 0)),
        out_shape=jax.ShapeDtypeStruct((B, T, A_HEADS * HEAD_DIM), BF16),
        compiler_params=_cparams("parallel", "parallel"),
        name="lat_a_attention",
    )(sink, qa, ka, vat, cka, cvat)


def _lat_b_kernel(q_ref, k_ref, vt_ref, ck_ref, cvt_ref, lamqk_ref, subln_ref, o_ref, *, tk, lam_init):
    tq = q_ref.shape[0]
    T = k_ref.shape[0]
    qs = _stack_pair_queries(q_ref[...])
    blocks = [(ck_ref[...], cvt_ref[...])]
    blocks += [(k_ref[j * tk:(j + 1) * tk, :], vt_ref[:, j * tk:(j + 1) * tk]) for j in range(T // tk)]

    def scores(k):
        return _dot_nt(k, qs)

    def accumulate(s, vt, carry):
        m, acc = carry
        m_new = jnp.maximum(m, jnp.max(s, axis=0, keepdims=True))
        p = jnp.exp2(s - m_new).astype(BF16)
        acc = jnp.exp2(m - m_new) * acc + _dot(_with_ones_rows(vt), p)
        return m_new, acc

    carry = (jnp.full((1, 2 * tq), -jnp.inf, F32), jnp.zeros((LANES + ONES_ROWS, 2 * tq), F32))
    s = scores(blocks[0][0])
    for j in range(len(blocks)):
        s_next = scores(blocks[j + 1][0]) if j + 1 < len(blocks) else None
        carry = accumulate(s, blocks[j][1], carry)
        s = s_next
    _, acc = carry
    ot = acc[:LANES] / acc[LANES:LANES + 1]
    lam = _diff_lambda(lamqk_ref[...], lam_init)
    o_ref[...] = _diff_combine(ot, lam, subln_ref[...], lam_init).astype(o_ref.dtype)


def _lat_b_attention(qb, kb, vbt, ckb, cvbt, lam_qk, subln, lam_init):
    B, T, _ = qb.shape
    L = ckb.shape[1]
    tq = 1024
    full = lambda a: pl.BlockSpec(a.shape, lambda b, h, i: (0,) * a.ndim)
    return pl.pallas_call(
        functools.partial(_lat_b_kernel, tk=512, lam_init=lam_init),
        grid=(B, B_HEADS, T // tq),
        in_specs=[
            pl.BlockSpec((None, tq, LANES), lambda b, h, i: (b, i, h)),
            pl.BlockSpec((None, T, LANES), lambda b, h, i: (b, 0, h)),
            pl.BlockSpec((None, LANES, T), lambda b, h, i: (b, h, 0)),
            pl.BlockSpec((None, L, LANES), lambda b, h, i: (b, 0, h)),
            pl.BlockSpec((None, LANES, L), lambda b, h, i: (b, h, 0)),
            full(lam_qk), full(subln),
        ],
        out_specs=pl.BlockSpec((None, tq, LANES), lambda b, h, i: (b, i, h)),
        out_shape=jax.ShapeDtypeStruct((B, T, B_W), BF16),
        compiler_params=_cparams("parallel", "parallel", "parallel"),
        name="lat_b_attention",
    )(qb, kb, vbt, ckb, cvbt, lam_qk, subln)


def _scan_chunk_len(n_steps):
    chunk = -(-n_steps // SUBLANES)
    while chunk % 8 != 4:
        chunk += 1
    return chunk


def _rec_kernel(gate_ref, xr_ref, cw_ref, cb_ref, wbd_ref, bbd_ref, lam_ref, h0f_ref, h0b_ref,
                y_ref, sf_ref, sb_ref, xp, af, uf, ab, ub, pf_s, hf_s, pb_s, hb_s, *, nblk, **kw):
    for n in range(nblk):
        sl = slice(n * LANES, (n + 1) * LANES)
        _rec_block(gate_ref.at[:, sl], xr_ref.at[:, sl], cw_ref.at[:, sl], cb_ref.at[:, sl], wbd_ref.at[n],
                   bbd_ref.at[n], lam_ref.at[:, sl], h0f_ref.at[:, sl], h0b_ref.at[:, sl], y_ref.at[:, sl],
                   sf_ref.at[:, sl], sb_ref.at[:, sl], xp.at[n], af.at[n], uf.at[n], ab.at[n], ub.at[n],
                   pf_s.at[n], hf_s.at[n], pb_s.at[n], hb_s.at[n], **kw)


def _rec_block(gate_ref, xr_ref, cw_ref, cb_ref, wbd_ref, bbd_ref, lam_ref, h0f_ref, h0b_ref,
               y_ref, sf_ref, sb_ref, xp, af, uf, ab, ub, pf_s, hf_s, pb_s, hb_s, *, tc, gates_unroll, chunk, unroll):
    T = xr_ref.shape[0]
    pad = SUBLANES
    xp[0:pad, :] = jnp.zeros((pad, LANES), F32)
    xp[T + pad:T + 2 * pad, :] = jnp.zeros((pad, LANES), F32)
    xp[pad:T + pad, :] = xr_ref[...]
    tail = SUBLANES * chunk - T
    for a_s, u_s in ((af, uf), (ab, ub)):
        a_s[T:T + tail, :] = jnp.ones((tail, LANES), F32)
        u_s[T:T + tail, :] = jnp.zeros((tail, LANES), F32)
    cw = cw_ref[...]
    cb = cb_ref[...]
    nl = -lam_ref[...]
    softplus = jnp.maximum(nl, 0.0) + jnp.log1p(jnp.exp(-jnp.abs(nl)))
    cp = (0.5 * RGLRU_C) * softplus

    def gates(it, _):
        for q in range(gates_unroll):
            gate_chunk(it * gates_unroll + q)
        return 0

    def gate_chunk(ci):
        t0 = pl.multiple_of(ci * tc, tc)
        y = cb
        for j in range(CONV_W):
            y = y + xp[pl.ds(t0 + (pad - CONV_LEFT + j), tc), :] * cw[j:j + 1, :]
        t = jnp.tanh(_dot(y.astype(BF16), wbd_ref[...]) + bbd_ref[...])
        yh = 0.5 * y
        for d, (a_s, u_s) in enumerate(((af, uf), (ab, ub))):
            t_r = t[:, 2 * d * LANES:(2 * d + 1) * LANES]
            t_i = t[:, (2 * d + 1) * LANES:(2 * d + 2) * LANES]
            neg_log_a = cp[d:d + 1, :] * t_r + cp[d:d + 1, :]
            a = jnp.exp2(neg_log_a * (-LOG2E))
            w = jnp.tanh(neg_log_a) * (a * a + 1.0)
            sqrt_w = jnp.where(w > 0.0, w * lax.rsqrt(w), 0.0)
            a_s[pl.ds(t0, tc), :] = a
            u_s[pl.ds(t0, tc), :] = (t_i + 1.0) * (yh * sqrt_w)

    lax.fori_loop(0, T // (tc * gates_unroll), gates, 0)

    def rows(k):
        return pl.ds(k, SUBLANES, stride=chunk)

    def local_scan(it, carry):
        hf, pf, hb, pb = carry
        for q in range(unroll):
            k = it * unroll + q
            a = af[rows(k), :]
            hf = a * hf + uf[rows(k), :]
            pf = a * pf
            hf_s[rows(k), :] = hf
            pf_s[rows(k), :] = pf
            kb = chunk - 1 - k
            a = ab[rows(kb), :]
            hb = a * hb + ub[rows(kb), :]
            pb = a * pb
            hb_s[rows(kb), :] = hb
            pb_s[rows(kb), :] = pb
        return hf, pf, hb, pb

    zero = jnp.zeros((SUBLANES, LANES), F32)
    one = jnp.ones((SUBLANES, LANES), F32)
    hf, pf, hb, pb = lax.fori_loop(0, chunk // unroll, local_scan, (zero, one, zero, one))

    cf = [h0f_ref[...]]
    for r in range(SUBLANES - 1):
        cf.append(hf[r:r + 1, :] + pf[r:r + 1, :] * cf[r])
    cb_rev = [h0b_ref[...]]
    for r in range(SUBLANES - 1, 0, -1):
        cb_rev.append(hb[r:r + 1, :] + pb[r:r + 1, :] * cb_rev[-1])
    carry_f = jnp.concatenate(cf, axis=0)
    carry_b = jnp.concatenate(cb_rev[::-1], axis=0)

    def apply_carry(it, _):
        for q in range(unroll):
            k = it * unroll + q
            uf[rows(k), :] = hf_s[rows(k), :] + pf_s[rows(k), :] * carry_f
            ub[rows(k), :] = hb_s[rows(k), :] + pb_s[rows(k), :] * carry_b
        return 0

    lax.fori_loop(0, chunk // unroll, apply_carry, 0)
    sf_ref[...] = uf[T - 1:T, :]
    sb_ref[...] = ub[0:1, :]

    k_gelu = math.sqrt(2.0 / math.pi)

    def combine(ci, _):
        t0 = pl.multiple_of(ci * tc, tc)
        g = gate_ref[pl.ds(t0, tc), :]
        gelu = g * (0.5 * (1.0 + jnp.tanh(k_gelu * (g + 0.044715 * (g * g * g)))))
        y_ref[pl.ds(t0, tc), :] = ((uf[pl.ds(t0, tc), :] + ub[pl.ds(t0, tc), :]) * gelu).astype(y_ref.dtype)
        return 0

    lax.fori_loop(0, T // tc, combine, 0)


def _rec_mixer(gate, xr, h0f, h0b, conv_w, conv_b, w_bd, b_bd, lam, *, nblk, tc, name):
    B, T, _ = xr.shape
    chunk = _scan_chunk_len(T)
    wid = nblk * LANES
    col = lambda rows: pl.BlockSpec((rows, wid), lambda b, n: (0, n))
    seq = pl.BlockSpec((None, T, wid), lambda b, n: (b, 0, n))
    st = pl.BlockSpec((None, 1, wid), lambda b, n: (b, 0, n))
    return pl.pallas_call(
        functools.partial(_rec_kernel, nblk=nblk, tc=tc, gates_unroll=2, chunk=chunk,
                          unroll=max(u for u in range(1, SCAN_MAX_UNROLL + 1) if chunk % u == 0)),
        grid=(B, RNN_BLOCKS // nblk),
        in_specs=[seq, seq, col(CONV_W), col(1),
                  pl.BlockSpec((nblk, LANES, 4 * LANES), lambda b, n: (n, 0, 0)),
                  pl.BlockSpec((nblk, 1, 4 * LANES), lambda b, n: (n, 0, 0)),
                  col(2), st, st],
        out_specs=[seq, st, st],
        out_shape=[jax.ShapeDtypeStruct((B, T, D_RNN), BF16),
                   jax.ShapeDtypeStruct((B, 1, D_RNN), F32),
                   jax.ShapeDtypeStruct((B, 1, D_RNN), F32)],
        scratch_shapes=[pltpu.VMEM((nblk, T + 2 * SUBLANES, LANES), F32)]
        + [pltpu.VMEM((nblk, SUBLANES * chunk, LANES), F32)] * 8,
        compiler_params=_cparams("parallel", "parallel"),
        name=name,
    )(gate, xr, conv_w, conv_b.reshape(1, D_RNN), w_bd, b_bd, lam, h0f, h0b)


def _post_kernel(*refs, n_mix, final, fc):
    x_ref, mod_ref, g2_ref = refs[:3]
    mix = refs[3:3 + 2 * n_mix]
    w1_ref, w2_ref = refs[3 + 2 * n_mix:5 + 2 * n_mix]
    rest = refs[5 + 2 * n_mix:]
    if final:
        gf_ref, o_ref = rest
    else:
        (o_ref,) = rest
    mixed = _dot(mix[0][...], mix[1][...])
    for i in range(1, n_mix):
        mixed = mixed + _dot(mix[2 * i][...], mix[2 * i + 1][...])
    x1 = x_ref[...] + mod_ref[2:3, :] * mixed
    h = _rms(x1, g2_ref[...])
    hb = (h * (1.0 + mod_ref[4:5, :]) + mod_ref[3:4, :]).astype(BF16)
    n_chunks = D_FF // fc
    up = lambda c: _dot(hb, w1_ref[:, c * fc:(c + 1) * fc])
    acc = None
    a = up(0)
    for c in range(n_chunks):
        a_next = up(c + 1) if c + 1 < n_chunks else None
        part = _dot(jnp.square(jnp.maximum(a, 0.0)).astype(BF16), w2_ref[c * fc:(c + 1) * fc, :])
        acc = part if acc is None else acc + part
        a = a_next
    x2 = x1 + mod_ref[5:6, :] * acc
    if final:
        x2 = _rms(x2, gf_ref[...])
    o_ref[...] = x2


def _post(x, mod, g2, mixes, w1, w2, *, tm, ctx, final_g=None, name):
    B, T, D = x.shape
    row = (lambda b, i: (CTX_ROW, 0, 0)) if ctx else (lambda b, i: (b, 0, 0))
    const = lambda a: pl.BlockSpec(a.shape, lambda b, i: (0,) * a.ndim, pipeline_mode=pl.Buffered(1))
    in_specs = [
        pl.BlockSpec((None, tm, D), lambda b, i: (b, i, 0)),
        pl.BlockSpec((None, 6, D), row),
        pl.BlockSpec((1, D), lambda b, i: (0, 0)),
    ]
    args = [x, mod, g2.reshape(1, D)]
    for o, w in mixes:
        in_specs += [pl.BlockSpec((None, tm, o.shape[-1]), lambda b, i: (b, i, 0)), const(w)]
        args += [o, w]
    in_specs += [const(w1), const(w2)]
    args += [w1, w2]
    if final_g is not None:
        in_specs.append(pl.BlockSpec((1, D), lambda b, i: (0, 0)))
        args.append(final_g.reshape(1, D))
    return pl.pallas_call(
        functools.partial(_post_kernel, n_mix=len(mixes), final=final_g is not None, fc=1024),
        grid=(B, T // tm),
        in_specs=in_specs,
        out_specs=pl.BlockSpec((None, tm, D), lambda b, i: (b, i, 0)),
        out_shape=jax.ShapeDtypeStruct((B, T, D), F32),
        compiler_params=_cparams("parallel", "parallel"),
        name=name,
    )(*args)


def _rope_tables(n_tokens):
    rows = n_tokens // GRID_W
    r, cl = jnp.meshgrid(jnp.arange(rows, dtype=F32), jnp.arange(GRID_W, dtype=F32), indexing='ij')
    quarter = HEAD_DIM // 4
    inv = ROPE_BASE ** (-jnp.arange(quarter, dtype=F32) / quarter)
    ang = jnp.stack([r.reshape(-1)[:, None] * inv, cl.reshape(-1)[:, None] * inv], axis=1)
    cos, sin = jnp.cos(ang), jnp.sin(ang)
    cos64 = jnp.concatenate([cos[:, 0], cos[:, 0], cos[:, 1], cos[:, 1]], axis=-1)
    sin64 = jnp.concatenate([-sin[:, 0], sin[:, 0], -sin[:, 1], sin[:, 1]], axis=-1)
    return jnp.tile(cos64, (1, LANES // HEAD_DIM)), jnp.tile(sin64, (1, LANES // HEAD_DIM))


def _att_in_weights(w_in):
    d = w_in.shape[0]
    nq = A_HEADS * HEAD_DIM
    wq = w_in[:, :nq].reshape(d, A_HEADS, HEAD_DIM)
    z = jnp.zeros_like(wq)
    in_first = (jnp.arange(A_HEADS) // A_GROUP == 0)[None, :, None]
    wq = jnp.where(in_first, jnp.concatenate([wq, z], axis=-1), jnp.concatenate([z, wq], axis=-1))
    return jnp.concatenate([wq.reshape(d, QA_W), w_in[:, nq:]], axis=1).astype(BF16)


def _block_diag_weights(w_a, b_a, w_x, b_x):
    w = jnp.concatenate([w_a[0], w_x[0], w_a[1], w_x[1]], axis=-1)
    b = jnp.concatenate([v.reshape(RNN_BLOCKS, 1, RNN_BW) for v in (b_a[0], b_x[0], b_a[1], b_x[1])], axis=-1)
    return (0.5 * w).astype(BF16), 0.5 * b


def kernel(x_prompt, x_sample, cache_a_k, cache_a_v, cache_b_k, cache_b_v, state_fwd, state_bwd, c, c_ctx, norm1, norm2, w_ada, b_ada, w_mlp1, w_mlp2, att_w_in, att_w_out, att_sink, att_lam_qk, att_subln, rec_w_in, rec_conv_w, rec_conv_b, rec_w_a, rec_b_a, rec_w_x, rec_b_x, rec_lam, rec_w_out, final_norm):
    nb, n_seq, _ = x_prompt.shape
    nd, d_seq, _ = x_sample.shape
    past = cache_a_k.shape[2]
    assert nd <= CTX_ROW and DEPTH == 2
    cvec = jnp.concatenate([c, jnp.zeros((CTX_ROW - nd, D_MODEL), F32), c_ctx[None],
                            jnp.zeros((MOD_ROWS - CTX_ROW - 1, D_MODEL), F32)], axis=0)
    mod = _ada_mod(cvec, w_ada, b_ada)
    w1 = w_mlp1.astype(BF16)
    w2 = w_mlp2.astype(BF16)
    tm_ctx = tm_lat = 512
    tok = lambda a: a.reshape(-1, tm_ctx, a.shape[-1])
    seq = lambda a: a.reshape(nb, n_seq, a.shape[-1])

    lam_init = 0.8 - 0.6 * math.exp(-0.3 * 0)
    w_in = _att_in_weights(att_w_in[0])
    w_out = att_w_out[0].astype(BF16)
    nqa = A_HEADS * HEAD_DIM
    c0 = [0, QA_W, QA_W + KA_W, QA_W + 2 * KA_W, QA_W + 2 * KA_W + B_W, QA_W + 2 * KA_W + 2 * B_W]
    widths = [QA_W, KA_W, KA_W, B_W, B_W, B_W]
    scales = [SCALE * LOG2E, 1.0, 1.0, SCALE * LOG2E, 1.0, 1.0]
    roped = [True, True, False, True, True, False]
    sink = att_sink[0]
    lam_qk = att_lam_qk[0]
    subln = att_subln[0].reshape(1, 2 * HEAD_DIM)

    segs_ctx = [(c0[i], widths[i], False, scales[i], False) for i in range(6)]
    qa, ka, va, qb, kb, vb = map(seq, _proj(tok(x_prompt), mod[0], norm1[0], w_in, segs_ctx,
                                            [BF16, F32, F32, BF16, F32, F32], tm=tm_ctx, ctx=True,
                                            name="proj_att_ctx"))
    o_ctx = _ctx_attention(qa, ka, va, qb, kb, vb, sink, lam_qk, subln, lam_init)
    xp = _post(tok(x_prompt), mod[0], norm2[0], [(tok(o_ctx), w_out)], w1[0], w2[0], tm=tm_ctx, ctx=True,
               name="post_att_ctx")
    new_a_k = ka.reshape(nb, 1, n_seq, A_KV_HEADS, HEAD_DIM)
    new_a_v = va.reshape(nb, 1, n_seq, A_KV_HEADS, HEAD_DIM)
    new_b_k = kb.reshape(nb, 1, n_seq, B_HEADS, 2 * HEAD_DIM)
    new_b_v = vb.reshape(nb, 1, n_seq, B_HEADS, 2 * HEAD_DIM)

    segs_lat = [(c0[i], widths[i], roped[i], scales[i], i in (2, 5)) for i in range(6)]
    qa, ka, vat, qb, kb, vbt = _proj(x_sample, mod[0], norm1[0], w_in, segs_lat, [BF16] * 6, tm=tm_lat, ctx=False,
                                     rope_tabs=_rope_tables(d_seq), name="proj_att_lat")
    cka = cache_a_k[:, 0].reshape(nd, past, KA_W).astype(BF16)
    cvat = jnp.swapaxes(cache_a_v[:, 0].reshape(nd, past, KA_W), 1, 2).astype(BF16)
    ckb = cache_b_k[:, 0].reshape(nd, past, B_W).astype(BF16)
    cvbt = jnp.swapaxes(cache_b_v[:, 0].reshape(nd, past, B_W), 1, 2).astype(BF16)
    oa = _lat_a_attention(qa, ka, vat, cka, cvat, sink)
    ob = _lat_b_attention(qb, kb, vbt, ckb, cvbt, lam_qk, subln, lam_init)
    xs = _post(x_sample, mod[0], norm2[0], [(oa, w_out[:nqa]), (ob, w_out[nqa:])], w1[0], w2[0], tm=tm_lat,
               ctx=False, name="post_att_lat")

    w_rin = rec_w_in[0].astype(BF16)
    w_rout = rec_w_out[0].astype(BF16)
    w_bd, b_bd = _block_diag_weights(rec_w_a[0], rec_b_a[0], rec_w_x[0], rec_b_x[0])
    segs_rec = [(0, D_RNN, False, 1.0, False), (D_RNN, D_RNN, False, 1.0, False)]
    zeros = jnp.zeros((nb, 1, D_RNN), F32)

    gate, xr = map(seq, _proj(xp, mod[1], norm1[1], w_rin, segs_rec, [F32, F32], tm=tm_ctx, ctx=True,
                              name="proj_rec_ctx"))
    y, sf, sb = _rec_mixer(gate, xr, zeros, zeros, rec_conv_w[0], rec_conv_b[0], w_bd, b_bd, rec_lam[0],
                           nblk=RNN_BLOCKS, tc=128, name="rec_mixer_ctx")
    y_prompt = seq(_post(xp, mod[1], norm2[1], [(tok(y), w_rout)], w1[1], w2[1], tm=tm_ctx, ctx=True,
                         final_g=final_norm, name="post_rec_ctx"))

    gate, xr = _proj(xs, mod[1], norm1[1], w_rin, segs_rec, [F32, F32], tm=tm_lat, ctx=False, name="proj_rec_lat")
    y, _, _ = _rec_mixer(gate, xr, state_fwd[:, 0:1], state_bwd[:, 0:1], rec_conv_w[0], rec_conv_b[0], w_bd, b_bd,
                         rec_lam[0], nblk=1, tc=256, name="rec_mixer_lat")
    y_sample = _post(xs, mod[1], norm2[1], [(y, w_rout)], w1[1], w2[1], tm=tm_lat, ctx=False, final_g=final_norm,
                     name="post_rec_lat")

    return (y_prompt, y_sample, new_a_k, new_a_v, new_b_k, new_b_v, sf, sb)
```

```python
import functools
import math

import jax
import jax.numpy as jnp
import numpy as np
from jax import lax
from jax.experimental import pallas as pl
from jax.experimental.pallas import tpu as pltpu

F32 = jnp.float32
BF16 = jnp.bfloat16

LANES = 128
SUBLANES = 8
VMEM_LIMIT_BYTES = 56 * 1024 * 1024

D_MODEL = 1024
DEPTH = 2
GRID_W = 64
HEAD_DIM = 64
A_HEADS = 8
A_KV_HEADS = 2
A_GROUP = A_HEADS // A_KV_HEADS
B_HEADS = 4
WINDOW = 128
ROPE_BASE = 10000.0
D_RNN = 1280
RNN_BLOCKS = 10
RNN_BW = D_RNN // RNN_BLOCKS
CONV_W = 4
CONV_LEFT = (CONV_W - 1) // 2
RGLRU_C = 8.0
D_FF = 4 * D_MODEL
EPS = 1e-6
SCALE = HEAD_DIM ** -0.5
NEG = -1e30

QA_W = A_HEADS * LANES
KA_W = A_KV_HEADS * HEAD_DIM
B_W = B_HEADS * 2 * HEAD_DIM
MOD_ROWS = 8
CTX_ROW = 4
LOG2E = math.log2(math.e)
ONES_ROWS = 16
SCAN_MAX_UNROLL = 48


def _cparams(*semantics):
    return pltpu.CompilerParams(dimension_semantics=semantics, vmem_limit_bytes=VMEM_LIMIT_BYTES)


def _dot(a, b):
    return jnp.dot(a, b, preferred_element_type=F32)


def _dot_nt(a, b):
    return lax.dot_general(a, b, (((1,), (1,)), ((), ())), preferred_element_type=F32)


def _rms(x, g):
    return x * lax.rsqrt(jnp.mean(x * x, axis=-1, keepdims=True) + EPS) * g


def _gelu_tanh(x):
    return x * (0.5 * (1.0 + jnp.tanh(math.sqrt(2.0 / math.pi) * (x + 0.044715 * (x * x * x)))))


def _ada_kernel(c_ref, w_ref, b_ref, o_ref):
    c = c_ref[...]
    s = c * jax.nn.sigmoid(c)
    o_ref[...] = _dot(s.astype(BF16), w_ref[...].astype(BF16)) + b_ref[...]


def _ada_mod(cvec, w_ada, b_ada):
    tn = 1536
    out = pl.pallas_call(
        _ada_kernel,
        grid=(DEPTH, 6 * D_MODEL // tn),
        in_specs=[
            pl.BlockSpec((MOD_ROWS, D_MODEL), lambda l, j: (0, 0)),
            pl.BlockSpec((None, D_MODEL, tn), lambda l, j: (l, 0, j)),
            pl.BlockSpec((None, 1, tn), lambda l, j: (l, 0, j)),
        ],
        out_specs=pl.BlockSpec((None, MOD_ROWS, tn), lambda l, j: (l, 0, j)),
        out_shape=jax.ShapeDtypeStruct((DEPTH, MOD_ROWS, 6 * D_MODEL), F32),
        compiler_params=_cparams("parallel", "parallel"),
        name="ada_mod",
    )(cvec, w_ada, b_ada.reshape(DEPTH, 1, 6 * D_MODEL))
    return out.reshape(DEPTH, MOD_ROWS, 6, D_MODEL)


def _proj_kernel(*refs, segs, rope):
    if rope:
        x_ref, mod_ref, g_ref, w_ref, cos_ref, sin_ref, *outs = refs
    else:
        x_ref, mod_ref, g_ref, w_ref, *outs = refs
    x = x_ref[...]
    h = _rms(x, g_ref[...])
    h = h * (1.0 + mod_ref[1:2, :]) + mod_ref[0:1, :]
    hb = h.astype(BF16)
    tm = x.shape[0]
    if rope:
        lane = lax.broadcasted_iota(jnp.int32, (tm, LANES), 1)
        first = (lane & 16) == 0
        cos = cos_ref[...]
        sin = sin_ref[...]
    for (c0, width, do_rope, scale, transposed, gelu), o_ref in zip(segs, outs):
        y = _dot(hb, w_ref[:, c0:c0 + width])
        for t in range(width // LANES):
            yt = y[:, t * LANES:(t + 1) * LANES]
            if do_rope:
                sw = jnp.where(first, pltpu.roll(yt, LANES - 16, 1), pltpu.roll(yt, 16, 1))
                yt = yt * cos + sw * sin
            if scale != 1.0:
                yt = yt * scale
            if gelu:
                yt = _gelu_tanh(yt)
            if transposed:
                o_ref[t * LANES:(t + 1) * LANES, :] = yt.T.astype(o_ref.dtype)
            else:
                o_ref[:, t * LANES:(t + 1) * LANES] = yt.astype(o_ref.dtype)


def _proj(x, mod, g, w, segs, out_dtypes, *, tm, ctx, rope_tabs=None, name):
    B, T, D = x.shape
    row = (lambda b, i: (CTX_ROW, 0, 0)) if ctx else (lambda b, i: (b, 0, 0))
    in_specs = [
        pl.BlockSpec((None, tm, D), lambda b, i: (b, i, 0)),
        pl.BlockSpec((None, 6, D), row),
        pl.BlockSpec((1, D), lambda b, i: (0, 0)),
        pl.BlockSpec(w.shape, lambda b, i: (0, 0)),
    ]
    args = [x, mod, g.reshape(1, D), w]
    if rope_tabs is not None:
        in_specs += [pl.BlockSpec((tm, LANES), lambda b, i: (i, 0))] * 2
        args += list(rope_tabs)
    out_specs = [pl.BlockSpec((None, s[1], tm), lambda b, i: (b, 0, i)) if s[4] else
                 pl.BlockSpec((None, tm, s[1]), lambda b, i: (b, i, 0)) for s in segs]
    out_shape = [jax.ShapeDtypeStruct((B, s[1], T) if s[4] else (B, T, s[1]), dt)
                 for s, dt in zip(segs, out_dtypes)]
    return pl.pallas_call(
        functools.partial(_proj_kernel, segs=tuple(segs), rope=rope_tabs is not None),
        grid=(B, T // tm),
        in_specs=in_specs,
        out_specs=out_specs,
        out_shape=out_shape,
        compiler_params=_cparams("parallel", "parallel"),
        name=name,
    )(*args)


def _diff_lambda(lq, lam_init):
    s1 = jnp.sum(lq[0:1, :] * lq[1:2, :], axis=1, keepdims=True)
    s2 = jnp.sum(lq[2:3, :] * lq[3:4, :], axis=1, keepdims=True)
    return jnp.exp(s1) - jnp.exp(s2) + lam_init


def _stack_group_queries(qa_ref, g, rows):
    return jnp.concatenate(
        [qa_ref[rows, (A_GROUP * g + hh) * LANES:(A_GROUP * g + hh + 1) * LANES] for hh in range(A_GROUP)], axis=0)


def _stack_pair_queries(q):
    lo = lax.broadcasted_iota(jnp.int32, q.shape, 1) < HEAD_DIM
    zero = jnp.zeros_like(q)
    return jnp.concatenate([jnp.where(lo, q, zero), jnp.where(lo, zero, q)], axis=0)


def _with_ones_rows(vt):
    return jnp.concatenate([vt, jnp.ones((ONES_ROWS, vt.shape[1]), BF16)], axis=0)


def _sink_row(sink_ref, g, tq):
    return jnp.concatenate(
        [jnp.full((1, tq), sink_ref[A_GROUP * g + hh] * LOG2E, F32) for hh in range(A_GROUP)], axis=1)


def _softmax_values(parts, sink=None):
    m = functools.reduce(jnp.maximum, [jnp.max(s, axis=0, keepdims=True) for s, _ in parts])
    if sink is not None:
        m = jnp.maximum(m, sink)
    ot = sum(_dot(vt1, jnp.exp2(s - m).astype(BF16)) for s, vt1 in parts)
    den = ot[LANES:LANES + 1]
    if sink is not None:
        den = den + jnp.exp2(sink - m)
    return ot[:LANES] / den


def _a_heads(ot, g, tq):
    return [ot[g * HEAD_DIM:(g + 1) * HEAD_DIM, hh * tq:(hh + 1) * tq] for hh in range(A_GROUP)]


def _subln(o, subln, lam_init):
    return _rms(o, subln) * (1.0 - lam_init)


def _diff_combine(ot, lam, subln, lam_init):
    tq = ot.shape[1] // 2
    return _subln((ot[:, :tq] - lam * ot[:, tq:]).T, subln, lam_init)


def _ctx_attn_kernel(sink_ref, qa_ref, ka_ref, va_ref, qb_ref, kb_ref, vb_ref, lamqk_ref, subln_ref, o_ref, *,
                     lam_init):
    T = qa_ref.shape[0]
    ka = ka_ref[...].astype(BF16)
    vat1 = _with_ones_rows(va_ref[...].T.astype(BF16))
    scores_a = [_dot_nt(ka, _stack_group_queries(qa_ref, g, slice(None))) for g in range(A_KV_HEADS)]
    scores_b, vbt1 = [], []
    for h in range(B_HEADS):
        sl = slice(h * LANES, (h + 1) * LANES)
        scores_b.append(_dot_nt(kb_ref[:, sl].astype(BF16), _stack_pair_queries(qb_ref[:, sl])))
        vbt1.append(_with_ones_rows(vb_ref[:, sl].T.astype(BF16)))
    heads = []
    for g in range(A_KV_HEADS):
        heads += _a_heads(_softmax_values([(scores_a[g], vat1)], _sink_row(sink_ref, g, T)), g, T)
    nqa = A_HEADS * HEAD_DIM
    o_ref[:, :nqa] = jnp.concatenate(heads, axis=0).T.astype(o_ref.dtype)
    lam = _diff_lambda(lamqk_ref[...], lam_init)
    for h in range(B_HEADS):
        ot = _softmax_values([(scores_b[h], vbt1[h])])
        o_ref[:, nqa + h * LANES:nqa + (h + 1) * LANES] = (
            _diff_combine(ot, lam, subln_ref[...], lam_init).astype(o_ref.dtype))


def _ctx_attention(qa, ka, va, qb, kb, vb, sink, lam_qk, subln, lam_init):
    B, T, _ = qa.shape
    blk = lambda w: pl.BlockSpec((None, T, w), lambda b: (b, 0, 0))
    full = lambda a: pl.BlockSpec(a.shape, lambda b: (0,) * a.ndim)
    return pl.pallas_call(
        functools.partial(_ctx_attn_kernel, lam_init=lam_init),
        grid=(B,),
        in_specs=[pl.BlockSpec(memory_space=pltpu.SMEM), blk(QA_W), blk(KA_W), blk(KA_W), blk(B_W), blk(B_W),
                  blk(B_W), full(lam_qk), full(subln)],
        out_specs=blk(A_HEADS * HEAD_DIM + B_W),
        out_shape=jax.ShapeDtypeStruct((B, T, A_HEADS * HEAD_DIM + B_W), BF16),
        compiler_params=_cparams("parallel"),
        name="ctx_attention",
    )(sink, qa, ka, va, qb, kb, vb, lam_qk, subln)


def _lat_a_kernel(sink_ref, qa_ref, k_ref, vt_ref, ck_ref, cvt_ref, o_ref, *, tq, nq, band):
    T = k_ref.shape[0]
    cols = A_GROUP * tq
    cvt1 = _with_ones_rows(cvt_ref[...])
    jobs = []
    for i in range(nq):
        qi = pl.program_id(1) * nq + i
        start = pl.multiple_of(jnp.clip(qi * tq - WINDOW, 0, T - band), WINDOW)
        kb = k_ref[pl.ds(start, band), :]
        vbt1 = _with_ones_rows(vt_ref[:, pl.ds(start, band)])
        kpos = start + lax.broadcasted_iota(jnp.int32, (band, cols), 0)
        qpos = qi * tq + lax.broadcasted_iota(jnp.int32, (band, cols), 1) % tq
        keep = jnp.abs(qpos - kpos) <= WINDOW
        for g in range(A_KV_HEADS):
            qg = _stack_group_queries(qa_ref, g, slice(i * tq, (i + 1) * tq))
            jobs.append((_dot_nt(ck_ref[...], qg), _dot_nt(kb, qg), keep, vbt1))
    for i in range(nq):
        heads = []
        for g in range(A_KV_HEADS):
            s_c, s_b, keep, vbt1 = jobs[i * A_KV_HEADS + g]
            ot = _softmax_values([(s_c, cvt1), (jnp.where(keep, s_b, NEG), vbt1)], _sink_row(sink_ref, g, tq))
            heads += _a_heads(ot, g, tq)
        o_ref[i * tq:(i + 1) * tq, :] = jnp.concatenate(heads, axis=0).T.astype(o_ref.dtype)


def _lat_a_attention(qa, ka, vat, cka, cvat, sink):
    B, T, _ = qa.shape
    L = cka.shape[1]
    tq, nq = WINDOW, 4
    band = 3 * WINDOW
    return pl.pallas_call(
        functools.partial(_lat_a_kernel, tq=tq, nq=nq, band=band),
        grid=(B, T // (tq * nq)),
        in_specs=[
            pl.BlockSpec(memory_space=pltpu.SMEM),
            pl.BlockSpec((None, tq * nq, QA_W), lambda b, i: (b, i, 0)),
            pl.BlockSpec((None, T, KA_W), lambda b, i: (b, 0, 0)),
            pl.BlockSpec((None, KA_W, T), lambda b, i: (b, 0, 0)),
            pl.BlockSpec((None, L, KA_W), lambda b, i: (b, 0, 0)),
            pl.BlockSpec((None, KA_W, L), lambda b, i: (b, 0, 0)),
        ],
        out_specs=pl.BlockSpec((None, tq * nq, A_HEADS * HEAD_DIM), lambda b, i: (b, i, 0)),
        out_shape=jax.ShapeDtypeStruct((B, T, A_HEADS * HEAD_DIM), BF16),
        compiler_params=_cparams("parallel", "parallel"),
        name="lat_a_attention",
    )(sink, qa, ka, vat, cka, cvat)


def _lat_b_kernel(q_ref, k_ref, vt_ref, ck_ref, cvt_ref, lamqk_ref, subln_ref, o_ref, *, tk, ahead, lam_init):
    tq = q_ref.shape[0]
    T = k_ref.shape[0]
    qs = _stack_pair_queries(q_ref[...])
    blocks = [(ck_ref[...], cvt_ref[...])]
    blocks += [(k_ref[j * tk:(j + 1) * tk, :], vt_ref[:, j * tk:(j + 1) * tk]) for j in range(T // tk)]

    def scores(k):
        return _dot_nt(k, qs)

    def accumulate(s, vt, carry):
        m, acc = carry
        m_new = jnp.maximum(m, jnp.max(s, axis=0, keepdims=True))
        p = jnp.exp2(s - m_new).astype(BF16)
        acc = jnp.exp2(m - m_new) * acc + _dot(_with_ones_rows(vt), p)
        return m_new, acc

    carry = (jnp.full((1, 2 * tq), -jnp.inf, F32), jnp.zeros((LANES + ONES_ROWS, 2 * tq), F32))
    pending = [scores(blocks[j][0]) for j in range(ahead)]
    for j in range(len(blocks)):
        if j + ahead < len(blocks):
            pending.append(scores(blocks[j + ahead][0]))
        carry = accumulate(pending.pop(0), blocks[j][1], carry)
    _, acc = carry
    ot = acc[:LANES] / acc[LANES:LANES + 1]
    lam = _diff_lambda(lamqk_ref[...], lam_init)
    o_ref[...] = _diff_combine(ot, lam, subln_ref[...], lam_init).astype(o_ref.dtype)


def _lat_b_attention(qb, kb, vbt, ckb, cvbt, lam_qk, subln, lam_init):
    B, T, _ = qb.shape
    L = ckb.shape[1]
    tq = 1024
    full = lambda a: pl.BlockSpec(a.shape, lambda b, h, i: (0,) * a.ndim)
    return pl.pallas_call(
        functools.partial(_lat_b_kernel, tk=512, ahead=2, lam_init=lam_init),
        grid=(B, B_HEADS, T // tq),
        in_specs=[
            pl.BlockSpec((None, tq, LANES), lambda b, h, i: (b, i, h)),
            pl.BlockSpec((None, T, LANES), lambda b, h, i: (b, 0, h)),
            pl.BlockSpec((None, LANES, T), lambda b, h, i: (b, h, 0)),
            pl.BlockSpec((None, L, LANES), lambda b, h, i: (b, 0, h)),
            pl.BlockSpec((None, LANES, L), lambda b, h, i: (b, h, 0)),
            full(lam_qk), full(subln),
        ],
        out_specs=pl.BlockSpec((None, tq, LANES), lambda b, h, i: (b, i, h)),
        out_shape=jax.ShapeDtypeStruct((B, T, B_W), BF16),
        compiler_params=_cparams("parallel", "parallel", "parallel"),
        name="lat_b_attention",
    )(qb, kb, vbt, ckb, cvbt, lam_qk, subln)


def _scan_chunk_len(n_steps):
    chunk = -(-n_steps // SUBLANES)
    while chunk % 8 != 4:
        chunk += 1
    return chunk


def _rec_kernel(gate_ref, xr_ref, cw_ref, cb_ref, wbd_ref, bbd_ref, lam_ref, h0f_ref, h0b_ref,
                y_ref, sf_ref, sb_ref, xp, af, uf, ab, ub, pf_s, hf_s, pb_s, hb_s, *, nblk, **kw):
    for n in range(nblk):
        sl = slice(n * LANES, (n + 1) * LANES)
        _rec_block(gate_ref.at[:, sl], xr_ref.at[:, sl], cw_ref.at[:, sl], cb_ref.at[:, sl], wbd_ref.at[n],
                   bbd_ref.at[n], lam_ref.at[:, sl], h0f_ref.at[:, sl], h0b_ref.at[:, sl], y_ref.at[:, sl],
                   sf_ref.at[:, sl], sb_ref.at[:, sl], xp.at[n], af.at[n], uf.at[n], ab.at[n], ub.at[n],
                   pf_s.at[n], hf_s.at[n], pb_s.at[n], hb_s.at[n], **kw)


def _rec_block(gate_ref, xr_ref, cw_ref, cb_ref, wbd_ref, bbd_ref, lam_ref, h0f_ref, h0b_ref,
               y_ref, sf_ref, sb_ref, xp, af, uf, ab, ub, pf_s, hf_s, pb_s, hb_s, *, tc, gates_unroll, chunk, unroll):
    T = xr_ref.shape[0]
    pad = SUBLANES
    xp[0:pad, :] = jnp.zeros((pad, LANES), F32)
    xp[T + pad:T + 2 * pad, :] = jnp.zeros((pad, LANES), F32)
    xp[pad:T + pad, :] = xr_ref[...]
    tail = SUBLANES * chunk - T
    for a_s, u_s in ((af, uf), (ab, ub)):
        a_s[T:T + tail, :] = jnp.ones((tail, LANES), F32)
        u_s[T:T + tail, :] = jnp.zeros((tail, LANES), F32)
    cw = cw_ref[...]
    cb = cb_ref[...]
    nl = -lam_ref[...]
    softplus = jnp.maximum(nl, 0.0) + jnp.log1p(jnp.exp(-jnp.abs(nl)))
    cp = (0.5 * RGLRU_C) * softplus

    def gates(it, _):
        for q in range(gates_unroll):
            gate_chunk(it * gates_unroll + q)
        return 0

    def gate_chunk(ci):
        t0 = pl.multiple_of(ci * tc, tc)
        y = cb
        for j in range(CONV_W):
            y = y + xp[pl.ds(t0 + (pad - CONV_LEFT + j), tc), :] * cw[j:j + 1, :]
        t = jnp.tanh(_dot(y.astype(BF16), wbd_ref[...]) + bbd_ref[...])
        yh = 0.5 * y
        for d, (a_s, u_s) in enumerate(((af, uf), (ab, ub))):
            t_r = t[:, 2 * d * LANES:(2 * d + 1) * LANES]
            t_i = t[:, (2 * d + 1) * LANES:(2 * d + 2) * LANES]
            neg_log_a = cp[d:d + 1, :] * t_r + cp[d:d + 1, :]
            a = jnp.exp2(neg_log_a * (-LOG2E))
            w = jnp.tanh(neg_log_a) * (a * a + 1.0)
            sqrt_w = jnp.where(w > 0.0, w * lax.rsqrt(w), 0.0)
            a_s[pl.ds(t0, tc), :] = a
            u_s[pl.ds(t0, tc), :] = (t_i + 1.0) * (yh * sqrt_w)

    lax.fori_loop(0, T // (tc * gates_unroll), gates, 0)

    def rows(k):
        return pl.ds(k, SUBLANES, stride=chunk)

    def local_scan(it, carry):
        hf, pf, hb, pb = carry
        for q in range(unroll):
            k = it * unroll + q
            a = af[rows(k), :]
            hf = a * hf + uf[rows(k), :]
            pf = a * pf
            hf_s[rows(k), :] = hf
            pf_s[rows(k), :] = pf
            kb = chunk - 1 - k
            a = ab[rows(kb), :]
            hb = a * hb + ub[rows(kb), :]
            pb = a * pb
            hb_s[rows(kb), :] = hb
            pb_s[rows(kb), :] = pb
        return hf, pf, hb, pb

    zero = jnp.zeros((SUBLANES, LANES), F32)
    one = jnp.ones((SUBLANES, LANES), F32)
    hf, pf, hb, pb = lax.fori_loop(0, chunk // unroll, local_scan, (zero, one, zero, one))

    cf = [h0f_ref[...]]
    for r in range(SUBLANES - 1):
        cf.append(hf[r:r + 1, :] + pf[r:r + 1, :] * cf[r])
    cb_rev = [h0b_ref[...]]
    for r in range(SUBLANES - 1, 0, -1):
        cb_rev.append(hb[r:r + 1, :] + pb[r:r + 1, :] * cb_rev[-1])
    carry_f = jnp.concatenate(cf, axis=0)
    carry_b = jnp.concatenate(cb_rev[::-1], axis=0)

    def apply_carry(it, _):
        for q in range(unroll):
            k = it * unroll + q
            uf[rows(k), :] = hf_s[rows(k), :] + pf_s[rows(k), :] * carry_f
            ub[rows(k), :] = hb_s[rows(k), :] + pb_s[rows(k), :] * carry_b
        return 0

    lax.fori_loop(0, chunk // unroll, apply_carry, 0)
    sf_ref[...] = uf[T - 1:T, :]
    sb_ref[...] = ub[0:1, :]

    def combine(ci, _):
        t0 = pl.multiple_of(ci * tc, tc)
        y_ref[pl.ds(t0, tc), :] = ((uf[pl.ds(t0, tc), :] + ub[pl.ds(t0, tc), :])
                                   * gate_ref[pl.ds(t0, tc), :]).astype(y_ref.dtype)
        return 0

    lax.fori_loop(0, T // tc, combine, 0)


def _rec_mixer(gate, xr, h0f, h0b, conv_w, conv_b, w_bd, b_bd, lam, *, nblk, tc, name):
    B, T, _ = xr.shape
    chunk = _scan_chunk_len(T)
    wid = nblk * LANES
    col = lambda rows: pl.BlockSpec((rows, wid), lambda b, n: (0, n))
    seq = pl.BlockSpec((None, T, wid), lambda b, n: (b, 0, n))
    st = pl.BlockSpec((None, 1, wid), lambda b, n: (b, 0, n))
    return pl.pallas_call(
        functools.partial(_rec_kernel, nblk=nblk, tc=tc, gates_unroll=2, chunk=chunk,
                          unroll=max(u for u in range(1, SCAN_MAX_UNROLL + 1) if chunk % u == 0)),
        grid=(B, RNN_BLOCKS // nblk),
        in_specs=[seq, seq, col(CONV_W), col(1),
                  pl.BlockSpec((nblk, LANES, 4 * LANES), lambda b, n: (n, 0, 0)),
                  pl.BlockSpec((nblk, 1, 4 * LANES), lambda b, n: (n, 0, 0)),
                  col(2), st, st],
        out_specs=[seq, st, st],
        out_shape=[jax.ShapeDtypeStruct((B, T, D_RNN), BF16),
                   jax.ShapeDtypeStruct((B, 1, D_RNN), F32),
                   jax.ShapeDtypeStruct((B, 1, D_RNN), F32)],
        scratch_shapes=[pltpu.VMEM((nblk, T + 2 * SUBLANES, LANES), F32)]
        + [pltpu.VMEM((nblk, SUBLANES * chunk, LANES), F32)] * 8,
        compiler_params=_cparams("parallel", "parallel"),
        name=name,
    )(gate, xr, conv_w, conv_b.reshape(1, D_RNN), w_bd, b_bd, lam, h0f, h0b)


def _post_kernel(*refs, n_mix, final, fc):
    x_ref, mod_ref, g2_ref = refs[:3]
    mix = refs[3:3 + 2 * n_mix]
    w1_ref, w2_ref = refs[3 + 2 * n_mix:5 + 2 * n_mix]
    rest = refs[5 + 2 * n_mix:]
    if final:
        gf_ref, o_ref = rest
    else:
        (o_ref,) = rest
    mixed = _dot(mix[0][...], mix[1][...])
    for i in range(1, n_mix):
        mixed = mixed + _dot(mix[2 * i][...], mix[2 * i + 1][...])
    x1 = x_ref[...] + mod_ref[2:3, :] * mixed
    h = _rms(x1, g2_ref[...])
    hb = (h * (1.0 + mod_ref[4:5, :]) + mod_ref[3:4, :]).astype(BF16)
    n_chunks = D_FF // fc
    up = lambda c: _dot(hb, w1_ref[:, c * fc:(c + 1) * fc])
    acc = None
    a = up(0)
    for c in range(n_chunks):
        a_next = up(c + 1) if c + 1 < n_chunks else None
        part = _dot(jnp.square(jnp.maximum(a, 0.0)).astype(BF16), w2_ref[c * fc:(c + 1) * fc, :])
        acc = part if acc is None else acc + part
        a = a_next
    x2 = x1 + mod_ref[5:6, :] * acc
    if final:
        x2 = _rms(x2, gf_ref[...])
    o_ref[...] = x2


def _post(x, mod, g2, mixes, w1, w2, *, tm, ctx, final_g=None, name):
    B, T, D = x.shape
    row = (lambda b, i: (CTX_ROW, 0, 0)) if ctx else (lambda b, i: (b, 0, 0))
    const = lambda a: pl.BlockSpec(a.shape, lambda b, i: (0,) * a.ndim, pipeline_mode=pl.Buffered(1))
    in_specs = [
        pl.BlockSpec((None, tm, D), lambda b, i: (b, i, 0)),
        pl.BlockSpec((None, 6, D), row),
        pl.BlockSpec((1, D), lambda b, i: (0, 0)),
    ]
    args = [x, mod, g2.reshape(1, D)]
    for o, w in mixes:
        in_specs += [pl.BlockSpec((None, tm, o.shape[-1]), lambda b, i: (b, i, 0)), const(w)]
        args += [o, w]
    in_specs += [const(w1), const(w2)]
    args += [w1, w2]
    if final_g is not None:
        in_specs.append(pl.BlockSpec((1, D), lambda b, i: (0, 0)))
        args.append(final_g.reshape(1, D))
    return pl.pallas_call(
        functools.partial(_post_kernel, n_mix=len(mixes), final=final_g is not None, fc=1024),
        grid=(B, T // tm),
        in_specs=in_specs,
        out_specs=pl.BlockSpec((None, tm, D), lambda b, i: (b, i, 0)),
        out_shape=jax.ShapeDtypeStruct((B, T, D), F32),
        compiler_params=_cparams("parallel", "parallel"),
        name=name,
    )(*args)


def _rope_tables(n_tokens):
    rows = n_tokens // GRID_W
    r, cl = jnp.meshgrid(jnp.arange(rows, dtype=F32), jnp.arange(GRID_W, dtype=F32), indexing='ij')
    quarter = HEAD_DIM // 4
    inv = ROPE_BASE ** (-jnp.arange(quarter, dtype=F32) / quarter)
    ang = jnp.stack([r.reshape(-1)[:, None] * inv, cl.reshape(-1)[:, None] * inv], axis=1)
    cos, sin = jnp.cos(ang), jnp.sin(ang)
    cos64 = jnp.concatenate([cos[:, 0], cos[:, 0], cos[:, 1], cos[:, 1]], axis=-1)
    sin64 = jnp.concatenate([-sin[:, 0], sin[:, 0], -sin[:, 1], sin[:, 1]], axis=-1)
    return jnp.tile(cos64, (1, LANES // HEAD_DIM)), jnp.tile(sin64, (1, LANES // HEAD_DIM))


def _att_in_weights(w_in):
    d = w_in.shape[0]
    nq = A_HEADS * HEAD_DIM
    wq = w_in[:, :nq].reshape(d, A_HEADS, HEAD_DIM)
    z = jnp.zeros_like(wq)
    in_first = (jnp.arange(A_HEADS) // A_GROUP == 0)[None, :, None]
    wq = jnp.where(in_first, jnp.concatenate([wq, z], axis=-1), jnp.concatenate([z, wq], axis=-1))
    return jnp.concatenate([wq.reshape(d, QA_W), w_in[:, nq:]], axis=1).astype(BF16)


def _block_diag_weights(w_a, b_a, w_x, b_x):
    w = jnp.concatenate([w_a[0], w_x[0], w_a[1], w_x[1]], axis=-1)
    b = jnp.concatenate([v.reshape(RNN_BLOCKS, 1, RNN_BW) for v in (b_a[0], b_x[0], b_a[1], b_x[1])], axis=-1)
    return (0.5 * w).astype(BF16), 0.5 * b


def kernel(x_prompt, x_sample, cache_a_k, cache_a_v, cache_b_k, cache_b_v, state_fwd, state_bwd, c, c_ctx, norm1, norm2, w_ada, b_ada, w_mlp1, w_mlp2, att_w_in, att_w_out, att_sink, att_lam_qk, att_subln, rec_w_in, rec_conv_w, rec_conv_b, rec_w_a, rec_b_a, rec_w_x, rec_b_x, rec_lam, rec_w_out, final_norm):
    nb, n_seq, _ = x_prompt.shape
    nd, d_seq, _ = x_sample.shape
    past = cache_a_k.shape[2]
    assert nd <= CTX_ROW and DEPTH == 2
    cvec = jnp.concatenate([c, jnp.zeros((CTX_ROW - nd, D_MODEL), F32), c_ctx[None],
                            jnp.zeros((MOD_ROWS - CTX_ROW - 1, D_MODEL), F32)], axis=0)
    mod = _ada_mod(cvec, w_ada, b_ada)
    w1 = w_mlp1.astype(BF16)
    w2 = w_mlp2.astype(BF16)
    tm_ctx = tm_lat = 512
    tok = lambda a: a.reshape(-1, tm_ctx, a.shape[-1])
    seq = lambda a: a.reshape(nb, n_seq, a.shape[-1])

    lam_init = 0.8 - 0.6 * math.exp(-0.3 * 0)
    w_in = _att_in_weights(att_w_in[0])
    w_out = att_w_out[0].astype(BF16)
    nqa = A_HEADS * HEAD_DIM
    c0 = [0, QA_W, QA_W + KA_W, QA_W + 2 * KA_W, QA_W + 2 * KA_W + B_W, QA_W + 2 * KA_W + 2 * B_W]
    widths = [QA_W, KA_W, KA_W, B_W, B_W, B_W]
    scales = [SCALE * LOG2E, 1.0, 1.0, SCALE * LOG2E, 1.0, 1.0]
    roped = [True, True, False, True, True, False]
    sink = att_sink[0]
    lam_qk = att_lam_qk[0]
    subln = att_subln[0].reshape(1, 2 * HEAD_DIM)

    segs_ctx = [(c0[i], widths[i], False, scales[i], False, False) for i in range(6)]
    qa, ka, va, qb, kb, vb = map(seq, _proj(tok(x_prompt), mod[0], norm1[0], w_in, segs_ctx,
                                            [BF16, F32, F32, BF16, F32, F32], tm=tm_ctx, ctx=True,
                                            name="proj_att_ctx"))
    o_ctx = _ctx_attention(qa, ka, va, qb, kb, vb, sink, lam_qk, subln, lam_init)
    xp = _post(tok(x_prompt), mod[0], norm2[0], [(tok(o_ctx), w_out)], w1[0], w2[0], tm=tm_ctx, ctx=True,
               name="post_att_ctx")
    new_a_k = ka.reshape(nb, 1, n_seq, A_KV_HEADS, HEAD_DIM)
    new_a_v = va.reshape(nb, 1, n_seq, A_KV_HEADS, HEAD_DIM)
    new_b_k = kb.reshape(nb, 1, n_seq, B_HEADS, 2 * HEAD_DIM)
    new_b_v = vb.reshape(nb, 1, n_seq, B_HEADS, 2 * HEAD_DIM)

    segs_lat = [(c0[i], widths[i], roped[i], scales[i], i in (2, 5), False) for i in range(6)]
    qa, ka, vat, qb, kb, vbt = _proj(x_sample, mod[0], norm1[0], w_in, segs_lat, [BF16] * 6, tm=tm_lat, ctx=False,
                                     rope_tabs=_rope_tables(d_seq), name="proj_att_lat")
    cka = cache_a_k[:, 0].reshape(nd, past, KA_W).astype(BF16)
    cvat = jnp.swapaxes(cache_a_v[:, 0].reshape(nd, past, KA_W), 1, 2).astype(BF16)
    ckb = cache_b_k[:, 0].reshape(nd, past, B_W).astype(BF16)
    cvbt = jnp.swapaxes(cache_b_v[:, 0].reshape(nd, past, B_W), 1, 2).astype(BF16)
    oa = _lat_a_attention(qa, ka, vat, cka, cvat, sink)
    ob = _lat_b_attention(qb, kb, vbt, ckb, cvbt, lam_qk, subln, lam_init)
    xs = _post(x_sample, mod[0], norm2[0], [(oa, w_out[:nqa]), (ob, w_out[nqa:])], w1[0], w2[0], tm=tm_lat,
               ctx=False, name="post_att_lat")

    w_rin = rec_w_in[0].astype(BF16)
    w_rout = rec_w_out[0].astype(BF16)
    w_bd, b_bd = _block_diag_weights(rec_w_a[0], rec_b_a[0], rec_w_x[0], rec_b_x[0])
    segs_rec = [(0, D_RNN, False, 1.0, False, True), (D_RNN, D_RNN, False, 1.0, False, False)]
    zeros = jnp.zeros((nb, 1, D_RNN), F32)

    gate, xr = map(seq, _proj(xp, mod[1], norm1[1], w_rin, segs_rec, [F32, F32], tm=tm_ctx, ctx=True,
                              name="proj_rec_ctx"))
    y, sf, sb = _rec_mixer(gate, xr, zeros, zeros, rec_conv_w[0], rec_conv_b[0], w_bd, b_bd, rec_lam[0],
                           nblk=RNN_BLOCKS, tc=128, name="rec_mixer_ctx")
    y_prompt = seq(_post(xp, mod[1], norm2[1], [(tok(y), w_rout)], w1[1], w2[1], tm=tm_ctx, ctx=True,
                         final_g=final_norm, name="post_rec_ctx"))

    gate, xr = _proj(xs, mod[1], norm1[1], w_rin, segs_rec, [F32, F32], tm=tm_lat, ctx=False, name="proj_rec_lat")
    y, _, _ = _rec_mixer(gate, xr, state_fwd[:, 0:1], state_bwd[:, 0:1], rec_conv_w[0], rec_conv_b[0], w_bd, b_bd,
                         rec_lam[0], nblk=1, tc=256, name="rec_mixer_lat")
    y_sample = _post(xs, mod[1], norm2[1], [(y, w_rout)], w1[1], w2[1], tm=tm_lat, ctx=False, final_g=final_norm,
                     name="post_rec_lat")

    return (y_prompt, y_sample, new_a_k, new_a_v, new_b_k, new_b_v, sf, sb)
```

```python
import functools
import math
from typing import NamedTuple

import jax
import jax.numpy as jnp
import numpy as np
from jax import lax
from jax.experimental import pallas as pl
from jax.experimental.pallas import tpu as pltpu

F32 = jnp.float32
BF16 = jnp.bfloat16

LANES = 128
SUBLANES = 8
VMEM_LIMIT_BYTES = 56 * 1024 * 1024

D_MODEL = 1024
DEPTH = 2
GRID_W = 64
HEAD_DIM = 64
A_HEADS = 8
A_KV_HEADS = 2
A_GROUP = A_HEADS // A_KV_HEADS
B_HEADS = 4
WINDOW = 128
ROPE_BASE = 10000.0
D_RNN = 1280
RNN_BLOCKS = 10
RNN_BW = D_RNN // RNN_BLOCKS
CONV_W = 4
CONV_LEFT = (CONV_W - 1) // 2
RGLRU_C = 8.0
D_FF = 4 * D_MODEL
EPS = 1e-6
SCALE = HEAD_DIM ** -0.5
NEG = -1e30

QA_W = A_HEADS * LANES
KA_W = A_KV_HEADS * HEAD_DIM
B_W = B_HEADS * 2 * HEAD_DIM
MOD_ROWS = 8
CTX_ROW = 4
LOG2E = math.log2(math.e)
ONES_ROWS = 16
SCAN_MAX_UNROLL = 1024


def _cparams(*semantics):
    return pltpu.CompilerParams(dimension_semantics=semantics, vmem_limit_bytes=VMEM_LIMIT_BYTES)


def _dot(a, b):
    return jnp.dot(a, b, preferred_element_type=F32)


def _dot_nt(a, b):
    return lax.dot_general(a, b, (((1,), (1,)), ((), ())), preferred_element_type=F32)


def _rms(x, g):
    return x * lax.rsqrt(jnp.mean(x * x, axis=-1, keepdims=True) + EPS) * g


def _gelu_tanh(x):
    return x * (0.5 * (1.0 + jnp.tanh(math.sqrt(2.0 / math.pi) * (x + 0.044715 * (x * x * x)))))


def _ada_kernel(c_ref, w_ref, b_ref, o_ref):
    c = c_ref[...]
    s = c * jax.nn.sigmoid(c)
    o_ref[...] = _dot(s.astype(BF16), w_ref[...].astype(BF16)) + b_ref[...]


def _ada_mod(cvec, w_ada, b_ada):
    tn = 1536
    out = pl.pallas_call(
        _ada_kernel,
        grid=(DEPTH, 6 * D_MODEL // tn),
        in_specs=[
            pl.BlockSpec((MOD_ROWS, D_MODEL), lambda l, j: (0, 0)),
            pl.BlockSpec((None, D_MODEL, tn), lambda l, j: (l, 0, j)),
            pl.BlockSpec((None, 1, tn), lambda l, j: (l, 0, j)),
        ],
        out_specs=pl.BlockSpec((None, MOD_ROWS, tn), lambda l, j: (l, 0, j)),
        out_shape=jax.ShapeDtypeStruct((DEPTH, MOD_ROWS, 6 * D_MODEL), F32),
        compiler_params=_cparams("parallel", "parallel"),
        name="ada_mod",
    )(cvec, w_ada, b_ada.reshape(DEPTH, 1, 6 * D_MODEL))
    return out.reshape(DEPTH, MOD_ROWS, 6, D_MODEL)


class _Seg(NamedTuple):
    col0: int
    width: int
    outs: tuple
    rope: bool = False
    scale: float = 1.0
    gelu: bool = False


def _proj_kernel(*refs, segs, rope):
    if rope:
        x_ref, mod_ref, g_ref, w_ref, cos_ref, sin_ref, *outs = refs
    else:
        x_ref, mod_ref, g_ref, w_ref, *outs = refs
    x = x_ref[...]
    h = _rms(x, g_ref[...])
    h = h * (1.0 + mod_ref[1:2, :]) + mod_ref[0:1, :]
    hb = h.astype(BF16)
    tm = x.shape[0]
    if rope:
        lane = lax.broadcasted_iota(jnp.int32, (tm, LANES), 1)
        first = (lane & 16) == 0
        cos = cos_ref[...]
        sin = sin_ref[...]
    outs = iter(outs)
    for seg in segs:
        y = _dot(hb, w_ref[:, seg.col0:seg.col0 + seg.width])
        o_refs = [next(outs) for _ in seg.outs]
        for t in range(seg.width // LANES):
            yt = y[:, t * LANES:(t + 1) * LANES]
            if seg.rope:
                sw = jnp.where(first, pltpu.roll(yt, LANES - 16, 1), pltpu.roll(yt, 16, 1))
                yt = yt * cos + sw * sin
            if seg.scale != 1.0:
                yt = yt * seg.scale
            if seg.gelu:
                yt = _gelu_tanh(yt)
            for (layout, _), o_ref in zip(seg.outs, o_refs):
                if layout == "cols":
                    o_ref[t * LANES:(t + 1) * LANES, :] = yt.T.astype(o_ref.dtype)
                elif layout == "rows":
                    o_ref[:, t * LANES:(t + 1) * LANES] = yt.astype(o_ref.dtype)
                else:
                    seq_len = layout[1]
                    for r in range(tm // seq_len):
                        o_ref[r, :, t, :] = yt[r * seq_len:(r + 1) * seq_len].astype(o_ref.dtype)


def _proj(x, mod, g, w, segs, *, tm, ctx, rope_tabs=None, name):
    B, T, D = x.shape

    def out_spec(width, layout):
        if layout == "cols":
            return pl.BlockSpec((None, width, tm), lambda b, i: (b, 0, i))
        if layout == "rows":
            return pl.BlockSpec((None, tm, width), lambda b, i: (b, i, 0))
        return pl.BlockSpec((tm // layout[1], None, layout[1], width // LANES, LANES),
                            lambda b, i: (b * (T // tm) + i, 0, 0, 0, 0))

    def out_struct(width, layout, dt):
        if layout == "cols":
            return jax.ShapeDtypeStruct((B, width, T), dt)
        if layout == "rows":
            return jax.ShapeDtypeStruct((B, T, width), dt)
        return jax.ShapeDtypeStruct((B * T // layout[1], 1, layout[1], width // LANES, LANES), dt)

    row = (lambda b, i: (CTX_ROW, 0, 0)) if ctx else (lambda b, i: (b, 0, 0))
    in_specs = [
        pl.BlockSpec((None, tm, D), lambda b, i: (b, i, 0)),
        pl.BlockSpec((None, 6, D), row),
        pl.BlockSpec((1, D), lambda b, i: (0, 0)),
        pl.BlockSpec(w.shape, lambda b, i: (0, 0)),
    ]
    args = [x, mod, g.reshape(1, D), w]
    if rope_tabs is not None:
        in_specs += [pl.BlockSpec((tm, LANES), lambda b, i: (i, 0))] * 2
        args += list(rope_tabs)
    out_specs = [out_spec(s.width, layout) for s in segs for layout, _ in s.outs]
    out_shape = [out_struct(s.width, layout, dt) for s in segs for layout, dt in s.outs]
    return pl.pallas_call(
        functools.partial(_proj_kernel, segs=tuple(segs), rope=rope_tabs is not None),
        grid=(B, T // tm),
        in_specs=in_specs,
        out_specs=out_specs,
        out_shape=out_shape,
        compiler_params=_cparams("parallel", "parallel"),
        name=name,
    )(*args)


def _diff_lambda(lq, lam_init):
    s1 = jnp.sum(lq[0:1, :] * lq[1:2, :], axis=1, keepdims=True)
    s2 = jnp.sum(lq[2:3, :] * lq[3:4, :], axis=1, keepdims=True)
    return jnp.exp(s1) - jnp.exp(s2) + lam_init


def _stack_group_queries(qa_ref, g, rows):
    return jnp.concatenate(
        [qa_ref[rows, (A_GROUP * g + hh) * LANES:(A_GROUP * g + hh + 1) * LANES] for hh in range(A_GROUP)], axis=0)


def _stack_pair_queries(q):
    lo = lax.broadcasted_iota(jnp.int32, q.shape, 1) < HEAD_DIM
    zero = jnp.zeros_like(q)
    return jnp.concatenate([jnp.where(lo, q, zero), jnp.where(lo, zero, q)], axis=0)


def _with_ones_rows(vt):
    return jnp.concatenate([vt, jnp.ones((ONES_ROWS, vt.shape[1]), BF16)], axis=0)


def _sink_row(sink_ref, g, tq):
    return jnp.concatenate(
        [jnp.full((1, tq), sink_ref[A_GROUP * g + hh] * LOG2E, F32) for hh in range(A_GROUP)], axis=1)


def _softmax_values(parts, sink=None):
    m = functools.reduce(jnp.maximum, [jnp.max(s, axis=0, keepdims=True) for s, _ in parts])
    if sink is not None:
        m = jnp.maximum(m, sink)
    ot = sum(_dot(vt1, jnp.exp2(s - m).astype(BF16)) for s, vt1 in parts)
    den = ot[LANES:LANES + 1]
    if sink is not None:
        den = den + jnp.exp2(sink - m)
    return ot[:LANES] / den


def _a_heads(ot, g, tq):
    return [ot[g * HEAD_DIM:(g + 1) * HEAD_DIM, hh * tq:(hh + 1) * tq] for hh in range(A_GROUP)]


def _subln(o, subln, lam_init):
    return _rms(o, subln) * (1.0 - lam_init)


def _diff_combine(ot, lam, subln, lam_init):
    tq = ot.shape[1] // 2
    return _subln((ot[:, :tq] - lam * ot[:, tq:]).T, subln, lam_init)


def _ctx_attn_kernel(sink_ref, qa_ref, ka_ref, va_ref, qb_ref, kb_ref, vb_ref, lamqk_ref, subln_ref, o_ref, *,
                     lam_init):
    T = qa_ref.shape[0]
    ka = ka_ref[...].astype(BF16)
    vat1 = _with_ones_rows(va_ref[...].T.astype(BF16))
    scores_a = [_dot_nt(ka, _stack_group_queries(qa_ref, g, slice(None))) for g in range(A_KV_HEADS)]
    scores_b, vbt1 = [], []
    for h in range(B_HEADS):
        sl = slice(h * LANES, (h + 1) * LANES)
        scores_b.append(_dot_nt(kb_ref[:, sl].astype(BF16), _stack_pair_queries(qb_ref[:, sl])))
        vbt1.append(_with_ones_rows(vb_ref[:, sl].T.astype(BF16)))
    heads = []
    for g in range(A_KV_HEADS):
        heads += _a_heads(_softmax_values([(scores_a[g], vat1)], _sink_row(sink_ref, g, T)), g, T)
    nqa = A_HEADS * HEAD_DIM
    o_ref[:, :nqa] = jnp.concatenate(heads, axis=0).T.astype(o_ref.dtype)
    lam = _diff_lambda(lamqk_ref[...], lam_init)
    for h in range(B_HEADS):
        ot = _softmax_values([(scores_b[h], vbt1[h])])
        o_ref[:, nqa + h * LANES:nqa + (h + 1) * LANES] = (
            _diff_combine(ot, lam, subln_ref[...], lam_init).astype(o_ref.dtype))


def _ctx_attention(qa, ka, va, qb, kb, vb, sink, lam_qk, subln, lam_init):
    B, T, _ = qa.shape
    blk = lambda w: pl.BlockSpec((None, T, w), lambda b: (b, 0, 0))
    full = lambda a: pl.BlockSpec(a.shape, lambda b: (0,) * a.ndim)
    return pl.pallas_call(
        functools.partial(_ctx_attn_kernel, lam_init=lam_init),
        grid=(B,),
        in_specs=[pl.BlockSpec(memory_space=pltpu.SMEM), blk(QA_W), blk(KA_W), blk(KA_W), blk(B_W), blk(B_W),
                  blk(B_W), full(lam_qk), full(subln)],
        out_specs=blk(A_HEADS * HEAD_DIM + B_W),
        out_shape=jax.ShapeDtypeStruct((B, T, A_HEADS * HEAD_DIM + B_W), BF16),
        compiler_params=_cparams("parallel"),
        name="ctx_attention",
    )(sink, qa, ka, va, qb, kb, vb, lam_qk, subln)


def _lat_a_kernel(sink_ref, qa_ref, k_ref, vt_ref, ck_ref, cvt_ref, o_ref, *, tq, nq, band):
    T = k_ref.shape[0]
    cols = A_GROUP * tq
    cvt1 = _with_ones_rows(cvt_ref[...])
    jobs = []
    for i in range(nq):
        qi = pl.program_id(1) * nq + i
        start = pl.multiple_of(jnp.clip(qi * tq - WINDOW, 0, T - band), WINDOW)
        kb = k_ref[pl.ds(start, band), :]
        vbt1 = _with_ones_rows(vt_ref[:, pl.ds(start, band)])
        kpos = start + lax.broadcasted_iota(jnp.int32, (band, cols), 0)
        qpos = qi * tq + lax.broadcasted_iota(jnp.int32, (band, cols), 1) % tq
        keep = jnp.abs(qpos - kpos) <= WINDOW
        for g in range(A_KV_HEADS):
            qg = _stack_group_queries(qa_ref, g, slice(i * tq, (i + 1) * tq))
            jobs.append((_dot_nt(ck_ref[...], qg), _dot_nt(kb, qg), keep, vbt1))
    for i in range(nq):
        heads = []
        for g in range(A_KV_HEADS):
            s_c, s_b, keep, vbt1 = jobs[i * A_KV_HEADS + g]
            ot = _softmax_values([(s_c, cvt1), (jnp.where(keep, s_b, NEG), vbt1)], _sink_row(sink_ref, g, tq))
            heads += _a_heads(ot, g, tq)
        o_ref[i * tq:(i + 1) * tq, :] = jnp.concatenate(heads, axis=0).T.astype(o_ref.dtype)


def _lat_a_attention(qa, ka, vat, cka, cvat, sink):
    B, T, _ = qa.shape
    L = cka.shape[1]
    tq, nq = WINDOW, 4
    band = 3 * WINDOW
    return pl.pallas_call(
        functools.partial(_lat_a_kernel, tq=tq, nq=nq, band=band),
        grid=(B, T // (tq * nq)),
        in_specs=[
            pl.BlockSpec(memory_space=pltpu.SMEM),
            pl.BlockSpec((None, tq * nq, QA_W), lambda b, i: (b, i, 0)),
            pl.BlockSpec((None, T, KA_W), lambda b, i: (b, 0, 0)),
            pl.BlockSpec((None, KA_W, T), lambda b, i: (b, 0, 0)),
            pl.BlockSpec((None, L, KA_W), lambda b, i: (b, 0, 0)),
            pl.BlockSpec((None, KA_W, L), lambda b, i: (b, 0, 0)),
        ],
        out_specs=pl.BlockSpec((None, tq * nq, A_HEADS * HEAD_DIM), lambda b, i: (b, i, 0)),
        out_shape=jax.ShapeDtypeStruct((B, T, A_HEADS * HEAD_DIM), BF16),
        compiler_params=_cparams("parallel", "parallel"),
        name="lat_a_attention",
    )(sink, qa, ka, vat, cka, cvat)


def _lat_b_kernel(q_ref, k_ref, vt_ref, ck_ref, cvt_ref, lamqk_ref, subln_ref, o_ref, *, tk, ahead, lam_init):
    tq = q_ref.shape[0]
    T = k_ref.shape[0]
    qs = _stack_pair_queries(q_ref[...])
    blocks = [(ck_ref[...], cvt_ref[...])]
    blocks += [(k_ref[j * tk:(j + 1) * tk, :], vt_ref[:, j * tk:(j + 1) * tk]) for j in range(T // tk)]

    def scores(k):
        return _dot_nt(k, qs)

    def accumulate(s, vt, carry):
        m, acc = carry
        m_new = jnp.maximum(m, jnp.max(s, axis=0, keepdims=True))
        p = jnp.exp2(s - m_new).astype(BF16)
        acc = jnp.exp2(m - m_new) * acc + _dot(_with_ones_rows(vt), p)
        return m_new, acc

    carry = (jnp.full((1, 2 * tq), -jnp.inf, F32), jnp.zeros((LANES + ONES_ROWS, 2 * tq), F32))
    pending = [scores(blocks[j][0]) for j in range(ahead)]
    for j in range(len(blocks)):
        if j + ahead < len(blocks):
            pending.append(scores(blocks[j + ahead][0]))
        carry = accumulate(pending.pop(0), blocks[j][1], carry)
    _, acc = carry
    ot = acc[:LANES] / acc[LANES:LANES + 1]
    lam = _diff_lambda(lamqk_ref[...], lam_init)
    o_ref[...] = _diff_combine(ot, lam, subln_ref[...], lam_init).astype(o_ref.dtype)


def _lat_b_attention(qb, kb, vbt, ckb, cvbt, lam_qk, subln, lam_init):
    B, T, _ = qb.shape
    L = ckb.shape[1]
    tq = 1024
    full = lambda a: pl.BlockSpec(a.shape, lambda b, h, i: (0,) * a.ndim)
    return pl.pallas_call(
        functools.partial(_lat_b_kernel, tk=512, ahead=2, lam_init=lam_init),
        grid=(B, B_HEADS, T // tq),
        in_specs=[
            pl.BlockSpec((None, tq, LANES), lambda b, h, i: (b, i, h)),
            pl.BlockSpec((None, T, LANES), lambda b, h, i: (b, 0, h)),
            pl.BlockSpec((None, LANES, T), lambda b, h, i: (b, h, 0)),
            pl.BlockSpec((None, L, LANES), lambda b, h, i: (b, 0, h)),
            pl.BlockSpec((None, LANES, L), lambda b, h, i: (b, h, 0)),
            full(lam_qk), full(subln),
        ],
        out_specs=pl.BlockSpec((None, tq, LANES), lambda b, h, i: (b, i, h)),
        out_shape=jax.ShapeDtypeStruct((B, T, B_W), BF16),
        compiler_params=_cparams("parallel", "parallel", "parallel"),
        name="lat_b_attention",
    )(qb, kb, vbt, ckb, cvbt, lam_qk, subln)


def _scan_chunk_len(n_steps):
    chunk = -(-n_steps // SUBLANES)
    while chunk % 8 != 4:
        chunk += 1
    return chunk


def _rec_kernel(gate_ref, xr_ref, cw_ref, cb_ref, wbd_ref, bbd_ref, lam_ref, h0f_ref, h0b_ref,
                y_ref, sf_ref, sb_ref, xp, af, uf, ab, ub, pf_s, hf_s, pb_s, hb_s, *, nblk, **kw):
    for n in range(nblk):
        sl = slice(n * LANES, (n + 1) * LANES)
        _rec_block(gate_ref.at[:, sl], xr_ref.at[:, sl], cw_ref.at[:, sl], cb_ref.at[:, sl], wbd_ref.at[n],
                   bbd_ref.at[n], lam_ref.at[:, sl], h0f_ref.at[:, sl], h0b_ref.at[:, sl], y_ref.at[:, sl],
                   sf_ref.at[:, sl], sb_ref.at[:, sl], xp.at[n], af.at[n], uf.at[n], ab.at[n], ub.at[n],
                   pf_s.at[n], hf_s.at[n], pb_s.at[n], hb_s.at[n], **kw)


def _rec_block(gate_ref, xr_ref, cw_ref, cb_ref, wbd_ref, bbd_ref, lam_ref, h0f_ref, h0b_ref,
               y_ref, sf_ref, sb_ref, xp, af, uf, ab, ub, pf_s, hf_s, pb_s, hb_s, *, tc, gates_unroll, chunk, unroll):
    T = xr_ref.shape[0]
    pad = SUBLANES
    xp[0:pad, :] = jnp.zeros((pad, LANES), F32)
    xp[T + pad:T + 2 * pad, :] = jnp.zeros((pad, LANES), F32)
    xp[pad:T + pad, :] = xr_ref[...]
    tail = SUBLANES * chunk - T
    for a_s, u_s in ((af, uf), (ab, ub)):
        a_s[T:T + tail, :] = jnp.ones((tail, LANES), F32)
        u_s[T:T + tail, :] = jnp.zeros((tail, LANES), F32)
    cw = cw_ref[...]
    cb = cb_ref[...]
    nl = -lam_ref[...]
    softplus = jnp.maximum(nl, 0.0) + jnp.log1p(jnp.exp(-jnp.abs(nl)))
    cp = (0.5 * RGLRU_C) * softplus

    def gates(it, _):
        for q in range(gates_unroll):
            gate_chunk(it * gates_unroll + q)
        return 0

    def gate_chunk(ci):
        t0 = pl.multiple_of(ci * tc, tc)
        y = cb
        for j in range(CONV_W):
            y = y + xp[pl.ds(t0 + (pad - CONV_LEFT + j), tc), :] * cw[j:j + 1, :]
        t = jnp.tanh(_dot(y.astype(BF16), wbd_ref[...]) + bbd_ref[...])
        yh = 0.5 * y
        for d, (a_s, u_s) in enumerate(((af, uf), (ab, ub))):
            t_r = t[:, 2 * d * LANES:(2 * d + 1) * LANES]
            t_i = t[:, (2 * d + 1) * LANES:(2 * d + 2) * LANES]
            neg_log_a = cp[d:d + 1, :] * t_r + cp[d:d + 1, :]
            a = jnp.exp2(neg_log_a * (-LOG2E))
            w = jnp.tanh(neg_log_a) * (a * a + 1.0)
            sqrt_w = jnp.where(w > 0.0, w * lax.rsqrt(w), 0.0)
            a_s[pl.ds(t0, tc), :] = a
            u_s[pl.ds(t0, tc), :] = (t_i + 1.0) * (yh * sqrt_w)

    lax.fori_loop(0, T // (tc * gates_unroll), gates, 0)

    def rows(k):
        return pl.ds(k, SUBLANES, stride=chunk)

    def local_scan(it, carry):
        hf, pf, hb, pb = carry
        for q in range(unroll):
            k = it * unroll + q
            a = af[rows(k), :]
            hf = a * hf + uf[rows(k), :]
            pf = a * pf
            hf_s[rows(k), :] = hf
            pf_s[rows(k), :] = pf
            kb = chunk - 1 - k
            a = ab[rows(kb), :]
            hb = a * hb + ub[rows(kb), :]
            pb = a * pb
            hb_s[rows(kb), :] = hb
            pb_s[rows(kb), :] = pb
        return hf, pf, hb, pb

    zero = jnp.zeros((SUBLANES, LANES), F32)
    one = jnp.ones((SUBLANES, LANES), F32)
    hf, pf, hb, pb = lax.fori_loop(0, chunk // unroll, local_scan, (zero, one, zero, one))

    cf = [h0f_ref[...]]
    for r in range(SUBLANES - 1):
        cf.append(hf[r:r + 1, :] + pf[r:r + 1, :] * cf[r])
    cb_rev = [h0b_ref[...]]
    for r in range(SUBLANES - 1, 0, -1):
        cb_rev.append(hb[r:r + 1, :] + pb[r:r + 1, :] * cb_rev[-1])
    carry_f = jnp.concatenate(cf, axis=0)
    carry_b = jnp.concatenate(cb_rev[::-1], axis=0)

    def apply_carry(it, _):
        for q in range(unroll):
            k = it * unroll + q
            uf[rows(k), :] = hf_s[rows(k), :] + pf_s[rows(k), :] * carry_f
            ub[rows(k), :] = hb_s[rows(k), :] + pb_s[rows(k), :] * carry_b
        return 0

    lax.fori_loop(0, chunk // unroll, apply_carry, 0)
    sf_ref[...] = uf[T - 1:T, :]
    sb_ref[...] = ub[0:1, :]

    def combine(ci, _):
        t0 = pl.multiple_of(ci * tc, tc)
        y_ref[pl.ds(t0, tc), :] = ((uf[pl.ds(t0, tc), :] + ub[pl.ds(t0, tc), :])
                                   * gate_ref[pl.ds(t0, tc), :]).astype(y_ref.dtype)
        return 0

    lax.fori_loop(0, T // tc, combine, 0)


def _rec_mixer(gate, xr, h0f, h0b, conv_w, conv_b, w_bd, b_bd, lam, *, nblk, tc, name):
    B, T, _ = xr.shape
    chunk = _scan_chunk_len(T)
    wid = nblk * LANES
    col = lambda rows: pl.BlockSpec((rows, wid), lambda b, n: (0, n))
    seq = pl.BlockSpec((None, T, wid), lambda b, n: (b, 0, n))
    st = pl.BlockSpec((None, 1, wid), lambda b, n: (b, 0, n))
    return pl.pallas_call(
        functools.partial(_rec_kernel, nblk=nblk, tc=tc, gates_unroll=2, chunk=chunk,
                          unroll=max(u for u in range(1, SCAN_MAX_UNROLL + 1) if chunk % u == 0)),
        grid=(B, RNN_BLOCKS // nblk),
        in_specs=[seq, seq, col(CONV_W), col(1),
                  pl.BlockSpec((nblk, LANES, 4 * LANES), lambda b, n: (n, 0, 0)),
                  pl.BlockSpec((nblk, 1, 4 * LANES), lambda b, n: (n, 0, 0)),
                  col(2), st, st],
        out_specs=[seq, st, st],
        out_shape=[jax.ShapeDtypeStruct((B, T, D_RNN), BF16),
                   jax.ShapeDtypeStruct((B, 1, D_RNN), F32),
                   jax.ShapeDtypeStruct((B, 1, D_RNN), F32)],
        scratch_shapes=[pltpu.VMEM((nblk, T + 2 * SUBLANES, LANES), F32)]
        + [pltpu.VMEM((nblk, SUBLANES * chunk, LANES), F32)] * 8,
        compiler_params=_cparams("parallel", "parallel"),
        name=name,
    )(gate, xr, conv_w, conv_b.reshape(1, D_RNN), w_bd, b_bd, lam, h0f, h0b)


def _post_kernel(*refs, n_mix, final, fc):
    x_ref, mod_ref, g2_ref = refs[:3]
    mix = refs[3:3 + 2 * n_mix]
    w1_ref, w2_ref = refs[3 + 2 * n_mix:5 + 2 * n_mix]
    rest = refs[5 + 2 * n_mix:]
    if final:
        gf_ref, o_ref = rest
    else:
        (o_ref,) = rest
    mixed = _dot(mix[0][...], mix[1][...])
    for i in range(1, n_mix):
        mixed = mixed + _dot(mix[2 * i][...], mix[2 * i + 1][...])
    x1 = x_ref[...] + mod_ref[2:3, :] * mixed
    h = _rms(x1, g2_ref[...])
    hb = (h * (1.0 + mod_ref[4:5, :]) + mod_ref[3:4, :]).astype(BF16)
    n_chunks = D_FF // fc
    up = lambda c: _dot(hb, w1_ref[:, c * fc:(c + 1) * fc])
    acc = None
    a = up(0)
    for c in range(n_chunks):
        a_next = up(c + 1) if c + 1 < n_chunks else None
        part = _dot(jnp.square(jnp.maximum(a, 0.0)).astype(BF16), w2_ref[c * fc:(c + 1) * fc, :])
        acc = part if acc is None else acc + part
        a = a_next
    x2 = x1 + mod_ref[5:6, :] * acc
    if final:
        x2 = _rms(x2, gf_ref[...])
    o_ref[...] = x2


def _post(x, mod, g2, mixes, w1, w2, *, tm, ctx, final_g=None, name):
    B, T, D = x.shape
    row = (lambda b, i: (CTX_ROW, 0, 0)) if ctx else (lambda b, i: (b, 0, 0))
    const = lambda a: pl.BlockSpec(a.shape, lambda b, i: (0,) * a.ndim, pipeline_mode=pl.Buffered(1))
    in_specs = [
        pl.BlockSpec((None, tm, D), lambda b, i: (b, i, 0)),
        pl.BlockSpec((None, 6, D), row),
        pl.BlockSpec((1, D), lambda b, i: (0, 0)),
    ]
    args = [x, mod, g2.reshape(1, D)]
    for o, w in mixes:
        in_specs += [pl.BlockSpec((None, tm, o.shape[-1]), lambda b, i: (b, i, 0)), const(w)]
        args += [o, w]
    in_specs += [const(w1), const(w2)]
    args += [w1, w2]
    if final_g is not None:
        in_specs.append(pl.BlockSpec((1, D), lambda b, i: (0, 0)))
        args.append(final_g.reshape(1, D))
    return pl.pallas_call(
        functools.partial(_post_kernel, n_mix=len(mixes), final=final_g is not None, fc=1024),
        grid=(B, T // tm),
        in_specs=in_specs,
        out_specs=pl.BlockSpec((None, tm, D), lambda b, i: (b, i, 0)),
        out_shape=jax.ShapeDtypeStruct((B, T, D), F32),
        compiler_params=_cparams("parallel", "parallel"),
        name=name,
    )(*args)


def _rope_tables(n_tokens):
    rows = n_tokens // GRID_W
    r, cl = jnp.meshgrid(jnp.arange(rows, dtype=F32), jnp.arange(GRID_W, dtype=F32), indexing='ij')
    quarter = HEAD_DIM // 4
    inv = ROPE_BASE ** (-jnp.arange(quarter, dtype=F32) / quarter)
    ang = jnp.stack([r.reshape(-1)[:, None] * inv, cl.reshape(-1)[:, None] * inv], axis=1)
    cos, sin = jnp.cos(ang), jnp.sin(ang)
    cos64 = jnp.concatenate([cos[:, 0], cos[:, 0], cos[:, 1], cos[:, 1]], axis=-1)
    sin64 = jnp.concatenate([-sin[:, 0], sin[:, 0], -sin[:, 1], sin[:, 1]], axis=-1)
    return jnp.tile(cos64, (1, LANES // HEAD_DIM)), jnp.tile(sin64, (1, LANES // HEAD_DIM))


def _att_in_weights(w_in):
    d = w_in.shape[0]
    nq = A_HEADS * HEAD_DIM
    wq = w_in[:, :nq].reshape(d, A_HEADS, HEAD_DIM)
    z = jnp.zeros_like(wq)
    in_first = (jnp.arange(A_HEADS) // A_GROUP == 0)[None, :, None]
    wq = jnp.where(in_first, jnp.concatenate([wq, z], axis=-1), jnp.concatenate([z, wq], axis=-1))
    return jnp.concatenate([wq.reshape(d, QA_W), w_in[:, nq:]], axis=1).astype(BF16)


def _block_diag_weights(w_a, b_a, w_x, b_x):
    w = jnp.concatenate([w_a[0], w_x[0], w_a[1], w_x[1]], axis=-1)
    b = jnp.concatenate([v.reshape(RNN_BLOCKS, 1, RNN_BW) for v in (b_a[0], b_x[0], b_a[1], b_x[1])], axis=-1)
    return (0.5 * w).astype(BF16), 0.5 * b


def kernel(x_prompt, x_sample, cache_a_k, cache_a_v, cache_b_k, cache_b_v, state_fwd, state_bwd, c, c_ctx, norm1, norm2, w_ada, b_ada, w_mlp1, w_mlp2, att_w_in, att_w_out, att_sink, att_lam_qk, att_subln, rec_w_in, rec_conv_w, rec_conv_b, rec_w_a, rec_b_a, rec_w_x, rec_b_x, rec_lam, rec_w_out, final_norm):
    nb, n_seq, _ = x_prompt.shape
    nd, d_seq, _ = x_sample.shape
    past = cache_a_k.shape[2]
    assert nd <= CTX_ROW and DEPTH == 2
    cvec = jnp.concatenate([c, jnp.zeros((CTX_ROW - nd, D_MODEL), F32), c_ctx[None],
                            jnp.zeros((MOD_ROWS - CTX_ROW - 1, D_MODEL), F32)], axis=0)
    mod = _ada_mod(cvec, w_ada, b_ada)
    w1 = w_mlp1.astype(BF16)
    w2 = w_mlp2.astype(BF16)
    tm_ctx = tm_lat = 512
    tok = lambda a: a.reshape(-1, tm_ctx, a.shape[-1])
    seq = lambda a: a.reshape(nb, n_seq, a.shape[-1])

    lam_init = 0.8 - 0.6 * math.exp(-0.3 * 0)
    w_in = _att_in_weights(att_w_in[0])
    w_out = att_w_out[0].astype(BF16)
    nqa = A_HEADS * HEAD_DIM
    c0 = [0, QA_W, QA_W + KA_W, QA_W + 2 * KA_W, QA_W + 2 * KA_W + B_W, QA_W + 2 * KA_W + 2 * B_W]
    widths = [QA_W, KA_W, KA_W, B_W, B_W, B_W]
    scales = [SCALE * LOG2E, 1.0, 1.0, SCALE * LOG2E, 1.0, 1.0]
    roped = [True, True, False, True, True, False]
    sink = att_sink[0]
    lam_qk = att_lam_qk[0]
    subln = att_subln[0].reshape(1, 2 * HEAD_DIM)

    rows = lambda dt: (("rows", dt),)
    cache_b = ((("heads", n_seq), F32), ("rows", BF16))
    segs_ctx = [_Seg(c0[0], QA_W, rows(BF16), scale=scales[0]), _Seg(c0[1], KA_W, rows(F32)),
                _Seg(c0[2], KA_W, rows(F32)), _Seg(c0[3], B_W, rows(BF16), scale=scales[3]),
                _Seg(c0[4], B_W, cache_b), _Seg(c0[5], B_W, cache_b)]
    qa, ka, va, qb, new_b_k, kb, new_b_v, vb = _proj(tok(x_prompt), mod[0], norm1[0], w_in, segs_ctx, tm=tm_ctx,
                                                     ctx=True, name="proj_att_ctx")
    qa, ka, va, qb, kb, vb = map(seq, (qa, ka, va, qb, kb, vb))
    o_ctx = _ctx_attention(qa, ka, va, qb, kb, vb, sink, lam_qk, subln, lam_init)
    xp = _post(tok(x_prompt), mod[0], norm2[0], [(tok(o_ctx), w_out)], w1[0], w2[0], tm=tm_ctx, ctx=True,
               name="post_att_ctx")
    new_a_k = ka.reshape(nb, 1, n_seq, A_KV_HEADS, HEAD_DIM)
    new_a_v = va.reshape(nb, 1, n_seq, A_KV_HEADS, HEAD_DIM)

    segs_lat = [_Seg(c0[i], widths[i], (("cols" if i in (2, 5) else "rows", BF16),), rope=roped[i], scale=scales[i])
                for i in range(6)]
    qa, ka, vat, qb, kb, vbt = _proj(x_sample, mod[0], norm1[0], w_in, segs_lat, tm=tm_lat, ctx=False,
                                     rope_tabs=_rope_tables(d_seq), name="proj_att_lat")
    cka = cache_a_k[:, 0].reshape(nd, past, KA_W).astype(BF16)
    cvat = jnp.swapaxes(cache_a_v[:, 0].reshape(nd, past, KA_W), 1, 2).astype(BF16)
    ckb = cache_b_k[:, 0].reshape(nd, past, B_W).astype(BF16)
    cvbt = jnp.swapaxes(cache_b_v[:, 0].reshape(nd, past, B_W), 1, 2).astype(BF16)
    oa = _lat_a_attention(qa, ka, vat, cka, cvat, sink)
    ob = _lat_b_attention(qb, kb, vbt, ckb, cvbt, lam_qk, subln, lam_init)
    xs = _post(x_sample, mod[0], norm2[0], [(oa, w_out[:nqa]), (ob, w_out[nqa:])], w1[0], w2[0], tm=tm_lat,
               ctx=False, name="post_att_lat")

    w_rin = rec_w_in[0].astype(BF16)
    w_rout = rec_w_out[0].astype(BF16)
    w_bd, b_bd = _block_diag_weights(rec_w_a[0], rec_b_a[0], rec_w_x[0], rec_b_x[0])
    segs_rec = [_Seg(0, D_RNN, rows(F32), gelu=True), _Seg(D_RNN, D_RNN, rows(F32))]
    zeros = jnp.zeros((nb, 1, D_RNN), F32)

    gate, xr = map(seq, _proj(xp, mod[1], norm1[1], w_rin, segs_rec, tm=tm_ctx, ctx=True,
                              name="proj_rec_ctx"))
    y, sf, sb = _rec_mixer(gate, xr, zeros, zeros, rec_conv_w[0], rec_conv_b[0], w_bd, b_bd, rec_lam[0],
                           nblk=RNN_BLOCKS, tc=128, name="rec_mixer_ctx")
    y_prompt = seq(_post(xp, mod[1], norm2[1], [(tok(y), w_rout)], w1[1], w2[1], tm=tm_ctx, ctx=True,
                         final_g=final_norm, name="post_rec_ctx"))

    gate, xr = _proj(xs, mod[1], norm1[1], w_rin, segs_rec, tm=tm_lat, ctx=False, name="proj_rec_lat")
    y, _, _ = _rec_mixer(gate, xr, state_fwd[:, 0:1], state_bwd[:, 0:1], rec_conv_w[0], rec_conv_b[0], w_bd, b_bd,
                         rec_lam[0], nblk=1, tc=256, name="rec_mixer_lat")
    y_sample = _post(xs, mod[1], norm2[1], [(y, w_rout)], w1[1], w2[1], tm=tm_lat, ctx=False, final_g=final_norm,
                     name="post_rec_lat")

    return (y_prompt, y_sample, new_a_k, new_a_v, new_b_k, new_b_v, sf, sb)
```

```python
import functools
import math
from typing import NamedTuple

import jax
import jax.numpy as jnp
import numpy as np
from jax import lax
from jax.experimental import pallas as pl
from jax.experimental.pallas import tpu as pltpu

F32 = jnp.float32
BF16 = jnp.bfloat16

LANES = 128
SUBLANES = 8
VMEM_LIMIT_BYTES = 56 * 1024 * 1024

D_MODEL = 1024
DEPTH = 2
GRID_W = 64
HEAD_DIM = 64
A_HEADS = 8
A_KV_HEADS = 2
A_GROUP = A_HEADS // A_KV_HEADS
B_HEADS = 4
WINDOW = 128
ROPE_BASE = 10000.0
D_RNN = 1280
RNN_BLOCKS = 10
RNN_BW = D_RNN // RNN_BLOCKS
CONV_W = 4
CONV_LEFT = (CONV_W - 1) // 2
RGLRU_C = 8.0
D_FF = 4 * D_MODEL
EPS = 1e-6
SCALE = HEAD_DIM ** -0.5
NEG = -1e30

QA_W = A_HEADS * LANES
KA_W = A_KV_HEADS * HEAD_DIM
B_W = B_HEADS * 2 * HEAD_DIM
MOD_ROWS = 8
CTX_ROW = 4
LOG2E = math.log2(math.e)
ONES_ROWS = 16
SCAN_MAX_UNROLL = 1024


def _cparams(*semantics):
    return pltpu.CompilerParams(dimension_semantics=semantics, vmem_limit_bytes=VMEM_LIMIT_BYTES)


def _dot(a, b):
    return jnp.dot(a, b, preferred_element_type=F32)


def _dot_nt(a, b):
    return lax.dot_general(a, b, (((1,), (1,)), ((), ())), preferred_element_type=F32)


def _rms(x, g):
    return x * lax.rsqrt(jnp.mean(x * x, axis=-1, keepdims=True) + EPS) * g


def _gelu_tanh(x):
    return x * (0.5 * (1.0 + jnp.tanh(math.sqrt(2.0 / math.pi) * (x + 0.044715 * (x * x * x)))))


def _ada_kernel(c_ref, w_ref, b_ref, o_ref):
    c = c_ref[...]
    s = c * jax.nn.sigmoid(c)
    o_ref[...] = _dot(s.astype(BF16), w_ref[...].astype(BF16)) + b_ref[...]


def _ada_mod(cvec, w_ada, b_ada):
    tn = 1536
    out = pl.pallas_call(
        _ada_kernel,
        grid=(DEPTH, 6 * D_MODEL // tn),
        in_specs=[
            pl.BlockSpec((MOD_ROWS, D_MODEL), lambda l, j: (0, 0)),
            pl.BlockSpec((None, D_MODEL, tn), lambda l, j: (l, 0, j)),
            pl.BlockSpec((None, 1, tn), lambda l, j: (l, 0, j)),
        ],
        out_specs=pl.BlockSpec((None, MOD_ROWS, tn), lambda l, j: (l, 0, j)),
        out_shape=jax.ShapeDtypeStruct((DEPTH, MOD_ROWS, 6 * D_MODEL), F32),
        compiler_params=_cparams("parallel", "parallel"),
        name="ada_mod",
    )(cvec, w_ada, b_ada.reshape(DEPTH, 1, 6 * D_MODEL))
    return out.reshape(DEPTH, MOD_ROWS, 6, D_MODEL)


class _Seg(NamedTuple):
    col0: int
    width: int
    outs: tuple
    rope: bool = False
    scale: float = 1.0
    gelu: bool = False


def _proj_kernel(*refs, segs, rope):
    if rope:
        x_ref, mod_ref, g_ref, w_ref, cos_ref, sin_ref, *outs = refs
    else:
        x_ref, mod_ref, g_ref, w_ref, *outs = refs
    x = x_ref[...]
    h = _rms(x, g_ref[...])
    h = h * (1.0 + mod_ref[1:2, :]) + mod_ref[0:1, :]
    hb = h.astype(BF16)
    tm = x.shape[0]
    if rope:
        lane = lax.broadcasted_iota(jnp.int32, (tm, LANES), 1)
        first = (lane & 16) == 0
        cos = cos_ref[...]
        sin = sin_ref[...]
    outs = iter(outs)
    for seg in segs:
        y = _dot(hb, w_ref[:, seg.col0:seg.col0 + seg.width])
        o_refs = [next(outs) for _ in seg.outs]
        for t in range(seg.width // LANES):
            yt = y[:, t * LANES:(t + 1) * LANES]
            if seg.rope:
                sw = jnp.where(first, pltpu.roll(yt, LANES - 16, 1), pltpu.roll(yt, 16, 1))
                yt = yt * cos + sw * sin
            if seg.scale != 1.0:
                yt = yt * seg.scale
            if seg.gelu:
                yt = _gelu_tanh(yt)
            for (layout, _), o_ref in zip(seg.outs, o_refs):
                if layout == "cols":
                    o_ref[t * LANES:(t + 1) * LANES, :] = yt.T.astype(o_ref.dtype)
                elif layout == "rows":
                    o_ref[:, t * LANES:(t + 1) * LANES] = yt.astype(o_ref.dtype)
                else:
                    seq_len = layout[1]
                    for r in range(tm // seq_len):
                        o_ref[r, :, t, :] = yt[r * seq_len:(r + 1) * seq_len].astype(o_ref.dtype)


def _proj(x, mod, g, w, segs, *, tm, ctx, rope_tabs=None, name):
    B, T, D = x.shape

    def out_spec(width, layout):
        if layout == "cols":
            return pl.BlockSpec((None, width, tm), lambda b, i: (b, 0, i))
        if layout == "rows":
            return pl.BlockSpec((None, tm, width), lambda b, i: (b, i, 0))
        return pl.BlockSpec((tm // layout[1], None, layout[1], width // LANES, LANES),
                            lambda b, i: (b * (T // tm) + i, 0, 0, 0, 0))

    def out_struct(width, layout, dt):
        if layout == "cols":
            return jax.ShapeDtypeStruct((B, width, T), dt)
        if layout == "rows":
            return jax.ShapeDtypeStruct((B, T, width), dt)
        return jax.ShapeDtypeStruct((B * T // layout[1], 1, layout[1], width // LANES, LANES), dt)

    row = (lambda b, i: (CTX_ROW, 0, 0)) if ctx else (lambda b, i: (b, 0, 0))
    in_specs = [
        pl.BlockSpec((None, tm, D), lambda b, i: (b, i, 0)),
        pl.BlockSpec((None, 6, D), row),
        pl.BlockSpec((1, D), lambda b, i: (0, 0)),
        pl.BlockSpec(w.shape, lambda b, i: (0, 0)),
    ]
    args = [x, mod, g.reshape(1, D), w]
    if rope_tabs is not None:
        in_specs += [pl.BlockSpec((tm, LANES), lambda b, i: (i, 0))] * 2
        args += list(rope_tabs)
    out_specs = [out_spec(s.width, layout) for s in segs for layout, _ in s.outs]
    out_shape = [out_struct(s.width, layout, dt) for s in segs for layout, dt in s.outs]
    return pl.pallas_call(
        functools.partial(_proj_kernel, segs=tuple(segs), rope=rope_tabs is not None),
        grid=(B, T // tm),
        in_specs=in_specs,
        out_specs=out_specs,
        out_shape=out_shape,
        compiler_params=_cparams("parallel", "parallel"),
        name=name,
    )(*args)


def _diff_lambda(lq, lam_init):
    s1 = jnp.sum(lq[0:1, :] * lq[1:2, :], axis=1, keepdims=True)
    s2 = jnp.sum(lq[2:3, :] * lq[3:4, :], axis=1, keepdims=True)
    return jnp.exp(s1) - jnp.exp(s2) + lam_init


def _stack_group_queries(qa_ref, g, rows):
    return jnp.concatenate(
        [qa_ref[rows, (A_GROUP * g + hh) * LANES:(A_GROUP * g + hh + 1) * LANES] for hh in range(A_GROUP)], axis=0)


def _stack_pair_queries(q):
    lo = lax.broadcasted_iota(jnp.int32, q.shape, 1) < HEAD_DIM
    zero = jnp.zeros_like(q)
    return jnp.concatenate([jnp.where(lo, q, zero), jnp.where(lo, zero, q)], axis=0)


def _with_ones_rows(vt):
    return jnp.concatenate([vt, jnp.ones((ONES_ROWS, vt.shape[1]), BF16)], axis=0)


def _sink_row(sink_ref, g, tq):
    return jnp.concatenate(
        [jnp.full((1, tq), sink_ref[A_GROUP * g + hh] * LOG2E, F32) for hh in range(A_GROUP)], axis=1)


def _softmax_values(parts, sink=None):
    m = functools.reduce(jnp.maximum, [jnp.max(s, axis=0, keepdims=True) for s, _ in parts])
    if sink is not None:
        m = jnp.maximum(m, sink)
    ot = sum(_dot(vt1, jnp.exp2(s - m).astype(BF16)) for s, vt1 in parts)
    den = ot[LANES:LANES + 1]
    if sink is not None:
        den = den + jnp.exp2(sink - m)
    return ot[:LANES] / den


def _a_heads(ot, g, tq):
    return [ot[g * HEAD_DIM:(g + 1) * HEAD_DIM, hh * tq:(hh + 1) * tq] for hh in range(A_GROUP)]


def _subln(o, subln, lam_init):
    return _rms(o, subln) * (1.0 - lam_init)


def _diff_combine(ot, lam, subln, lam_init):
    tq = ot.shape[1] // 2
    return _subln((ot[:, :tq] - lam * ot[:, tq:]).T, subln, lam_init)


def _ctx_attn_kernel(sink_ref, qa_ref, ka_ref, va_ref, qb_ref, kb_ref, vb_ref, lamqk_ref, subln_ref, o_ref, *,
                     lam_init):
    T = qa_ref.shape[0]
    ka = ka_ref[...].astype(BF16)
    vat1 = _with_ones_rows(va_ref[...].T.astype(BF16))
    scores_a = [_dot_nt(ka, _stack_group_queries(qa_ref, g, slice(None))) for g in range(A_KV_HEADS)]
    scores_b, vbt1 = [], []
    for h in range(B_HEADS):
        sl = slice(h * LANES, (h + 1) * LANES)
        scores_b.append(_dot_nt(kb_ref[:, sl].astype(BF16), _stack_pair_queries(qb_ref[:, sl])))
        vbt1.append(_with_ones_rows(vb_ref[:, sl].T.astype(BF16)))
    heads = []
    for g in range(A_KV_HEADS):
        heads += _a_heads(_softmax_values([(scores_a[g], vat1)], _sink_row(sink_ref, g, T)), g, T)
    nqa = A_HEADS * HEAD_DIM
    o_ref[:, :nqa] = jnp.concatenate(heads, axis=0).T.astype(o_ref.dtype)
    lam = _diff_lambda(lamqk_ref[...], lam_init)
    for h in range(B_HEADS):
        ot = _softmax_values([(scores_b[h], vbt1[h])])
        o_ref[:, nqa + h * LANES:nqa + (h + 1) * LANES] = (
            _diff_combine(ot, lam, subln_ref[...], lam_init).astype(o_ref.dtype))


def _ctx_attention(qa, ka, va, qb, kb, vb, sink, lam_qk, subln, lam_init):
    B, T, _ = qa.shape
    blk = lambda w: pl.BlockSpec((None, T, w), lambda b: (b, 0, 0))
    full = lambda a: pl.BlockSpec(a.shape, lambda b: (0,) * a.ndim)
    return pl.pallas_call(
        functools.partial(_ctx_attn_kernel, lam_init=lam_init),
        grid=(B,),
        in_specs=[pl.BlockSpec(memory_space=pltpu.SMEM), blk(QA_W), blk(KA_W), blk(KA_W), blk(B_W), blk(B_W),
                  blk(B_W), full(lam_qk), full(subln)],
        out_specs=blk(A_HEADS * HEAD_DIM + B_W),
        out_shape=jax.ShapeDtypeStruct((B, T, A_HEADS * HEAD_DIM + B_W), BF16),
        compiler_params=_cparams("parallel"),
        name="ctx_attention",
    )(sink, qa, ka, va, qb, kb, vb, lam_qk, subln)


def _lat_a_kernel(sink_ref, qa_ref, k_ref, vt_ref, ck_ref, cvt_ref, o_ref, *, tq, nq, band):
    T = k_ref.shape[0]
    cols = A_GROUP * tq
    cvt1 = _with_ones_rows(cvt_ref[...])
    jobs = []
    for i in range(nq):
        qi = pl.program_id(1) * nq + i
        start = pl.multiple_of(jnp.clip(qi * tq - WINDOW, 0, T - band), WINDOW)
        kb = k_ref[pl.ds(start, band), :]
        vbt1 = _with_ones_rows(vt_ref[:, pl.ds(start, band)])
        kpos = start + lax.broadcasted_iota(jnp.int32, (band, cols), 0)
        qpos = qi * tq + lax.broadcasted_iota(jnp.int32, (band, cols), 1) % tq
        keep = jnp.abs(qpos - kpos) <= WINDOW
        for g in range(A_KV_HEADS):
            qg = _stack_group_queries(qa_ref, g, slice(i * tq, (i + 1) * tq))
            jobs.append((_dot_nt(ck_ref[...], qg), _dot_nt(kb, qg), keep, vbt1))
    for i in range(nq):
        heads = []
        for g in range(A_KV_HEADS):
            s_c, s_b, keep, vbt1 = jobs[i * A_KV_HEADS + g]
            ot = _softmax_values([(s_c, cvt1), (jnp.where(keep, s_b, NEG), vbt1)], _sink_row(sink_ref, g, tq))
            heads += _a_heads(ot, g, tq)
        o_ref[i * tq:(i + 1) * tq, :] = jnp.concatenate(heads, axis=0).T.astype(o_ref.dtype)


def _lat_a_attention(qa, ka, vat, cka, cvat, sink):
    B, T, _ = qa.shape
    L = cka.shape[1]
    tq, nq = WINDOW, 8
    band = 3 * WINDOW
    return pl.pallas_call(
        functools.partial(_lat_a_kernel, tq=tq, nq=nq, band=band),
        grid=(B, T // (tq * nq)),
        in_specs=[
            pl.BlockSpec(memory_space=pltpu.SMEM),
            pl.BlockSpec((None, tq * nq, QA_W), lambda b, i: (b, i, 0)),
            pl.BlockSpec((None, T, KA_W), lambda b, i: (b, 0, 0)),
            pl.BlockSpec((None, KA_W, T), lambda b, i: (b, 0, 0)),
            pl.BlockSpec((None, L, KA_W), lambda b, i: (b, 0, 0)),
            pl.BlockSpec((None, KA_W, L), lambda b, i: (b, 0, 0)),
        ],
        out_specs=pl.BlockSpec((None, tq * nq, A_HEADS * HEAD_DIM), lambda b, i: (b, i, 0)),
        out_shape=jax.ShapeDtypeStruct((B, T, A_HEADS * HEAD_DIM), BF16),
        compiler_params=_cparams("parallel", "parallel"),
        name="lat_a_attention",
    )(sink, qa, ka, vat, cka, cvat)


def _lat_b_kernel(q_ref, k_ref, vt_ref, ck_ref, cvt_ref, lamqk_ref, subln_ref, o_ref, *, tk, ahead, lam_init):
    tq = q_ref.shape[0]
    T = k_ref.shape[0]
    qs = _stack_pair_queries(q_ref[...])
    blocks = [(ck_ref[...], cvt_ref[...])]
    blocks += [(k_ref[j * tk:(j + 1) * tk, :], vt_ref[:, j * tk:(j + 1) * tk]) for j in range(T // tk)]

    def scores(k):
        return _dot_nt(k, qs)

    def accumulate(s, vt, carry):
        m, acc = carry
        m_new = jnp.maximum(m, jnp.max(s, axis=0, keepdims=True))
        p = jnp.exp2(s - m_new).astype(BF16)
        acc = jnp.exp2(m - m_new) * acc + _dot(_with_ones_rows(vt), p)
        return m_new, acc

    carry = (jnp.full((1, 2 * tq), -jnp.inf, F32), jnp.zeros((LANES + ONES_ROWS, 2 * tq), F32))
    pending = [scores(blocks[j][0]) for j in range(ahead)]
    for j in range(len(blocks)):
        if j + ahead < len(blocks):
            pending.append(scores(blocks[j + ahead][0]))
        carry = accumulate(pending.pop(0), blocks[j][1], carry)
    _, acc = carry
    ot = acc[:LANES] / acc[LANES:LANES + 1]
    lam = _diff_lambda(lamqk_ref[...], lam_init)
    o_ref[...] = _diff_combine(ot, lam, subln_ref[...], lam_init).astype(o_ref.dtype)


def _lat_b_attention(qb, kb, vbt, ckb, cvbt, lam_qk, subln, lam_init):
    B, T, _ = qb.shape
    L = ckb.shape[1]
    tq = 1024
    full = lambda a: pl.BlockSpec(a.shape, lambda b, h, i: (0,) * a.ndim)
    return pl.pallas_call(
        functools.partial(_lat_b_kernel, tk=512, ahead=2, lam_init=lam_init),
        grid=(B, B_HEADS, T // tq),
        in_specs=[
            pl.BlockSpec((None, tq, LANES), lambda b, h, i: (b, i, h)),
            pl.BlockSpec((None, T, LANES), lambda b, h, i: (b, 0, h)),
            pl.BlockSpec((None, LANES, T), lambda b, h, i: (b, h, 0)),
            pl.BlockSpec((None, L, LANES), lambda b, h, i: (b, 0, h)),
            pl.BlockSpec((None, LANES, L), lambda b, h, i: (b, h, 0)),
            full(lam_qk), full(subln),
        ],
        out_specs=pl.BlockSpec((None, tq, LANES), lambda b, h, i: (b, i, h)),
        out_shape=jax.ShapeDtypeStruct((B, T, B_W), BF16),
        compiler_params=_cparams("parallel", "parallel", "parallel"),
        name="lat_b_attention",
    )(qb, kb, vbt, ckb, cvbt, lam_qk, subln)


def _scan_chunk_len(n_steps):
    chunk = -(-n_steps // SUBLANES)
    while chunk % 8 != 4:
        chunk += 1
    return chunk


class _RecBlock(NamedTuple):
    gate: object
    xr: object
    cw: object
    cb: object
    wbd: object
    bbd: object
    lam: object
    h0f: object
    h0b: object
    y: object
    sf: object
    sb: object
    xp: object
    af: object
    uf: object
    ab: object
    ub: object
    pf: object
    hf: object
    pb: object
    hb: object


def _rec_kernel(gate_ref, xr_ref, cw_ref, cb_ref, wbd_ref, bbd_ref, lam_ref, h0f_ref, h0b_ref,
                y_ref, sf_ref, sb_ref, *scratch, nblk, tc, gates_unroll, chunk, unroll):
    blocks = []
    for n in range(nblk):
        sl = slice(n * LANES, (n + 1) * LANES)
        lanes = [r.at[:, sl] for r in (gate_ref, xr_ref, cw_ref, cb_ref)]
        lanes += [wbd_ref.at[n], bbd_ref.at[n]]
        lanes += [r.at[:, sl] for r in (lam_ref, h0f_ref, h0b_ref, y_ref, sf_ref, sb_ref)]
        xp, af, uf, ab, ub = [r.at[n] for r in scratch]
        blocks.append(_RecBlock(*lanes, xp, af, uf, ab, ub, pf=af, hf=uf, pb=ab, hb=ub))
    for blk in blocks:
        _rec_gates(blk, tc=tc, gates_unroll=gates_unroll, chunk=chunk)
    _rec_scan(blocks, chunk=chunk, unroll=unroll)
    for blk in blocks:
        _rec_combine(blk, tc=tc)


def _rec_gates(blk, *, tc, gates_unroll, chunk):
    T = blk.xr.shape[0]
    pad = SUBLANES
    xp = blk.xp
    xp[0:pad, :] = jnp.zeros((pad, LANES), F32)
    xp[T + pad:T + 2 * pad, :] = jnp.zeros((pad, LANES), F32)
    xp[pad:T + pad, :] = blk.xr[...]
    tail = SUBLANES * chunk - T
    for a_s, u_s in ((blk.af, blk.uf), (blk.ab, blk.ub)):
        a_s[T:T + tail, :] = jnp.ones((tail, LANES), F32)
        u_s[T:T + tail, :] = jnp.zeros((tail, LANES), F32)
    cw = blk.cw[...]
    cb = blk.cb[...]
    nl = -blk.lam[...]
    softplus = jnp.maximum(nl, 0.0) + jnp.log1p(jnp.exp(-jnp.abs(nl)))
    cp = (0.5 * RGLRU_C) * softplus

    def gates(it, _):
        for q in range(gates_unroll):
            gate_chunk(it * gates_unroll + q)
        return 0

    def gate_chunk(ci):
        t0 = pl.multiple_of(ci * tc, tc)
        y = cb
        for j in range(CONV_W):
            y = y + xp[pl.ds(t0 + (pad - CONV_LEFT + j), tc), :] * cw[j:j + 1, :]
        t = jnp.tanh(_dot(y.astype(BF16), blk.wbd[...]) + blk.bbd[...])
        yh = 0.5 * y
        for d, (a_s, u_s) in enumerate(((blk.af, blk.uf), (blk.ab, blk.ub))):
            t_r = t[:, 2 * d * LANES:(2 * d + 1) * LANES]
            t_i = t[:, (2 * d + 1) * LANES:(2 * d + 2) * LANES]
            neg_log_a = cp[d:d + 1, :] * t_r + cp[d:d + 1, :]
            a = jnp.exp2(neg_log_a * (-LOG2E))
            w = jnp.tanh(neg_log_a) * (a * a + 1.0)
            sqrt_w = jnp.where(w > 0.0, w * lax.rsqrt(w), 0.0)
            a_s[pl.ds(t0, tc), :] = a
            u_s[pl.ds(t0, tc), :] = (t_i + 1.0) * (yh * sqrt_w)

    lax.fori_loop(0, T // (tc * gates_unroll), gates, 0)


def _rec_scan(blocks, *, chunk, unroll):
    T = blocks[0].xr.shape[0]

    def rows(k):
        return pl.ds(k, SUBLANES, stride=chunk)

    def local_scan(it, carry):
        carry = list(carry)
        for q in range(unroll):
            k = it * unroll + q
            kb = chunk - 1 - k
            for i, blk in enumerate(blocks):
                hf, pf, hb, pb = carry[4 * i:4 * i + 4]
                a = blk.af[rows(k), :]
                hf = a * hf + blk.uf[rows(k), :]
                pf = a * pf
                blk.hf[rows(k), :] = hf
                blk.pf[rows(k), :] = pf
                a = blk.ab[rows(kb), :]
                hb = a * hb + blk.ub[rows(kb), :]
                pb = a * pb
                blk.hb[rows(kb), :] = hb
                blk.pb[rows(kb), :] = pb
                carry[4 * i:4 * i + 4] = [hf, pf, hb, pb]
        return tuple(carry)

    zero = jnp.zeros((SUBLANES, LANES), F32)
    one = jnp.ones((SUBLANES, LANES), F32)
    ends = lax.fori_loop(0, chunk // unroll, local_scan, (zero, one, zero, one) * len(blocks))

    carries = []
    for i, blk in enumerate(blocks):
        hf, pf, hb, pb = ends[4 * i:4 * i + 4]
        cf = [blk.h0f[...]]
        for r in range(SUBLANES - 1):
            cf.append(hf[r:r + 1, :] + pf[r:r + 1, :] * cf[r])
        cb_rev = [blk.h0b[...]]
        for r in range(SUBLANES - 1, 0, -1):
            cb_rev.append(hb[r:r + 1, :] + pb[r:r + 1, :] * cb_rev[-1])
        carries.append((jnp.concatenate(cf, axis=0), jnp.concatenate(cb_rev[::-1], axis=0)))

    def apply_carry(it, _):
        for q in range(unroll):
            k = it * unroll + q
            for blk, (carry_f, carry_b) in zip(blocks, carries):
                blk.uf[rows(k), :] = blk.hf[rows(k), :] + blk.pf[rows(k), :] * carry_f
                blk.ub[rows(k), :] = blk.hb[rows(k), :] + blk.pb[rows(k), :] * carry_b
        return 0

    lax.fori_loop(0, chunk // unroll, apply_carry, 0)
    for blk in blocks:
        blk.sf[...] = blk.uf[T - 1:T, :]
        blk.sb[...] = blk.ub[0:1, :]


def _rec_combine(blk, *, tc):
    T = blk.xr.shape[0]

    def combine(ci, _):
        t0 = pl.multiple_of(ci * tc, tc)
        blk.y[pl.ds(t0, tc), :] = ((blk.uf[pl.ds(t0, tc), :] + blk.ub[pl.ds(t0, tc), :])
                                   * blk.gate[pl.ds(t0, tc), :]).astype(blk.y.dtype)
        return 0

    lax.fori_loop(0, T // tc, combine, 0)


def _rec_mixer(gate, xr, h0f, h0b, conv_w, conv_b, w_bd, b_bd, lam, *, nblk, tc, name):
    B, T, _ = xr.shape
    chunk = _scan_chunk_len(T)
    wid = nblk * LANES
    col = lambda rows: pl.BlockSpec((rows, wid), lambda b, n: (0, n))
    seq = pl.BlockSpec((None, T, wid), lambda b, n: (b, 0, n))
    st = pl.BlockSpec((None, 1, wid), lambda b, n: (b, 0, n))
    return pl.pallas_call(
        functools.partial(_rec_kernel, nblk=nblk, tc=tc, gates_unroll=2, chunk=chunk,
                          unroll=max(u for u in range(1, SCAN_MAX_UNROLL + 1) if chunk % u == 0)),
        grid=(B, RNN_BLOCKS // nblk),
        in_specs=[seq, seq, col(CONV_W), col(1),
                  pl.BlockSpec((nblk, LANES, 4 * LANES), lambda b, n: (n, 0, 0)),
                  pl.BlockSpec((nblk, 1, 4 * LANES), lambda b, n: (n, 0, 0)),
                  col(2), st, st],
        out_specs=[seq, st, st],
        out_shape=[jax.ShapeDtypeStruct((B, T, D_RNN), BF16),
                   jax.ShapeDtypeStruct((B, 1, D_RNN), F32),
                   jax.ShapeDtypeStruct((B, 1, D_RNN), F32)],
        scratch_shapes=[pltpu.VMEM((nblk, T + 2 * SUBLANES, LANES), F32)]
        + [pltpu.VMEM((nblk, SUBLANES * chunk, LANES), F32)] * 4,
        compiler_params=_cparams("parallel", "parallel"),
        name=name,
    )(gate, xr, conv_w, conv_b.reshape(1, D_RNN), w_bd, b_bd, lam, h0f, h0b)


def _post_kernel(*refs, n_mix, final, fc):
    x_ref, mod_ref, g2_ref = refs[:3]
    mix = refs[3:3 + 2 * n_mix]
    w1_ref, w2_ref = refs[3 + 2 * n_mix:5 + 2 * n_mix]
    rest = refs[5 + 2 * n_mix:]
    if final:
        gf_ref, o_ref = rest
    else:
        (o_ref,) = rest
    mixed = _dot(mix[0][...], mix[1][...])
    for i in range(1, n_mix):
        mixed = mixed + _dot(mix[2 * i][...], mix[2 * i + 1][...])
    x1 = x_ref[...] + mod_ref[2:3, :] * mixed
    h = _rms(x1, g2_ref[...])
    hb = (h * (1.0 + mod_ref[4:5, :]) + mod_ref[3:4, :]).astype(BF16)
    n_chunks = D_FF // fc
    up = lambda c: _dot(hb, w1_ref[:, c * fc:(c + 1) * fc])
    acc = None
    a = up(0)
    for c in range(n_chunks):
        a_next = up(c + 1) if c + 1 < n_chunks else None
        part = _dot(jnp.square(jnp.maximum(a, 0.0)).astype(BF16), w2_ref[c * fc:(c + 1) * fc, :])
        acc = part if acc is None else acc + part
        a = a_next
    x2 = x1 + mod_ref[5:6, :] * acc
    if final:
        x2 = _rms(x2, gf_ref[...])
    o_ref[...] = x2


def _post(x, mod, g2, mixes, w1, w2, *, tm, ctx, final_g=None, name):
    B, T, D = x.shape
    row = (lambda b, i: (CTX_ROW, 0, 0)) if ctx else (lambda b, i: (b, 0, 0))
    const = lambda a: pl.BlockSpec(a.shape, lambda b, i: (0,) * a.ndim, pipeline_mode=pl.Buffered(1))
    in_specs = [
        pl.BlockSpec((None, tm, D), lambda b, i: (b, i, 0)),
        pl.BlockSpec((None, 6, D), row),
        pl.BlockSpec((1, D), lambda b, i: (0, 0)),
    ]
    args = [x, mod, g2.reshape(1, D)]
    for o, w in mixes:
        in_specs += [pl.BlockSpec((None, tm, o.shape[-1]), lambda b, i: (b, i, 0)), const(w)]
        args += [o, w]
    in_specs += [const(w1), const(w2)]
    args += [w1, w2]
    if final_g is not None:
        in_specs.append(pl.BlockSpec((1, D), lambda b, i: (0, 0)))
        args.append(final_g.reshape(1, D))
    return pl.pallas_call(
        functools.partial(_post_kernel, n_mix=len(mixes), final=final_g is not None, fc=1024),
        grid=(B, T // tm),
        in_specs=in_specs,
        out_specs=pl.BlockSpec((None, tm, D), lambda b, i: (b, i, 0)),
        out_shape=jax.ShapeDtypeStruct((B, T, D), F32),
        compiler_params=_cparams("parallel", "parallel"),
        name=name,
    )(*args)


def _rope_tables(n_tokens):
    rows = n_tokens // GRID_W
    r, cl = jnp.meshgrid(jnp.arange(rows, dtype=F32), jnp.arange(GRID_W, dtype=F32), indexing='ij')
    quarter = HEAD_DIM // 4
    inv = ROPE_BASE ** (-jnp.arange(quarter, dtype=F32) / quarter)
    ang = jnp.stack([r.reshape(-1)[:, None] * inv, cl.reshape(-1)[:, None] * inv], axis=1)
    cos, sin = jnp.cos(ang), jnp.sin(ang)
    cos64 = jnp.concatenate([cos[:, 0], cos[:, 0], cos[:, 1], cos[:, 1]], axis=-1)
    sin64 = jnp.concatenate([-sin[:, 0], sin[:, 0], -sin[:, 1], sin[:, 1]], axis=-1)
    return jnp.tile(cos64, (1, LANES // HEAD_DIM)), jnp.tile(sin64, (1, LANES // HEAD_DIM))


def _att_in_weights(w_in):
    d = w_in.shape[0]
    nq = A_HEADS * HEAD_DIM
    wq = w_in[:, :nq].reshape(d, A_HEADS, HEAD_DIM)
    z = jnp.zeros_like(wq)
    in_first = (jnp.arange(A_HEADS) // A_GROUP == 0)[None, :, None]
    wq = jnp.where(in_first, jnp.concatenate([wq, z], axis=-1), jnp.concatenate([z, wq], axis=-1))
    return jnp.concatenate([wq.reshape(d, QA_W), w_in[:, nq:]], axis=1).astype(BF16)


def _block_diag_weights(w_a, b_a, w_x, b_x):
    w = jnp.concatenate([w_a[0], w_x[0], w_a[1], w_x[1]], axis=-1)
    b = jnp.concatenate([v.reshape(RNN_BLOCKS, 1, RNN_BW) for v in (b_a[0], b_x[0], b_a[1], b_x[1])], axis=-1)
    return (0.5 * w).astype(BF16), 0.5 * b


def kernel(x_prompt, x_sample, cache_a_k, cache_a_v, cache_b_k, cache_b_v, state_fwd, state_bwd, c, c_ctx, norm1, norm2, w_ada, b_ada, w_mlp1, w_mlp2, att_w_in, att_w_out, att_sink, att_lam_qk, att_subln, rec_w_in, rec_conv_w, rec_conv_b, rec_w_a, rec_b_a, rec_w_x, rec_b_x, rec_lam, rec_w_out, final_norm):
    nb, n_seq, _ = x_prompt.shape
    nd, d_seq, _ = x_sample.shape
    past = cache_a_k.shape[2]
    assert nd <= CTX_ROW and DEPTH == 2
    cvec = jnp.concatenate([c, jnp.zeros((CTX_ROW - nd, D_MODEL), F32), c_ctx[None],
                            jnp.zeros((MOD_ROWS - CTX_ROW - 1, D_MODEL), F32)], axis=0)
    mod = _ada_mod(cvec, w_ada, b_ada)
    w1 = w_mlp1.astype(BF16)
    w2 = w_mlp2.astype(BF16)
    tm_ctx = tm_lat = 512
    tok = lambda a: a.reshape(-1, tm_ctx, a.shape[-1])
    seq = lambda a: a.reshape(nb, n_seq, a.shape[-1])

    lam_init = 0.8 - 0.6 * math.exp(-0.3 * 0)
    w_in = _att_in_weights(att_w_in[0])
    w_out = att_w_out[0].astype(BF16)
    nqa = A_HEADS * HEAD_DIM
    c0 = [0, QA_W, QA_W + KA_W, QA_W + 2 * KA_W, QA_W + 2 * KA_W + B_W, QA_W + 2 * KA_W + 2 * B_W]
    widths = [QA_W, KA_W, KA_W, B_W, B_W, B_W]
    scales = [SCALE * LOG2E, 1.0, 1.0, SCALE * LOG2E, 1.0, 1.0]
    roped = [True, True, False, True, True, False]
    sink = att_sink[0]
    lam_qk = att_lam_qk[0]
    subln = att_subln[0].reshape(1, 2 * HEAD_DIM)

    rows = lambda dt: (("rows", dt),)
    cache_b = ((("heads", n_seq), F32), ("rows", BF16))
    segs_ctx = [_Seg(c0[0], QA_W, rows(BF16), scale=scales[0]), _Seg(c0[1], KA_W, rows(F32)),
                _Seg(c0[2], KA_W, rows(F32)), _Seg(c0[3], B_W, rows(BF16), scale=scales[3]),
                _Seg(c0[4], B_W, cache_b), _Seg(c0[5], B_W, cache_b)]
    qa, ka, va, qb, new_b_k, kb, new_b_v, vb = _proj(tok(x_prompt), mod[0], norm1[0], w_in, segs_ctx, tm=tm_ctx,
                                                     ctx=True, name="proj_att_ctx")
    qa, ka, va, qb, kb, vb = map(seq, (qa, ka, va, qb, kb, vb))
    o_ctx = _ctx_attention(qa, ka, va, qb, kb, vb, sink, lam_qk, subln, lam_init)
    xp = _post(tok(x_prompt), mod[0], norm2[0], [(tok(o_ctx), w_out)], w1[0], w2[0], tm=tm_ctx, ctx=True,
               name="post_att_ctx")
    new_a_k = ka.reshape(nb, 1, n_seq, A_KV_HEADS, HEAD_DIM)
    new_a_v = va.reshape(nb, 1, n_seq, A_KV_HEADS, HEAD_DIM)

    segs_lat = [_Seg(c0[i], widths[i], (("cols" if i in (2, 5) else "rows", BF16),), rope=roped[i], scale=scales[i])
                for i in range(6)]
    qa, ka, vat, qb, kb, vbt = _proj(x_sample, mod[0], norm1[0], w_in, segs_lat, tm=tm_lat, ctx=False,
                                     rope_tabs=_rope_tables(d_seq), name="proj_att_lat")
    cka = cache_a_k[:, 0].reshape(nd, past, KA_W).astype(BF16)
    cvat = jnp.swapaxes(cache_a_v[:, 0].reshape(nd, past, KA_W), 1, 2).astype(BF16)
    ckb = cache_b_k[:, 0].reshape(nd, past, B_W).astype(BF16)
    cvbt = jnp.swapaxes(cache_b_v[:, 0].reshape(nd, past, B_W), 1, 2).astype(BF16)
    oa = _lat_a_attention(qa, ka, vat, cka, cvat, sink)
    ob = _lat_b_attention(qb, kb, vbt, ckb, cvbt, lam_qk, subln, lam_init)
    xs = _post(x_sample, mod[0], norm2[0], [(oa, w_out[:nqa]), (ob, w_out[nqa:])], w1[0], w2[0], tm=tm_lat,
               ctx=False, name="post_att_lat")

    w_rin = rec_w_in[0].astype(BF16)
    w_rout = rec_w_out[0].astype(BF16)
    w_bd, b_bd = _block_diag_weights(rec_w_a[0], rec_b_a[0], rec_w_x[0], rec_b_x[0])
    segs_rec = [_Seg(0, D_RNN, rows(F32), gelu=True), _Seg(D_RNN, D_RNN, rows(F32))]
    zeros = jnp.zeros((nb, 1, D_RNN), F32)

    gate, xr = map(seq, _proj(xp, mod[1], norm1[1], w_rin, segs_rec, tm=tm_ctx, ctx=True,
                              name="proj_rec_ctx"))
    y, sf, sb = _rec_mixer(gate, xr, zeros, zeros, rec_conv_w[0], rec_conv_b[0], w_bd, b_bd, rec_lam[0],
                           nblk=RNN_BLOCKS, tc=128, name="rec_mixer_ctx")
    y_prompt = seq(_post(xp, mod[1], norm2[1], [(tok(y), w_rout)], w1[1], w2[1], tm=tm_ctx, ctx=True,
                         final_g=final_norm, name="post_rec_ctx"))

    gate, xr = _proj(xs, mod[1], norm1[1], w_rin, segs_rec, tm=tm_lat, ctx=False, name="proj_rec_lat")
    y, _, _ = _rec_mixer(gate, xr, state_fwd[:, 0:1], state_bwd[:, 0:1], rec_conv_w[0], rec_conv_b[0], w_bd, b_bd,
                         rec_lam[0], nblk=2, tc=256, name="rec_mixer_lat")
    y_sample = _post(xs, mod[1], norm2[1], [(y, w_rout)], w1[1], w2[1], tm=tm_lat, ctx=False, final_g=final_norm,
                     name="post_rec_lat")

    return (y_prompt, y_sample, new_a_k, new_a_v, new_b_k, new_b_v, sf, sb)
```

```python
import functools
import math
from typing import NamedTuple

import jax
import jax.numpy as jnp
import numpy as np
from jax import lax
from jax.experimental import pallas as pl
from jax.experimental.pallas import tpu as pltpu

F32 = jnp.float32
BF16 = jnp.bfloat16

LANES = 128
SUBLANES = 8
VMEM_LIMIT_BYTES = 56 * 1024 * 1024

D_MODEL = 1024
DEPTH = 2
GRID_W = 64
HEAD_DIM = 64
A_HEADS = 8
A_KV_HEADS = 2
A_GROUP = A_HEADS // A_KV_HEADS
B_HEADS = 4
WINDOW = 128
ROPE_BASE = 10000.0
D_RNN = 1280
RNN_BLOCKS = 10
RNN_BW = D_RNN // RNN_BLOCKS
CONV_W = 4
CONV_LEFT = (CONV_W - 1) // 2
RGLRU_C = 8.0
D_FF = 4 * D_MODEL
EPS = 1e-6
SCALE = HEAD_DIM ** -0.5
NEG = -1e30

QA_W = A_HEADS * LANES
KA_W = A_KV_HEADS * HEAD_DIM
B_W = B_HEADS * 2 * HEAD_DIM
MOD_ROWS = 8
CTX_ROW = 4
LOG2E = math.log2(math.e)
ONES_ROWS = 16
SCAN_MAX_UNROLL = 1024


def _cparams(*semantics):
    return pltpu.CompilerParams(dimension_semantics=semantics, vmem_limit_bytes=VMEM_LIMIT_BYTES)


def _dot(a, b):
    return jnp.dot(a, b, preferred_element_type=F32)


def _dot_nt(a, b):
    return lax.dot_general(a, b, (((1,), (1,)), ((), ())), preferred_element_type=F32)


def _rms(x, g):
    return x * lax.rsqrt(jnp.mean(x * x, axis=-1, keepdims=True) + EPS) * g


def _gelu_tanh(x):
    return x * (0.5 * (1.0 + jnp.tanh(math.sqrt(2.0 / math.pi) * (x + 0.044715 * (x * x * x)))))


def _ada_kernel(c_ref, w_ref, b_ref, o_ref):
    c = c_ref[...]
    s = c * jax.nn.sigmoid(c)
    o_ref[...] = _dot(s.astype(BF16), w_ref[...].astype(BF16)) + b_ref[...]


def _ada_mod(cvec, w_ada, b_ada):
    tn = 1536
    out = pl.pallas_call(
        _ada_kernel,
        grid=(DEPTH, 6 * D_MODEL // tn),
        in_specs=[
            pl.BlockSpec((MOD_ROWS, D_MODEL), lambda l, j: (0, 0)),
            pl.BlockSpec((None, D_MODEL, tn), lambda l, j: (l, 0, j)),
            pl.BlockSpec((None, 1, tn), lambda l, j: (l, 0, j)),
        ],
        out_specs=pl.BlockSpec((None, MOD_ROWS, tn), lambda l, j: (l, 0, j)),
        out_shape=jax.ShapeDtypeStruct((DEPTH, MOD_ROWS, 6 * D_MODEL), F32),
        compiler_params=_cparams("parallel", "parallel"),
        name="ada_mod",
    )(cvec, w_ada, b_ada.reshape(DEPTH, 1, 6 * D_MODEL))
    return out.reshape(DEPTH, MOD_ROWS, 6, D_MODEL)


class _Seg(NamedTuple):
    col0: int
    width: int
    outs: tuple
    rope: bool = False
    scale: float = 1.0
    gelu: bool = False


def _proj_kernel(*refs, segs, rope):
    if rope:
        x_ref, mod_ref, g_ref, w_ref, cos_ref, sin_ref, *outs = refs
    else:
        x_ref, mod_ref, g_ref, w_ref, *outs = refs
    x = x_ref[...]
    h = _rms(x, g_ref[...])
    h = h * (1.0 + mod_ref[1:2, :]) + mod_ref[0:1, :]
    hb = h.astype(BF16)
    tm = x.shape[0]
    if rope:
        lane = lax.broadcasted_iota(jnp.int32, (tm, LANES), 1)
        first = (lane & 16) == 0
        cos = cos_ref[...]
        sin = sin_ref[...]
    outs = iter(outs)
    for seg in segs:
        y = _dot(hb, w_ref[:, seg.col0:seg.col0 + seg.width])
        o_refs = [next(outs) for _ in seg.outs]
        for t in range(seg.width // LANES):
            yt = y[:, t * LANES:(t + 1) * LANES]
            if seg.rope:
                sw = jnp.where(first, pltpu.roll(yt, LANES - 16, 1), pltpu.roll(yt, 16, 1))
                yt = yt * cos + sw * sin
            if seg.scale != 1.0:
                yt = yt * seg.scale
            if seg.gelu:
                yt = _gelu_tanh(yt)
            for (layout, _), o_ref in zip(seg.outs, o_refs):
                if layout == "cols":
                    o_ref[t * LANES:(t + 1) * LANES, :] = yt.T.astype(o_ref.dtype)
                elif layout == "rows":
                    o_ref[:, t * LANES:(t + 1) * LANES] = yt.astype(o_ref.dtype)
                else:
                    seq_len = layout[1]
                    for r in range(tm // seq_len):
                        o_ref[r, :, t, :] = yt[r * seq_len:(r + 1) * seq_len].astype(o_ref.dtype)


def _proj(x, mod, g, w, segs, *, tm, ctx, rope_tabs=None, name):
    B, T, D = x.shape

    def out_spec(width, layout):
        if layout == "cols":
            return pl.BlockSpec((None, width, tm), lambda b, i: (b, 0, i))
        if layout == "rows":
            return pl.BlockSpec((None, tm, width), lambda b, i: (b, i, 0))
        return pl.BlockSpec((tm // layout[1], None, layout[1], width // LANES, LANES),
                            lambda b, i: (b * (T // tm) + i, 0, 0, 0, 0))

    def out_struct(width, layout, dt):
        if layout == "cols":
            return jax.ShapeDtypeStruct((B, width, T), dt)
        if layout == "rows":
            return jax.ShapeDtypeStruct((B, T, width), dt)
        return jax.ShapeDtypeStruct((B * T // layout[1], 1, layout[1], width // LANES, LANES), dt)

    row = (lambda b, i: (CTX_ROW, 0, 0)) if ctx else (lambda b, i: (b, 0, 0))
    in_specs = [
        pl.BlockSpec((None, tm, D), lambda b, i: (b, i, 0)),
        pl.BlockSpec((None, 6, D), row),
        pl.BlockSpec((1, D), lambda b, i: (0, 0)),
        pl.BlockSpec(w.shape, lambda b, i: (0, 0)),
    ]
    args = [x, mod, g.reshape(1, D), w]
    if rope_tabs is not None:
        in_specs += [pl.BlockSpec((tm, LANES), lambda b, i: (i, 0))] * 2
        args += list(rope_tabs)
    out_specs = [out_spec(s.width, layout) for s in segs for layout, _ in s.outs]
    out_shape = [out_struct(s.width, layout, dt) for s in segs for layout, dt in s.outs]
    return pl.pallas_call(
        functools.partial(_proj_kernel, segs=tuple(segs), rope=rope_tabs is not None),
        grid=(B, T // tm),
        in_specs=in_specs,
        out_specs=out_specs,
        out_shape=out_shape,
        compiler_params=_cparams("parallel", "parallel"),
        name=name,
    )(*args)


def _diff_lambda(lq, lam_init):
    s1 = jnp.sum(lq[0:1, :] * lq[1:2, :], axis=1, keepdims=True)
    s2 = jnp.sum(lq[2:3, :] * lq[3:4, :], axis=1, keepdims=True)
    return jnp.exp(s1) - jnp.exp(s2) + lam_init


def _stack_group_queries(qa_ref, g, rows):
    return jnp.concatenate(
        [qa_ref[rows, (A_GROUP * g + hh) * LANES:(A_GROUP * g + hh + 1) * LANES] for hh in range(A_GROUP)], axis=0)


def _stack_pair_queries(q):
    lo = lax.broadcasted_iota(jnp.int32, q.shape, 1) < HEAD_DIM
    zero = jnp.zeros_like(q)
    return jnp.concatenate([jnp.where(lo, q, zero), jnp.where(lo, zero, q)], axis=0)


def _with_ones_rows(vt):
    return jnp.concatenate([vt, jnp.ones((ONES_ROWS, vt.shape[1]), BF16)], axis=0)


def _sink_row(sink_ref, g, tq):
    return jnp.concatenate(
        [jnp.full((1, tq), sink_ref[A_GROUP * g + hh] * LOG2E, F32) for hh in range(A_GROUP)], axis=1)


def _softmax_values(parts, sink=None):
    m = functools.reduce(jnp.maximum, [jnp.max(s, axis=0, keepdims=True) for s, _ in parts])
    if sink is not None:
        m = jnp.maximum(m, sink)
    ot = sum(_dot(vt1, jnp.exp2(s - m).astype(BF16)) for s, vt1 in parts)
    den = ot[LANES:LANES + 1]
    if sink is not None:
        den = den + jnp.exp2(sink - m)
    return ot[:LANES] / den


def _a_heads(ot, g, tq):
    return [ot[g * HEAD_DIM:(g + 1) * HEAD_DIM, hh * tq:(hh + 1) * tq] for hh in range(A_GROUP)]


def _subln(o, subln, lam_init):
    return _rms(o, subln) * (1.0 - lam_init)


def _diff_combine(ot, lam, subln, lam_init):
    tq = ot.shape[1] // 2
    return _subln((ot[:, :tq] - lam * ot[:, tq:]).T, subln, lam_init)


def _ctx_attn_kernel(sink_ref, qa_ref, ka_ref, va_ref, qb_ref, kb_ref, vb_ref, lamqk_ref, subln_ref, o_ref, *,
                     lam_init):
    T = qa_ref.shape[0]
    ka = ka_ref[...].astype(BF16)
    vat1 = _with_ones_rows(va_ref[...].T.astype(BF16))
    scores_a = [_dot_nt(ka, _stack_group_queries(qa_ref, g, slice(None))) for g in range(A_KV_HEADS)]
    scores_b, vbt1 = [], []
    for h in range(B_HEADS):
        sl = slice(h * LANES, (h + 1) * LANES)
        scores_b.append(_dot_nt(kb_ref[:, sl].astype(BF16), _stack_pair_queries(qb_ref[:, sl])))
        vbt1.append(_with_ones_rows(vb_ref[:, sl].T.astype(BF16)))
    heads = []
    for g in range(A_KV_HEADS):
        heads += _a_heads(_softmax_values([(scores_a[g], vat1)], _sink_row(sink_ref, g, T)), g, T)
    nqa = A_HEADS * HEAD_DIM
    o_ref[:, :nqa] = jnp.concatenate(heads, axis=0).T.astype(o_ref.dtype)
    lam = _diff_lambda(lamqk_ref[...], lam_init)
    for h in range(B_HEADS):
        ot = _softmax_values([(scores_b[h], vbt1[h])])
        o_ref[:, nqa + h * LANES:nqa + (h + 1) * LANES] = (
            _diff_combine(ot, lam, subln_ref[...], lam_init).astype(o_ref.dtype))


def _ctx_attention(qa, ka, va, qb, kb, vb, sink, lam_qk, subln, lam_init):
    B, T, _ = qa.shape
    blk = lambda w: pl.BlockSpec((None, T, w), lambda b: (b, 0, 0))
    full = lambda a: pl.BlockSpec(a.shape, lambda b: (0,) * a.ndim)
    return pl.pallas_call(
        functools.partial(_ctx_attn_kernel, lam_init=lam_init),
        grid=(B,),
        in_specs=[pl.BlockSpec(memory_space=pltpu.SMEM), blk(QA_W), blk(KA_W), blk(KA_W), blk(B_W), blk(B_W),
                  blk(B_W), full(lam_qk), full(subln)],
        out_specs=blk(A_HEADS * HEAD_DIM + B_W),
        out_shape=jax.ShapeDtypeStruct((B, T, A_HEADS * HEAD_DIM + B_W), BF16),
        compiler_params=_cparams("parallel"),
        name="ctx_attention",
    )(sink, qa, ka, va, qb, kb, vb, lam_qk, subln)


def _lat_a_kernel(sink_ref, qa_ref, k_ref, vt_ref, ck_ref, cvt_ref, o_ref, *, tq, nq, band):
    T = k_ref.shape[0]
    cols = A_GROUP * tq
    cvt1 = _with_ones_rows(cvt_ref[...])
    jobs = []
    for i in range(nq):
        qi = pl.program_id(1) * nq + i
        start = pl.multiple_of(jnp.clip(qi * tq - WINDOW, 0, T - band), WINDOW)
        kb = k_ref[pl.ds(start, band), :]
        vbt1 = _with_ones_rows(vt_ref[:, pl.ds(start, band)])
        kpos = start + lax.broadcasted_iota(jnp.int32, (band, cols), 0)
        qpos = qi * tq + lax.broadcasted_iota(jnp.int32, (band, cols), 1) % tq
        keep = jnp.abs(qpos - kpos) <= WINDOW
        for g in range(A_KV_HEADS):
            qg = _stack_group_queries(qa_ref, g, slice(i * tq, (i + 1) * tq))
            jobs.append((_dot_nt(ck_ref[...], qg), _dot_nt(kb, qg), keep, vbt1))
    for i in range(nq):
        heads = []
        for g in range(A_KV_HEADS):
            s_c, s_b, keep, vbt1 = jobs[i * A_KV_HEADS + g]
            ot = _softmax_values([(s_c, cvt1), (jnp.where(keep, s_b, NEG), vbt1)], _sink_row(sink_ref, g, tq))
            heads += _a_heads(ot, g, tq)
        o_ref[i * tq:(i + 1) * tq, :] = jnp.concatenate(heads, axis=0).T.astype(o_ref.dtype)


def _lat_a_attention(qa, ka, vat, cka, cvat, sink):
    B, T, _ = qa.shape
    L = cka.shape[1]
    tq, nq = WINDOW, 8
    band = 3 * WINDOW
    return pl.pallas_call(
        functools.partial(_lat_a_kernel, tq=tq, nq=nq, band=band),
        grid=(B, T // (tq * nq)),
        in_specs=[
            pl.BlockSpec(memory_space=pltpu.SMEM),
            pl.BlockSpec((None, tq * nq, QA_W), lambda b, i: (b, i, 0)),
            pl.BlockSpec((None, T, KA_W), lambda b, i: (b, 0, 0)),
            pl.BlockSpec((None, KA_W, T), lambda b, i: (b, 0, 0)),
            pl.BlockSpec((None, L, KA_W), lambda b, i: (b, 0, 0)),
            pl.BlockSpec((None, KA_W, L), lambda b, i: (b, 0, 0)),
        ],
        out_specs=pl.BlockSpec((None, tq * nq, A_HEADS * HEAD_DIM), lambda b, i: (b, i, 0)),
        out_shape=jax.ShapeDtypeStruct((B, T, A_HEADS * HEAD_DIM), BF16),
        compiler_params=_cparams("parallel", "parallel"),
        name="lat_a_attention",
    )(sink, qa, ka, vat, cka, cvat)


def _lat_b_kernel(qt_ref, k_ref, vt_ref, ck_ref, cvt_ref, lamqk_ref, subln_ref, o_ref, *, tk, ahead, lam_init):
    tq = qt_ref.shape[1]
    T = k_ref.shape[0]
    qt = qt_ref[...]
    zero = jnp.zeros((HEAD_DIM, tq), BF16)
    qst = jnp.concatenate([jnp.concatenate([qt[:HEAD_DIM], zero], axis=0),
                           jnp.concatenate([zero, qt[HEAD_DIM:]], axis=0)], axis=1)
    blocks = [(ck_ref[...], cvt_ref[...])]
    blocks += [(k_ref[j * tk:(j + 1) * tk, :], vt_ref[:, j * tk:(j + 1) * tk]) for j in range(T // tk)]

    def scores(k):
        return _dot(k, qst)

    def accumulate(s, vt, carry):
        m, acc = carry
        m_new = jnp.maximum(m, jnp.max(s, axis=0, keepdims=True))
        p = jnp.exp2(s - m_new).astype(BF16)
        acc = jnp.exp2(m - m_new) * acc + _dot(_with_ones_rows(vt), p)
        return m_new, acc

    carry = (jnp.full((1, 2 * tq), -jnp.inf, F32), jnp.zeros((LANES + ONES_ROWS, 2 * tq), F32))
    pending = [scores(blocks[j][0]) for j in range(ahead)]
    for j in range(len(blocks)):
        if j + ahead < len(blocks):
            pending.append(scores(blocks[j + ahead][0]))
        carry = accumulate(pending.pop(0), blocks[j][1], carry)
    _, acc = carry
    ot = acc[:LANES] / acc[LANES:LANES + 1]
    lam = _diff_lambda(lamqk_ref[...], lam_init)
    o_ref[...] = _diff_combine(ot, lam, subln_ref[...], lam_init).astype(o_ref.dtype)


def _lat_b_attention(qbt, kb, vbt, ckb, cvbt, lam_qk, subln, lam_init):
    B, T, _ = kb.shape
    L = ckb.shape[1]
    tq = 1024
    full = lambda a: pl.BlockSpec(a.shape, lambda b, h, i: (0,) * a.ndim)
    return pl.pallas_call(
        functools.partial(_lat_b_kernel, tk=512, ahead=2, lam_init=lam_init),
        grid=(B, B_HEADS, T // tq),
        in_specs=[
            pl.BlockSpec((None, LANES, tq), lambda b, h, i: (b, h, i)),
            pl.BlockSpec((None, T, LANES), lambda b, h, i: (b, 0, h)),
            pl.BlockSpec((None, LANES, T), lambda b, h, i: (b, h, 0)),
            pl.BlockSpec((None, L, LANES), lambda b, h, i: (b, 0, h)),
            pl.BlockSpec((None, LANES, L), lambda b, h, i: (b, h, 0)),
            full(lam_qk), full(subln),
        ],
        out_specs=pl.BlockSpec((None, tq, LANES), lambda b, h, i: (b, i, h)),
        out_shape=jax.ShapeDtypeStruct((B, T, B_W), BF16),
        compiler_params=_cparams("parallel", "parallel", "parallel"),
        name="lat_b_attention",
    )(qbt, kb, vbt, ckb, cvbt, lam_qk, subln)


def _scan_chunk_len(n_steps):
    chunk = -(-n_steps // SUBLANES)
    while chunk % 8 != 4:
        chunk += 1
    return chunk


class _RecBlock(NamedTuple):
    gate: object
    xr: object
    cw: object
    cb: object
    wbd: object
    bbd: object
    lam: object
    h0f: object
    h0b: object
    y: object
    sf: object
    sb: object
    xp: object
    af: object
    uf: object
    ab: object
    ub: object
    pf: object
    hf: object
    pb: object
    hb: object


def _rec_kernel(gate_ref, xr_ref, cw_ref, cb_ref, wbd_ref, bbd_ref, lam_ref, h0f_ref, h0b_ref,
                y_ref, sf_ref, sb_ref, *scratch, nblk, tc, gates_unroll, chunk, unroll):
    blocks = []
    for n in range(nblk):
        sl = slice(n * LANES, (n + 1) * LANES)
        lanes = [r.at[:, sl] for r in (gate_ref, xr_ref, cw_ref, cb_ref)]
        lanes += [wbd_ref.at[n], bbd_ref.at[n]]
        lanes += [r.at[:, sl] for r in (lam_ref, h0f_ref, h0b_ref, y_ref, sf_ref, sb_ref)]
        xp, af, uf, ab, ub = [r.at[n] for r in scratch]
        blocks.append(_RecBlock(*lanes, xp, af, uf, ab, ub, pf=af, hf=uf, pb=ab, hb=ub))
    for blk in blocks:
        _rec_gates(blk, tc=tc, gates_unroll=gates_unroll, chunk=chunk)
    _rec_scan(blocks, chunk=chunk, unroll=unroll)
    for blk in blocks:
        _rec_combine(blk, tc=tc)


def _rec_gates(blk, *, tc, gates_unroll, chunk):
    T = blk.xr.shape[0]
    pad = SUBLANES
    xp = blk.xp
    xp[0:pad, :] = jnp.zeros((pad, LANES), F32)
    xp[T + pad:T + 2 * pad, :] = jnp.zeros((pad, LANES), F32)
    xp[pad:T + pad, :] = blk.xr[...]
    tail = SUBLANES * chunk - T
    for a_s, u_s in ((blk.af, blk.uf), (blk.ab, blk.ub)):
        a_s[T:T + tail, :] = jnp.ones((tail, LANES), F32)
        u_s[T:T + tail, :] = jnp.zeros((tail, LANES), F32)
    cw = blk.cw[...]
    cb = blk.cb[...]
    nl = -blk.lam[...]
    softplus = jnp.maximum(nl, 0.0) + jnp.log1p(jnp.exp(-jnp.abs(nl)))
    cp = (0.5 * RGLRU_C) * softplus

    def gates(it, _):
        for q in range(gates_unroll):
            gate_chunk(it * gates_unroll + q)
        return 0

    def gate_chunk(ci):
        t0 = pl.multiple_of(ci * tc, tc)
        y = cb
        for j in range(CONV_W):
            y = y + xp[pl.ds(t0 + (pad - CONV_LEFT + j), tc), :] * cw[j:j + 1, :]
        t = jnp.tanh(_dot(y.astype(BF16), blk.wbd[...]) + blk.bbd[...])
        yh = 0.5 * y
        for d, (a_s, u_s) in enumerate(((blk.af, blk.uf), (blk.ab, blk.ub))):
            t_r = t[:, 2 * d * LANES:(2 * d + 1) * LANES]
            t_i = t[:, (2 * d + 1) * LANES:(2 * d + 2) * LANES]
            neg_log_a = cp[d:d + 1, :] * t_r + cp[d:d + 1, :]
            a = jnp.exp2(neg_log_a * (-LOG2E))
            w = jnp.tanh(neg_log_a) * (a * a + 1.0)
            sqrt_w = jnp.where(w > 0.0, w * lax.rsqrt(w), 0.0)
            a_s[pl.ds(t0, tc), :] = a
            u_s[pl.ds(t0, tc), :] = (t_i + 1.0) * (yh * sqrt_w)

    lax.fori_loop(0, T // (tc * gates_unroll), gates, 0)


def _rec_scan(blocks, *, chunk, unroll):
    T = blocks[0].xr.shape[0]

    def rows(k):
        return pl.ds(k, SUBLANES, stride=chunk)

    def local_scan(it, carry):
        carry = list(carry)
        for q in range(unroll):
            k = it * unroll + q
            kb = chunk - 1 - k
            for i, blk in enumerate(blocks):
                hf, pf, hb, pb = carry[4 * i:4 * i + 4]
                a = blk.af[rows(k), :]
                hf = a * hf + blk.uf[rows(k), :]
                pf = a * pf
                blk.hf[rows(k), :] = hf
                blk.pf[rows(k), :] = pf
                a = blk.ab[rows(kb), :]
                hb = a * hb + blk.ub[rows(kb), :]
                pb = a * pb
                blk.hb[rows(kb), :] = hb
                blk.pb[rows(kb), :] = pb
                carry[4 * i:4 * i + 4] = [hf, pf, hb, pb]
        return tuple(carry)

    zero = jnp.zeros((SUBLANES, LANES), F32)
    one = jnp.ones((SUBLANES, LANES), F32)
    ends = lax.fori_loop(0, chunk // unroll, local_scan, (zero, one, zero, one) * len(blocks))

    carries = []
    for i, blk in enumerate(blocks):
        hf, pf, hb, pb = ends[4 * i:4 * i + 4]
        cf = [blk.h0f[...]]
        for r in range(SUBLANES - 1):
            cf.append(hf[r:r + 1, :] + pf[r:r + 1, :] * cf[r])
        cb_rev = [blk.h0b[...]]
        for r in range(SUBLANES - 1, 0, -1):
            cb_rev.append(hb[r:r + 1, :] + pb[r:r + 1, :] * cb_rev[-1])
        carries.append((jnp.concatenate(cf, axis=0), jnp.concatenate(cb_rev[::-1], axis=0)))

    def apply_carry(it, _):
        for q in range(unroll):
            k = it * unroll + q
            for blk, (carry_f, carry_b) in zip(blocks, carries):
                blk.uf[rows(k), :] = blk.hf[rows(k), :] + blk.pf[rows(k), :] * carry_f
                blk.ub[rows(k), :] = blk.hb[rows(k), :] + blk.pb[rows(k), :] * carry_b
        return 0

    lax.fori_loop(0, chunk // unroll, apply_carry, 0)
    for blk in blocks:
        blk.sf[...] = blk.uf[T - 1:T, :]
        blk.sb[...] = blk.ub[0:1, :]


def _rec_combine(blk, *, tc):
    T = blk.xr.shape[0]

    def combine(ci, _):
        t0 = pl.multiple_of(ci * tc, tc)
        blk.y[pl.ds(t0, tc), :] = ((blk.uf[pl.ds(t0, tc), :] + blk.ub[pl.ds(t0, tc), :])
                                   * blk.gate[pl.ds(t0, tc), :]).astype(blk.y.dtype)
        return 0

    lax.fori_loop(0, T // tc, combine, 0)


def _rec_mixer(gate, xr, h0f, h0b, conv_w, conv_b, w_bd, b_bd, lam, *, nblk, tc, name):
    B, T, _ = xr.shape
    chunk = _scan_chunk_len(T)
    wid = nblk * LANES
    col = lambda rows: pl.BlockSpec((rows, wid), lambda b, n: (0, n))
    seq = pl.BlockSpec((None, T, wid), lambda b, n: (b, 0, n))
    st = pl.BlockSpec((None, 1, wid), lambda b, n: (b, 0, n))
    return pl.pallas_call(
        functools.partial(_rec_kernel, nblk=nblk, tc=tc, gates_unroll=2, chunk=chunk,
                          unroll=max(u for u in range(1, SCAN_MAX_UNROLL + 1) if chunk % u == 0)),
        grid=(B, RNN_BLOCKS // nblk),
        in_specs=[seq, seq, col(CONV_W), col(1),
                  pl.BlockSpec((nblk, LANES, 4 * LANES), lambda b, n: (n, 0, 0)),
                  pl.BlockSpec((nblk, 1, 4 * LANES), lambda b, n: (n, 0, 0)),
                  col(2), st, st],
        out_specs=[seq, st, st],
        out_shape=[jax.ShapeDtypeStruct((B, T, D_RNN), BF16),
                   jax.ShapeDtypeStruct((B, 1, D_RNN), F32),
                   jax.ShapeDtypeStruct((B, 1, D_RNN), F32)],
        scratch_shapes=[pltpu.VMEM((nblk, T + 2 * SUBLANES, LANES), F32)]
        + [pltpu.VMEM((nblk, SUBLANES * chunk, LANES), F32)] * 4,
        compiler_params=_cparams("parallel", "parallel"),
        name=name,
    )(gate, xr, conv_w, conv_b.reshape(1, D_RNN), w_bd, b_bd, lam, h0f, h0b)


def _post_kernel(*refs, n_mix, final, fc):
    x_ref, mod_ref, g2_ref = refs[:3]
    mix = refs[3:3 + 2 * n_mix]
    w1_ref, w2_ref = refs[3 + 2 * n_mix:5 + 2 * n_mix]
    rest = refs[5 + 2 * n_mix:]
    if final:
        gf_ref, o_ref = rest
    else:
        (o_ref,) = rest
    mixed = _dot(mix[0][...], mix[1][...])
    for i in range(1, n_mix):
        mixed = mixed + _dot(mix[2 * i][...], mix[2 * i + 1][...])
    x1 = x_ref[...] + mod_ref[2:3, :] * mixed
    h = _rms(x1, g2_ref[...])
    hb = (h * (1.0 + mod_ref[4:5, :]) + mod_ref[3:4, :]).astype(BF16)
    n_chunks = D_FF // fc
    up = lambda c: _dot(hb, w1_ref[:, c * fc:(c + 1) * fc])
    acc = None
    a = up(0)
    for c in range(n_chunks):
        a_next = up(c + 1) if c + 1 < n_chunks else None
        part = _dot(jnp.square(jnp.maximum(a, 0.0)).astype(BF16), w2_ref[c * fc:(c + 1) * fc, :])
        acc = part if acc is None else acc + part
        a = a_next
    x2 = x1 + mod_ref[5:6, :] * acc
    if final:
        x2 = _rms(x2, gf_ref[...])
    o_ref[...] = x2


def _post(x, mod, g2, mixes, w1, w2, *, tm, ctx, final_g=None, name):
    B, T, D = x.shape
    row = (lambda b, i: (CTX_ROW, 0, 0)) if ctx else (lambda b, i: (b, 0, 0))
    const = lambda a: pl.BlockSpec(a.shape, lambda b, i: (0,) * a.ndim, pipeline_mode=pl.Buffered(1))
    in_specs = [
        pl.BlockSpec((None, tm, D), lambda b, i: (b, i, 0)),
        pl.BlockSpec((None, 6, D), row),
        pl.BlockSpec((1, D), lambda b, i: (0, 0)),
    ]
    args = [x, mod, g2.reshape(1, D)]
    for o, w in mixes:
        in_specs += [pl.BlockSpec((None, tm, o.shape[-1]), lambda b, i: (b, i, 0)), const(w)]
        args += [o, w]
    in_specs += [const(w1), const(w2)]
    args += [w1, w2]
    if final_g is not None:
        in_specs.append(pl.BlockSpec((1, D), lambda b, i: (0, 0)))
        args.append(final_g.reshape(1, D))
    return pl.pallas_call(
        functools.partial(_post_kernel, n_mix=len(mixes), final=final_g is not None, fc=1024),
        grid=(B, T // tm),
        in_specs=in_specs,
        out_specs=pl.BlockSpec((None, tm, D), lambda b, i: (b, i, 0)),
        out_shape=jax.ShapeDtypeStruct((B, T, D), F32),
        compiler_params=_cparams("parallel", "parallel"),
        name=name,
    )(*args)


def _rope_tables(n_tokens):
    rows = n_tokens // GRID_W
    r, cl = jnp.meshgrid(jnp.arange(rows, dtype=F32), jnp.arange(GRID_W, dtype=F32), indexing='ij')
    quarter = HEAD_DIM // 4
    inv = ROPE_BASE ** (-jnp.arange(quarter, dtype=F32) / quarter)
    ang = jnp.stack([r.reshape(-1)[:, None] * inv, cl.reshape(-1)[:, None] * inv], axis=1)
    cos, sin = jnp.cos(ang), jnp.sin(ang)
    cos64 = jnp.concatenate([cos[:, 0], cos[:, 0], cos[:, 1], cos[:, 1]], axis=-1)
    sin64 = jnp.concatenate([-sin[:, 0], sin[:, 0], -sin[:, 1], sin[:, 1]], axis=-1)
    return jnp.tile(cos64, (1, LANES // HEAD_DIM)), jnp.tile(sin64, (1, LANES // HEAD_DIM))


def _att_in_weights(w_in):
    d = w_in.shape[0]
    nq = A_HEADS * HEAD_DIM
    wq = w_in[:, :nq].reshape(d, A_HEADS, HEAD_DIM)
    z = jnp.zeros_like(wq)
    in_first = (jnp.arange(A_HEADS) // A_GROUP == 0)[None, :, None]
    wq = jnp.where(in_first, jnp.concatenate([wq, z], axis=-1), jnp.concatenate([z, wq], axis=-1))
    return jnp.concatenate([wq.reshape(d, QA_W), w_in[:, nq:]], axis=1).astype(BF16)


def _block_diag_weights(w_a, b_a, w_x, b_x):
    w = jnp.concatenate([w_a[0], w_x[0], w_a[1], w_x[1]], axis=-1)
    b = jnp.concatenate([v.reshape(RNN_BLOCKS, 1, RNN_BW) for v in (b_a[0], b_x[0], b_a[1], b_x[1])], axis=-1)
    return (0.5 * w).astype(BF16), 0.5 * b


def kernel(x_prompt, x_sample, cache_a_k, cache_a_v, cache_b_k, cache_b_v, state_fwd, state_bwd, c, c_ctx, norm1, norm2, w_ada, b_ada, w_mlp1, w_mlp2, att_w_in, att_w_out, att_sink, att_lam_qk, att_subln, rec_w_in, rec_conv_w, rec_conv_b, rec_w_a, rec_b_a, rec_w_x, rec_b_x, rec_lam, rec_w_out, final_norm):
    nb, n_seq, _ = x_prompt.shape
    nd, d_seq, _ = x_sample.shape
    past = cache_a_k.shape[2]
    assert nd <= CTX_ROW and DEPTH == 2
    cvec = jnp.concatenate([c, jnp.zeros((CTX_ROW - nd, D_MODEL), F32), c_ctx[None],
                            jnp.zeros((MOD_ROWS - CTX_ROW - 1, D_MODEL), F32)], axis=0)
    mod = _ada_mod(cvec, w_ada, b_ada)
    w1 = w_mlp1.astype(BF16)
    w2 = w_mlp2.astype(BF16)
    tm_ctx = tm_lat = 512
    tok = lambda a: a.reshape(-1, tm_ctx, a.shape[-1])
    seq = lambda a: a.reshape(nb, n_seq, a.shape[-1])

    lam_init = 0.8 - 0.6 * math.exp(-0.3 * 0)
    w_in = _att_in_weights(att_w_in[0])
    w_out = att_w_out[0].astype(BF16)
    nqa = A_HEADS * HEAD_DIM
    c0 = [0, QA_W, QA_W + KA_W, QA_W + 2 * KA_W, QA_W + 2 * KA_W + B_W, QA_W + 2 * KA_W + 2 * B_W]
    widths = [QA_W, KA_W, KA_W, B_W, B_W, B_W]
    scales = [SCALE * LOG2E, 1.0, 1.0, SCALE * LOG2E, 1.0, 1.0]
    roped = [True, True, False, True, True, False]
    sink = att_sink[0]
    lam_qk = att_lam_qk[0]
    subln = att_subln[0].reshape(1, 2 * HEAD_DIM)

    rows = lambda dt: (("rows", dt),)
    cache_b = ((("heads", n_seq), F32), ("rows", BF16))
    segs_ctx = [_Seg(c0[0], QA_W, rows(BF16), scale=scales[0]), _Seg(c0[1], KA_W, rows(F32)),
                _Seg(c0[2], KA_W, rows(F32)), _Seg(c0[3], B_W, rows(BF16), scale=scales[3]),
                _Seg(c0[4], B_W, cache_b), _Seg(c0[5], B_W, cache_b)]
    qa, ka, va, qb, new_b_k, kb, new_b_v, vb = _proj(tok(x_prompt), mod[0], norm1[0], w_in, segs_ctx, tm=tm_ctx,
                                                     ctx=True, name="proj_att_ctx")
    qa, ka, va, qb, kb, vb = map(seq, (qa, ka, va, qb, kb, vb))
    o_ctx = _ctx_attention(qa, ka, va, qb, kb, vb, sink, lam_qk, subln, lam_init)
    xp = _post(tok(x_prompt), mod[0], norm2[0], [(tok(o_ctx), w_out)], w1[0], w2[0], tm=tm_ctx, ctx=True,
               name="post_att_ctx")
    new_a_k = ka.reshape(nb, 1, n_seq, A_KV_HEADS, HEAD_DIM)
    new_a_v = va.reshape(nb, 1, n_seq, A_KV_HEADS, HEAD_DIM)

    segs_lat = [_Seg(c0[i], widths[i], (("cols" if i in (2, 3, 5) else "rows", BF16),), rope=roped[i],
                     scale=scales[i]) for i in range(6)]
    qa, ka, vat, qbt, kb, vbt = _proj(x_sample, mod[0], norm1[0], w_in, segs_lat, tm=tm_lat, ctx=False,
                                      rope_tabs=_rope_tables(d_seq), name="proj_att_lat")
    cka = cache_a_k[:, 0].reshape(nd, past, KA_W).astype(BF16)
    cvat = jnp.swapaxes(cache_a_v[:, 0].reshape(nd, past, KA_W), 1, 2).astype(BF16)
    ckb = cache_b_k[:, 0].reshape(nd, past, B_W).astype(BF16)
    cvbt = jnp.swapaxes(cache_b_v[:, 0].reshape(nd, past, B_W), 1, 2).astype(BF16)
    oa = _lat_a_attention(qa, ka, vat, cka, cvat, sink)
    ob = _lat_b_attention(qbt, kb, vbt, ckb, cvbt, lam_qk, subln, lam_init)
    xs = _post(x_sample, mod[0], norm2[0], [(oa, w_out[:nqa]), (ob, w_out[nqa:])], w1[0], w2[0], tm=tm_lat,
               ctx=False, name="post_att_lat")

    w_rin = rec_w_in[0].astype(BF16)
    w_rout = rec_w_out[0].astype(BF16)
    w_bd, b_bd = _block_diag_weights(rec_w_a[0], rec_b_a[0], rec_w_x[0], rec_b_x[0])
    segs_rec = [_Seg(0, D_RNN, rows(F32), gelu=True), _Seg(D_RNN, D_RNN, rows(F32))]
    zeros = jnp.zeros((nb, 1, D_RNN), F32)

    gate, xr = map(seq, _proj(xp, mod[1], norm1[1], w_rin, segs_rec, tm=tm_ctx, ctx=True,
                              name="proj_rec_ctx"))
    y, sf, sb = _rec_mixer(gate, xr, zeros, zeros, rec_conv_w[0], rec_conv_b[0], w_bd, b_bd, rec_lam[0],
                           nblk=RNN_BLOCKS, tc=128, name="rec_mixer_ctx")
    y_prompt = seq(_post(xp, mod[1], norm2[1], [(tok(y), w_rout)], w1[1], w2[1], tm=tm_ctx, ctx=True,
                         final_g=final_norm, name="post_rec_ctx"))

    gate, xr = _proj(xs, mod[1], norm1[1], w_rin, segs_rec, tm=tm_lat, ctx=False, name="proj_rec_lat")
    y, _, _ = _rec_mixer(gate, xr, state_fwd[:, 0:1], state_bwd[:, 0:1], rec_conv_w[0], rec_conv_b[0], w_bd, b_bd,
                         rec_lam[0], nblk=2, tc=256, name="rec_mixer_lat")
    y_sample = _post(xs, mod[1], norm2[1], [(y, w_rout)], w1[1], w2[1], tm=tm_lat, ctx=False, final_g=final_norm,
                     name="post_rec_lat")

    return (y_prompt, y_sample, new_a_k, new_a_v, new_b_k, new_b_v, sf, sb)
```

```python
import functools
import math
from typing import NamedTuple

import jax
import jax.numpy as jnp
import numpy as np
from jax import lax
from jax.experimental import pallas as pl
from jax.experimental.pallas import tpu as pltpu

F32 = jnp.float32
BF16 = jnp.bfloat16

LANES = 128
SUBLANES = 8
VMEM_LIMIT_BYTES = 56 * 1024 * 1024

D_MODEL = 1024
DEPTH = 2
GRID_W = 64
HEAD_DIM = 64
A_HEADS = 8
A_KV_HEADS = 2
A_GROUP = A_HEADS // A_KV_HEADS
B_HEADS = 4
WINDOW = 128
ROPE_BASE = 10000.0
D_RNN = 1280
RNN_BLOCKS = 10
RNN_BW = D_RNN // RNN_BLOCKS
CONV_W = 4
CONV_LEFT = (CONV_W - 1) // 2
RGLRU_C = 8.0
D_FF = 4 * D_MODEL
EPS = 1e-6
SCALE = HEAD_DIM ** -0.5
NEG = -1e30

QA_W = A_HEADS * LANES
KA_W = A_KV_HEADS * HEAD_DIM
B_W = B_HEADS * 2 * HEAD_DIM
MOD_ROWS = 8
CTX_ROW = 4
LOG2E = math.log2(math.e)
ONES_ROWS = 16
SCAN_MAX_UNROLL = 1024


def _cparams(*semantics):
    return pltpu.CompilerParams(dimension_semantics=semantics, vmem_limit_bytes=VMEM_LIMIT_BYTES)


def _dot(a, b):
    return jnp.dot(a, b, preferred_element_type=F32)


def _dot_nt(a, b):
    return lax.dot_general(a, b, (((1,), (1,)), ((), ())), preferred_element_type=F32)


def _rms(x, g):
    return x * lax.rsqrt(jnp.mean(x * x, axis=-1, keepdims=True) + EPS) * g


def _gelu_tanh(x):
    return x * (0.5 * (1.0 + jnp.tanh(math.sqrt(2.0 / math.pi) * (x + 0.044715 * (x * x * x)))))


def _ada_kernel(c_ref, w_ref, b_ref, o_ref):
    c = c_ref[...]
    s = c * jax.nn.sigmoid(c)
    o_ref[...] = _dot(s.astype(BF16), w_ref[...].astype(BF16)) + b_ref[...]


def _ada_mod(cvec, w_ada, b_ada):
    tn = 1536
    out = pl.pallas_call(
        _ada_kernel,
        grid=(DEPTH, 6 * D_MODEL // tn),
        in_specs=[
            pl.BlockSpec((MOD_ROWS, D_MODEL), lambda l, j: (0, 0)),
            pl.BlockSpec((None, D_MODEL, tn), lambda l, j: (l, 0, j)),
            pl.BlockSpec((None, 1, tn), lambda l, j: (l, 0, j)),
        ],
        out_specs=pl.BlockSpec((None, MOD_ROWS, tn), lambda l, j: (l, 0, j)),
        out_shape=jax.ShapeDtypeStruct((DEPTH, MOD_ROWS, 6 * D_MODEL), F32),
        compiler_params=_cparams("parallel", "parallel"),
        name="ada_mod",
    )(cvec, w_ada, b_ada.reshape(DEPTH, 1, 6 * D_MODEL))
    return out.reshape(DEPTH, MOD_ROWS, 6, D_MODEL)


class _Seg(NamedTuple):
    col0: int
    width: int
    outs: tuple
    rope: bool = False
    scale: float = 1.0
    gelu: bool = False


def _proj_kernel(*refs, segs, rope):
    if rope:
        x_ref, mod_ref, g_ref, w_ref, cos_ref, sin_ref, *outs = refs
    else:
        x_ref, mod_ref, g_ref, w_ref, *outs = refs
    x = x_ref[...]
    h = _rms(x, g_ref[...])
    h = h * (1.0 + mod_ref[1:2, :]) + mod_ref[0:1, :]
    hb = h.astype(BF16)
    tm = x.shape[0]
    if rope:
        lane = lax.broadcasted_iota(jnp.int32, (tm, LANES), 1)
        first = (lane & 16) == 0
        cos = cos_ref[...]
        sin = sin_ref[...]
    outs = iter(outs)
    for seg in segs:
        y = _dot(hb, w_ref[:, seg.col0:seg.col0 + seg.width])
        o_refs = [next(outs) for _ in seg.outs]
        for t in range(seg.width // LANES):
            yt = y[:, t * LANES:(t + 1) * LANES]
            if seg.rope:
                sw = jnp.where(first, pltpu.roll(yt, LANES - 16, 1), pltpu.roll(yt, 16, 1))
                yt = yt * cos + sw * sin
            if seg.scale != 1.0:
                yt = yt * seg.scale
            if seg.gelu:
                yt = _gelu_tanh(yt)
            for (layout, _), o_ref in zip(seg.outs, o_refs):
                if layout == "cols":
                    o_ref[t * LANES:(t + 1) * LANES, :] = yt.T.astype(o_ref.dtype)
                elif layout == "rows":
                    o_ref[:, t * LANES:(t + 1) * LANES] = yt.astype(o_ref.dtype)
                else:
                    seq_len = layout[1]
                    for r in range(tm // seq_len):
                        o_ref[r, :, t, :] = yt[r * seq_len:(r + 1) * seq_len].astype(o_ref.dtype)


def _proj(x, mod, g, w, segs, *, tm, ctx, rope_tabs=None, name):
    B, T, D = x.shape

    def out_spec(width, layout):
        if layout == "cols":
            return pl.BlockSpec((None, width, tm), lambda b, i: (b, 0, i))
        if layout == "rows":
            return pl.BlockSpec((None, tm, width), lambda b, i: (b, i, 0))
        return pl.BlockSpec((tm // layout[1], None, layout[1], width // LANES, LANES),
                            lambda b, i: (b * (T // tm) + i, 0, 0, 0, 0))

    def out_struct(width, layout, dt):
        if layout == "cols":
            return jax.ShapeDtypeStruct((B, width, T), dt)
        if layout == "rows":
            return jax.ShapeDtypeStruct((B, T, width), dt)
        return jax.ShapeDtypeStruct((B * T // layout[1], 1, layout[1], width // LANES, LANES), dt)

    row = (lambda b, i: (CTX_ROW, 0, 0)) if ctx else (lambda b, i: (b, 0, 0))
    in_specs = [
        pl.BlockSpec((None, tm, D), lambda b, i: (b, i, 0)),
        pl.BlockSpec((None, 6, D), row),
        pl.BlockSpec((1, D), lambda b, i: (0, 0)),
        pl.BlockSpec(w.shape, lambda b, i: (0, 0)),
    ]
    args = [x, mod, g.reshape(1, D), w]
    if rope_tabs is not None:
        in_specs += [pl.BlockSpec((tm, LANES), lambda b, i: (i, 0))] * 2
        args += list(rope_tabs)
    out_specs = [out_spec(s.width, layout) for s in segs for layout, _ in s.outs]
    out_shape = [out_struct(s.width, layout, dt) for s in segs for layout, dt in s.outs]
    return pl.pallas_call(
        functools.partial(_proj_kernel, segs=tuple(segs), rope=rope_tabs is not None),
        grid=(B, T // tm),
        in_specs=in_specs,
        out_specs=out_specs,
        out_shape=out_shape,
        compiler_params=_cparams("parallel", "parallel"),
        name=name,
    )(*args)


def _diff_lambda(lq, lam_init):
    s1 = jnp.sum(lq[0:1, :] * lq[1:2, :], axis=1, keepdims=True)
    s2 = jnp.sum(lq[2:3, :] * lq[3:4, :], axis=1, keepdims=True)
    return jnp.exp(s1) - jnp.exp(s2) + lam_init


def _stack_group_queries(qa_ref, g, rows):
    return jnp.concatenate(
        [qa_ref[rows, (A_GROUP * g + hh) * LANES:(A_GROUP * g + hh + 1) * LANES] for hh in range(A_GROUP)], axis=0)


def _stack_pair_queries(q):
    lo = lax.broadcasted_iota(jnp.int32, q.shape, 1) < HEAD_DIM
    zero = jnp.zeros_like(q)
    return jnp.concatenate([jnp.where(lo, q, zero), jnp.where(lo, zero, q)], axis=0)


def _with_ones_rows(vt):
    return jnp.concatenate([vt, jnp.ones((ONES_ROWS, vt.shape[1]), BF16)], axis=0)


def _sink_row(sink_ref, g, tq):
    return jnp.concatenate(
        [jnp.full((1, tq), sink_ref[A_GROUP * g + hh] * LOG2E, F32) for hh in range(A_GROUP)], axis=1)


def _softmax_values(parts, sink=None):
    m = functools.reduce(jnp.maximum, [jnp.max(s, axis=0, keepdims=True) for s, _ in parts])
    if sink is not None:
        m = jnp.maximum(m, sink)
    ot = sum(_dot(vt1, jnp.exp2(s - m).astype(BF16)) for s, vt1 in parts)
    den = ot[LANES:LANES + 1]
    if sink is not None:
        den = den + jnp.exp2(sink - m)
    return ot[:LANES] / den


def _a_heads(ot, g, tq):
    return [ot[g * HEAD_DIM:(g + 1) * HEAD_DIM, hh * tq:(hh + 1) * tq] for hh in range(A_GROUP)]


def _subln(o, subln, lam_init):
    return _rms(o, subln) * (1.0 - lam_init)


def _diff_combine(ot, lam, subln, lam_init):
    tq = ot.shape[1] // 2
    return _subln((ot[:, :tq] - lam * ot[:, tq:]).T, subln, lam_init)


def _ctx_attn_kernel(sink_ref, qa_ref, ka_ref, va_ref, qb_ref, kb_ref, vb_ref, lamqk_ref, subln_ref, o_ref, *,
                     lam_init):
    T = qa_ref.shape[0]
    ka = ka_ref[...].astype(BF16)
    vat1 = _with_ones_rows(va_ref[...].T.astype(BF16))
    scores_a = [_dot_nt(ka, _stack_group_queries(qa_ref, g, slice(None))) for g in range(A_KV_HEADS)]
    scores_b, vbt1 = [], []
    for h in range(B_HEADS):
        sl = slice(h * LANES, (h + 1) * LANES)
        scores_b.append(_dot_nt(kb_ref[:, sl].astype(BF16), _stack_pair_queries(qb_ref[:, sl])))
        vbt1.append(_with_ones_rows(vb_ref[:, sl].T.astype(BF16)))
    heads = []
    for g in range(A_KV_HEADS):
        heads += _a_heads(_softmax_values([(scores_a[g], vat1)], _sink_row(sink_ref, g, T)), g, T)
    nqa = A_HEADS * HEAD_DIM
    o_ref[:, :nqa] = jnp.concatenate(heads, axis=0).T.astype(o_ref.dtype)
    lam = _diff_lambda(lamqk_ref[...], lam_init)
    for h in range(B_HEADS):
        ot = _softmax_values([(scores_b[h], vbt1[h])])
        o_ref[:, nqa + h * LANES:nqa + (h + 1) * LANES] = (
            _diff_combine(ot, lam, subln_ref[...], lam_init).astype(o_ref.dtype))


def _ctx_attention(qa, ka, va, qb, kb, vb, sink, lam_qk, subln, lam_init):
    B, T, _ = qa.shape
    blk = lambda w: pl.BlockSpec((None, T, w), lambda b: (b, 0, 0))
    full = lambda a: pl.BlockSpec(a.shape, lambda b: (0,) * a.ndim)
    return pl.pallas_call(
        functools.partial(_ctx_attn_kernel, lam_init=lam_init),
        grid=(B,),
        in_specs=[pl.BlockSpec(memory_space=pltpu.SMEM), blk(QA_W), blk(KA_W), blk(KA_W), blk(B_W), blk(B_W),
                  blk(B_W), full(lam_qk), full(subln)],
        out_specs=blk(A_HEADS * HEAD_DIM + B_W),
        out_shape=jax.ShapeDtypeStruct((B, T, A_HEADS * HEAD_DIM + B_W), BF16),
        compiler_params=_cparams("parallel"),
        name="ctx_attention",
    )(sink, qa, ka, va, qb, kb, vb, lam_qk, subln)


def _lat_a_kernel(sink_ref, qa_ref, k_ref, vt_ref, ck_ref, cvt_ref, o_ref, *, tq, nq, band):
    T = k_ref.shape[0]
    cols = A_GROUP * tq
    cvt1 = _with_ones_rows(cvt_ref[...])
    jobs = []
    for i in range(nq):
        qi = pl.program_id(1) * nq + i
        start = pl.multiple_of(jnp.clip(qi * tq - WINDOW, 0, T - band), WINDOW)
        kb = k_ref[pl.ds(start, band), :]
        vbt1 = _with_ones_rows(vt_ref[:, pl.ds(start, band)])
        kpos = start + lax.broadcasted_iota(jnp.int32, (band, cols), 0)
        qpos = qi * tq + lax.broadcasted_iota(jnp.int32, (band, cols), 1) % tq
        keep = jnp.abs(qpos - kpos) <= WINDOW
        for g in range(A_KV_HEADS):
            qg = _stack_group_queries(qa_ref, g, slice(i * tq, (i + 1) * tq))
            jobs.append((_dot_nt(ck_ref[...], qg), _dot_nt(kb, qg), keep, vbt1))
    for i in range(nq):
        heads = []
        for g in range(A_KV_HEADS):
            s_c, s_b, keep, vbt1 = jobs[i * A_KV_HEADS + g]
            ot = _softmax_values([(s_c, cvt1), (jnp.where(keep, s_b, NEG), vbt1)], _sink_row(sink_ref, g, tq))
            heads += _a_heads(ot, g, tq)
        o_ref[i * tq:(i + 1) * tq, :] = jnp.concatenate(heads, axis=0).T.astype(o_ref.dtype)


def _lat_a_attention(qa, ka, vat, cka, cvat, sink):
    B, T, _ = qa.shape
    L = cka.shape[1]
    tq, nq = WINDOW, 8
    band = 3 * WINDOW
    return pl.pallas_call(
        functools.partial(_lat_a_kernel, tq=tq, nq=nq, band=band),
        grid=(B, T // (tq * nq)),
        in_specs=[
            pl.BlockSpec(memory_space=pltpu.SMEM),
            pl.BlockSpec((None, tq * nq, QA_W), lambda b, i: (b, i, 0)),
            pl.BlockSpec((None, T, KA_W), lambda b, i: (b, 0, 0)),
            pl.BlockSpec((None, KA_W, T), lambda b, i: (b, 0, 0)),
            pl.BlockSpec((None, L, KA_W), lambda b, i: (b, 0, 0)),
            pl.BlockSpec((None, KA_W, L), lambda b, i: (b, 0, 0)),
        ],
        out_specs=pl.BlockSpec((None, tq * nq, A_HEADS * HEAD_DIM), lambda b, i: (b, i, 0)),
        out_shape=jax.ShapeDtypeStruct((B, T, A_HEADS * HEAD_DIM), BF16),
        compiler_params=_cparams("parallel", "parallel"),
        name="lat_a_attention",
    )(sink, qa, ka, vat, cka, cvat)


def _lat_b_kernel(qt_ref, k_ref, vt_ref, ck_ref, cvt_ref, lamqk_ref, subln_ref, o_ref, *, tk, ahead, lam_init):
    tq = qt_ref.shape[1]
    T = k_ref.shape[0]
    qt = qt_ref[...]
    zero = jnp.zeros((HEAD_DIM, tq), BF16)
    qst = jnp.concatenate([jnp.concatenate([qt[:HEAD_DIM], zero], axis=0),
                           jnp.concatenate([zero, qt[HEAD_DIM:]], axis=0)], axis=1)
    blocks = [(ck_ref[...], cvt_ref[...])]
    blocks += [(k_ref[j * tk:(j + 1) * tk, :], vt_ref[:, j * tk:(j + 1) * tk]) for j in range(T // tk)]

    def scores(k):
        return _dot(k, qst)

    def accumulate(s, vt, carry):
        m, acc = carry
        m_new = jnp.maximum(m, jnp.max(s, axis=0, keepdims=True))
        p = jnp.exp2(s - m_new).astype(BF16)
        acc = jnp.exp2(m - m_new) * acc + _dot(_with_ones_rows(vt), p)
        return m_new, acc

    carry = (jnp.full((1, 2 * tq), -jnp.inf, F32), jnp.zeros((LANES + ONES_ROWS, 2 * tq), F32))
    pending = [scores(blocks[j][0]) for j in range(ahead)]
    for j in range(len(blocks)):
        if j + ahead < len(blocks):
            pending.append(scores(blocks[j + ahead][0]))
        carry = accumulate(pending.pop(0), blocks[j][1], carry)
    _, acc = carry
    ot = acc[:LANES] / acc[LANES:LANES + 1]
    lam = _diff_lambda(lamqk_ref[...], lam_init)
    o_ref[...] = _diff_combine(ot, lam, subln_ref[...], lam_init).astype(o_ref.dtype)


def _lat_b_attention(qbt, kb, vbt, ckb, cvbt, lam_qk, subln, lam_init):
    B, T, _ = kb.shape
    L = ckb.shape[1]
    tq = 1024
    full = lambda a: pl.BlockSpec(a.shape, lambda b, h, i: (0,) * a.ndim)
    return pl.pallas_call(
        functools.partial(_lat_b_kernel, tk=512, ahead=2, lam_init=lam_init),
        grid=(B, B_HEADS, T // tq),
        in_specs=[
            pl.BlockSpec((None, LANES, tq), lambda b, h, i: (b, h, i)),
            pl.BlockSpec((None, T, LANES), lambda b, h, i: (b, 0, h)),
            pl.BlockSpec((None, LANES, T), lambda b, h, i: (b, h, 0)),
            pl.BlockSpec((None, L, LANES), lambda b, h, i: (b, 0, h)),
            pl.BlockSpec((None, LANES, L), lambda b, h, i: (b, h, 0)),
            full(lam_qk), full(subln),
        ],
        out_specs=pl.BlockSpec((None, tq, LANES), lambda b, h, i: (b, i, h)),
        out_shape=jax.ShapeDtypeStruct((B, T, B_W), BF16),
        compiler_params=_cparams("parallel", "parallel", "parallel"),
        name="lat_b_attention",
    )(qbt, kb, vbt, ckb, cvbt, lam_qk, subln)


def _scan_chunk_len(n_steps):
    chunk = -(-n_steps // SUBLANES)
    while chunk % 8 != 4:
        chunk += 1
    return chunk


class _RecBlock(NamedTuple):
    gate: object
    xr: object
    cw: object
    cb: object
    wbd: object
    bbd: object
    lam: object
    h0f: object
    h0b: object
    y: object
    sf: object
    sb: object
    xp: object
    af: object
    uf: object
    ab: object
    ub: object
    pf: object
    hf: object
    pb: object
    hb: object


def _rec_kernel(gate_ref, xr_ref, cw_ref, cb_ref, wbd_ref, bbd_ref, lam_ref, h0f_ref, h0b_ref,
                y_ref, sf_ref, sb_ref, *scratch, nblk, tc, gates_unroll, chunk, unroll):
    blocks = []
    for n in range(nblk):
        sl = slice(n * LANES, (n + 1) * LANES)
        lanes = [r.at[:, sl] for r in (gate_ref, xr_ref, cw_ref, cb_ref)]
        lanes += [wbd_ref.at[n], bbd_ref.at[n]]
        lanes += [r.at[:, sl] for r in (lam_ref, h0f_ref, h0b_ref, y_ref, sf_ref, sb_ref)]
        xp, af, uf, ab, ub = [r.at[n] for r in scratch]
        blocks.append(_RecBlock(*lanes, xp, af, uf, ab, ub, pf=af, hf=uf, pb=ab, hb=ub))
    for blk in blocks:
        _rec_gates(blk, tc=tc, gates_unroll=gates_unroll, chunk=chunk)
    _rec_scan(blocks, chunk=chunk, unroll=unroll)
    for blk in blocks:
        _rec_combine(blk, tc=tc)


def _rec_gates(blk, *, tc, gates_unroll, chunk):
    T = blk.xr.shape[0]
    pad = SUBLANES
    xp = blk.xp
    xp[0:pad, :] = jnp.zeros((pad, LANES), F32)
    xp[T + pad:T + 2 * pad, :] = jnp.zeros((pad, LANES), F32)
    xp[pad:T + pad, :] = blk.xr[...]
    tail = SUBLANES * chunk - T
    for a_s, u_s in ((blk.af, blk.uf), (blk.ab, blk.ub)):
        a_s[T:T + tail, :] = jnp.ones((tail, LANES), F32)
        u_s[T:T + tail, :] = jnp.zeros((tail, LANES), F32)
    cw = blk.cw[...]
    cb = blk.cb[...]
    nl = -blk.lam[...]
    softplus = jnp.maximum(nl, 0.0) + jnp.log1p(jnp.exp(-jnp.abs(nl)))
    cp = (0.5 * RGLRU_C) * softplus

    def gates(it, _):
        for q in range(gates_unroll):
            gate_chunk(it * gates_unroll + q)
        return 0

    def gate_chunk(ci):
        t0 = pl.multiple_of(ci * tc, tc)
        y = cb
        for j in range(CONV_W):
            y = y + xp[pl.ds(t0 + (pad - CONV_LEFT + j), tc), :] * cw[j:j + 1, :]
        t = jnp.tanh(_dot(y.astype(BF16), blk.wbd[...]) + blk.bbd[...])
        yh = 0.5 * y
        for d, (a_s, u_s) in enumerate(((blk.af, blk.uf), (blk.ab, blk.ub))):
            t_r = t[:, 2 * d * LANES:(2 * d + 1) * LANES]
            t_i = t[:, (2 * d + 1) * LANES:(2 * d + 2) * LANES]
            neg_log_a = cp[d:d + 1, :] * t_r + cp[d:d + 1, :]
            a = jnp.exp2(neg_log_a * (-LOG2E))
            w = jnp.tanh(neg_log_a) * (a * a + 1.0)
            sqrt_w = jnp.where(w > 0.0, w * lax.rsqrt(w), 0.0)
            a_s[pl.ds(t0, tc), :] = a
            u_s[pl.ds(t0, tc), :] = (t_i + 1.0) * (yh * sqrt_w)

    lax.fori_loop(0, T // (tc * gates_unroll), gates, 0)


def _rec_scan(blocks, *, chunk, unroll):
    T = blocks[0].xr.shape[0]

    def rows(k):
        return pl.ds(k, SUBLANES, stride=chunk)

    def local_scan(it, carry):
        carry = list(carry)
        for q in range(unroll):
            k = it * unroll + q
            kb = chunk - 1 - k
            for i, blk in enumerate(blocks):
                hf, pf, hb, pb = carry[4 * i:4 * i + 4]
                a = blk.af[rows(k), :]
                hf = a * hf + blk.uf[rows(k), :]
                pf = a * pf
                blk.hf[rows(k), :] = hf
                blk.pf[rows(k), :] = pf
                a = blk.ab[rows(kb), :]
                hb = a * hb + blk.ub[rows(kb), :]
                pb = a * pb
                blk.hb[rows(kb), :] = hb
                blk.pb[rows(kb), :] = pb
                carry[4 * i:4 * i + 4] = [hf, pf, hb, pb]
        return tuple(carry)

    zero = jnp.zeros((SUBLANES, LANES), F32)
    one = jnp.ones((SUBLANES, LANES), F32)
    ends = lax.fori_loop(0, chunk // unroll, local_scan, (zero, one, zero, one) * len(blocks))

    carries = []
    for i, blk in enumerate(blocks):
        hf, pf, hb, pb = ends[4 * i:4 * i + 4]
        cf = [blk.h0f[...]]
        for r in range(SUBLANES - 1):
            cf.append(hf[r:r + 1, :] + pf[r:r + 1, :] * cf[r])
        cb_rev = [blk.h0b[...]]
        for r in range(SUBLANES - 1, 0, -1):
            cb_rev.append(hb[r:r + 1, :] + pb[r:r + 1, :] * cb_rev[-1])
        carries.append((jnp.concatenate(cf, axis=0), jnp.concatenate(cb_rev[::-1], axis=0)))

    def apply_carry(it, _):
        for q in range(unroll):
            k = it * unroll + q
            for blk, (carry_f, carry_b) in zip(blocks, carries):
                blk.uf[rows(k), :] = blk.hf[rows(k), :] + blk.pf[rows(k), :] * carry_f
                blk.ub[rows(k), :] = blk.hb[rows(k), :] + blk.pb[rows(k), :] * carry_b
        return 0

    lax.fori_loop(0, chunk // unroll, apply_carry, 0)
    for blk in blocks:
        blk.sf[...] = blk.uf[T - 1:T, :]
        blk.sb[...] = blk.ub[0:1, :]


def _rec_combine(blk, *, tc):
    T = blk.xr.shape[0]

    def combine(ci, _):
        t0 = pl.multiple_of(ci * tc, tc)
        blk.y[pl.ds(t0, tc), :] = ((blk.uf[pl.ds(t0, tc), :] + blk.ub[pl.ds(t0, tc), :])
                                   * blk.gate[pl.ds(t0, tc), :]).astype(blk.y.dtype)
        return 0

    lax.fori_loop(0, T // tc, combine, 0)


def _rec_mixer(gate, xr, h0f, h0b, conv_w, conv_b, w_bd, b_bd, lam, *, nblk, tc, name):
    B, T, _ = xr.shape
    chunk = _scan_chunk_len(T)
    wid = nblk * LANES
    col = lambda rows: pl.BlockSpec((rows, wid), lambda b, n: (0, n))
    seq = pl.BlockSpec((None, T, wid), lambda b, n: (b, 0, n))
    st = pl.BlockSpec((None, 1, wid), lambda b, n: (b, 0, n))
    return pl.pallas_call(
        functools.partial(_rec_kernel, nblk=nblk, tc=tc, gates_unroll=2, chunk=chunk,
                          unroll=max(u for u in range(1, SCAN_MAX_UNROLL + 1) if chunk % u == 0)),
        grid=(B, RNN_BLOCKS // nblk),
        in_specs=[seq, seq, col(CONV_W), col(1),
                  pl.BlockSpec((nblk, LANES, 4 * LANES), lambda b, n: (n, 0, 0)),
                  pl.BlockSpec((nblk, 1, 4 * LANES), lambda b, n: (n, 0, 0)),
                  col(2), st, st],
        out_specs=[seq, st, st],
        out_shape=[jax.ShapeDtypeStruct((B, T, D_RNN), BF16),
                   jax.ShapeDtypeStruct((B, 1, D_RNN), F32),
                   jax.ShapeDtypeStruct((B, 1, D_RNN), F32)],
        scratch_shapes=[pltpu.VMEM((nblk, T + 2 * SUBLANES, LANES), F32)]
        + [pltpu.VMEM((nblk, SUBLANES * chunk, LANES), F32)] * 4,
        compiler_params=_cparams("parallel", "parallel"),
        name=name,
    )(gate, xr, conv_w, conv_b.reshape(1, D_RNN), w_bd, b_bd, lam, h0f, h0b)


def _post_kernel(*refs, n_mix, final, fc):
    x_ref, mod_ref, g2_ref = refs[:3]
    mix = refs[3:3 + 2 * n_mix]
    w1_ref, w2_ref = refs[3 + 2 * n_mix:5 + 2 * n_mix]
    rest = refs[5 + 2 * n_mix:]
    if final:
        gf_ref, o_ref = rest
    else:
        (o_ref,) = rest
    mixed = _dot(mix[0][...], mix[1][...])
    for i in range(1, n_mix):
        mixed = mixed + _dot(mix[2 * i][...], mix[2 * i + 1][...])
    x1 = x_ref[...] + mod_ref[2:3, :] * mixed
    h = _rms(x1, g2_ref[...])
    hb = (h * (1.0 + mod_ref[4:5, :]) + mod_ref[3:4, :]).astype(BF16)
    n_chunks = D_FF // fc
    up = lambda c: _dot(hb, w1_ref[:, c * fc:(c + 1) * fc])
    acc = None
    a = up(0)
    for c in range(n_chunks):
        a_next = up(c + 1) if c + 1 < n_chunks else None
        part = _dot(jnp.square(jnp.maximum(a, 0.0)).astype(BF16), w2_ref[c * fc:(c + 1) * fc, :])
        acc = part if acc is None else acc + part
        a = a_next
    x2 = x1 + mod_ref[5:6, :] * acc
    if final:
        x2 = _rms(x2, gf_ref[...])
    o_ref[...] = x2


def _post(x, mod, g2, mixes, w1, w2, *, tm, ctx, final_g=None, name):
    B, T, D = x.shape
    row = (lambda b, i: (CTX_ROW, 0, 0)) if ctx else (lambda b, i: (b, 0, 0))
    const = lambda a: pl.BlockSpec(a.shape, lambda b, i: (0,) * a.ndim, pipeline_mode=pl.Buffered(1))
    in_specs = [
        pl.BlockSpec((None, tm, D), lambda b, i: (b, i, 0)),
        pl.BlockSpec((None, 6, D), row),
        pl.BlockSpec((1, D), lambda b, i: (0, 0)),
    ]
    args = [x, mod, g2.reshape(1, D)]
    for o, w in mixes:
        in_specs += [pl.BlockSpec((None, tm, o.shape[-1]), lambda b, i: (b, i, 0)), const(w)]
        args += [o, w]
    in_specs += [const(w1), const(w2)]
    args += [w1, w2]
    if final_g is not None:
        in_specs.append(pl.BlockSpec((1, D), lambda b, i: (0, 0)))
        args.append(final_g.reshape(1, D))
    return pl.pallas_call(
        functools.partial(_post_kernel, n_mix=len(mixes), final=final_g is not None, fc=1024),
        grid=(B, T // tm),
        in_specs=in_specs,
        out_specs=pl.BlockSpec((None, tm, D), lambda b, i: (b, i, 0)),
        out_shape=jax.ShapeDtypeStruct((B, T, D), F32),
        compiler_params=_cparams("parallel", "parallel"),
        name=name,
    )(*args)


def _rope_tables(n_tokens):
    rows = n_tokens // GRID_W
    r, cl = jnp.meshgrid(jnp.arange(rows, dtype=F32), jnp.arange(GRID_W, dtype=F32), indexing='ij')
    quarter = HEAD_DIM // 4
    inv = ROPE_BASE ** (-jnp.arange(quarter, dtype=F32) / quarter)
    ang = jnp.stack([r.reshape(-1)[:, None] * inv, cl.reshape(-1)[:, None] * inv], axis=1)
    cos, sin = jnp.cos(ang), jnp.sin(ang)
    cos64 = jnp.concatenate([cos[:, 0], cos[:, 0], cos[:, 1], cos[:, 1]], axis=-1)
    sin64 = jnp.concatenate([-sin[:, 0], sin[:, 0], -sin[:, 1], sin[:, 1]], axis=-1)
    return jnp.tile(cos64, (1, LANES // HEAD_DIM)), jnp.tile(sin64, (1, LANES // HEAD_DIM))


def _att_in_weights(w_in):
    d = w_in.shape[0]
    nq = A_HEADS * HEAD_DIM
    wq = w_in[:, :nq].reshape(d, A_HEADS, HEAD_DIM)
    z = jnp.zeros_like(wq)
    in_first = (jnp.arange(A_HEADS) // A_GROUP == 0)[None, :, None]
    wq = jnp.where(in_first, jnp.concatenate([wq, z], axis=-1), jnp.concatenate([z, wq], axis=-1))
    return jnp.concatenate([wq.reshape(d, QA_W), w_in[:, nq:]], axis=1).astype(BF16)


def _block_diag_weights(w_a, b_a, w_x, b_x):
    w = jnp.concatenate([w_a[0], w_x[0], w_a[1], w_x[1]], axis=-1)
    b = jnp.concatenate([v.reshape(RNN_BLOCKS, 1, RNN_BW) for v in (b_a[0], b_x[0], b_a[1], b_x[1])], axis=-1)
    return (0.5 * w).astype(BF16), 0.5 * b


def kernel(x_prompt, x_sample, cache_a_k, cache_a_v, cache_b_k, cache_b_v, state_fwd, state_bwd, c, c_ctx, norm1, norm2, w_ada, b_ada, w_mlp1, w_mlp2, att_w_in, att_w_out, att_sink, att_lam_qk, att_subln, rec_w_in, rec_conv_w, rec_conv_b, rec_w_a, rec_b_a, rec_w_x, rec_b_x, rec_lam, rec_w_out, final_norm):
    nb, n_seq, _ = x_prompt.shape
    nd, d_seq, _ = x_sample.shape
    past = cache_a_k.shape[2]
    assert nd <= CTX_ROW and DEPTH == 2
    cvec = jnp.concatenate([c, jnp.zeros((CTX_ROW - nd, D_MODEL), F32), c_ctx[None],
                            jnp.zeros((MOD_ROWS - CTX_ROW - 1, D_MODEL), F32)], axis=0)
    mod = _ada_mod(cvec, w_ada, b_ada)
    w1 = w_mlp1.astype(BF16)
    w2 = w_mlp2.astype(BF16)
    tm_proj, tm_post = 512, 1024
    tok = lambda a: a.reshape(1, -1, a.shape[-1])
    seq = lambda a: a.reshape(nb, n_seq, a.shape[-1])

    lam_init = 0.8 - 0.6 * math.exp(-0.3 * 0)
    w_in = _att_in_weights(att_w_in[0])
    w_out = att_w_out[0].astype(BF16)
    nqa = A_HEADS * HEAD_DIM
    c0 = [0, QA_W, QA_W + KA_W, QA_W + 2 * KA_W, QA_W + 2 * KA_W + B_W, QA_W + 2 * KA_W + 2 * B_W]
    widths = [QA_W, KA_W, KA_W, B_W, B_W, B_W]
    scales = [SCALE * LOG2E, 1.0, 1.0, SCALE * LOG2E, 1.0, 1.0]
    roped = [True, True, False, True, True, False]
    sink = att_sink[0]
    lam_qk = att_lam_qk[0]
    subln = att_subln[0].reshape(1, 2 * HEAD_DIM)

    rows = lambda dt: (("rows", dt),)
    cache_b = ((("heads", n_seq), F32), ("rows", BF16))
    segs_ctx = [_Seg(c0[0], QA_W, rows(BF16), scale=scales[0]), _Seg(c0[1], KA_W, rows(F32)),
                _Seg(c0[2], KA_W, rows(F32)), _Seg(c0[3], B_W, rows(BF16), scale=scales[3]),
                _Seg(c0[4], B_W, cache_b), _Seg(c0[5], B_W, cache_b)]
    qa, ka, va, qb, new_b_k, kb, new_b_v, vb = _proj(tok(x_prompt), mod[0], norm1[0], w_in, segs_ctx, tm=tm_proj,
                                                     ctx=True, name="proj_att_ctx")
    qa, ka, va, qb, kb, vb = map(seq, (qa, ka, va, qb, kb, vb))
    o_ctx = _ctx_attention(qa, ka, va, qb, kb, vb, sink, lam_qk, subln, lam_init)
    xp = _post(tok(x_prompt), mod[0], norm2[0], [(tok(o_ctx), w_out)], w1[0], w2[0], tm=tm_post, ctx=True,
               name="post_att_ctx")
    new_a_k = ka.reshape(nb, 1, n_seq, A_KV_HEADS, HEAD_DIM)
    new_a_v = va.reshape(nb, 1, n_seq, A_KV_HEADS, HEAD_DIM)

    segs_lat = [_Seg(c0[i], widths[i], (("cols" if i in (2, 3, 5) else "rows", BF16),), rope=roped[i],
                     scale=scales[i]) for i in range(6)]
    qa, ka, vat, qbt, kb, vbt = _proj(x_sample, mod[0], norm1[0], w_in, segs_lat, tm=tm_proj, ctx=False,
                                      rope_tabs=_rope_tables(d_seq), name="proj_att_lat")
    cka = cache_a_k[:, 0].reshape(nd, past, KA_W).astype(BF16)
    cvat = jnp.swapaxes(cache_a_v[:, 0].reshape(nd, past, KA_W), 1, 2).astype(BF16)
    ckb = cache_b_k[:, 0].reshape(nd, past, B_W).astype(BF16)
    cvbt = jnp.swapaxes(cache_b_v[:, 0].reshape(nd, past, B_W), 1, 2).astype(BF16)
    oa = _lat_a_attention(qa, ka, vat, cka, cvat, sink)
    ob = _lat_b_attention(qbt, kb, vbt, ckb, cvbt, lam_qk, subln, lam_init)
    xs = _post(x_sample, mod[0], norm2[0], [(oa, w_out[:nqa]), (ob, w_out[nqa:])], w1[0], w2[0], tm=tm_post,
               ctx=False, name="post_att_lat")

    w_rin = rec_w_in[0].astype(BF16)
    w_rout = rec_w_out[0].astype(BF16)
    w_bd, b_bd = _block_diag_weights(rec_w_a[0], rec_b_a[0], rec_w_x[0], rec_b_x[0])
    segs_rec = [_Seg(0, D_RNN, rows(F32), gelu=True), _Seg(D_RNN, D_RNN, rows(F32))]
    zeros = jnp.zeros((nb, 1, D_RNN), F32)

    gate, xr = map(seq, _proj(xp, mod[1], norm1[1], w_rin, segs_rec, tm=tm_proj, ctx=True,
                              name="proj_rec_ctx"))
    y, sf, sb = _rec_mixer(gate, xr, zeros, zeros, rec_conv_w[0], rec_conv_b[0], w_bd, b_bd, rec_lam[0],
                           nblk=RNN_BLOCKS, tc=128, name="rec_mixer_ctx")
    y_prompt = seq(_post(xp, mod[1], norm2[1], [(tok(y), w_rout)], w1[1], w2[1], tm=tm_post, ctx=True,
                         final_g=final_norm, name="post_rec_ctx"))

    gate, xr = _proj(xs, mod[1], norm1[1], w_rin, segs_rec, tm=tm_proj, ctx=False, name="proj_rec_lat")
    y, _, _ = _rec_mixer(gate, xr, state_fwd[:, 0:1], state_bwd[:, 0:1], rec_conv_w[0], rec_conv_b[0], w_bd, b_bd,
                         rec_lam[0], nblk=2, tc=256, name="rec_mixer_lat")
    y_sample = _post(xs, mod[1], norm2[1], [(y, w_rout)], w1[1], w2[1], tm=tm_post, ctx=False, final_g=final_norm,
                     name="post_rec_lat")

    return (y_prompt, y_sample, new_a_k, new_a_v, new_b_k, new_b_v, sf, sb)
```

```python
import functools
import math
from typing import NamedTuple

import jax
import jax.numpy as jnp
import numpy as np
from jax import lax
from jax.experimental import pallas as pl
from jax.experimental.pallas import tpu as pltpu

F32 = jnp.float32
BF16 = jnp.bfloat16

LANES = 128
SUBLANES = 8
VMEM_LIMIT_BYTES = 56 * 1024 * 1024

D_MODEL = 1024
DEPTH = 2
GRID_W = 64
HEAD_DIM = 64
A_HEADS = 8
A_KV_HEADS = 2
A_GROUP = A_HEADS // A_KV_HEADS
B_HEADS = 4
WINDOW = 128
ROPE_BASE = 10000.0
D_RNN = 1280
RNN_BLOCKS = 10
RNN_BW = D_RNN // RNN_BLOCKS
CONV_W = 4
CONV_LEFT = (CONV_W - 1) // 2
RGLRU_C = 8.0
D_FF = 4 * D_MODEL
EPS = 1e-6
SCALE = HEAD_DIM ** -0.5
NEG = -1e30

QA_W = A_HEADS * LANES
KA_W = A_KV_HEADS * HEAD_DIM
B_W = B_HEADS * 2 * HEAD_DIM
MOD_ROWS = 8
CTX_ROW = 4
LOG2E = math.log2(math.e)
ONES_ROWS = 16
PROJ_ROW_GROUPS = 2
SCAN_MAX_UNROLL = 1024


def _cparams(*semantics):
    return pltpu.CompilerParams(dimension_semantics=semantics, vmem_limit_bytes=VMEM_LIMIT_BYTES)


def _dot(a, b):
    return jnp.dot(a, b, preferred_element_type=F32)


def _dot_nt(a, b):
    return lax.dot_general(a, b, (((1,), (1,)), ((), ())), preferred_element_type=F32)


def _rms(x, g):
    return x * lax.rsqrt(jnp.mean(x * x, axis=-1, keepdims=True) + EPS) * g


def _gelu_tanh(x):
    return x * (0.5 * (1.0 + jnp.tanh(math.sqrt(2.0 / math.pi) * (x + 0.044715 * (x * x * x)))))


def _ada_kernel(c_ref, w_ref, b_ref, o_ref):
    c = c_ref[...]
    s = c * jax.nn.sigmoid(c)
    o_ref[...] = _dot(s.astype(BF16), w_ref[...].astype(BF16)) + b_ref[...]


def _ada_mod(cvec, w_ada, b_ada):
    tn = 1536
    out = pl.pallas_call(
        _ada_kernel,
        grid=(DEPTH, 6 * D_MODEL // tn),
        in_specs=[
            pl.BlockSpec((MOD_ROWS, D_MODEL), lambda l, j: (0, 0)),
            pl.BlockSpec((None, D_MODEL, tn), lambda l, j: (l, 0, j)),
            pl.BlockSpec((None, 1, tn), lambda l, j: (l, 0, j)),
        ],
        out_specs=pl.BlockSpec((None, MOD_ROWS, tn), lambda l, j: (l, 0, j)),
        out_shape=jax.ShapeDtypeStruct((DEPTH, MOD_ROWS, 6 * D_MODEL), F32),
        compiler_params=_cparams("parallel", "parallel"),
        name="ada_mod",
    )(cvec, w_ada, b_ada.reshape(DEPTH, 1, 6 * D_MODEL))
    return out.reshape(DEPTH, MOD_ROWS, 6, D_MODEL)


class _Seg(NamedTuple):
    col0: int
    width: int
    outs: tuple
    rope: bool = False
    scale: float = 1.0
    gelu: bool = False


def _proj_kernel(*refs, segs, rope):
    if rope:
        x_ref, mod_ref, g_ref, w_ref, cos_ref, sin_ref, *outs = refs
    else:
        x_ref, mod_ref, g_ref, w_ref, *outs = refs
    tm = x_ref.shape[0]
    rows = tm // PROJ_ROW_GROUPS
    groups = [slice(i * rows, (i + 1) * rows) for i in range(PROJ_ROW_GROUPS)]
    hbs = []
    for grp in groups:
        h = _rms(x_ref[grp, :], g_ref[...])
        hbs.append((h * (1.0 + mod_ref[1:2, :]) + mod_ref[0:1, :]).astype(BF16))
    if rope:
        first = (lax.broadcasted_iota(jnp.int32, (rows, LANES), 1) & 16) == 0
    outs = iter(outs)
    for seg in segs:
        ys = [_dot(hb, w_ref[:, seg.col0:seg.col0 + seg.width]) for hb in hbs]
        o_refs = [next(outs) for _ in seg.outs]
        for grp, y in zip(groups, ys):
            for t in range(seg.width // LANES):
                lanes = slice(t * LANES, (t + 1) * LANES)
                yt = y[:, lanes]
                if seg.rope:
                    sw = jnp.where(first, pltpu.roll(yt, LANES - 16, 1), pltpu.roll(yt, 16, 1))
                    yt = yt * cos_ref[grp, :] + sw * sin_ref[grp, :]
                if seg.scale != 1.0:
                    yt = yt * seg.scale
                if seg.gelu:
                    yt = _gelu_tanh(yt)
                for (layout, _), o_ref in zip(seg.outs, o_refs):
                    if layout == "cols":
                        o_ref[lanes, grp] = yt.T.astype(o_ref.dtype)
                    elif layout == "rows":
                        o_ref[grp, lanes] = yt.astype(o_ref.dtype)
                    else:
                        seq_len = layout[1]
                        assert rows % seq_len == 0
                        for r in range(rows // seq_len):
                            o_ref[grp.start // seq_len + r, :, t, :] = (
                                yt[r * seq_len:(r + 1) * seq_len].astype(o_ref.dtype))


def _proj(x, mod, g, w, segs, *, tm, ctx, rope_tabs=None, name):
    B, T, D = x.shape

    def out_spec(width, layout):
        if layout == "cols":
            return pl.BlockSpec((None, width, tm), lambda b, i: (b, 0, i))
        if layout == "rows":
            return pl.BlockSpec((None, tm, width), lambda b, i: (b, i, 0))
        return pl.BlockSpec((tm // layout[1], None, layout[1], width // LANES, LANES),
                            lambda b, i: (b * (T // tm) + i, 0, 0, 0, 0))

    def out_struct(width, layout, dt):
        if layout == "cols":
            return jax.ShapeDtypeStruct((B, width, T), dt)
        if layout == "rows":
            return jax.ShapeDtypeStruct((B, T, width), dt)
        return jax.ShapeDtypeStruct((B * T // layout[1], 1, layout[1], width // LANES, LANES), dt)

    row = (lambda b, i: (CTX_ROW, 0, 0)) if ctx else (lambda b, i: (b, 0, 0))
    in_specs = [
        pl.BlockSpec((None, tm, D), lambda b, i: (b, i, 0)),
        pl.BlockSpec((None, 6, D), row),
        pl.BlockSpec((1, D), lambda b, i: (0, 0)),
        pl.BlockSpec(w.shape, lambda b, i: (0, 0)),
    ]
    args = [x, mod, g.reshape(1, D), w]
    if rope_tabs is not None:
        in_specs += [pl.BlockSpec((tm, LANES), lambda b, i: (i, 0))] * 2
        args += list(rope_tabs)
    out_specs = [out_spec(s.width, layout) for s in segs for layout, _ in s.outs]
    out_shape = [out_struct(s.width, layout, dt) for s in segs for layout, dt in s.outs]
    return pl.pallas_call(
        functools.partial(_proj_kernel, segs=tuple(segs), rope=rope_tabs is not None),
        grid=(B, T // tm),
        in_specs=in_specs,
        out_specs=out_specs,
        out_shape=out_shape,
        compiler_params=_cparams("parallel", "parallel"),
        name=name,
    )(*args)


def _diff_lambda(lq, lam_init):
    s1 = jnp.sum(lq[0:1, :] * lq[1:2, :], axis=1, keepdims=True)
    s2 = jnp.sum(lq[2:3, :] * lq[3:4, :], axis=1, keepdims=True)
    return jnp.exp(s1) - jnp.exp(s2) + lam_init


def _stack_group_queries(qa_ref, g, rows):
    return jnp.concatenate(
        [qa_ref[rows, (A_GROUP * g + hh) * LANES:(A_GROUP * g + hh + 1) * LANES] for hh in range(A_GROUP)], axis=0)


def _stack_pair_queries(q):
    lo = lax.broadcasted_iota(jnp.int32, q.shape, 1) < HEAD_DIM
    zero = jnp.zeros_like(q)
    return jnp.concatenate([jnp.where(lo, q, zero), jnp.where(lo, zero, q)], axis=0)


def _with_ones_rows(vt):
    return jnp.concatenate([vt, jnp.ones((ONES_ROWS, vt.shape[1]), BF16)], axis=0)


def _sink_row(sink_ref, g, tq):
    return jnp.concatenate(
        [jnp.full((1, tq), sink_ref[A_GROUP * g + hh] * LOG2E, F32) for hh in range(A_GROUP)], axis=1)


def _softmax_values(parts, sink=None):
    m = functools.reduce(jnp.maximum, [jnp.max(s, axis=0, keepdims=True) for s, _ in parts])
    if sink is not None:
        m = jnp.maximum(m, sink)
    ot = sum(_dot(vt1, jnp.exp2(s - m).astype(BF16)) for s, vt1 in parts)
    den = ot[LANES:LANES + 1]
    if sink is not None:
        den = den + jnp.exp2(sink - m)
    return ot[:LANES] / den


def _a_heads(ot, g, tq):
    return [ot[g * HEAD_DIM:(g + 1) * HEAD_DIM, hh * tq:(hh + 1) * tq] for hh in range(A_GROUP)]


def _subln(o, subln, lam_init):
    return _rms(o, subln) * (1.0 - lam_init)


def _diff_combine(ot, lam, subln, lam_init):
    tq = ot.shape[1] // 2
    return _subln((ot[:, :tq] - lam * ot[:, tq:]).T, subln, lam_init)


def _ctx_attn_kernel(sink_ref, qa_ref, ka_ref, va_ref, qb_ref, kb_ref, vb_ref, lamqk_ref, subln_ref, o_ref, *,
                     lam_init):
    T = qa_ref.shape[0]
    ka = ka_ref[...].astype(BF16)
    vat1 = _with_ones_rows(va_ref[...].T.astype(BF16))
    scores_a = [_dot_nt(ka, _stack_group_queries(qa_ref, g, slice(None))) for g in range(A_KV_HEADS)]
    scores_b, vbt1 = [], []
    for h in range(B_HEADS):
        sl = slice(h * LANES, (h + 1) * LANES)
        scores_b.append(_dot_nt(kb_ref[:, sl].astype(BF16), _stack_pair_queries(qb_ref[:, sl])))
        vbt1.append(_with_ones_rows(vb_ref[:, sl].T.astype(BF16)))
    heads = []
    for g in range(A_KV_HEADS):
        heads += _a_heads(_softmax_values([(scores_a[g], vat1)], _sink_row(sink_ref, g, T)), g, T)
    nqa = A_HEADS * HEAD_DIM
    o_ref[:, :nqa] = jnp.concatenate(heads, axis=0).T.astype(o_ref.dtype)
    lam = _diff_lambda(lamqk_ref[...], lam_init)
    for h in range(B_HEADS):
        ot = _softmax_values([(scores_b[h], vbt1[h])])
        o_ref[:, nqa + h * LANES:nqa + (h + 1) * LANES] = (
            _diff_combine(ot, lam, subln_ref[...], lam_init).astype(o_ref.dtype))


def _ctx_attention(qa, ka, va, qb, kb, vb, sink, lam_qk, subln, lam_init):
    B, T, _ = qa.shape
    blk = lambda w: pl.BlockSpec((None, T, w), lambda b: (b, 0, 0))
    full = lambda a: pl.BlockSpec(a.shape, lambda b: (0,) * a.ndim)
    return pl.pallas_call(
        functools.partial(_ctx_attn_kernel, lam_init=lam_init),
        grid=(B,),
        in_specs=[pl.BlockSpec(memory_space=pltpu.SMEM), blk(QA_W), blk(KA_W), blk(KA_W), blk(B_W), blk(B_W),
                  blk(B_W), full(lam_qk), full(subln)],
        out_specs=blk(A_HEADS * HEAD_DIM + B_W),
        out_shape=jax.ShapeDtypeStruct((B, T, A_HEADS * HEAD_DIM + B_W), BF16),
        compiler_params=_cparams("parallel"),
        name="ctx_attention",
    )(sink, qa, ka, va, qb, kb, vb, lam_qk, subln)


def _lat_a_kernel(sink_ref, qa_ref, k_ref, vt_ref, ck_ref, cvt_ref, o_ref, *, tq, nq, band):
    T = k_ref.shape[0]
    cols = A_GROUP * tq
    cvt1 = _with_ones_rows(cvt_ref[...])
    jobs = []
    for i in range(nq):
        qi = pl.program_id(1) * nq + i
        start = pl.multiple_of(jnp.clip(qi * tq - WINDOW, 0, T - band), WINDOW)
        kb = k_ref[pl.ds(start, band), :]
        vbt1 = _with_ones_rows(vt_ref[:, pl.ds(start, band)])
        kpos = start + lax.broadcasted_iota(jnp.int32, (band, cols), 0)
        qpos = qi * tq + lax.broadcasted_iota(jnp.int32, (band, cols), 1) % tq
        keep = jnp.abs(qpos - kpos) <= WINDOW
        for g in range(A_KV_HEADS):
            qg = _stack_group_queries(qa_ref, g, slice(i * tq, (i + 1) * tq))
            jobs.append((_dot_nt(ck_ref[...], qg), _dot_nt(kb, qg), keep, vbt1))
    for i in range(nq):
        heads = []
        for g in range(A_KV_HEADS):
            s_c, s_b, keep, vbt1 = jobs[i * A_KV_HEADS + g]
            ot = _softmax_values([(s_c, cvt1), (jnp.where(keep, s_b, NEG), vbt1)], _sink_row(sink_ref, g, tq))
            heads += _a_heads(ot, g, tq)
        o_ref[i * tq:(i + 1) * tq, :] = jnp.concatenate(heads, axis=0).T.astype(o_ref.dtype)


def _lat_a_attention(qa, ka, vat, cka, cvat, sink):
    B, T, _ = qa.shape
    L = cka.shape[1]
    tq, nq = WINDOW, 8
    band = 3 * WINDOW
    return pl.pallas_call(
        functools.partial(_lat_a_kernel, tq=tq, nq=nq, band=band),
        grid=(B, T // (tq * nq)),
        in_specs=[
            pl.BlockSpec(memory_space=pltpu.SMEM),
            pl.BlockSpec((None, tq * nq, QA_W), lambda b, i: (b, i, 0)),
            pl.BlockSpec((None, T, KA_W), lambda b, i: (b, 0, 0)),
            pl.BlockSpec((None, KA_W, T), lambda b, i: (b, 0, 0)),
            pl.BlockSpec((None, L, KA_W), lambda b, i: (b, 0, 0)),
            pl.BlockSpec((None, KA_W, L), lambda b, i: (b, 0, 0)),
        ],
        out_specs=pl.BlockSpec((None, tq * nq, A_HEADS * HEAD_DIM), lambda b, i: (b, i, 0)),
        out_shape=jax.ShapeDtypeStruct((B, T, A_HEADS * HEAD_DIM), BF16),
        compiler_params=_cparams("parallel", "parallel"),
        name="lat_a_attention",
    )(sink, qa, ka, vat, cka, cvat)


def _lat_b_kernel(qt_ref, k_ref, vt_ref, ck_ref, cvt_ref, lamqk_ref, subln_ref, o_ref, *, tk, ahead, lam_init):
    tq = qt_ref.shape[1]
    T = k_ref.shape[0]
    qt = qt_ref[...]
    zero = jnp.zeros((HEAD_DIM, tq), BF16)
    qst = jnp.concatenate([jnp.concatenate([qt[:HEAD_DIM], zero], axis=0),
                           jnp.concatenate([zero, qt[HEAD_DIM:]], axis=0)], axis=1)
    blocks = [(ck_ref[...], cvt_ref[...])]
    blocks += [(k_ref[j * tk:(j + 1) * tk, :], vt_ref[:, j * tk:(j + 1) * tk]) for j in range(T // tk)]

    def scores(k):
        return _dot(k, qst)

    def accumulate(s, vt, carry):
        m, acc = carry
        m_new = jnp.maximum(m, jnp.max(s, axis=0, keepdims=True))
        p = jnp.exp2(s - m_new).astype(BF16)
        acc = jnp.exp2(m - m_new) * acc + _dot(_with_ones_rows(vt), p)
        return m_new, acc

    carry = (jnp.full((1, 2 * tq), -jnp.inf, F32), jnp.zeros((LANES + ONES_ROWS, 2 * tq), F32))
    pending = [scores(blocks[j][0]) for j in range(ahead)]
    for j in range(len(blocks)):
        if j + ahead < len(blocks):
            pending.append(scores(blocks[j + ahead][0]))
        carry = accumulate(pending.pop(0), blocks[j][1], carry)
    _, acc = carry
    ot = acc[:LANES] / acc[LANES:LANES + 1]
    lam = _diff_lambda(lamqk_ref[...], lam_init)
    o_ref[...] = _diff_combine(ot, lam, subln_ref[...], lam_init).astype(o_ref.dtype)


def _lat_b_attention(qbt, kb, vbt, ckb, cvbt, lam_qk, subln, lam_init):
    B, T, _ = kb.shape
    L = ckb.shape[1]
    tq = 1024
    full = lambda a: pl.BlockSpec(a.shape, lambda b, h, i: (0,) * a.ndim)
    return pl.pallas_call(
        functools.partial(_lat_b_kernel, tk=512, ahead=2, lam_init=lam_init),
        grid=(B, B_HEADS, T // tq),
        in_specs=[
            pl.BlockSpec((None, LANES, tq), lambda b, h, i: (b, h, i)),
            pl.BlockSpec((None, T, LANES), lambda b, h, i: (b, 0, h)),
            pl.BlockSpec((None, LANES, T), lambda b, h, i: (b, h, 0)),
            pl.BlockSpec((None, L, LANES), lambda b, h, i: (b, 0, h)),
            pl.BlockSpec((None, LANES, L), lambda b, h, i: (b, h, 0)),
            full(lam_qk), full(subln),
        ],
        out_specs=pl.BlockSpec((None, tq, LANES), lambda b, h, i: (b, i, h)),
        out_shape=jax.ShapeDtypeStruct((B, T, B_W), BF16),
        compiler_params=_cparams("parallel", "parallel", "parallel"),
        name="lat_b_attention",
    )(qbt, kb, vbt, ckb, cvbt, lam_qk, subln)


def _scan_chunk_len(n_steps):
    chunk = -(-n_steps // SUBLANES)
    while chunk % 8 != 4:
        chunk += 1
    return chunk


class _RecBlock(NamedTuple):
    gate: object
    xr: object
    cw: object
    cb: object
    wbd: object
    bbd: object
    lam: object
    h0f: object
    h0b: object
    y: object
    sf: object
    sb: object
    xp: object
    af: object
    uf: object
    ab: object
    ub: object
    pf: object
    hf: object
    pb: object
    hb: object


def _rec_kernel(gate_ref, xr_ref, cw_ref, cb_ref, wbd_ref, bbd_ref, lam_ref, h0f_ref, h0b_ref,
                y_ref, sf_ref, sb_ref, *scratch, nblk, tc, gates_unroll, chunk, unroll):
    blocks = []
    for n in range(nblk):
        sl = slice(n * LANES, (n + 1) * LANES)
        lanes = [r.at[:, sl] for r in (gate_ref, xr_ref, cw_ref, cb_ref)]
        lanes += [wbd_ref.at[n], bbd_ref.at[n]]
        lanes += [r.at[:, sl] for r in (lam_ref, h0f_ref, h0b_ref, y_ref, sf_ref, sb_ref)]
        xp, af, uf, ab, ub = [r.at[n] for r in scratch]
        blocks.append(_RecBlock(*lanes, xp, af, uf, ab, ub, pf=af, hf=uf, pb=ab, hb=ub))
    for blk in blocks:
        _rec_gates(blk, tc=tc, gates_unroll=gates_unroll, chunk=chunk)
    _rec_scan(blocks, chunk=chunk, unroll=unroll)
    for blk in blocks:
        _rec_combine(blk, tc=tc)


def _rec_gates(blk, *, tc, gates_unroll, chunk):
    T = blk.xr.shape[0]
    pad = SUBLANES
    xp = blk.xp
    xp[0:pad, :] = jnp.zeros((pad, LANES), F32)
    xp[T + pad:T + 2 * pad, :] = jnp.zeros((pad, LANES), F32)
    xp[pad:T + pad, :] = blk.xr[...]
    tail = SUBLANES * chunk - T
    for a_s, u_s in ((blk.af, blk.uf), (blk.ab, blk.ub)):
        a_s[T:T + tail, :] = jnp.ones((tail, LANES), F32)
        u_s[T:T + tail, :] = jnp.zeros((tail, LANES), F32)
    cw = blk.cw[...]
    cb = blk.cb[...]
    nl = -blk.lam[...]
    softplus = jnp.maximum(nl, 0.0) + jnp.log1p(jnp.exp(-jnp.abs(nl)))
    cp = (0.5 * RGLRU_C) * softplus

    def gates(it, _):
        for q in range(gates_unroll):
            gate_chunk(it * gates_unroll + q)
        return 0

    def gate_chunk(ci):
        t0 = pl.multiple_of(ci * tc, tc)
        y = cb
        for j in range(CONV_W):
            y = y + xp[pl.ds(t0 + (pad - CONV_LEFT + j), tc), :] * cw[j:j + 1, :]
        t = jnp.tanh(_dot(y.astype(BF16), blk.wbd[...]) + blk.bbd[...])
        yh = 0.5 * y
        for d, (a_s, u_s) in enumerate(((blk.af, blk.uf), (blk.ab, blk.ub))):
            t_r = t[:, 2 * d * LANES:(2 * d + 1) * LANES]
            t_i = t[:, (2 * d + 1) * LANES:(2 * d + 2) * LANES]
            neg_log_a = cp[d:d + 1, :] * t_r + cp[d:d + 1, :]
            a = jnp.exp2(neg_log_a * (-LOG2E))
            w = jnp.tanh(neg_log_a) * (a * a + 1.0)
            sqrt_w = jnp.where(w > 0.0, w * lax.rsqrt(w), 0.0)
            a_s[pl.ds(t0, tc), :] = a
            u_s[pl.ds(t0, tc), :] = (t_i + 1.0) * (yh * sqrt_w)

    lax.fori_loop(0, T // (tc * gates_unroll), gates, 0)


def _rec_scan(blocks, *, chunk, unroll):
    T = blocks[0].xr.shape[0]

    def rows(k):
        return pl.ds(k, SUBLANES, stride=chunk)

    def local_scan(it, carry):
        carry = list(carry)
        for q in range(unroll):
            k = it * unroll + q
            kb = chunk - 1 - k
            for i, blk in enumerate(blocks):
                hf, pf, hb, pb = carry[4 * i:4 * i + 4]
                a = blk.af[rows(k), :]
                hf = a * hf + blk.uf[rows(k), :]
                pf = a * pf
                blk.hf[rows(k), :] = hf
                blk.pf[rows(k), :] = pf
                a = blk.ab[rows(kb), :]
                hb = a * hb + blk.ub[rows(kb), :]
                pb = a * pb
                blk.hb[rows(kb), :] = hb
                blk.pb[rows(kb), :] = pb
                carry[4 * i:4 * i + 4] = [hf, pf, hb, pb]
        return tuple(carry)

    zero = jnp.zeros((SUBLANES, LANES), F32)
    one = jnp.ones((SUBLANES, LANES), F32)
    ends = lax.fori_loop(0, chunk // unroll, local_scan, (zero, one, zero, one) * len(blocks))

    carries = []
    for i, blk in enumerate(blocks):
        hf, pf, hb, pb = ends[4 * i:4 * i + 4]
        cf = [blk.h0f[...]]
        for r in range(SUBLANES - 1):
            cf.append(hf[r:r + 1, :] + pf[r:r + 1, :] * cf[r])
        cb_rev = [blk.h0b[...]]
        for r in range(SUBLANES - 1, 0, -1):
            cb_rev.append(hb[r:r + 1, :] + pb[r:r + 1, :] * cb_rev[-1])
        carries.append((jnp.concatenate(cf, axis=0), jnp.concatenate(cb_rev[::-1], axis=0)))

    def apply_carry(it, _):
        for q in range(unroll):
            k = it * unroll + q
            for blk, (carry_f, carry_b) in zip(blocks, carries):
                blk.uf[rows(k), :] = blk.hf[rows(k), :] + blk.pf[rows(k), :] * carry_f
                blk.ub[rows(k), :] = blk.hb[rows(k), :] + blk.pb[rows(k), :] * carry_b
        return 0

    lax.fori_loop(0, chunk // unroll, apply_carry, 0)
    for blk in blocks:
        blk.sf[...] = blk.uf[T - 1:T, :]
        blk.sb[...] = blk.ub[0:1, :]


def _rec_combine(blk, *, tc):
    T = blk.xr.shape[0]

    def combine(ci, _):
        t0 = pl.multiple_of(ci * tc, tc)
        blk.y[pl.ds(t0, tc), :] = ((blk.uf[pl.ds(t0, tc), :] + blk.ub[pl.ds(t0, tc), :])
                                   * blk.gate[pl.ds(t0, tc), :]).astype(blk.y.dtype)
        return 0

    lax.fori_loop(0, T // tc, combine, 0)


def _rec_mixer(gate, xr, h0f, h0b, conv_w, conv_b, w_bd, b_bd, lam, *, nblk, tc, name):
    B, T, _ = xr.shape
    chunk = _scan_chunk_len(T)
    wid = nblk * LANES
    col = lambda rows: pl.BlockSpec((rows, wid), lambda b, n: (0, n))
    seq = pl.BlockSpec((None, T, wid), lambda b, n: (b, 0, n))
    st = pl.BlockSpec((None, 1, wid), lambda b, n: (b, 0, n))
    return pl.pallas_call(
        functools.partial(_rec_kernel, nblk=nblk, tc=tc, gates_unroll=2, chunk=chunk,
                          unroll=max(u for u in range(1, SCAN_MAX_UNROLL + 1) if chunk % u == 0)),
        grid=(B, RNN_BLOCKS // nblk),
        in_specs=[seq, seq, col(CONV_W), col(1),
                  pl.BlockSpec((nblk, LANES, 4 * LANES), lambda b, n: (n, 0, 0)),
                  pl.BlockSpec((nblk, 1, 4 * LANES), lambda b, n: (n, 0, 0)),
                  col(2), st, st],
        out_specs=[seq, st, st],
        out_shape=[jax.ShapeDtypeStruct((B, T, D_RNN), BF16),
                   jax.ShapeDtypeStruct((B, 1, D_RNN), F32),
                   jax.ShapeDtypeStruct((B, 1, D_RNN), F32)],
        scratch_shapes=[pltpu.VMEM((nblk, T + 2 * SUBLANES, LANES), F32)]
        + [pltpu.VMEM((nblk, SUBLANES * chunk, LANES), F32)] * 4,
        compiler_params=_cparams("parallel", "parallel"),
        name=name,
    )(gate, xr, conv_w, conv_b.reshape(1, D_RNN), w_bd, b_bd, lam, h0f, h0b)


def _post_kernel(*refs, n_mix, final, fc):
    x_ref, mod_ref, g2_ref = refs[:3]
    mix = refs[3:3 + 2 * n_mix]
    w1_ref, w2_ref = refs[3 + 2 * n_mix:5 + 2 * n_mix]
    rest = refs[5 + 2 * n_mix:]
    if final:
        gf_ref, o_ref = rest
    else:
        (o_ref,) = rest
    mixed = _dot(mix[0][...], mix[1][...])
    for i in range(1, n_mix):
        mixed = mixed + _dot(mix[2 * i][...], mix[2 * i + 1][...])
    x1 = x_ref[...] + mod_ref[2:3, :] * mixed
    h = _rms(x1, g2_ref[...])
    hb = (h * (1.0 + mod_ref[4:5, :]) + mod_ref[3:4, :]).astype(BF16)
    n_chunks = D_FF // fc
    up = lambda c: _dot(hb, w1_ref[:, c * fc:(c + 1) * fc])
    acc = None
    a = up(0)
    for c in range(n_chunks):
        a_next = up(c + 1) if c + 1 < n_chunks else None
        part = _dot(jnp.square(jnp.maximum(a, 0.0)).astype(BF16), w2_ref[c * fc:(c + 1) * fc, :])
        acc = part if acc is None else acc + part
        a = a_next
    x2 = x1 + mod_ref[5:6, :] * acc
    if final:
        x2 = _rms(x2, gf_ref[...])
    o_ref[...] = x2


def _post(x, mod, g2, mixes, w1, w2, *, tm, ctx, final_g=None, name):
    B, T, D = x.shape
    row = (lambda b, i: (CTX_ROW, 0, 0)) if ctx else (lambda b, i: (b, 0, 0))
    const = lambda a: pl.BlockSpec(a.shape, lambda b, i: (0,) * a.ndim, pipeline_mode=pl.Buffered(1))
    in_specs = [
        pl.BlockSpec((None, tm, D), lambda b, i: (b, i, 0)),
        pl.BlockSpec((None, 6, D), row),
        pl.BlockSpec((1, D), lambda b, i: (0, 0)),
    ]
    args = [x, mod, g2.reshape(1, D)]
    for o, w in mixes:
        in_specs += [pl.BlockSpec((None, tm, o.shape[-1]), lambda b, i: (b, i, 0)), const(w)]
        args += [o, w]
    in_specs += [const(w1), const(w2)]
    args += [w1, w2]
    if final_g is not None:
        in_specs.append(pl.BlockSpec((1, D), lambda b, i: (0, 0)))
        args.append(final_g.reshape(1, D))
    return pl.pallas_call(
        functools.partial(_post_kernel, n_mix=len(mixes), final=final_g is not None, fc=1024),
        grid=(B, T // tm),
        in_specs=in_specs,
        out_specs=pl.BlockSpec((None, tm, D), lambda b, i: (b, i, 0)),
        out_shape=jax.ShapeDtypeStruct((B, T, D), F32),
        compiler_params=_cparams("parallel", "parallel"),
        name=name,
    )(*args)


def _rope_tables(n_tokens):
    rows = n_tokens // GRID_W
    r, cl = jnp.meshgrid(jnp.arange(rows, dtype=F32), jnp.arange(GRID_W, dtype=F32), indexing='ij')
    quarter = HEAD_DIM // 4
    inv = ROPE_BASE ** (-jnp.arange(quarter, dtype=F32) / quarter)
    ang = jnp.stack([r.reshape(-1)[:, None] * inv, cl.reshape(-1)[:, None] * inv], axis=1)
    cos, sin = jnp.cos(ang), jnp.sin(ang)
    cos64 = jnp.concatenate([cos[:, 0], cos[:, 0], cos[:, 1], cos[:, 1]], axis=-1)
    sin64 = jnp.concatenate([-sin[:, 0], sin[:, 0], -sin[:, 1], sin[:, 1]], axis=-1)
    return jnp.tile(cos64, (1, LANES // HEAD_DIM)), jnp.tile(sin64, (1, LANES // HEAD_DIM))


def _att_in_weights(w_in):
    d = w_in.shape[0]
    nq = A_HEADS * HEAD_DIM
    wq = w_in[:, :nq].reshape(d, A_HEADS, HEAD_DIM)
    z = jnp.zeros_like(wq)
    in_first = (jnp.arange(A_HEADS) // A_GROUP == 0)[None, :, None]
    wq = jnp.where(in_first, jnp.concatenate([wq, z], axis=-1), jnp.concatenate([z, wq], axis=-1))
    return jnp.concatenate([wq.reshape(d, QA_W), w_in[:, nq:]], axis=1).astype(BF16)


def _block_diag_weights(w_a, b_a, w_x, b_x):
    w = jnp.concatenate([w_a[0], w_x[0], w_a[1], w_x[1]], axis=-1)
    b = jnp.concatenate([v.reshape(RNN_BLOCKS, 1, RNN_BW) for v in (b_a[0], b_x[0], b_a[1], b_x[1])], axis=-1)
    return (0.5 * w).astype(BF16), 0.5 * b


def kernel(x_prompt, x_sample, cache_a_k, cache_a_v, cache_b_k, cache_b_v, state_fwd, state_bwd, c, c_ctx, norm1, norm2, w_ada, b_ada, w_mlp1, w_mlp2, att_w_in, att_w_out, att_sink, att_lam_qk, att_subln, rec_w_in, rec_conv_w, rec_conv_b, rec_w_a, rec_b_a, rec_w_x, rec_b_x, rec_lam, rec_w_out, final_norm):
    nb, n_seq, _ = x_prompt.shape
    nd, d_seq, _ = x_sample.shape
    past = cache_a_k.shape[2]
    assert nd <= CTX_ROW and DEPTH == 2
    cvec = jnp.concatenate([c, jnp.zeros((CTX_ROW - nd, D_MODEL), F32), c_ctx[None],
                            jnp.zeros((MOD_ROWS - CTX_ROW - 1, D_MODEL), F32)], axis=0)
    mod = _ada_mod(cvec, w_ada, b_ada)
    w1 = w_mlp1.astype(BF16)
    w2 = w_mlp2.astype(BF16)
    tm_proj, tm_post = 512, 1024
    tok = lambda a: a.reshape(1, -1, a.shape[-1])
    seq = lambda a: a.reshape(nb, n_seq, a.shape[-1])

    lam_init = 0.8 - 0.6 * math.exp(-0.3 * 0)
    w_in = _att_in_weights(att_w_in[0])
    w_out = att_w_out[0].astype(BF16)
    nqa = A_HEADS * HEAD_DIM
    c0 = [0, QA_W, QA_W + KA_W, QA_W + 2 * KA_W, QA_W + 2 * KA_W + B_W, QA_W + 2 * KA_W + 2 * B_W]
    widths = [QA_W, KA_W, KA_W, B_W, B_W, B_W]
    scales = [SCALE * LOG2E, 1.0, 1.0, SCALE * LOG2E, 1.0, 1.0]
    roped = [True, True, False, True, True, False]
    sink = att_sink[0]
    lam_qk = att_lam_qk[0]
    subln = att_subln[0].reshape(1, 2 * HEAD_DIM)

    rows = lambda dt: (("rows", dt),)
    cache_b = ((("heads", n_seq), F32), ("rows", BF16))
    segs_ctx = [_Seg(c0[0], QA_W, rows(BF16), scale=scales[0]), _Seg(c0[1], KA_W, rows(F32)),
                _Seg(c0[2], KA_W, rows(F32)), _Seg(c0[3], B_W, rows(BF16), scale=scales[3]),
                _Seg(c0[4], B_W, cache_b), _Seg(c0[5], B_W, cache_b)]
    qa, ka, va, qb, new_b_k, kb, new_b_v, vb = _proj(tok(x_prompt), mod[0], norm1[0], w_in, segs_ctx, tm=tm_proj,
                                                     ctx=True, name="proj_att_ctx")
    qa, ka, va, qb, kb, vb = map(seq, (qa, ka, va, qb, kb, vb))
    o_ctx = _ctx_attention(qa, ka, va, qb, kb, vb, sink, lam_qk, subln, lam_init)
    xp = _post(tok(x_prompt), mod[0], norm2[0], [(tok(o_ctx), w_out)], w1[0], w2[0], tm=tm_post, ctx=True,
               name="post_att_ctx")
    new_a_k = ka.reshape(nb, 1, n_seq, A_KV_HEADS, HEAD_DIM)
    new_a_v = va.reshape(nb, 1, n_seq, A_KV_HEADS, HEAD_DIM)

    segs_lat = [_Seg(c0[i], widths[i], (("cols" if i in (2, 3, 5) else "rows", BF16),), rope=roped[i],
                     scale=scales[i]) for i in range(6)]
    qa, ka, vat, qbt, kb, vbt = _proj(x_sample, mod[0], norm1[0], w_in, segs_lat, tm=tm_proj, ctx=False,
                                      rope_tabs=_rope_tables(d_seq), name="proj_att_lat")
    cka = cache_a_k[:, 0].reshape(nd, past, KA_W).astype(BF16)
    cvat = jnp.swapaxes(cache_a_v[:, 0].reshape(nd, past, KA_W), 1, 2).astype(BF16)
    ckb = cache_b_k[:, 0].reshape(nd, past, B_W).astype(BF16)
    cvbt = jnp.swapaxes(cache_b_v[:, 0].reshape(nd, past, B_W), 1, 2).astype(BF16)
    oa = _lat_a_attention(qa, ka, vat, cka, cvat, sink)
    ob = _lat_b_attention(qbt, kb, vbt, ckb, cvbt, lam_qk, subln, lam_init)
    xs = _post(x_sample, mod[0], norm2[0], [(oa, w_out[:nqa]), (ob, w_out[nqa:])], w1[0], w2[0], tm=tm_post,
               ctx=False, name="post_att_lat")

    w_rin = rec_w_in[0].astype(BF16)
    w_rout = rec_w_out[0].astype(BF16)
    w_bd, b_bd = _block_diag_weights(rec_w_a[0], rec_b_a[0], rec_w_x[0], rec_b_x[0])
    segs_rec = [_Seg(0, D_RNN, rows(F32), gelu=True), _Seg(D_RNN, D_RNN, rows(F32))]
    zeros = jnp.zeros((nb, 1, D_RNN), F32)

    gate, xr = map(seq, _proj(xp, mod[1], norm1[1], w_rin, segs_rec, tm=tm_proj, ctx=True,
                              name="proj_rec_ctx"))
    y, sf, sb = _rec_mixer(gate, xr, zeros, zeros, rec_conv_w[0], rec_conv_b[0], w_bd, b_bd, rec_lam[0],
                           nblk=RNN_BLOCKS, tc=128, name="rec_mixer_ctx")
    y_prompt = seq(_post(xp, mod[1], norm2[1], [(tok(y), w_rout)], w1[1], w2[1], tm=tm_post, ctx=True,
                         final_g=final_norm, name="post_rec_ctx"))

    gate, xr = _proj(xs, mod[1], norm1[1], w_rin, segs_rec, tm=tm_proj, ctx=False, name="proj_rec_lat")
    y, _, _ = _rec_mixer(gate, xr, state_fwd[:, 0:1], state_bwd[:, 0:1], rec_conv_w[0], rec_conv_b[0], w_bd, b_bd,
                         rec_lam[0], nblk=2, tc=256, name="rec_mixer_lat")
    y_sample = _post(xs, mod[1], norm2[1], [(y, w_rout)], w1[1], w2[1], tm=tm_post, ctx=False, final_g=final_norm,
                     name="post_rec_lat")

    return (y_prompt, y_sample, new_a_k, new_a_v, new_b_k, new_b_v, sf, sb)
```

```python
import functools
import math
from typing import NamedTuple

import jax
import jax.numpy as jnp
import numpy as np
from jax import lax
from jax.experimental import pallas as pl
from jax.experimental.pallas import tpu as pltpu

F32 = jnp.float32
BF16 = jnp.bfloat16

LANES = 128
SUBLANES = 8
VMEM_LIMIT_BYTES = 56 * 1024 * 1024

D_MODEL = 1024
DEPTH = 2
GRID_W = 64
HEAD_DIM = 64
A_HEADS = 8
A_KV_HEADS = 2
A_GROUP = A_HEADS // A_KV_HEADS
B_HEADS = 4
WINDOW = 128
ROPE_BASE = 10000.0
D_RNN = 1280
RNN_BLOCKS = 10
RNN_BW = D_RNN // RNN_BLOCKS
CONV_W = 4
CONV_LEFT = (CONV_W - 1) // 2
RGLRU_C = 8.0
D_FF = 4 * D_MODEL
EPS = 1e-6
SCALE = HEAD_DIM ** -0.5
NEG = -1e30

QA_W = A_HEADS * LANES
KA_W = A_KV_HEADS * HEAD_DIM
B_W = B_HEADS * 2 * HEAD_DIM
MOD_ROWS = 8
CTX_ROW = 4
LOG2E = math.log2(math.e)
ONES_ROWS = 16
PROJ_ROW_GROUPS = 2
SCAN_MAX_UNROLL = 1024


def _cparams(*semantics):
    return pltpu.CompilerParams(dimension_semantics=semantics, vmem_limit_bytes=VMEM_LIMIT_BYTES)


def _dot(a, b):
    return jnp.dot(a, b, preferred_element_type=F32)


def _dot_nt(a, b):
    return lax.dot_general(a, b, (((1,), (1,)), ((), ())), preferred_element_type=F32)


def _rms(x, g):
    return x * lax.rsqrt(jnp.mean(x * x, axis=-1, keepdims=True) + EPS) * g


def _gelu_tanh(x):
    return x * (0.5 * (1.0 + jnp.tanh(math.sqrt(2.0 / math.pi) * (x + 0.044715 * (x * x * x)))))


def _ada_kernel(c_ref, w_ref, b_ref, o_ref):
    c = c_ref[...]
    s = c * jax.nn.sigmoid(c)
    o_ref[...] = _dot(s.astype(BF16), w_ref[...].astype(BF16)) + b_ref[...]


def _ada_mod(cvec, w_ada, b_ada):
    tn = 1536
    out = pl.pallas_call(
        _ada_kernel,
        grid=(DEPTH, 6 * D_MODEL // tn),
        in_specs=[
            pl.BlockSpec((MOD_ROWS, D_MODEL), lambda l, j: (0, 0)),
            pl.BlockSpec((None, D_MODEL, tn), lambda l, j: (l, 0, j)),
            pl.BlockSpec((None, 1, tn), lambda l, j: (l, 0, j)),
        ],
        out_specs=pl.BlockSpec((None, MOD_ROWS, tn), lambda l, j: (l, 0, j)),
        out_shape=jax.ShapeDtypeStruct((DEPTH, MOD_ROWS, 6 * D_MODEL), F32),
        compiler_params=_cparams("parallel", "parallel"),
        name="ada_mod",
    )(cvec, w_ada, b_ada.reshape(DEPTH, 1, 6 * D_MODEL))
    return out.reshape(DEPTH, MOD_ROWS, 6, D_MODEL)


class _Seg(NamedTuple):
    col0: int
    width: int
    outs: tuple
    rope: bool = False
    scale: float = 1.0
    gelu: bool = False


def _proj_kernel(*refs, segs, rope):
    if rope:
        x_ref, mod_ref, g_ref, w_ref, cos_ref, sin_ref, *outs = refs
    else:
        x_ref, mod_ref, g_ref, w_ref, *outs = refs
    tm = x_ref.shape[0]
    rows = tm // PROJ_ROW_GROUPS
    groups = [slice(i * rows, (i + 1) * rows) for i in range(PROJ_ROW_GROUPS)]
    hbs = []
    for grp in groups:
        h = _rms(x_ref[grp, :], g_ref[...])
        hbs.append((h * (1.0 + mod_ref[1:2, :]) + mod_ref[0:1, :]).astype(BF16))
    if rope:
        first = (lax.broadcasted_iota(jnp.int32, (rows, LANES), 1) & 16) == 0
    outs = iter(outs)
    for seg in segs:
        ys = [_dot(hb, w_ref[:, seg.col0:seg.col0 + seg.width]) for hb in hbs]
        o_refs = [next(outs) for _ in seg.outs]
        for grp, y in zip(groups, ys):
            for t in range(seg.width // LANES):
                lanes = slice(t * LANES, (t + 1) * LANES)
                yt = y[:, lanes]
                if seg.rope:
                    sw = jnp.where(first, pltpu.roll(yt, LANES - 16, 1), pltpu.roll(yt, 16, 1))
                    yt = yt * cos_ref[grp, :] + sw * sin_ref[grp, :]
                if seg.scale != 1.0:
                    yt = yt * seg.scale
                if seg.gelu:
                    yt = _gelu_tanh(yt)
                for (layout, _), o_ref in zip(seg.outs, o_refs):
                    if layout == "cols":
                        o_ref[lanes, grp] = yt.T.astype(o_ref.dtype)
                    elif layout == "rows":
                        o_ref[grp, lanes] = yt.astype(o_ref.dtype)
                    else:
                        seq_len = layout[1]
                        assert rows % seq_len == 0
                        for r in range(rows // seq_len):
                            o_ref[grp.start // seq_len + r, :, t, :] = (
                                yt[r * seq_len:(r + 1) * seq_len].astype(o_ref.dtype))


def _proj(x, mod, g, w, segs, *, tm, ctx, rope_tabs=None, name):
    B, T, D = x.shape

    def out_spec(width, layout):
        if layout == "cols":
            return pl.BlockSpec((None, width, tm), lambda b, i: (b, 0, i))
        if layout == "rows":
            return pl.BlockSpec((None, tm, width), lambda b, i: (b, i, 0))
        return pl.BlockSpec((tm // layout[1], None, layout[1], width // LANES, LANES),
                            lambda b, i: (b * (T // tm) + i, 0, 0, 0, 0))

    def out_struct(width, layout, dt):
        if layout == "cols":
            return jax.ShapeDtypeStruct((B, width, T), dt)
        if layout == "rows":
            return jax.ShapeDtypeStruct((B, T, width), dt)
        return jax.ShapeDtypeStruct((B * T // layout[1], 1, layout[1], width // LANES, LANES), dt)

    row = (lambda b, i: (CTX_ROW, 0, 0)) if ctx else (lambda b, i: (b, 0, 0))
    in_specs = [
        pl.BlockSpec((None, tm, D), lambda b, i: (b, i, 0)),
        pl.BlockSpec((None, 6, D), row),
        pl.BlockSpec((1, D), lambda b, i: (0, 0)),
        pl.BlockSpec(w.shape, lambda b, i: (0, 0)),
    ]
    args = [x, mod, g.reshape(1, D), w]
    if rope_tabs is not None:
        in_specs += [pl.BlockSpec((tm, LANES), lambda b, i: (i, 0))] * 2
        args += list(rope_tabs)
    out_specs = [out_spec(s.width, layout) for s in segs for layout, _ in s.outs]
    out_shape = [out_struct(s.width, layout, dt) for s in segs for layout, dt in s.outs]
    return pl.pallas_call(
        functools.partial(_proj_kernel, segs=tuple(segs), rope=rope_tabs is not None),
        grid=(B, T // tm),
        in_specs=in_specs,
        out_specs=out_specs,
        out_shape=out_shape,
        compiler_params=_cparams("parallel", "parallel"),
        name=name,
    )(*args)


def _diff_lambda(lq, lam_init):
    s1 = jnp.sum(lq[0:1, :] * lq[1:2, :], axis=1, keepdims=True)
    s2 = jnp.sum(lq[2:3, :] * lq[3:4, :], axis=1, keepdims=True)
    return jnp.exp(s1) - jnp.exp(s2) + lam_init


def _stack_group_queries(qa_ref, g, rows):
    return jnp.concatenate(
        [qa_ref[rows, (A_GROUP * g + hh) * LANES:(A_GROUP * g + hh + 1) * LANES] for hh in range(A_GROUP)], axis=0)


def _stack_pair_queries(q):
    lo = lax.broadcasted_iota(jnp.int32, q.shape, 1) < HEAD_DIM
    zero = jnp.zeros_like(q)
    return jnp.concatenate([jnp.where(lo, q, zero), jnp.where(lo, zero, q)], axis=0)


def _with_ones_rows(vt):
    return jnp.concatenate([vt, jnp.ones((ONES_ROWS, vt.shape[1]), BF16)], axis=0)


def _sink_row(sink_ref, g, tq):
    return jnp.concatenate(
        [jnp.full((1, tq), sink_ref[A_GROUP * g + hh] * LOG2E, F32) for hh in range(A_GROUP)], axis=1)


def _softmax_values(parts, sink=None):
    m = functools.reduce(jnp.maximum, [jnp.max(s, axis=0, keepdims=True) for s, _ in parts])
    if sink is not None:
        m = jnp.maximum(m, sink)
    ot = sum(_dot(vt1, jnp.exp2(s - m).astype(BF16)) for s, vt1 in parts)
    den = ot[LANES:LANES + 1]
    if sink is not None:
        den = den + jnp.exp2(sink - m)
    return ot[:LANES] / den


def _a_heads(ot, g, tq):
    return [ot[g * HEAD_DIM:(g + 1) * HEAD_DIM, hh * tq:(hh + 1) * tq] for hh in range(A_GROUP)]


def _subln(o, subln, lam_init):
    return _rms(o, subln) * (1.0 - lam_init)


def _diff_combine(ot, lam, subln, lam_init):
    tq = ot.shape[1] // 2
    return _subln((ot[:, :tq] - lam * ot[:, tq:]).T, subln, lam_init)


def _ctx_attn_kernel(sink_ref, qa_ref, ka_ref, va_ref, qb_ref, kb_ref, vb_ref, lamqk_ref, subln_ref, o_ref, *,
                     lam_init):
    T = qa_ref.shape[0]
    ka = ka_ref[...].astype(BF16)
    vat1 = _with_ones_rows(va_ref[...].T.astype(BF16))
    scores_a = [_dot_nt(ka, _stack_group_queries(qa_ref, g, slice(None))) for g in range(A_KV_HEADS)]
    scores_b, vbt1 = [], []
    for h in range(B_HEADS):
        sl = slice(h * LANES, (h + 1) * LANES)
        scores_b.append(_dot_nt(kb_ref[:, sl].astype(BF16), _stack_pair_queries(qb_ref[:, sl])))
        vbt1.append(_with_ones_rows(vb_ref[:, sl].T.astype(BF16)))
    heads = []
    for g in range(A_KV_HEADS):
        heads += _a_heads(_softmax_values([(scores_a[g], vat1)], _sink_row(sink_ref, g, T)), g, T)
    nqa = A_HEADS * HEAD_DIM
    o_ref[:, :nqa] = jnp.concatenate(heads, axis=0).T.astype(o_ref.dtype)
    lam = _diff_lambda(lamqk_ref[...], lam_init)
    for h in range(B_HEADS):
        ot = _softmax_values([(scores_b[h], vbt1[h])])
        o_ref[:, nqa + h * LANES:nqa + (h + 1) * LANES] = (
            _diff_combine(ot, lam, subln_ref[...], lam_init).astype(o_ref.dtype))


def _ctx_attention(qa, ka, va, qb, kb, vb, sink, lam_qk, subln, lam_init):
    B, T, _ = qa.shape
    blk = lambda w: pl.BlockSpec((None, T, w), lambda b: (b, 0, 0))
    full = lambda a: pl.BlockSpec(a.shape, lambda b: (0,) * a.ndim)
    return pl.pallas_call(
        functools.partial(_ctx_attn_kernel, lam_init=lam_init),
        grid=(B,),
        in_specs=[pl.BlockSpec(memory_space=pltpu.SMEM), blk(QA_W), blk(KA_W), blk(KA_W), blk(B_W), blk(B_W),
                  blk(B_W), full(lam_qk), full(subln)],
        out_specs=blk(A_HEADS * HEAD_DIM + B_W),
        out_shape=jax.ShapeDtypeStruct((B, T, A_HEADS * HEAD_DIM + B_W), BF16),
        compiler_params=_cparams("parallel"),
        name="ctx_attention",
    )(sink, qa, ka, va, qb, kb, vb, lam_qk, subln)


def _lat_a_kernel(sink_ref, qa_ref, k_ref, vt_ref, ck_ref, cvt_ref, o_ref, *, tq, nq, band):
    T = k_ref.shape[0]
    cols = A_GROUP * tq
    cvt1 = _with_ones_rows(cvt_ref[...])
    jobs = []
    for i in range(nq):
        qi = pl.program_id(1) * nq + i
        start = pl.multiple_of(jnp.clip(qi * tq - WINDOW, 0, T - band), WINDOW)
        kb = k_ref[pl.ds(start, band), :]
        vbt1 = _with_ones_rows(vt_ref[:, pl.ds(start, band)])
        kpos = start + lax.broadcasted_iota(jnp.int32, (band, cols), 0)
        qpos = qi * tq + lax.broadcasted_iota(jnp.int32, (band, cols), 1) % tq
        keep = jnp.abs(qpos - kpos) <= WINDOW
        for g in range(A_KV_HEADS):
            qg = _stack_group_queries(qa_ref, g, slice(i * tq, (i + 1) * tq))
            jobs.append((_dot_nt(ck_ref[...], qg), _dot_nt(kb, qg), keep, vbt1))
    for i in range(nq):
        heads = []
        for g in range(A_KV_HEADS):
            s_c, s_b, keep, vbt1 = jobs[i * A_KV_HEADS + g]
            ot = _softmax_values([(s_c, cvt1), (jnp.where(keep, s_b, NEG), vbt1)], _sink_row(sink_ref, g, tq))
            heads += _a_heads(ot, g, tq)
        o_ref[i * tq:(i + 1) * tq, :] = jnp.concatenate(heads, axis=0).T.astype(o_ref.dtype)


def _lat_a_attention(qa, ka, vat, cka, cvat, sink):
    B, T, _ = qa.shape
    L = cka.shape[1]
    tq, nq = WINDOW, 8
    band = 3 * WINDOW
    return pl.pallas_call(
        functools.partial(_lat_a_kernel, tq=tq, nq=nq, band=band),
        grid=(B, T // (tq * nq)),
        in_specs=[
            pl.BlockSpec(memory_space=pltpu.SMEM),
            pl.BlockSpec((None, tq * nq, QA_W), lambda b, i: (b, i, 0)),
            pl.BlockSpec((None, T, KA_W), lambda b, i: (b, 0, 0)),
            pl.BlockSpec((None, KA_W, T), lambda b, i: (b, 0, 0)),
            pl.BlockSpec((None, L, KA_W), lambda b, i: (b, 0, 0)),
            pl.BlockSpec((None, KA_W, L), lambda b, i: (b, 0, 0)),
        ],
        out_specs=pl.BlockSpec((None, tq * nq, A_HEADS * HEAD_DIM), lambda b, i: (b, i, 0)),
        out_shape=jax.ShapeDtypeStruct((B, T, A_HEADS * HEAD_DIM), BF16),
        compiler_params=_cparams("parallel", "parallel"),
        name="lat_a_attention",
    )(sink, qa, ka, vat, cka, cvat)


def _lat_b_kernel(qt_ref, k_ref, vt_ref, ck_ref, cvt_ref, lamqk_ref, subln_ref, o_ref, *, tk, ahead, lam_init):
    tq = qt_ref.shape[1]
    T = k_ref.shape[0]
    qt = qt_ref[...]
    zero = jnp.zeros((HEAD_DIM, tq), BF16)
    qst = jnp.concatenate([jnp.concatenate([qt[:HEAD_DIM], zero], axis=0),
                           jnp.concatenate([zero, qt[HEAD_DIM:]], axis=0)], axis=1)
    blocks = [(ck_ref[...], cvt_ref[...])]
    blocks += [(k_ref[j * tk:(j + 1) * tk, :], vt_ref[:, j * tk:(j + 1) * tk]) for j in range(T // tk)]

    def scores(k):
        return _dot(k, qst)

    def accumulate(s, vt, carry):
        m, acc = carry
        m_new = jnp.maximum(m, jnp.max(s, axis=0, keepdims=True))
        p = jnp.exp2(s - m_new).astype(BF16)
        acc = jnp.exp2(m - m_new) * acc + _dot(_with_ones_rows(vt), p)
        return m_new, acc

    carry = (jnp.full((1, 2 * tq), -jnp.inf, F32), jnp.zeros((LANES + ONES_ROWS, 2 * tq), F32))
    pending = [scores(blocks[j][0]) for j in range(ahead)]
    for j in range(len(blocks)):
        if j + ahead < len(blocks):
            pending.append(scores(blocks[j + ahead][0]))
        carry = accumulate(pending.pop(0), blocks[j][1], carry)
    _, acc = carry
    ot = acc[:LANES] / acc[LANES:LANES + 1]
    lam = _diff_lambda(lamqk_ref[...], lam_init)
    o_ref[...] = _diff_combine(ot, lam, subln_ref[...], lam_init).astype(o_ref.dtype)


def _lat_b_attention(qbt, kb, vbt, ckb, cvbt, lam_qk, subln, lam_init):
    B, T, _ = kb.shape
    L = ckb.shape[1]
    tq = 1024
    full = lambda a: pl.BlockSpec(a.shape, lambda b, h, i: (0,) * a.ndim)
    return pl.pallas_call(
        functools.partial(_lat_b_kernel, tk=512, ahead=2, lam_init=lam_init),
        grid=(B, B_HEADS, T // tq),
        in_specs=[
            pl.BlockSpec((None, LANES, tq), lambda b, h, i: (b, h, i)),
            pl.BlockSpec((None, T, LANES), lambda b, h, i: (b, 0, h)),
            pl.BlockSpec((None, LANES, T), lambda b, h, i: (b, h, 0)),
            pl.BlockSpec((None, L, LANES), lambda b, h, i: (b, 0, h)),
            pl.BlockSpec((None, LANES, L), lambda b, h, i: (b, h, 0)),
            full(lam_qk), full(subln),
        ],
        out_specs=pl.BlockSpec((None, tq, LANES), lambda b, h, i: (b, i, h)),
        out_shape=jax.ShapeDtypeStruct((B, T, B_W), BF16),
        compiler_params=_cparams("parallel", "parallel", "parallel"),
        name="lat_b_attention",
    )(qbt, kb, vbt, ckb, cvbt, lam_qk, subln)


def _scan_chunk_len(n_steps):
    chunk = -(-n_steps // SUBLANES)
    while chunk % 8 != 4:
        chunk += 1
    return chunk


class _RecBlock(NamedTuple):
    gate: object
    xr: object
    cw: object
    cb: object
    wbd: object
    bbd: object
    lam: object
    h0f: object
    h0b: object
    y: object
    sf: object
    sb: object
    xp: object
    af: object
    uf: object
    ab: object
    ub: object
    pf: object
    hf: object
    pb: object
    hb: object


def _rec_kernel(gate_ref, xr_ref, cw_ref, cb_ref, wbd_ref, bbd_ref, lam_ref, h0f_ref, h0b_ref,
                y_ref, sf_ref, sb_ref, *scratch, nblk, tc, gates_unroll, chunk, unroll):
    blocks = []
    for n in range(nblk):
        sl = slice(n * LANES, (n + 1) * LANES)
        lanes = [r.at[:, sl] for r in (gate_ref, xr_ref, cw_ref, cb_ref)]
        lanes += [wbd_ref.at[n], bbd_ref.at[n]]
        lanes += [r.at[:, sl] for r in (lam_ref, h0f_ref, h0b_ref, y_ref, sf_ref, sb_ref)]
        xp, af, uf, ab, ub = [r.at[n] for r in scratch]
        blocks.append(_RecBlock(*lanes, xp, af, uf, ab, ub, pf=af, hf=uf, pb=ab, hb=ub))
    for blk in blocks:
        _rec_gates(blk, tc=tc, gates_unroll=gates_unroll, chunk=chunk)
    _rec_scan(blocks, chunk=chunk, unroll=unroll)
    for blk in blocks:
        _rec_combine(blk, tc=tc)


def _rec_gates(blk, *, tc, gates_unroll, chunk):
    T = blk.xr.shape[0]
    pad = SUBLANES
    xp = blk.xp
    xp[0:pad, :] = jnp.zeros((pad, LANES), F32)
    xp[T + pad:T + 2 * pad, :] = jnp.zeros((pad, LANES), F32)
    xp[pad:T + pad, :] = blk.xr[...]
    tail = SUBLANES * chunk - T
    for a_s, u_s in ((blk.af, blk.uf), (blk.ab, blk.ub)):
        a_s[T:T + tail, :] = jnp.ones((tail, LANES), F32)
        u_s[T:T + tail, :] = jnp.zeros((tail, LANES), F32)
    cw = blk.cw[...]
    cb = blk.cb[...]
    nl = -blk.lam[...]
    softplus = jnp.maximum(nl, 0.0) + jnp.log1p(jnp.exp(-jnp.abs(nl)))
    cp = (0.5 * RGLRU_C) * softplus

    def gates(it, _):
        for q in range(gates_unroll):
            gate_chunk(it * gates_unroll + q)
        return 0

    def gate_chunk(ci):
        t0 = pl.multiple_of(ci * tc, tc)
        y = cb
        for j in range(CONV_W):
            y = y + xp[pl.ds(t0 + (pad - CONV_LEFT + j), tc), :] * cw[j:j + 1, :]
        t = jnp.tanh(_dot(y.astype(BF16), blk.wbd[...]) + blk.bbd[...])
        yh = 0.5 * y
        for d, (a_s, u_s) in enumerate(((blk.af, blk.uf), (blk.ab, blk.ub))):
            t_r = t[:, 2 * d * LANES:(2 * d + 1) * LANES]
            t_i = t[:, (2 * d + 1) * LANES:(2 * d + 2) * LANES]
            neg_log_a = cp[d:d + 1, :] * t_r + cp[d:d + 1, :]
            a = jnp.exp2(neg_log_a * (-LOG2E))
            w = jnp.tanh(neg_log_a) * (a * a + 1.0)
            sqrt_w = jnp.where(w > 0.0, w * lax.rsqrt(w), 0.0)
            a_s[pl.ds(t0, tc), :] = a
            u_s[pl.ds(t0, tc), :] = (t_i + 1.0) * (yh * sqrt_w)

    lax.fori_loop(0, T // (tc * gates_unroll), gates, 0)


def _rec_scan(blocks, *, chunk, unroll):
    T = blocks[0].xr.shape[0]

    def rows(k):
        return pl.ds(k, SUBLANES, stride=chunk)

    def local_scan(it, carry):
        carry = list(carry)
        for q in range(unroll):
            k = it * unroll + q
            kb = chunk - 1 - k
            for i, blk in enumerate(blocks):
                hf, pf, hb, pb = carry[4 * i:4 * i + 4]
                a = blk.af[rows(k), :]
                hf = a * hf + blk.uf[rows(k), :]
                pf = a * pf
                blk.hf[rows(k), :] = hf
                blk.pf[rows(k), :] = pf
                a = blk.ab[rows(kb), :]
                hb = a * hb + blk.ub[rows(kb), :]
                pb = a * pb
                blk.hb[rows(kb), :] = hb
                blk.pb[rows(kb), :] = pb
                carry[4 * i:4 * i + 4] = [hf, pf, hb, pb]
        return tuple(carry)

    zero = jnp.zeros((SUBLANES, LANES), F32)
    one = jnp.ones((SUBLANES, LANES), F32)
    ends = lax.fori_loop(0, chunk // unroll, local_scan, (zero, one, zero, one) * len(blocks))

    carries = []
    for i, blk in enumerate(blocks):
        hf, pf, hb, pb = ends[4 * i:4 * i + 4]
        cf = [blk.h0f[...]]
        for r in range(SUBLANES - 1):
            cf.append(hf[r:r + 1, :] + pf[r:r + 1, :] * cf[r])
        cb_rev = [blk.h0b[...]]
        for r in range(SUBLANES - 1, 0, -1):
            cb_rev.append(hb[r:r + 1, :] + pb[r:r + 1, :] * cb_rev[-1])
        carries.append((jnp.concatenate(cf, axis=0), jnp.concatenate(cb_rev[::-1], axis=0)))

    def apply_carry(it, _):
        for q in range(unroll):
            k = it * unroll + q
            for blk, (carry_f, carry_b) in zip(blocks, carries):
                blk.uf[rows(k), :] = blk.hf[rows(k), :] + blk.pf[rows(k), :] * carry_f
                blk.ub[rows(k), :] = blk.hb[rows(k), :] + blk.pb[rows(k), :] * carry_b
        return 0

    lax.fori_loop(0, chunk // unroll, apply_carry, 0)
    for blk in blocks:
        blk.sf[...] = blk.uf[T - 1:T, :]
        blk.sb[...] = blk.ub[0:1, :]


def _rec_combine(blk, *, tc):
    T = blk.xr.shape[0]

    def combine(ci, _):
        t0 = pl.multiple_of(ci * tc, tc)
        blk.y[pl.ds(t0, tc), :] = ((blk.uf[pl.ds(t0, tc), :] + blk.ub[pl.ds(t0, tc), :])
                                   * blk.gate[pl.ds(t0, tc), :].astype(F32)).astype(blk.y.dtype)
        return 0

    lax.fori_loop(0, T // tc, combine, 0)


def _rec_mixer(gate, xr, h0f, h0b, conv_w, conv_b, w_bd, b_bd, lam, *, nblk, tc, name):
    B, T, _ = xr.shape
    chunk = _scan_chunk_len(T)
    wid = nblk * LANES
    col = lambda rows: pl.BlockSpec((rows, wid), lambda b, n: (0, n))
    seq = pl.BlockSpec((None, T, wid), lambda b, n: (b, 0, n))
    st = pl.BlockSpec((None, 1, wid), lambda b, n: (b, 0, n))
    return pl.pallas_call(
        functools.partial(_rec_kernel, nblk=nblk, tc=tc, gates_unroll=2, chunk=chunk,
                          unroll=max(u for u in range(1, SCAN_MAX_UNROLL + 1) if chunk % u == 0)),
        grid=(B, RNN_BLOCKS // nblk),
        in_specs=[seq, seq, col(CONV_W), col(1),
                  pl.BlockSpec((nblk, LANES, 4 * LANES), lambda b, n: (n, 0, 0)),
                  pl.BlockSpec((nblk, 1, 4 * LANES), lambda b, n: (n, 0, 0)),
                  col(2), st, st],
        out_specs=[seq, st, st],
        out_shape=[jax.ShapeDtypeStruct((B, T, D_RNN), BF16),
                   jax.ShapeDtypeStruct((B, 1, D_RNN), F32),
                   jax.ShapeDtypeStruct((B, 1, D_RNN), F32)],
        scratch_shapes=[pltpu.VMEM((nblk, T + 2 * SUBLANES, LANES), F32)]
        + [pltpu.VMEM((nblk, SUBLANES * chunk, LANES), F32)] * 4,
        compiler_params=_cparams("parallel", "parallel"),
        name=name,
    )(gate, xr, conv_w, conv_b.reshape(1, D_RNN), w_bd, b_bd, lam, h0f, h0b)


def _post_kernel(*refs, n_mix, final, fc):
    x_ref, mod_ref, g2_ref = refs[:3]
    mix = refs[3:3 + 2 * n_mix]
    w1_ref, w2_ref = refs[3 + 2 * n_mix:5 + 2 * n_mix]
    rest = refs[5 + 2 * n_mix:]
    if final:
        gf_ref, o_ref = rest
    else:
        (o_ref,) = rest
    mixed = _dot(mix[0][...], mix[1][...])
    for i in range(1, n_mix):
        mixed = mixed + _dot(mix[2 * i][...], mix[2 * i + 1][...])
    x1 = x_ref[...] + mod_ref[2:3, :] * mixed
    h = _rms(x1, g2_ref[...])
    hb = (h * (1.0 + mod_ref[4:5, :]) + mod_ref[3:4, :]).astype(BF16)
    n_chunks = D_FF // fc
    up = lambda c: _dot(hb, w1_ref[:, c * fc:(c + 1) * fc])
    acc = None
    a = up(0)
    for c in range(n_chunks):
        a_next = up(c + 1) if c + 1 < n_chunks else None
        part = _dot(jnp.square(jnp.maximum(a, 0.0)).astype(BF16), w2_ref[c * fc:(c + 1) * fc, :])
        acc = part if acc is None else acc + part
        a = a_next
    x2 = x1 + mod_ref[5:6, :] * acc
    if final:
        x2 = _rms(x2, gf_ref[...])
    o_ref[...] = x2


def _post(x, mod, g2, mixes, w1, w2, *, tm, ctx, final_g=None, name):
    B, T, D = x.shape
    row = (lambda b, i: (CTX_ROW, 0, 0)) if ctx else (lambda b, i: (b, 0, 0))
    const = lambda a: pl.BlockSpec(a.shape, lambda b, i: (0,) * a.ndim, pipeline_mode=pl.Buffered(1))
    in_specs = [
        pl.BlockSpec((None, tm, D), lambda b, i: (b, i, 0)),
        pl.BlockSpec((None, 6, D), row),
        pl.BlockSpec((1, D), lambda b, i: (0, 0)),
    ]
    args = [x, mod, g2.reshape(1, D)]
    for o, w in mixes:
        in_specs += [pl.BlockSpec((None, tm, o.shape[-1]), lambda b, i: (b, i, 0)), const(w)]
        args += [o, w]
    in_specs += [const(w1), const(w2)]
    args += [w1, w2]
    if final_g is not None:
        in_specs.append(pl.BlockSpec((1, D), lambda b, i: (0, 0)))
        args.append(final_g.reshape(1, D))
    return pl.pallas_call(
        functools.partial(_post_kernel, n_mix=len(mixes), final=final_g is not None, fc=1024),
        grid=(B, T // tm),
        in_specs=in_specs,
        out_specs=pl.BlockSpec((None, tm, D), lambda b, i: (b, i, 0)),
        out_shape=jax.ShapeDtypeStruct((B, T, D), F32),
        compiler_params=_cparams("parallel", "parallel"),
        name=name,
    )(*args)


def _rope_tables(n_tokens):
    rows = n_tokens // GRID_W
    r, cl = jnp.meshgrid(jnp.arange(rows, dtype=F32), jnp.arange(GRID_W, dtype=F32), indexing='ij')
    quarter = HEAD_DIM // 4
    inv = ROPE_BASE ** (-jnp.arange(quarter, dtype=F32) / quarter)
    ang = jnp.stack([r.reshape(-1)[:, None] * inv, cl.reshape(-1)[:, None] * inv], axis=1)
    cos, sin = jnp.cos(ang), jnp.sin(ang)
    cos64 = jnp.concatenate([cos[:, 0], cos[:, 0], cos[:, 1], cos[:, 1]], axis=-1)
    sin64 = jnp.concatenate([-sin[:, 0], sin[:, 0], -sin[:, 1], sin[:, 1]], axis=-1)
    return jnp.tile(cos64, (1, LANES // HEAD_DIM)), jnp.tile(sin64, (1, LANES // HEAD_DIM))


def _att_in_weights(w_in):
    d = w_in.shape[0]
    nq = A_HEADS * HEAD_DIM
    wq = w_in[:, :nq].reshape(d, A_HEADS, HEAD_DIM)
    z = jnp.zeros_like(wq)
    in_first = (jnp.arange(A_HEADS) // A_GROUP == 0)[None, :, None]
    wq = jnp.where(in_first, jnp.concatenate([wq, z], axis=-1), jnp.concatenate([z, wq], axis=-1))
    return jnp.concatenate([wq.reshape(d, QA_W), w_in[:, nq:]], axis=1).astype(BF16)


def _block_diag_weights(w_a, b_a, w_x, b_x):
    w = jnp.concatenate([w_a[0], w_x[0], w_a[1], w_x[1]], axis=-1)
    b = jnp.concatenate([v.reshape(RNN_BLOCKS, 1, RNN_BW) for v in (b_a[0], b_x[0], b_a[1], b_x[1])], axis=-1)
    return (0.5 * w).astype(BF16), 0.5 * b


def kernel(x_prompt, x_sample, cache_a_k, cache_a_v, cache_b_k, cache_b_v, state_fwd, state_bwd, c, c_ctx, norm1, norm2, w_ada, b_ada, w_mlp1, w_mlp2, att_w_in, att_w_out, att_sink, att_lam_qk, att_subln, rec_w_in, rec_conv_w, rec_conv_b, rec_w_a, rec_b_a, rec_w_x, rec_b_x, rec_lam, rec_w_out, final_norm):
    nb, n_seq, _ = x_prompt.shape
    nd, d_seq, _ = x_sample.shape
    past = cache_a_k.shape[2]
    assert nd <= CTX_ROW and DEPTH == 2
    cvec = jnp.concatenate([c, jnp.zeros((CTX_ROW - nd, D_MODEL), F32), c_ctx[None],
                            jnp.zeros((MOD_ROWS - CTX_ROW - 1, D_MODEL), F32)], axis=0)
    mod = _ada_mod(cvec, w_ada, b_ada)
    w1 = w_mlp1.astype(BF16)
    w2 = w_mlp2.astype(BF16)
    tm_proj, tm_post = 512, 1024
    tok = lambda a: a.reshape(1, -1, a.shape[-1])
    seq = lambda a: a.reshape(nb, n_seq, a.shape[-1])

    lam_init = 0.8 - 0.6 * math.exp(-0.3 * 0)
    w_in = _att_in_weights(att_w_in[0])
    w_out = att_w_out[0].astype(BF16)
    nqa = A_HEADS * HEAD_DIM
    c0 = [0, QA_W, QA_W + KA_W, QA_W + 2 * KA_W, QA_W + 2 * KA_W + B_W, QA_W + 2 * KA_W + 2 * B_W]
    widths = [QA_W, KA_W, KA_W, B_W, B_W, B_W]
    scales = [SCALE * LOG2E, 1.0, 1.0, SCALE * LOG2E, 1.0, 1.0]
    roped = [True, True, False, True, True, False]
    sink = att_sink[0]
    lam_qk = att_lam_qk[0]
    subln = att_subln[0].reshape(1, 2 * HEAD_DIM)

    rows = lambda dt: (("rows", dt),)
    cache_b = ((("heads", n_seq), F32), ("rows", BF16))
    segs_ctx = [_Seg(c0[0], QA_W, rows(BF16), scale=scales[0]), _Seg(c0[1], KA_W, rows(F32)),
                _Seg(c0[2], KA_W, rows(F32)), _Seg(c0[3], B_W, rows(BF16), scale=scales[3]),
                _Seg(c0[4], B_W, cache_b), _Seg(c0[5], B_W, cache_b)]
    qa, ka, va, qb, new_b_k, kb, new_b_v, vb = _proj(tok(x_prompt), mod[0], norm1[0], w_in, segs_ctx, tm=tm_proj,
                                                     ctx=True, name="proj_att_ctx")
    qa, ka, va, qb, kb, vb = map(seq, (qa, ka, va, qb, kb, vb))
    o_ctx = _ctx_attention(qa, ka, va, qb, kb, vb, sink, lam_qk, subln, lam_init)
    xp = _post(tok(x_prompt), mod[0], norm2[0], [(tok(o_ctx), w_out)], w1[0], w2[0], tm=tm_post, ctx=True,
               name="post_att_ctx")
    new_a_k = ka.reshape(nb, 1, n_seq, A_KV_HEADS, HEAD_DIM)
    new_a_v = va.reshape(nb, 1, n_seq, A_KV_HEADS, HEAD_DIM)

    segs_lat = [_Seg(c0[i], widths[i], (("cols" if i in (2, 3, 5) else "rows", BF16),), rope=roped[i],
                     scale=scales[i]) for i in range(6)]
    qa, ka, vat, qbt, kb, vbt = _proj(x_sample, mod[0], norm1[0], w_in, segs_lat, tm=tm_proj, ctx=False,
                                      rope_tabs=_rope_tables(d_seq), name="proj_att_lat")
    cka = cache_a_k[:, 0].reshape(nd, past, KA_W).astype(BF16)
    cvat = jnp.swapaxes(cache_a_v[:, 0].reshape(nd, past, KA_W), 1, 2).astype(BF16)
    ckb = cache_b_k[:, 0].reshape(nd, past, B_W).astype(BF16)
    cvbt = jnp.swapaxes(cache_b_v[:, 0].reshape(nd, past, B_W), 1, 2).astype(BF16)
    oa = _lat_a_attention(qa, ka, vat, cka, cvat, sink)
    ob = _lat_b_attention(qbt, kb, vbt, ckb, cvbt, lam_qk, subln, lam_init)
    xs = _post(x_sample, mod[0], norm2[0], [(oa, w_out[:nqa]), (ob, w_out[nqa:])], w1[0], w2[0], tm=tm_post,
               ctx=False, name="post_att_lat")

    w_rin = rec_w_in[0].astype(BF16)
    w_rout = rec_w_out[0].astype(BF16)
    w_bd, b_bd = _block_diag_weights(rec_w_a[0], rec_b_a[0], rec_w_x[0], rec_b_x[0])
    segs_rec = [_Seg(0, D_RNN, rows(BF16), gelu=True), _Seg(D_RNN, D_RNN, rows(F32))]
    zeros = jnp.zeros((nb, 1, D_RNN), F32)

    gate, xr = map(seq, _proj(xp, mod[1], norm1[1], w_rin, segs_rec, tm=tm_proj, ctx=True,
                              name="proj_rec_ctx"))
    y, sf, sb = _rec_mixer(gate, xr, zeros, zeros, rec_conv_w[0], rec_conv_b[0], w_bd, b_bd, rec_lam[0],
                           nblk=RNN_BLOCKS, tc=128, name="rec_mixer_ctx")
    y_prompt = seq(_post(xp, mod[1], norm2[1], [(tok(y), w_rout)], w1[1], w2[1], tm=tm_post, ctx=True,
                         final_g=final_norm, name="post_rec_ctx"))

    gate, xr = _proj(xs, mod[1], norm1[1], w_rin, segs_rec, tm=tm_proj, ctx=False, name="proj_rec_lat")
    y, _, _ = _rec_mixer(gate, xr, state_fwd[:, 0:1], state_bwd[:, 0:1], rec_conv_w[0], rec_conv_b[0], w_bd, b_bd,
                         rec_lam[0], nblk=2, tc=256, name="rec_mixer_lat")
    y_sample = _post(xs, mod[1], norm2[1], [(y, w_rout)], w1[1], w2[1], tm=tm_post, ctx=False, final_g=final_norm,
                     name="post_rec_lat")

    return (y_prompt, y_sample, new_a_k, new_a_v, new_b_k, new_b_v, sf, sb)
```

```python
import functools
import math
from typing import NamedTuple

import jax
import jax.numpy as jnp
import numpy as np
from jax import lax
from jax.experimental import pallas as pl
from jax.experimental.pallas import tpu as pltpu

F32 = jnp.float32
BF16 = jnp.bfloat16

LANES = 128
SUBLANES = 8
VMEM_LIMIT_BYTES = 56 * 1024 * 1024

D_MODEL = 1024
DEPTH = 2
GRID_W = 64
HEAD_DIM = 64
A_HEADS = 8
A_KV_HEADS = 2
A_GROUP = A_HEADS // A_KV_HEADS
B_HEADS = 4
WINDOW = 128
ROPE_BASE = 10000.0
D_RNN = 1280
RNN_BLOCKS = 10
RNN_BW = D_RNN // RNN_BLOCKS
CONV_W = 4
CONV_LEFT = (CONV_W - 1) // 2
RGLRU_C = 8.0
D_FF = 4 * D_MODEL
EPS = 1e-6
SCALE = HEAD_DIM ** -0.5
NEG = -1e30

QA_W = A_HEADS * LANES
KA_W = A_KV_HEADS * HEAD_DIM
B_W = B_HEADS * 2 * HEAD_DIM
MOD_ROWS = 8
CTX_ROW = 4
LOG2E = math.log2(math.e)
ONES_ROWS = 16
PROJ_ROW_GROUPS = 2
SCAN_MAX_UNROLL = 1024


def _cparams(*semantics):
    return pltpu.CompilerParams(dimension_semantics=semantics, vmem_limit_bytes=VMEM_LIMIT_BYTES)


def _dot(a, b):
    return jnp.dot(a, b, preferred_element_type=F32)


def _dot_nt(a, b):
    return lax.dot_general(a, b, (((1,), (1,)), ((), ())), preferred_element_type=F32)


def _rms(x, g):
    return x * lax.rsqrt(jnp.mean(x * x, axis=-1, keepdims=True) + EPS) * g


def _gelu_tanh(x):
    return x * (0.5 * (1.0 + jnp.tanh(math.sqrt(2.0 / math.pi) * (x + 0.044715 * (x * x * x)))))


def _ada_kernel(c_ref, w_ref, b_ref, o_ref):
    c = c_ref[...]
    s = c * jax.nn.sigmoid(c)
    o_ref[...] = _dot(s.astype(BF16), w_ref[...].astype(BF16)) + b_ref[...]


def _ada_mod(cvec, w_ada, b_ada):
    tn = 1536
    out = pl.pallas_call(
        _ada_kernel,
        grid=(DEPTH, 6 * D_MODEL // tn),
        in_specs=[
            pl.BlockSpec((MOD_ROWS, D_MODEL), lambda l, j: (0, 0)),
            pl.BlockSpec((None, D_MODEL, tn), lambda l, j: (l, 0, j)),
            pl.BlockSpec((None, 1, tn), lambda l, j: (l, 0, j)),
        ],
        out_specs=pl.BlockSpec((None, MOD_ROWS, tn), lambda l, j: (l, 0, j)),
        out_shape=jax.ShapeDtypeStruct((DEPTH, MOD_ROWS, 6 * D_MODEL), F32),
        compiler_params=_cparams("parallel", "parallel"),
        name="ada_mod",
    )(cvec, w_ada, b_ada.reshape(DEPTH, 1, 6 * D_MODEL))
    return out.reshape(DEPTH, MOD_ROWS, 6, D_MODEL)


class _Seg(NamedTuple):
    col0: int
    width: int
    outs: tuple
    rope: bool = False
    scale: float = 1.0
    gelu: bool = False


def _proj_kernel(*refs, segs, rope):
    if rope:
        x_ref, mod_ref, g_ref, w_ref, cos_ref, sin_ref, *outs = refs
    else:
        x_ref, mod_ref, g_ref, w_ref, *outs = refs
    tm = x_ref.shape[0]
    rows = tm // PROJ_ROW_GROUPS
    groups = [slice(i * rows, (i + 1) * rows) for i in range(PROJ_ROW_GROUPS)]
    hbs = []
    for grp in groups:
        h = _rms(x_ref[grp, :], g_ref[...])
        hbs.append((h * (1.0 + mod_ref[1:2, :]) + mod_ref[0:1, :]).astype(BF16))
    if rope:
        first = (lax.broadcasted_iota(jnp.int32, (rows, LANES), 1) & 16) == 0
    outs = iter(outs)
    for seg in segs:
        ys = [_dot(hb, w_ref[:, seg.col0:seg.col0 + seg.width]) for hb in hbs]
        o_refs = [next(outs) for _ in seg.outs]
        for grp, y in zip(groups, ys):
            for t in range(seg.width // LANES):
                lanes = slice(t * LANES, (t + 1) * LANES)
                yt = y[:, lanes]
                if seg.rope:
                    sw = jnp.where(first, pltpu.roll(yt, LANES - 16, 1), pltpu.roll(yt, 16, 1))
                    yt = yt * cos_ref[grp, :] + sw * sin_ref[grp, :]
                if seg.scale != 1.0:
                    yt = yt * seg.scale
                if seg.gelu:
                    yt = _gelu_tanh(yt)
                for (layout, _), o_ref in zip(seg.outs, o_refs):
                    if layout == "cols":
                        o_ref[lanes, grp] = yt.T.astype(o_ref.dtype)
                    elif layout == "rows":
                        o_ref[grp, lanes] = yt.astype(o_ref.dtype)
                    else:
                        seq_len = layout[1]
                        assert rows % seq_len == 0
                        for r in range(rows // seq_len):
                            o_ref[grp.start // seq_len + r, :, t, :] = (
                                yt[r * seq_len:(r + 1) * seq_len].astype(o_ref.dtype))


def _proj(x, mod, g, w, segs, *, tm, ctx, rope_tabs=None, name):
    B, T, D = x.shape

    def out_spec(width, layout):
        if layout == "cols":
            return pl.BlockSpec((None, width, tm), lambda b, i: (b, 0, i))
        if layout == "rows":
            return pl.BlockSpec((None, tm, width), lambda b, i: (b, i, 0))
        return pl.BlockSpec((tm // layout[1], None, layout[1], width // LANES, LANES),
                            lambda b, i: (b * (T // tm) + i, 0, 0, 0, 0))

    def out_struct(width, layout, dt):
        if layout == "cols":
            return jax.ShapeDtypeStruct((B, width, T), dt)
        if layout == "rows":
            return jax.ShapeDtypeStruct((B, T, width), dt)
        return jax.ShapeDtypeStruct((B * T // layout[1], 1, layout[1], width // LANES, LANES), dt)

    row = (lambda b, i: (CTX_ROW, 0, 0)) if ctx else (lambda b, i: (b, 0, 0))
    in_specs = [
        pl.BlockSpec((None, tm, D), lambda b, i: (b, i, 0)),
        pl.BlockSpec((None, 6, D), row),
        pl.BlockSpec((1, D), lambda b, i: (0, 0)),
        pl.BlockSpec(w.shape, lambda b, i: (0, 0)),
    ]
    args = [x, mod, g.reshape(1, D), w]
    if rope_tabs is not None:
        in_specs += [pl.BlockSpec((tm, LANES), lambda b, i: (i, 0))] * 2
        args += list(rope_tabs)
    out_specs = [out_spec(s.width, layout) for s in segs for layout, _ in s.outs]
    out_shape = [out_struct(s.width, layout, dt) for s in segs for layout, dt in s.outs]
    return pl.pallas_call(
        functools.partial(_proj_kernel, segs=tuple(segs), rope=rope_tabs is not None),
        grid=(B, T // tm),
        in_specs=in_specs,
        out_specs=out_specs,
        out_shape=out_shape,
        compiler_params=_cparams("parallel", "parallel"),
        name=name,
    )(*args)


def _diff_lambda(lq, lam_init):
    s1 = jnp.sum(lq[0:1, :] * lq[1:2, :], axis=1, keepdims=True)
    s2 = jnp.sum(lq[2:3, :] * lq[3:4, :], axis=1, keepdims=True)
    return jnp.exp(s1) - jnp.exp(s2) + lam_init


def _stack_group_queries(qa_ref, g, rows):
    return jnp.concatenate(
        [qa_ref[rows, (A_GROUP * g + hh) * LANES:(A_GROUP * g + hh + 1) * LANES] for hh in range(A_GROUP)], axis=0)


def _stack_pair_queries(q):
    lo = lax.broadcasted_iota(jnp.int32, q.shape, 1) < HEAD_DIM
    zero = jnp.zeros_like(q)
    return jnp.concatenate([jnp.where(lo, q, zero), jnp.where(lo, zero, q)], axis=0)


def _with_ones_rows(vt):
    return jnp.concatenate([vt, jnp.ones((ONES_ROWS, vt.shape[1]), BF16)], axis=0)


def _sink_row(sink_ref, g, tq):
    return jnp.concatenate(
        [jnp.full((1, tq), sink_ref[A_GROUP * g + hh] * LOG2E, F32) for hh in range(A_GROUP)], axis=1)


def _softmax_values(parts, sink=None):
    m = functools.reduce(jnp.maximum, [jnp.max(s, axis=0, keepdims=True) for s, _ in parts])
    if sink is not None:
        m = jnp.maximum(m, sink)
    ot = sum(_dot(vt1, jnp.exp2(s - m).astype(BF16)) for s, vt1 in parts)
    den = ot[LANES:LANES + 1]
    if sink is not None:
        den = den + jnp.exp2(sink - m)
    return ot[:LANES] / den


def _a_heads(ot, g, tq):
    return [ot[g * HEAD_DIM:(g + 1) * HEAD_DIM, hh * tq:(hh + 1) * tq] for hh in range(A_GROUP)]


def _subln(o, subln, lam_init):
    return _rms(o, subln) * (1.0 - lam_init)


def _diff_combine(ot, lam, subln, lam_init):
    tq = ot.shape[1] // 2
    return _subln((ot[:, :tq] - lam * ot[:, tq:]).T, subln, lam_init)


def _ctx_attn_kernel(sink_ref, qa_ref, ka_ref, va_ref, qb_ref, kb_ref, vb_ref, lamqk_ref, subln_ref, o_ref, *,
                     lam_init):
    T = qa_ref.shape[0]
    ka = ka_ref[...].astype(BF16)
    vat1 = _with_ones_rows(va_ref[...].T.astype(BF16))
    scores_a = [_dot_nt(ka, _stack_group_queries(qa_ref, g, slice(None))) for g in range(A_KV_HEADS)]
    scores_b, vbt1 = [], []
    for h in range(B_HEADS):
        sl = slice(h * LANES, (h + 1) * LANES)
        scores_b.append(_dot_nt(kb_ref[:, sl].astype(BF16), _stack_pair_queries(qb_ref[:, sl])))
        vbt1.append(_with_ones_rows(vb_ref[:, sl].T.astype(BF16)))
    heads = []
    for g in range(A_KV_HEADS):
        heads += _a_heads(_softmax_values([(scores_a[g], vat1)], _sink_row(sink_ref, g, T)), g, T)
    nqa = A_HEADS * HEAD_DIM
    o_ref[:, :nqa] = jnp.concatenate(heads, axis=0).T.astype(o_ref.dtype)
    lam = _diff_lambda(lamqk_ref[...], lam_init)
    for h in range(B_HEADS):
        ot = _softmax_values([(scores_b[h], vbt1[h])])
        o_ref[:, nqa + h * LANES:nqa + (h + 1) * LANES] = (
            _diff_combine(ot, lam, subln_ref[...], lam_init).astype(o_ref.dtype))


def _ctx_attention(qa, ka, va, qb, kb, vb, sink, lam_qk, subln, lam_init):
    B, T, _ = qa.shape
    blk = lambda w: pl.BlockSpec((None, T, w), lambda b: (b, 0, 0))
    full = lambda a: pl.BlockSpec(a.shape, lambda b: (0,) * a.ndim)
    return pl.pallas_call(
        functools.partial(_ctx_attn_kernel, lam_init=lam_init),
        grid=(B,),
        in_specs=[pl.BlockSpec(memory_space=pltpu.SMEM), blk(QA_W), blk(KA_W), blk(KA_W), blk(B_W), blk(B_W),
                  blk(B_W), full(lam_qk), full(subln)],
        out_specs=blk(A_HEADS * HEAD_DIM + B_W),
        out_shape=jax.ShapeDtypeStruct((B, T, A_HEADS * HEAD_DIM + B_W), BF16),
        compiler_params=_cparams("parallel"),
        name="ctx_attention",
    )(sink, qa, ka, va, qb, kb, vb, lam_qk, subln)


def _lat_a_kernel(sink_ref, qa_ref, k_ref, vt_ref, ck_ref, cvt_ref, o_ref, *, tq, nq, band):
    T = k_ref.shape[0]
    cols = A_GROUP * tq
    cvt1 = _with_ones_rows(cvt_ref[...])
    jobs = []
    for i in range(nq):
        qi = pl.program_id(1) * nq + i
        start = pl.multiple_of(jnp.clip(qi * tq - WINDOW, 0, T - band), WINDOW)
        kb = k_ref[pl.ds(start, band), :]
        vbt1 = _with_ones_rows(vt_ref[:, pl.ds(start, band)])
        kpos = start + lax.broadcasted_iota(jnp.int32, (band, cols), 0)
        qpos = qi * tq + lax.broadcasted_iota(jnp.int32, (band, cols), 1) % tq
        keep = jnp.abs(qpos - kpos) <= WINDOW
        for g in range(A_KV_HEADS):
            qg = _stack_group_queries(qa_ref, g, slice(i * tq, (i + 1) * tq))
            jobs.append((_dot_nt(ck_ref[...], qg), _dot_nt(kb, qg), keep, vbt1))
    for i in range(nq):
        heads = []
        for g in range(A_KV_HEADS):
            s_c, s_b, keep, vbt1 = jobs[i * A_KV_HEADS + g]
            ot = _softmax_values([(s_c, cvt1), (jnp.where(keep, s_b, NEG), vbt1)], _sink_row(sink_ref, g, tq))
            heads += _a_heads(ot, g, tq)
        o_ref[i * tq:(i + 1) * tq, :] = jnp.concatenate(heads, axis=0).T.astype(o_ref.dtype)


def _lat_a_attention(qa, ka, vat, cka, cvat, sink):
    B, T, _ = qa.shape
    L = cka.shape[1]
    tq, nq = WINDOW, 8
    band = 3 * WINDOW
    return pl.pallas_call(
        functools.partial(_lat_a_kernel, tq=tq, nq=nq, band=band),
        grid=(B, T // (tq * nq)),
        in_specs=[
            pl.BlockSpec(memory_space=pltpu.SMEM),
            pl.BlockSpec((None, tq * nq, QA_W), lambda b, i: (b, i, 0)),
            pl.BlockSpec((None, T, KA_W), lambda b, i: (b, 0, 0)),
            pl.BlockSpec((None, KA_W, T), lambda b, i: (b, 0, 0)),
            pl.BlockSpec((None, L, KA_W), lambda b, i: (b, 0, 0)),
            pl.BlockSpec((None, KA_W, L), lambda b, i: (b, 0, 0)),
        ],
        out_specs=pl.BlockSpec((None, tq * nq, A_HEADS * HEAD_DIM), lambda b, i: (b, i, 0)),
        out_shape=jax.ShapeDtypeStruct((B, T, A_HEADS * HEAD_DIM), BF16),
        compiler_params=_cparams("parallel", "parallel"),
        name="lat_a_attention",
    )(sink, qa, ka, vat, cka, cvat)


def _lat_b_kernel(qt_ref, k_ref, vt_ref, ck_ref, cvt_ref, lamqk_ref, subln_ref, o_ref, *, tk, ahead, lam_init):
    tq = qt_ref.shape[1]
    T = k_ref.shape[0]
    qt = qt_ref[...]
    zero = jnp.zeros((HEAD_DIM, tq), BF16)
    qst = jnp.concatenate([jnp.concatenate([qt[:HEAD_DIM], zero], axis=0),
                           jnp.concatenate([zero, qt[HEAD_DIM:]], axis=0)], axis=1)
    blocks = [(ck_ref[...], cvt_ref[...])]
    blocks += [(k_ref[j * tk:(j + 1) * tk, :], vt_ref[:, j * tk:(j + 1) * tk]) for j in range(T // tk)]

    def scores(k):
        return _dot(k, qst)

    def accumulate(s, vt, carry):
        m, acc = carry
        m_new = jnp.maximum(m, jnp.max(s, axis=0, keepdims=True))
        p = jnp.exp2(s - m_new).astype(BF16)
        acc = jnp.exp2(m - m_new) * acc + _dot(_with_ones_rows(vt), p)
        return m_new, acc

    carry = (jnp.full((1, 2 * tq), -jnp.inf, F32), jnp.zeros((LANES + ONES_ROWS, 2 * tq), F32))
    pending = [scores(blocks[j][0]) for j in range(ahead)]
    for j in range(len(blocks)):
        if j + ahead < len(blocks):
            pending.append(scores(blocks[j + ahead][0]))
        carry = accumulate(pending.pop(0), blocks[j][1], carry)
    _, acc = carry
    ot = acc[:LANES] / acc[LANES:LANES + 1]
    lam = _diff_lambda(lamqk_ref[...], lam_init)
    o_ref[...] = _diff_combine(ot, lam, subln_ref[...], lam_init).astype(o_ref.dtype)


def _lat_b_attention(qbt, kb, vbt, ckb, cvbt, lam_qk, subln, lam_init):
    B, T, _ = kb.shape
    L = ckb.shape[1]
    tq = 1024
    full = lambda a: pl.BlockSpec(a.shape, lambda b, h, i: (0,) * a.ndim)
    return pl.pallas_call(
        functools.partial(_lat_b_kernel, tk=512, ahead=2, lam_init=lam_init),
        grid=(B, B_HEADS, T // tq),
        in_specs=[
            pl.BlockSpec((None, LANES, tq), lambda b, h, i: (b, h, i)),
            pl.BlockSpec((None, T, LANES), lambda b, h, i: (b, 0, h)),
            pl.BlockSpec((None, LANES, T), lambda b, h, i: (b, h, 0)),
            pl.BlockSpec((None, L, LANES), lambda b, h, i: (b, 0, h)),
            pl.BlockSpec((None, LANES, L), lambda b, h, i: (b, h, 0)),
            full(lam_qk), full(subln),
        ],
        out_specs=pl.BlockSpec((None, tq, LANES), lambda b, h, i: (b, i, h)),
        out_shape=jax.ShapeDtypeStruct((B, T, B_W), BF16),
        compiler_params=_cparams("parallel", "parallel", "parallel"),
        name="lat_b_attention",
    )(qbt, kb, vbt, ckb, cvbt, lam_qk, subln)


def _scan_chunk_len(n_steps):
    chunk = -(-n_steps // SUBLANES)
    while chunk % 8 != 4:
        chunk += 1
    return chunk


class _RecBlock(NamedTuple):
    gate: object
    xr: object
    cw: object
    cb: object
    wbd: object
    bbd: object
    lam: object
    h0f: object
    h0b: object
    y: object
    sf: object
    sb: object
    xp: object
    af: object
    uf: object
    ab: object
    ub: object
    pf: object
    hf: object
    pb: object
    hb: object


def _rec_kernel(gate_ref, xr_ref, cw_ref, cb_ref, wbd_ref, bbd_ref, lam_ref, h0f_ref, h0b_ref,
                y_ref, sf_ref, sb_ref, *scratch, nblk, tc, gates_unroll, chunk, unroll):
    blocks = []
    for n in range(nblk):
        sl = slice(n * LANES, (n + 1) * LANES)
        lanes = [r.at[:, sl] for r in (gate_ref, xr_ref, cw_ref, cb_ref)]
        lanes += [wbd_ref.at[n], bbd_ref.at[n]]
        lanes += [r.at[:, sl] for r in (lam_ref, h0f_ref, h0b_ref, y_ref, sf_ref, sb_ref)]
        xp, af, uf, ab, ub = [r.at[n] for r in scratch]
        blocks.append(_RecBlock(*lanes, xp, af, uf, ab, ub, pf=af, hf=uf, pb=ab, hb=ub))
    for blk in blocks:
        _rec_gates(blk, tc=tc, gates_unroll=gates_unroll, chunk=chunk)
    _rec_scan(blocks, chunk=chunk, unroll=unroll)
    for blk in blocks:
        _rec_combine(blk, tc=tc)


def _rec_gates(blk, *, tc, gates_unroll, chunk):
    T = blk.xr.shape[0]
    pad = SUBLANES
    xp = blk.xp
    xp[0:pad, :] = jnp.zeros((pad, LANES), F32)
    xp[T + pad:T + 2 * pad, :] = jnp.zeros((pad, LANES), F32)
    xp[pad:T + pad, :] = blk.xr[...]
    tail = SUBLANES * chunk - T
    for a_s, u_s in ((blk.af, blk.uf), (blk.ab, blk.ub)):
        a_s[T:T + tail, :] = jnp.ones((tail, LANES), F32)
        u_s[T:T + tail, :] = jnp.zeros((tail, LANES), F32)
    cw = blk.cw[...]
    cb = blk.cb[...]
    nl = -blk.lam[...]
    softplus = jnp.maximum(nl, 0.0) + jnp.log1p(jnp.exp(-jnp.abs(nl)))
    cp = (0.5 * RGLRU_C) * softplus

    def gates(it, _):
        for q in range(gates_unroll):
            gate_chunk(it * gates_unroll + q)
        return 0

    def gate_chunk(ci):
        t0 = pl.multiple_of(ci * tc, tc)
        y = cb
        for j in range(CONV_W):
            y = y + xp[pl.ds(t0 + (pad - CONV_LEFT + j), tc), :] * cw[j:j + 1, :]
        t = jnp.tanh(_dot(y.astype(BF16), blk.wbd[...]) + blk.bbd[...])
        yh = 0.5 * y
        for d, (a_s, u_s) in enumerate(((blk.af, blk.uf), (blk.ab, blk.ub))):
            t_r = t[:, 2 * d * LANES:(2 * d + 1) * LANES]
            t_i = t[:, (2 * d + 1) * LANES:(2 * d + 2) * LANES]
            neg_log_a = cp[d:d + 1, :] * t_r + cp[d:d + 1, :]
            a = jnp.exp2(neg_log_a * (-LOG2E))
            w = jnp.tanh(neg_log_a) * (a * a + 1.0)
            sqrt_w = jnp.where(w > 0.0, w * lax.rsqrt(w), 0.0)
            a_s[pl.ds(t0, tc), :] = a
            u_s[pl.ds(t0, tc), :] = (t_i + 1.0) * (yh * sqrt_w)

    lax.fori_loop(0, T // (tc * gates_unroll), gates, 0)


def _rec_scan(blocks, *, chunk, unroll):
    T = blocks[0].xr.shape[0]

    def rows(k):
        return pl.ds(k, SUBLANES, stride=chunk)

    def local_scan(it, carry):
        carry = list(carry)
        for q in range(unroll):
            k = it * unroll + q
            kb = chunk - 1 - k
            for i, blk in enumerate(blocks):
                hf, pf, hb, pb = carry[4 * i:4 * i + 4]
                a = blk.af[rows(k), :]
                hf = a * hf + blk.uf[rows(k), :]
                pf = a * pf
                blk.hf[rows(k), :] = hf
                blk.pf[rows(k), :] = pf
                a = blk.ab[rows(kb), :]
                hb = a * hb + blk.ub[rows(kb), :]
                pb = a * pb
                blk.hb[rows(kb), :] = hb
                blk.pb[rows(kb), :] = pb
                carry[4 * i:4 * i + 4] = [hf, pf, hb, pb]
        return tuple(carry)

    zero = jnp.zeros((SUBLANES, LANES), F32)
    one = jnp.ones((SUBLANES, LANES), F32)
    ends = lax.fori_loop(0, chunk // unroll, local_scan, (zero, one, zero, one) * len(blocks))

    carries = []
    for i, blk in enumerate(blocks):
        hf, pf, hb, pb = ends[4 * i:4 * i + 4]
        cf = [blk.h0f[...]]
        for r in range(SUBLANES - 1):
            cf.append(hf[r:r + 1, :] + pf[r:r + 1, :] * cf[r])
        cb_rev = [blk.h0b[...]]
        for r in range(SUBLANES - 1, 0, -1):
            cb_rev.append(hb[r:r + 1, :] + pb[r:r + 1, :] * cb_rev[-1])
        carries.append((jnp.concatenate(cf, axis=0), jnp.concatenate(cb_rev[::-1], axis=0)))

    def apply_carry(it, _):
        for q in range(unroll):
            k = it * unroll + q
            for blk, (carry_f, carry_b) in zip(blocks, carries):
                blk.uf[rows(k), :] = blk.hf[rows(k), :] + blk.pf[rows(k), :] * carry_f
                blk.ub[rows(k), :] = blk.hb[rows(k), :] + blk.pb[rows(k), :] * carry_b
        return 0

    lax.fori_loop(0, chunk // unroll, apply_carry, 0)
    for blk in blocks:
        blk.sf[...] = blk.uf[T - 1:T, :]
        blk.sb[...] = blk.ub[0:1, :]


def _rec_combine(blk, *, tc):
    T = blk.xr.shape[0]

    def combine(ci, _):
        t0 = pl.multiple_of(ci * tc, tc)
        blk.y[pl.ds(t0, tc), :] = ((blk.uf[pl.ds(t0, tc), :] + blk.ub[pl.ds(t0, tc), :])
                                   * blk.gate[pl.ds(t0, tc), :]).astype(blk.y.dtype)
        return 0

    lax.fori_loop(0, T // tc, combine, 0)


def _rec_mixer(gate, xr, h0f, h0b, conv_w, conv_b, w_bd, b_bd, lam, *, nblk, tc, name):
    B, T, _ = xr.shape
    chunk = _scan_chunk_len(T)
    wid = nblk * LANES
    col = lambda rows: pl.BlockSpec((rows, wid), lambda b, n: (0, n))
    seq = pl.BlockSpec((None, T, wid), lambda b, n: (b, 0, n))
    st = pl.BlockSpec((None, 1, wid), lambda b, n: (b, 0, n))
    return pl.pallas_call(
        functools.partial(_rec_kernel, nblk=nblk, tc=tc, gates_unroll=2, chunk=chunk,
                          unroll=max(u for u in range(1, SCAN_MAX_UNROLL + 1) if chunk % u == 0)),
        grid=(B, RNN_BLOCKS // nblk),
        in_specs=[seq, seq, col(CONV_W), col(1),
                  pl.BlockSpec((nblk, LANES, 4 * LANES), lambda b, n: (n, 0, 0)),
                  pl.BlockSpec((nblk, 1, 4 * LANES), lambda b, n: (n, 0, 0)),
                  col(2), st, st],
        out_specs=[seq, st, st],
        out_shape=[jax.ShapeDtypeStruct((B, T, D_RNN), BF16),
                   jax.ShapeDtypeStruct((B, 1, D_RNN), F32),
                   jax.ShapeDtypeStruct((B, 1, D_RNN), F32)],
        scratch_shapes=[pltpu.VMEM((nblk, T + 2 * SUBLANES, LANES), F32)]
        + [pltpu.VMEM((nblk, SUBLANES * chunk, LANES), F32)] * 4,
        compiler_params=_cparams("parallel", "parallel"),
        name=name,
    )(gate, xr, conv_w, conv_b.reshape(1, D_RNN), w_bd, b_bd, lam, h0f, h0b)


def _post_kernel(*refs, n_mix, final, fc):
    x_ref, mod_ref, g2_ref = refs[:3]
    mix = refs[3:3 + 2 * n_mix]
    w1_ref, w2_ref = refs[3 + 2 * n_mix:5 + 2 * n_mix]
    rest = refs[5 + 2 * n_mix:]
    if final:
        gf_ref, o_ref = rest
    else:
        (o_ref,) = rest
    mixed = _dot(mix[0][...], mix[1][...])
    for i in range(1, n_mix):
        mixed = mixed + _dot(mix[2 * i][...], mix[2 * i + 1][...])
    x1 = x_ref[...] + mod_ref[2:3, :] * mixed
    h = _rms(x1, g2_ref[...])
    hb = (h * (1.0 + mod_ref[4:5, :]) + mod_ref[3:4, :]).astype(BF16)
    n_chunks = D_FF // fc
    up = lambda c: _dot(hb, w1_ref[:, c * fc:(c + 1) * fc])
    acc = None
    a = up(0)
    for c in range(n_chunks):
        a_next = up(c + 1) if c + 1 < n_chunks else None
        part = _dot(jnp.square(jnp.maximum(a, 0.0)).astype(BF16), w2_ref[c * fc:(c + 1) * fc, :])
        acc = part if acc is None else acc + part
        a = a_next
    x2 = x1 + mod_ref[5:6, :] * acc
    if final:
        x2 = _rms(x2, gf_ref[...])
    o_ref[...] = x2


def _post(x, mod, g2, mixes, w1, w2, *, tm, ctx, final_g=None, name):
    B, T, D = x.shape
    row = (lambda b, i: (CTX_ROW, 0, 0)) if ctx else (lambda b, i: (b, 0, 0))
    const = lambda a: pl.BlockSpec(a.shape, lambda b, i: (0,) * a.ndim, pipeline_mode=pl.Buffered(1))
    in_specs = [
        pl.BlockSpec((None, tm, D), lambda b, i: (b, i, 0)),
        pl.BlockSpec((None, 6, D), row),
        pl.BlockSpec((1, D), lambda b, i: (0, 0)),
    ]
    args = [x, mod, g2.reshape(1, D)]
    for o, w in mixes:
        in_specs += [pl.BlockSpec((None, tm, o.shape[-1]), lambda b, i: (b, i, 0)), const(w)]
        args += [o, w]
    in_specs += [const(w1), const(w2)]
    args += [w1, w2]
    if final_g is not None:
        in_specs.append(pl.BlockSpec((1, D), lambda b, i: (0, 0)))
        args.append(final_g.reshape(1, D))
    return pl.pallas_call(
        functools.partial(_post_kernel, n_mix=len(mixes), final=final_g is not None, fc=1024),
        grid=(B, T // tm),
        in_specs=in_specs,
        out_specs=pl.BlockSpec((None, tm, D), lambda b, i: (b, i, 0)),
        out_shape=jax.ShapeDtypeStruct((B, T, D), F32),
        compiler_params=_cparams("parallel", "parallel"),
        name=name,
    )(*args)


def _rope_tables(n_tokens):
    rows = n_tokens // GRID_W
    r, cl = jnp.meshgrid(jnp.arange(rows, dtype=F32), jnp.arange(GRID_W, dtype=F32), indexing='ij')
    quarter = HEAD_DIM // 4
    inv = ROPE_BASE ** (-jnp.arange(quarter, dtype=F32) / quarter)
    ang = jnp.stack([r.reshape(-1)[:, None] * inv, cl.reshape(-1)[:, None] * inv], axis=1)
    cos, sin = jnp.cos(ang), jnp.sin(ang)
    cos64 = jnp.concatenate([cos[:, 0], cos[:, 0], cos[:, 1], cos[:, 1]], axis=-1)
    sin64 = jnp.concatenate([-sin[:, 0], sin[:, 0], -sin[:, 1], sin[:, 1]], axis=-1)
    return jnp.tile(cos64, (1, LANES // HEAD_DIM)), jnp.tile(sin64, (1, LANES // HEAD_DIM))


def _att_in_weights(w_in):
    d = w_in.shape[0]
    nq = A_HEADS * HEAD_DIM
    wq = w_in[:, :nq].reshape(d, A_HEADS, HEAD_DIM)
    z = jnp.zeros_like(wq)
    in_first = (jnp.arange(A_HEADS) // A_GROUP == 0)[None, :, None]
    wq = jnp.where(in_first, jnp.concatenate([wq, z], axis=-1), jnp.concatenate([z, wq], axis=-1))
    return jnp.concatenate([wq.reshape(d, QA_W), w_in[:, nq:]], axis=1).astype(BF16)


def _block_diag_weights(w_a, b_a, w_x, b_x):
    w = jnp.concatenate([w_a[0], w_x[0], w_a[1], w_x[1]], axis=-1)
    b = jnp.concatenate([v.reshape(RNN_BLOCKS, 1, RNN_BW) for v in (b_a[0], b_x[0], b_a[1], b_x[1])], axis=-1)
    return (0.5 * w).astype(BF16), 0.5 * b


def kernel(x_prompt, x_sample, cache_a_k, cache_a_v, cache_b_k, cache_b_v, state_fwd, state_bwd, c, c_ctx, norm1, norm2, w_ada, b_ada, w_mlp1, w_mlp2, att_w_in, att_w_out, att_sink, att_lam_qk, att_subln, rec_w_in, rec_conv_w, rec_conv_b, rec_w_a, rec_b_a, rec_w_x, rec_b_x, rec_lam, rec_w_out, final_norm):
    nb, n_seq, _ = x_prompt.shape
    nd, d_seq, _ = x_sample.shape
    past = cache_a_k.shape[2]
    assert nd <= CTX_ROW and DEPTH == 2
    cvec = jnp.concatenate([c, jnp.zeros((CTX_ROW - nd, D_MODEL), F32), c_ctx[None],
                            jnp.zeros((MOD_ROWS - CTX_ROW - 1, D_MODEL), F32)], axis=0)
    mod = _ada_mod(cvec, w_ada, b_ada)
    w1 = w_mlp1.astype(BF16)
    w2 = w_mlp2.astype(BF16)
    tm_proj, tm_post = 512, 1024
    tok = lambda a, tm: a.reshape(-1, tm, a.shape[-1])
    seq = lambda a: a.reshape(nb, n_seq, a.shape[-1])

    lam_init = 0.8 - 0.6 * math.exp(-0.3 * 0)
    w_in = _att_in_weights(att_w_in[0])
    w_out = att_w_out[0].astype(BF16)
    nqa = A_HEADS * HEAD_DIM
    c0 = [0, QA_W, QA_W + KA_W, QA_W + 2 * KA_W, QA_W + 2 * KA_W + B_W, QA_W + 2 * KA_W + 2 * B_W]
    widths = [QA_W, KA_W, KA_W, B_W, B_W, B_W]
    scales = [SCALE * LOG2E, 1.0, 1.0, SCALE * LOG2E, 1.0, 1.0]
    roped = [True, True, False, True, True, False]
    sink = att_sink[0]
    lam_qk = att_lam_qk[0]
    subln = att_subln[0].reshape(1, 2 * HEAD_DIM)

    rows = lambda dt: (("rows", dt),)
    cache_b = ((("heads", n_seq), F32), ("rows", BF16))
    segs_ctx = [_Seg(c0[0], QA_W, rows(BF16), scale=scales[0]), _Seg(c0[1], KA_W, rows(F32)),
                _Seg(c0[2], KA_W, rows(F32)), _Seg(c0[3], B_W, rows(BF16), scale=scales[3]),
                _Seg(c0[4], B_W, cache_b), _Seg(c0[5], B_W, cache_b)]
    qa, ka, va, qb, new_b_k, kb, new_b_v, vb = _proj(tok(x_prompt, tm_proj), mod[0], norm1[0], w_in, segs_ctx, tm=tm_proj,
                                                     ctx=True, name="proj_att_ctx")
    qa, ka, va, qb, kb, vb = map(seq, (qa, ka, va, qb, kb, vb))
    o_ctx = _ctx_attention(qa, ka, va, qb, kb, vb, sink, lam_qk, subln, lam_init)
    xp = _post(tok(x_prompt, tm_post), mod[0], norm2[0], [(tok(o_ctx, tm_post), w_out)], w1[0], w2[0], tm=tm_post, ctx=True,
               name="post_att_ctx")
    new_a_k = ka.reshape(nb, 1, n_seq, A_KV_HEADS, HEAD_DIM)
    new_a_v = va.reshape(nb, 1, n_seq, A_KV_HEADS, HEAD_DIM)

    segs_lat = [_Seg(c0[i], widths[i], (("cols" if i in (2, 3, 5) else "rows", BF16),), rope=roped[i],
                     scale=scales[i]) for i in range(6)]
    qa, ka, vat, qbt, kb, vbt = _proj(x_sample, mod[0], norm1[0], w_in, segs_lat, tm=tm_proj, ctx=False,
                                      rope_tabs=_rope_tables(d_seq), name="proj_att_lat")
    cka = cache_a_k[:, 0].reshape(nd, past, KA_W).astype(BF16)
    cvat = jnp.swapaxes(cache_a_v[:, 0].reshape(nd, past, KA_W), 1, 2).astype(BF16)
    ckb = cache_b_k[:, 0].reshape(nd, past, B_W).astype(BF16)
    cvbt = jnp.swapaxes(cache_b_v[:, 0].reshape(nd, past, B_W), 1, 2).astype(BF16)
    oa = _lat_a_attention(qa, ka, vat, cka, cvat, sink)
    ob = _lat_b_attention(qbt, kb, vbt, ckb, cvbt, lam_qk, subln, lam_init)
    xs = _post(x_sample, mod[0], norm2[0], [(oa, w_out[:nqa]), (ob, w_out[nqa:])], w1[0], w2[0], tm=tm_post,
               ctx=False, name="post_att_lat")

    w_rin = rec_w_in[0].astype(BF16)
    w_rout = rec_w_out[0].astype(BF16)
    w_bd, b_bd = _block_diag_weights(rec_w_a[0], rec_b_a[0], rec_w_x[0], rec_b_x[0])
    segs_rec = [_Seg(0, D_RNN, rows(F32), gelu=True), _Seg(D_RNN, D_RNN, rows(F32))]
    zeros = jnp.zeros((nb, 1, D_RNN), F32)

    gate, xr = map(seq, _proj(tok(xp, tm_proj), mod[1], norm1[1], w_rin, segs_rec, tm=tm_proj, ctx=True,
                              name="proj_rec_ctx"))
    y, sf, sb = _rec_mixer(gate, xr, zeros, zeros, rec_conv_w[0], rec_conv_b[0], w_bd, b_bd, rec_lam[0],
                           nblk=RNN_BLOCKS, tc=128, name="rec_mixer_ctx")
    y_prompt = seq(_post(xp, mod[1], norm2[1], [(tok(y, tm_post), w_rout)], w1[1], w2[1], tm=tm_post, ctx=True,
                         final_g=final_norm, name="post_rec_ctx"))

    gate, xr = _proj(xs, mod[1], norm1[1], w_rin, segs_rec, tm=tm_proj, ctx=False, name="proj_rec_lat")
    y, _, _ = _rec_mixer(gate, xr, state_fwd[:, 0:1], state_bwd[:, 0:1], rec_conv_w[0], rec_conv_b[0], w_bd, b_bd,
                         rec_lam[0], nblk=2, tc=256, name="rec_mixer_lat")
    y_sample = _post(xs, mod[1], norm2[1], [(y, w_rout)], w1[1], w2[1], tm=tm_post, ctx=False, final_g=final_norm,
                     name="post_rec_lat")

    return (y_prompt, y_sample, new_a_k, new_a_v, new_b_k, new_b_v, sf, sb)
```

```python
import functools
import math
from typing import NamedTuple

import jax
import jax.numpy as jnp
import numpy as np
from jax import lax
from jax.experimental import pallas as pl
from jax.experimental.pallas import tpu as pltpu

F32 = jnp.float32
BF16 = jnp.bfloat16

LANES = 128
SUBLANES = 8
VMEM_LIMIT_BYTES = 56 * 1024 * 1024

D_MODEL = 1024
DEPTH = 2
GRID_W = 64
HEAD_DIM = 64
A_HEADS = 8
A_KV_HEADS = 2
A_GROUP = A_HEADS // A_KV_HEADS
B_HEADS = 4
WINDOW = 128
ROPE_BASE = 10000.0
D_RNN = 1280
RNN_BLOCKS = 10
RNN_BW = D_RNN // RNN_BLOCKS
CONV_W = 4
CONV_LEFT = (CONV_W - 1) // 2
RGLRU_C = 8.0
D_FF = 4 * D_MODEL
EPS = 1e-6
SCALE = HEAD_DIM ** -0.5
NEG = -1e30

QA_W = A_HEADS * LANES
KA_W = A_KV_HEADS * HEAD_DIM
B_W = B_HEADS * 2 * HEAD_DIM
MOD_ROWS = 8
CTX_ROW = 4
LOG2E = math.log2(math.e)
ONES_ROWS = 16
PROJ_ROW_GROUPS = 2
SCAN_MAX_UNROLL = 1024


def _cparams(*semantics):
    return pltpu.CompilerParams(dimension_semantics=semantics, vmem_limit_bytes=VMEM_LIMIT_BYTES)


def _dot(a, b):
    return jnp.dot(a, b, preferred_element_type=F32)


def _dot_nt(a, b):
    return lax.dot_general(a, b, (((1,), (1,)), ((), ())), preferred_element_type=F32)


def _rms(x, g):
    return x * lax.rsqrt(jnp.mean(x * x, axis=-1, keepdims=True) + EPS) * g


def _gelu_tanh(x):
    return x * (0.5 * (1.0 + jnp.tanh(math.sqrt(2.0 / math.pi) * (x + 0.044715 * (x * x * x)))))


def _ada_kernel(c_ref, w_ref, b_ref, o_ref):
    c = c_ref[...]
    s = c * jax.nn.sigmoid(c)
    o_ref[...] = _dot(s.astype(BF16), w_ref[...].astype(BF16)) + b_ref[...]


def _ada_mod(cvec, w_ada, b_ada):
    tn = 1536
    out = pl.pallas_call(
        _ada_kernel,
        grid=(DEPTH, 6 * D_MODEL // tn),
        in_specs=[
            pl.BlockSpec((MOD_ROWS, D_MODEL), lambda l, j: (0, 0)),
            pl.BlockSpec((None, D_MODEL, tn), lambda l, j: (l, 0, j)),
            pl.BlockSpec((None, 1, tn), lambda l, j: (l, 0, j)),
        ],
        out_specs=pl.BlockSpec((None, MOD_ROWS, tn), lambda l, j: (l, 0, j)),
        out_shape=jax.ShapeDtypeStruct((DEPTH, MOD_ROWS, 6 * D_MODEL), F32),
        compiler_params=_cparams("parallel", "parallel"),
        name="ada_mod",
    )(cvec, w_ada, b_ada.reshape(DEPTH, 1, 6 * D_MODEL))
    return out.reshape(DEPTH, MOD_ROWS, 6, D_MODEL)


class _Seg(NamedTuple):
    col0: int
    width: int
    outs: tuple
    rope: bool = False
    scale: float = 1.0
    gelu: bool = False


def _proj_kernel(*refs, segs, rope):
    if rope:
        x_ref, mod_ref, g_ref, w_ref, cos_ref, sin_ref, *outs = refs
    else:
        x_ref, mod_ref, g_ref, w_ref, *outs = refs
    tm = x_ref.shape[0]
    rows = tm // PROJ_ROW_GROUPS
    groups = [slice(i * rows, (i + 1) * rows) for i in range(PROJ_ROW_GROUPS)]
    hbs = []
    for grp in groups:
        h = _rms(x_ref[grp, :], g_ref[...])
        hbs.append((h * (1.0 + mod_ref[1:2, :]) + mod_ref[0:1, :]).astype(BF16))
    if rope:
        first = (lax.broadcasted_iota(jnp.int32, (rows, LANES), 1) & 16) == 0
    outs = iter(outs)
    for seg in segs:
        ys = [_dot(hb, w_ref[:, seg.col0:seg.col0 + seg.width]) for hb in hbs]
        o_refs = [next(outs) for _ in seg.outs]
        for grp, y in zip(groups, ys):
            for t in range(seg.width // LANES):
                lanes = slice(t * LANES, (t + 1) * LANES)
                yt = y[:, lanes]
                if seg.rope:
                    sw = jnp.where(first, pltpu.roll(yt, LANES - 16, 1), pltpu.roll(yt, 16, 1))
                    yt = yt * cos_ref[grp, :] + sw * sin_ref[grp, :]
                if seg.scale != 1.0:
                    yt = yt * seg.scale
                if seg.gelu:
                    yt = _gelu_tanh(yt)
                for (layout, _), o_ref in zip(seg.outs, o_refs):
                    if layout == "cols":
                        o_ref[lanes, grp] = yt.T.astype(o_ref.dtype)
                    elif layout == "rows":
                        o_ref[grp, lanes] = yt.astype(o_ref.dtype)
                    else:
                        seq_len = layout[1]
                        assert rows % seq_len == 0
                        for r in range(rows // seq_len):
                            o_ref[grp.start // seq_len + r, :, t, :] = (
                                yt[r * seq_len:(r + 1) * seq_len].astype(o_ref.dtype))


def _proj(x, mod, g, w, segs, *, tm, ctx, rope_tabs=None, name):
    B, T, D = x.shape

    def out_spec(width, layout):
        if layout == "cols":
            return pl.BlockSpec((None, width, tm), lambda b, i: (b, 0, i))
        if layout == "rows":
            return pl.BlockSpec((None, tm, width), lambda b, i: (b, i, 0))
        return pl.BlockSpec((tm // layout[1], None, layout[1], width // LANES, LANES),
                            lambda b, i: (b * (T // tm) + i, 0, 0, 0, 0))

    def out_struct(width, layout, dt):
        if layout == "cols":
            return jax.ShapeDtypeStruct((B, width, T), dt)
        if layout == "rows":
            return jax.ShapeDtypeStruct((B, T, width), dt)
        return jax.ShapeDtypeStruct((B * T // layout[1], 1, layout[1], width // LANES, LANES), dt)

    row = (lambda b, i: (CTX_ROW, 0, 0)) if ctx else (lambda b, i: (b, 0, 0))
    in_specs = [
        pl.BlockSpec((None, tm, D), lambda b, i: (b, i, 0)),
        pl.BlockSpec((None, 6, D), row),
        pl.BlockSpec((1, D), lambda b, i: (0, 0)),
        pl.BlockSpec(w.shape, lambda b, i: (0, 0)),
    ]
    args = [x, mod, g.reshape(1, D), w]
    if rope_tabs is not None:
        in_specs += [pl.BlockSpec((tm, LANES), lambda b, i: (i, 0))] * 2
        args += list(rope_tabs)
    out_specs = [out_spec(s.width, layout) for s in segs for layout, _ in s.outs]
    out_shape = [out_struct(s.width, layout, dt) for s in segs for layout, dt in s.outs]
    return pl.pallas_call(
        functools.partial(_proj_kernel, segs=tuple(segs), rope=rope_tabs is not None),
        grid=(B, T // tm),
        in_specs=in_specs,
        out_specs=out_specs,
        out_shape=out_shape,
        compiler_params=_cparams("parallel", "parallel"),
        name=name,
    )(*args)


def _diff_lambda(lq, lam_init):
    s1 = jnp.sum(lq[0:1, :] * lq[1:2, :], axis=1, keepdims=True)
    s2 = jnp.sum(lq[2:3, :] * lq[3:4, :], axis=1, keepdims=True)
    return jnp.exp(s1) - jnp.exp(s2) + lam_init


def _stack_group_queries(qa_ref, g, rows):
    return jnp.concatenate(
        [qa_ref[rows, (A_GROUP * g + hh) * LANES:(A_GROUP * g + hh + 1) * LANES] for hh in range(A_GROUP)], axis=0)


def _stack_pair_queries(q):
    lo = lax.broadcasted_iota(jnp.int32, q.shape, 1) < HEAD_DIM
    zero = jnp.zeros_like(q)
    return jnp.concatenate([jnp.where(lo, q, zero), jnp.where(lo, zero, q)], axis=0)


def _with_ones_rows(vt):
    return jnp.concatenate([vt, jnp.ones((ONES_ROWS, vt.shape[1]), BF16)], axis=0)


def _sink_row(sink_ref, g, tq):
    return jnp.concatenate(
        [jnp.full((1, tq), sink_ref[A_GROUP * g + hh] * LOG2E, F32) for hh in range(A_GROUP)], axis=1)


def _softmax_values(parts, sink=None):
    m = functools.reduce(jnp.maximum, [jnp.max(s, axis=0, keepdims=True) for s, _ in parts])
    if sink is not None:
        m = jnp.maximum(m, sink)
    ot = sum(_dot(vt1, jnp.exp2(s - m).astype(BF16)) for s, vt1 in parts)
    den = ot[LANES:LANES + 1]
    if sink is not None:
        den = den + jnp.exp2(sink - m)
    return ot[:LANES] / den


def _a_heads(ot, g, tq):
    return [ot[g * HEAD_DIM:(g + 1) * HEAD_DIM, hh * tq:(hh + 1) * tq] for hh in range(A_GROUP)]


def _subln(o, subln, lam_init):
    return _rms(o, subln) * (1.0 - lam_init)


def _diff_combine(ot, lam, subln, lam_init):
    tq = ot.shape[1] // 2
    return _subln((ot[:, :tq] - lam * ot[:, tq:]).T, subln, lam_init)


def _ctx_attn_kernel(sink_ref, qa_ref, ka_ref, va_ref, qb_ref, kb_ref, vb_ref, lamqk_ref, subln_ref, o_ref, *,
                     lam_init):
    T = qa_ref.shape[0]
    ka = ka_ref[...].astype(BF16)
    vat1 = _with_ones_rows(va_ref[...].T.astype(BF16))
    scores_a = [_dot_nt(ka, _stack_group_queries(qa_ref, g, slice(None))) for g in range(A_KV_HEADS)]
    scores_b, vbt1 = [], []
    for h in range(B_HEADS):
        sl = slice(h * LANES, (h + 1) * LANES)
        scores_b.append(_dot_nt(kb_ref[:, sl].astype(BF16), _stack_pair_queries(qb_ref[:, sl])))
        vbt1.append(_with_ones_rows(vb_ref[:, sl].T.astype(BF16)))
    heads = []
    for g in range(A_KV_HEADS):
        heads += _a_heads(_softmax_values([(scores_a[g], vat1)], _sink_row(sink_ref, g, T)), g, T)
    nqa = A_HEADS * HEAD_DIM
    o_ref[:, :nqa] = jnp.concatenate(heads, axis=0).T.astype(o_ref.dtype)
    lam = _diff_lambda(lamqk_ref[...], lam_init)
    for h in range(B_HEADS):
        ot = _softmax_values([(scores_b[h], vbt1[h])])
        o_ref[:, nqa + h * LANES:nqa + (h + 1) * LANES] = (
            _diff_combine(ot, lam, subln_ref[...], lam_init).astype(o_ref.dtype))


def _ctx_attention(qa, ka, va, qb, kb, vb, sink, lam_qk, subln, lam_init):
    B, T, _ = qa.shape
    blk = lambda w: pl.BlockSpec((None, T, w), lambda b: (b, 0, 0))
    full = lambda a: pl.BlockSpec(a.shape, lambda b: (0,) * a.ndim)
    return pl.pallas_call(
        functools.partial(_ctx_attn_kernel, lam_init=lam_init),
        grid=(B,),
        in_specs=[pl.BlockSpec(memory_space=pltpu.SMEM), blk(QA_W), blk(KA_W), blk(KA_W), blk(B_W), blk(B_W),
                  blk(B_W), full(lam_qk), full(subln)],
        out_specs=blk(A_HEADS * HEAD_DIM + B_W),
        out_shape=jax.ShapeDtypeStruct((B, T, A_HEADS * HEAD_DIM + B_W), BF16),
        compiler_params=_cparams("parallel"),
        name="ctx_attention",
    )(sink, qa, ka, va, qb, kb, vb, lam_qk, subln)


def _lat_a_kernel(sink_ref, qa_ref, k_ref, vt_ref, ck_ref, cvt_ref, o_ref, *, tq, nq, band):
    T = k_ref.shape[0]
    cols = A_GROUP * tq
    cvt1 = _with_ones_rows(cvt_ref[...])
    jobs = []
    for i in range(nq):
        qi = pl.program_id(1) * nq + i
        start = pl.multiple_of(jnp.clip(qi * tq - WINDOW, 0, T - band), WINDOW)
        kb = k_ref[pl.ds(start, band), :]
        vbt1 = _with_ones_rows(vt_ref[:, pl.ds(start, band)])
        kpos = start + lax.broadcasted_iota(jnp.int32, (band, cols), 0)
        qpos = qi * tq + lax.broadcasted_iota(jnp.int32, (band, cols), 1) % tq
        keep = jnp.abs(qpos - kpos) <= WINDOW
        for g in range(A_KV_HEADS):
            qg = _stack_group_queries(qa_ref, g, slice(i * tq, (i + 1) * tq))
            jobs.append((_dot_nt(ck_ref[...], qg), _dot_nt(kb, qg), keep, vbt1))
    for i in range(nq):
        heads = []
        for g in range(A_KV_HEADS):
            s_c, s_b, keep, vbt1 = jobs[i * A_KV_HEADS + g]
            ot = _softmax_values([(s_c, cvt1), (jnp.where(keep, s_b, NEG), vbt1)], _sink_row(sink_ref, g, tq))
            heads += _a_heads(ot, g, tq)
        o_ref[i * tq:(i + 1) * tq, :] = jnp.concatenate(heads, axis=0).T.astype(o_ref.dtype)


def _lat_a_attention(qa, ka, vat, cka, cvat, sink):
    B, T, _ = qa.shape
    L = cka.shape[1]
    tq, nq = WINDOW, 8
    band = 3 * WINDOW
    return pl.pallas_call(
        functools.partial(_lat_a_kernel, tq=tq, nq=nq, band=band),
        grid=(B, T // (tq * nq)),
        in_specs=[
            pl.BlockSpec(memory_space=pltpu.SMEM),
            pl.BlockSpec((None, tq * nq, QA_W), lambda b, i: (b, i, 0)),
            pl.BlockSpec((None, T, KA_W), lambda b, i: (b, 0, 0)),
            pl.BlockSpec((None, KA_W, T), lambda b, i: (b, 0, 0)),
            pl.BlockSpec((None, L, KA_W), lambda b, i: (b, 0, 0)),
            pl.BlockSpec((None, KA_W, L), lambda b, i: (b, 0, 0)),
        ],
        out_specs=pl.BlockSpec((None, tq * nq, A_HEADS * HEAD_DIM), lambda b, i: (b, i, 0)),
        out_shape=jax.ShapeDtypeStruct((B, T, A_HEADS * HEAD_DIM), BF16),
        compiler_params=_cparams("parallel", "parallel"),
        name="lat_a_attention",
    )(sink, qa, ka, vat, cka, cvat)


def _lat_b_kernel(qt_ref, k_ref, vt_ref, ck_ref, cvt_ref, lamqk_ref, subln_ref, o_ref, *, tk, ahead, lam_init):
    tq = qt_ref.shape[1]
    T = k_ref.shape[0]
    qt = qt_ref[...]
    zero = jnp.zeros((HEAD_DIM, tq), BF16)
    qst = jnp.concatenate([jnp.concatenate([qt[:HEAD_DIM], zero], axis=0),
                           jnp.concatenate([zero, qt[HEAD_DIM:]], axis=0)], axis=1)
    blocks = [(ck_ref[...], cvt_ref[...])]
    blocks += [(k_ref[j * tk:(j + 1) * tk, :], vt_ref[:, j * tk:(j + 1) * tk]) for j in range(T // tk)]

    def scores(k):
        return _dot(k, qst)

    def accumulate(s, vt, carry):
        m, acc = carry
        m_new = jnp.maximum(m, jnp.max(s, axis=0, keepdims=True))
        p = jnp.exp2(s - m_new).astype(BF16)
        acc = jnp.exp2(m - m_new) * acc + _dot(_with_ones_rows(vt), p)
        return m_new, acc

    carry = (jnp.full((1, 2 * tq), -jnp.inf, F32), jnp.zeros((LANES + ONES_ROWS, 2 * tq), F32))
    pending = [scores(blocks[j][0]) for j in range(ahead)]
    for j in range(len(blocks)):
        if j + ahead < len(blocks):
            pending.append(scores(blocks[j + ahead][0]))
        carry = accumulate(pending.pop(0), blocks[j][1], carry)
    _, acc = carry
    ot = acc[:LANES] / acc[LANES:LANES + 1]
    lam = _diff_lambda(lamqk_ref[...], lam_init)
    o_ref[...] = _diff_combine(ot, lam, subln_ref[...], lam_init).astype(o_ref.dtype)


def _lat_b_attention(qbt, kb, vbt, ckb, cvbt, lam_qk, subln, lam_init):
    B, T, _ = kb.shape
    L = ckb.shape[1]
    tq = 1024
    full = lambda a: pl.BlockSpec(a.shape, lambda b, h, i: (0,) * a.ndim)
    return pl.pallas_call(
        functools.partial(_lat_b_kernel, tk=512, ahead=2, lam_init=lam_init),
        grid=(B, B_HEADS, T // tq),
        in_specs=[
            pl.BlockSpec((None, LANES, tq), lambda b, h, i: (b, h, i)),
            pl.BlockSpec((None, T, LANES), lambda b, h, i: (b, 0, h)),
            pl.BlockSpec((None, LANES, T), lambda b, h, i: (b, h, 0)),
            pl.BlockSpec((None, L, LANES), lambda b, h, i: (b, 0, h)),
            pl.BlockSpec((None, LANES, L), lambda b, h, i: (b, h, 0)),
            full(lam_qk), full(subln),
        ],
        out_specs=pl.BlockSpec((None, tq, LANES), lambda b, h, i: (b, i, h)),
        out_shape=jax.ShapeDtypeStruct((B, T, B_W), BF16),
        compiler_params=_cparams("parallel", "parallel", "parallel"),
        name="lat_b_attention",
    )(qbt, kb, vbt, ckb, cvbt, lam_qk, subln)


def _scan_chunk_len(n_steps):
    chunk = -(-n_steps // SUBLANES)
    while chunk % 8 != 4:
        chunk += 1
    return chunk


class _RecBlock(NamedTuple):
    gate: object
    xr: object
    cw: object
    cb: object
    wbd: object
    bbd: object
    lam: object
    h0f: object
    h0b: object
    y: object
    sf: object
    sb: object
    xp: object
    af: object
    uf: object
    ab: object
    ub: object
    pf: object
    hf: object
    pb: object
    hb: object


def _rec_kernel(gate_ref, xr_ref, cw_ref, cb_ref, wbd_ref, bbd_ref, lam_ref, h0f_ref, h0b_ref,
                y_ref, sf_ref, sb_ref, *scratch, nblk, tc, gates_unroll, chunk, unroll):
    blocks = []
    for n in range(nblk):
        sl = slice(n * LANES, (n + 1) * LANES)
        lanes = [r.at[:, sl] for r in (gate_ref, xr_ref, cw_ref, cb_ref)]
        lanes += [wbd_ref.at[n], bbd_ref.at[n]]
        lanes += [r.at[:, sl] for r in (lam_ref, h0f_ref, h0b_ref, y_ref, sf_ref, sb_ref)]
        xp, af, uf, ab, ub = [r.at[n] for r in scratch]
        blocks.append(_RecBlock(*lanes, xp, af, uf, ab, ub, pf=af, hf=uf, pb=ab, hb=ub))
    for blk in blocks:
        _rec_gates(blk, tc=tc, gates_unroll=gates_unroll, chunk=chunk)
    _rec_scan(blocks, chunk=chunk, unroll=unroll)
    for blk in blocks:
        _rec_combine(blk, tc=tc)


def _rec_gates(blk, *, tc, gates_unroll, chunk):
    T = blk.xr.shape[0]
    pad = SUBLANES
    xp = blk.xp
    xp[0:pad, :] = jnp.zeros((pad, LANES), F32)
    xp[T + pad:T + 2 * pad, :] = jnp.zeros((pad, LANES), F32)
    xp[pad:T + pad, :] = blk.xr[...]
    tail = SUBLANES * chunk - T
    for a_s, u_s in ((blk.af, blk.uf), (blk.ab, blk.ub)):
        a_s[T:T + tail, :] = jnp.ones((tail, LANES), F32)
        u_s[T:T + tail, :] = jnp.zeros((tail, LANES), F32)
    cw = blk.cw[...]
    cb = blk.cb[...]
    nl = -blk.lam[...]
    softplus = jnp.maximum(nl, 0.0) + jnp.log1p(jnp.exp(-jnp.abs(nl)))
    cp = (0.5 * RGLRU_C) * softplus

    def gates(it, _):
        for q in range(gates_unroll):
            gate_chunk(it * gates_unroll + q)
        return 0

    def gate_chunk(ci):
        t0 = pl.multiple_of(ci * tc, tc)
        y = cb
        for j in range(CONV_W):
            y = y + xp[pl.ds(t0 + (pad - CONV_LEFT + j), tc), :] * cw[j:j + 1, :]
        t = jnp.tanh(_dot(y.astype(BF16), blk.wbd[...]) + blk.bbd[...])
        yh = 0.5 * y
        for d, (a_s, u_s) in enumerate(((blk.af, blk.uf), (blk.ab, blk.ub))):
            t_r = t[:, 2 * d * LANES:(2 * d + 1) * LANES]
            t_i = t[:, (2 * d + 1) * LANES:(2 * d + 2) * LANES]
            neg_log_a = cp[d:d + 1, :] * t_r + cp[d:d + 1, :]
            a = jnp.exp2(neg_log_a * (-LOG2E))
            w = jnp.tanh(neg_log_a) * (a * a + 1.0)
            sqrt_w = jnp.where(w > 0.0, w * lax.rsqrt(w), 0.0)
            a_s[pl.ds(t0, tc), :] = a
            u_s[pl.ds(t0, tc), :] = (t_i + 1.0) * (yh * sqrt_w)

    lax.fori_loop(0, T // (tc * gates_unroll), gates, 0)


def _rec_scan(blocks, *, chunk, unroll):
    T = blocks[0].xr.shape[0]

    def rows(k):
        return pl.ds(k, SUBLANES, stride=chunk)

    def local_scan(it, carry):
        carry = list(carry)
        for q in range(unroll):
            k = it * unroll + q
            kb = chunk - 1 - k
            for i, blk in enumerate(blocks):
                hf, pf, hb, pb = carry[4 * i:4 * i + 4]
                a = blk.af[rows(k), :]
                hf = a * hf + blk.uf[rows(k), :]
                pf = a * pf
                blk.hf[rows(k), :] = hf
                blk.pf[rows(k), :] = pf
                a = blk.ab[rows(kb), :]
                hb = a * hb + blk.ub[rows(kb), :]
                pb = a * pb
                blk.hb[rows(kb), :] = hb
                blk.pb[rows(kb), :] = pb
                carry[4 * i:4 * i + 4] = [hf, pf, hb, pb]
        return tuple(carry)

    zero = jnp.zeros((SUBLANES, LANES), F32)
    one = jnp.ones((SUBLANES, LANES), F32)
    ends = lax.fori_loop(0, chunk // unroll, local_scan, (zero, one, zero, one) * len(blocks))

    carries = []
    for i, blk in enumerate(blocks):
        hf, pf, hb, pb = ends[4 * i:4 * i + 4]
        cf = [blk.h0f[...]]
        for r in range(SUBLANES - 1):
            cf.append(hf[r:r + 1, :] + pf[r:r + 1, :] * cf[r])
        cb_rev = [blk.h0b[...]]
        for r in range(SUBLANES - 1, 0, -1):
            cb_rev.append(hb[r:r + 1, :] + pb[r:r + 1, :] * cb_rev[-1])
        carries.append((jnp.concatenate(cf, axis=0), jnp.concatenate(cb_rev[::-1], axis=0)))

    def apply_carry(it, _):
        for q in range(unroll):
            k = it * unroll + q
            for blk, (carry_f, carry_b) in zip(blocks, carries):
                blk.uf[rows(k), :] = blk.hf[rows(k), :] + blk.pf[rows(k), :] * carry_f
                blk.ub[rows(k), :] = blk.hb[rows(k), :] + blk.pb[rows(k), :] * carry_b
        return 0

    lax.fori_loop(0, chunk // unroll, apply_carry, 0)
    for blk in blocks:
        blk.sf[...] = blk.uf[T - 1:T, :]
        blk.sb[...] = blk.ub[0:1, :]


def _rec_combine(blk, *, tc):
    T = blk.xr.shape[0]

    def combine(ci, _):
        t0 = pl.multiple_of(ci * tc, tc)
        blk.y[pl.ds(t0, tc), :] = ((blk.uf[pl.ds(t0, tc), :] + blk.ub[pl.ds(t0, tc), :])
                                   * blk.gate[pl.ds(t0, tc), :]).astype(blk.y.dtype)
        return 0

    lax.fori_loop(0, T // tc, combine, 0)


def _rec_mixer(gate, xr, h0f, h0b, conv_w, conv_b, w_bd, b_bd, lam, *, nblk, tc, name):
    B, T, _ = xr.shape
    chunk = _scan_chunk_len(T)
    wid = nblk * LANES
    col = lambda rows: pl.BlockSpec((rows, wid), lambda b, n: (0, n))
    seq = pl.BlockSpec((None, T, wid), lambda b, n: (b, 0, n))
    st = pl.BlockSpec((None, 1, wid), lambda b, n: (b, 0, n))
    return pl.pallas_call(
        functools.partial(_rec_kernel, nblk=nblk, tc=tc, gates_unroll=2, chunk=chunk,
                          unroll=max(u for u in range(1, SCAN_MAX_UNROLL + 1) if chunk % u == 0)),
        grid=(B, RNN_BLOCKS // nblk),
        in_specs=[seq, seq, col(CONV_W), col(1),
                  pl.BlockSpec((nblk, LANES, 4 * LANES), lambda b, n: (n, 0, 0)),
                  pl.BlockSpec((nblk, 1, 4 * LANES), lambda b, n: (n, 0, 0)),
                  col(2), st, st],
        out_specs=[seq, st, st],
        out_shape=[jax.ShapeDtypeStruct((B, T, D_RNN), BF16),
                   jax.ShapeDtypeStruct((B, 1, D_RNN), F32),
                   jax.ShapeDtypeStruct((B, 1, D_RNN), F32)],
        scratch_shapes=[pltpu.VMEM((nblk, T + 2 * SUBLANES, LANES), F32)]
        + [pltpu.VMEM((nblk, SUBLANES * chunk, LANES), F32)] * 4,
        compiler_params=_cparams("parallel", "parallel"),
        name=name,
    )(gate, xr, conv_w, conv_b.reshape(1, D_RNN), w_bd, b_bd, lam, h0f, h0b)


def _post_kernel(*refs, n_mix, final, fc):
    x_ref, mod_ref, g2_ref = refs[:3]
    mix = refs[3:3 + 2 * n_mix]
    w1_ref, w2_ref = refs[3 + 2 * n_mix:5 + 2 * n_mix]
    rest = refs[5 + 2 * n_mix:]
    if final:
        gf_ref, o_ref = rest
    else:
        (o_ref,) = rest
    mixed = _dot(mix[0][...], mix[1][...])
    for i in range(1, n_mix):
        mixed = mixed + _dot(mix[2 * i][...], mix[2 * i + 1][...])
    x1 = x_ref[...] + mod_ref[2:3, :] * mixed
    h = _rms(x1, g2_ref[...])
    hb = (h * (1.0 + mod_ref[4:5, :]) + mod_ref[3:4, :]).astype(BF16)
    n_chunks = D_FF // fc
    up = lambda c: _dot(hb, w1_ref[:, c * fc:(c + 1) * fc])
    acc = None
    a = up(0)
    for c in range(n_chunks):
        a_next = up(c + 1) if c + 1 < n_chunks else None
        part = _dot(jnp.square(jnp.maximum(a, 0.0)).astype(BF16), w2_ref[c * fc:(c + 1) * fc, :])
        acc = part if acc is None else acc + part
        a = a_next
    x2 = x1 + mod_ref[5:6, :] * acc
    if final:
        x2 = _rms(x2, gf_ref[...])
    o_ref[...] = x2


def _post(x, mod, g2, mixes, w1, w2, *, tm, ctx, final_g=None, name):
    B, T, D = x.shape
    row = (lambda b, i: (CTX_ROW, 0, 0)) if ctx else (lambda b, i: (b, 0, 0))
    const = lambda a: pl.BlockSpec(a.shape, lambda b, i: (0,) * a.ndim, pipeline_mode=pl.Buffered(1))
    in_specs = [
        pl.BlockSpec((None, tm, D), lambda b, i: (b, i, 0)),
        pl.BlockSpec((None, 6, D), row),
        pl.BlockSpec((1, D), lambda b, i: (0, 0)),
    ]
    args = [x, mod, g2.reshape(1, D)]
    for o, w in mixes:
        in_specs += [pl.BlockSpec((None, tm, o.shape[-1]), lambda b, i: (b, i, 0)), const(w)]
        args += [o, w]
    in_specs += [const(w1), const(w2)]
    args += [w1, w2]
    if final_g is not None:
        in_specs.append(pl.BlockSpec((1, D), lambda b, i: (0, 0)))
        args.append(final_g.reshape(1, D))
    return pl.pallas_call(
        functools.partial(_post_kernel, n_mix=len(mixes), final=final_g is not None, fc=1024),
        grid=(B, T // tm),
        in_specs=in_specs,
        out_specs=pl.BlockSpec((None, tm, D), lambda b, i: (b, i, 0)),
        out_shape=jax.ShapeDtypeStruct((B, T, D), F32),
        compiler_params=_cparams("parallel", "parallel"),
        name=name,
    )(*args)


def _rope_tables(n_tokens):
    rows = n_tokens // GRID_W
    r, cl = jnp.meshgrid(jnp.arange(rows, dtype=F32), jnp.arange(GRID_W, dtype=F32), indexing='ij')
    quarter = HEAD_DIM // 4
    inv = ROPE_BASE ** (-jnp.arange(quarter, dtype=F32) / quarter)
    ang = jnp.stack([r.reshape(-1)[:, None] * inv, cl.reshape(-1)[:, None] * inv], axis=1)
    cos, sin = jnp.cos(ang), jnp.sin(ang)
    cos64 = jnp.concatenate([cos[:, 0], cos[:, 0], cos[:, 1], cos[:, 1]], axis=-1)
    sin64 = jnp.concatenate([-sin[:, 0], sin[:, 0], -sin[:, 1], sin[:, 1]], axis=-1)
    return jnp.tile(cos64, (1, LANES // HEAD_DIM)), jnp.tile(sin64, (1, LANES // HEAD_DIM))


def _att_in_weights(w_in):
    d = w_in.shape[0]
    nq = A_HEADS * HEAD_DIM
    wq = w_in[:, :nq].reshape(d, A_HEADS, HEAD_DIM)
    z = jnp.zeros_like(wq)
    in_first = (jnp.arange(A_HEADS) // A_GROUP == 0)[None, :, None]
    wq = jnp.where(in_first, jnp.concatenate([wq, z], axis=-1), jnp.concatenate([z, wq], axis=-1))
    return jnp.concatenate([wq.reshape(d, QA_W), w_in[:, nq:]], axis=1).astype(BF16)


def _block_diag_weights(w_a, b_a, w_x, b_x):
    w = jnp.concatenate([w_a[0], w_x[0], w_a[1], w_x[1]], axis=-1)
    b = jnp.concatenate([v.reshape(RNN_BLOCKS, 1, RNN_BW) for v in (b_a[0], b_x[0], b_a[1], b_x[1])], axis=-1)
    return (0.5 * w).astype(BF16), 0.5 * b


def kernel(x_prompt, x_sample, cache_a_k, cache_a_v, cache_b_k, cache_b_v, state_fwd, state_bwd, c, c_ctx, norm1, norm2, w_ada, b_ada, w_mlp1, w_mlp2, att_w_in, att_w_out, att_sink, att_lam_qk, att_subln, rec_w_in, rec_conv_w, rec_conv_b, rec_w_a, rec_b_a, rec_w_x, rec_b_x, rec_lam, rec_w_out, final_norm):
    nb, n_seq, _ = x_prompt.shape
    nd, d_seq, _ = x_sample.shape
    past = cache_a_k.shape[2]
    assert nd <= CTX_ROW and DEPTH == 2
    cvec = jnp.concatenate([c, jnp.zeros((CTX_ROW - nd, D_MODEL), F32), c_ctx[None],
                            jnp.zeros((MOD_ROWS - CTX_ROW - 1, D_MODEL), F32)], axis=0)
    mod = _ada_mod(cvec, w_ada, b_ada)
    w1 = w_mlp1.astype(BF16)
    w2 = w_mlp2.astype(BF16)
    tm_proj, tm_post, tm_ctx = 512, 1024, 512
    tok = lambda a, tm: a.reshape(-1, tm, a.shape[-1])
    seq = lambda a: a.reshape(nb, n_seq, a.shape[-1])

    lam_init = 0.8 - 0.6 * math.exp(-0.3 * 0)
    w_in = _att_in_weights(att_w_in[0])
    w_out = att_w_out[0].astype(BF16)
    nqa = A_HEADS * HEAD_DIM
    c0 = [0, QA_W, QA_W + KA_W, QA_W + 2 * KA_W, QA_W + 2 * KA_W + B_W, QA_W + 2 * KA_W + 2 * B_W]
    widths = [QA_W, KA_W, KA_W, B_W, B_W, B_W]
    scales = [SCALE * LOG2E, 1.0, 1.0, SCALE * LOG2E, 1.0, 1.0]
    roped = [True, True, False, True, True, False]
    sink = att_sink[0]
    lam_qk = att_lam_qk[0]
    subln = att_subln[0].reshape(1, 2 * HEAD_DIM)

    rows = lambda dt: (("rows", dt),)
    cache_b = ((("heads", n_seq), F32), ("rows", BF16))
    segs_ctx = [_Seg(c0[0], QA_W, rows(BF16), scale=scales[0]), _Seg(c0[1], KA_W, rows(F32)),
                _Seg(c0[2], KA_W, rows(F32)), _Seg(c0[3], B_W, rows(BF16), scale=scales[3]),
                _Seg(c0[4], B_W, cache_b), _Seg(c0[5], B_W, cache_b)]
    qa, ka, va, qb, new_b_k, kb, new_b_v, vb = _proj(tok(x_prompt, tm_proj), mod[0], norm1[0], w_in, segs_ctx, tm=tm_proj,
                                                     ctx=True, name="proj_att_ctx")
    qa, ka, va, qb, kb, vb = map(seq, (qa, ka, va, qb, kb, vb))
    o_ctx = _ctx_attention(qa, ka, va, qb, kb, vb, sink, lam_qk, subln, lam_init)
    xp = _post(tok(x_prompt, tm_ctx), mod[0], norm2[0], [(tok(o_ctx, tm_ctx), w_out)], w1[0], w2[0], tm=tm_ctx, ctx=True,
               name="post_att_ctx")
    new_a_k = ka.reshape(nb, 1, n_seq, A_KV_HEADS, HEAD_DIM)
    new_a_v = va.reshape(nb, 1, n_seq, A_KV_HEADS, HEAD_DIM)

    segs_lat = [_Seg(c0[i], widths[i], (("cols" if i in (2, 3, 5) else "rows", BF16),), rope=roped[i],
                     scale=scales[i]) for i in range(6)]
    qa, ka, vat, qbt, kb, vbt = _proj(x_sample, mod[0], norm1[0], w_in, segs_lat, tm=tm_proj, ctx=False,
                                      rope_tabs=_rope_tables(d_seq), name="proj_att_lat")
    cka = cache_a_k[:, 0].reshape(nd, past, KA_W).astype(BF16)
    cvat = jnp.swapaxes(cache_a_v[:, 0].reshape(nd, past, KA_W), 1, 2).astype(BF16)
    ckb = cache_b_k[:, 0].reshape(nd, past, B_W).astype(BF16)
    cvbt = jnp.swapaxes(cache_b_v[:, 0].reshape(nd, past, B_W), 1, 2).astype(BF16)
    oa = _lat_a_attention(qa, ka, vat, cka, cvat, sink)
    ob = _lat_b_attention(qbt, kb, vbt, ckb, cvbt, lam_qk, subln, lam_init)
    xs = _post(x_sample, mod[0], norm2[0], [(oa, w_out[:nqa]), (ob, w_out[nqa:])], w1[0], w2[0], tm=tm_post,
               ctx=False, name="post_att_lat")

    w_rin = rec_w_in[0].astype(BF16)
    w_rout = rec_w_out[0].astype(BF16)
    w_bd, b_bd = _block_diag_weights(rec_w_a[0], rec_b_a[0], rec_w_x[0], rec_b_x[0])
    segs_rec = [_Seg(0, D_RNN, rows(F32), gelu=True), _Seg(D_RNN, D_RNN, rows(F32))]
    zeros = jnp.zeros((nb, 1, D_RNN), F32)

    gate, xr = map(seq, _proj(tok(xp, tm_proj), mod[1], norm1[1], w_rin, segs_rec, tm=tm_proj, ctx=True,
                              name="proj_rec_ctx"))
    y, sf, sb = _rec_mixer(gate, xr, zeros, zeros, rec_conv_w[0], rec_conv_b[0], w_bd, b_bd, rec_lam[0],
                           nblk=RNN_BLOCKS, tc=128, name="rec_mixer_ctx")
    y_prompt = seq(_post(xp, mod[1], norm2[1], [(tok(y, tm_ctx), w_rout)], w1[1], w2[1], tm=tm_ctx, ctx=True,
                         final_g=final_norm, name="post_rec_ctx"))

    gate, xr = _proj(xs, mod[1], norm1[1], w_rin, segs_rec, tm=tm_proj, ctx=False, name="proj_rec_lat")
    y, _, _ = _rec_mixer(gate, xr, state_fwd[:, 0:1], state_bwd[:, 0:1], rec_conv_w[0], rec_conv_b[0], w_bd, b_bd,
                         rec_lam[0], nblk=2, tc=256, name="rec_mixer_lat")
    y_sample = _post(xs, mod[1], norm2[1], [(y, w_rout)], w1[1], w2[1], tm=tm_post, ctx=False, final_g=final_norm,
                     name="post_rec_lat")

    return (y_prompt, y_sample, new_a_k, new_a_v, new_b_k, new_b_v, sf, sb)
```

```python
import functools
import math
from typing import NamedTuple

import jax
import jax.numpy as jnp
import numpy as np
from jax import lax
from jax.experimental import pallas as pl
from jax.experimental.pallas import tpu as pltpu

F32 = jnp.float32
BF16 = jnp.bfloat16

LANES = 128
SUBLANES = 8
VMEM_LIMIT_BYTES = 56 * 1024 * 1024

D_MODEL = 1024
DEPTH = 2
GRID_W = 64
HEAD_DIM = 64
A_HEADS = 8
A_KV_HEADS = 2
A_GROUP = A_HEADS // A_KV_HEADS
B_HEADS = 4
WINDOW = 128
ROPE_BASE = 10000.0
D_RNN = 1280
RNN_BLOCKS = 10
RNN_BW = D_RNN // RNN_BLOCKS
CONV_W = 4
CONV_LEFT = (CONV_W - 1) // 2
RGLRU_C = 8.0
D_FF = 4 * D_MODEL
EPS = 1e-6
SCALE = HEAD_DIM ** -0.5
NEG = -1e30

QA_W = A_HEADS * LANES
KA_W = A_KV_HEADS * HEAD_DIM
B_W = B_HEADS * 2 * HEAD_DIM
MOD_ROWS = 8
CTX_ROW = 4
LOG2E = math.log2(math.e)
ONES_ROWS = 16
PROJ_ROW_GROUPS = 2
SCAN_MAX_UNROLL = 1024


def _cparams(*semantics):
    return pltpu.CompilerParams(dimension_semantics=semantics, vmem_limit_bytes=VMEM_LIMIT_BYTES)


def _dot(a, b):
    return jnp.dot(a, b, preferred_element_type=F32)


def _dot_nt(a, b):
    return lax.dot_general(a, b, (((1,), (1,)), ((), ())), preferred_element_type=F32)


def _rms(x, g):
    return x * lax.rsqrt(jnp.mean(x * x, axis=-1, keepdims=True) + EPS) * g


def _gelu_tanh(x):
    return x * (0.5 * (1.0 + jnp.tanh(math.sqrt(2.0 / math.pi) * (x + 0.044715 * (x * x * x)))))


def _ada_kernel(c_ref, w_ref, b_ref, o_ref):
    c = c_ref[...]
    s = c * jax.nn.sigmoid(c)
    o_ref[...] = _dot(s.astype(BF16), w_ref[...].astype(BF16)) + b_ref[...]


def _ada_mod(cvec, w_ada, b_ada):
    tn = 1536
    out = pl.pallas_call(
        _ada_kernel,
        grid=(DEPTH, 6 * D_MODEL // tn),
        in_specs=[
            pl.BlockSpec((MOD_ROWS, D_MODEL), lambda l, j: (0, 0)),
            pl.BlockSpec((None, D_MODEL, tn), lambda l, j: (l, 0, j)),
            pl.BlockSpec((None, 1, tn), lambda l, j: (l, 0, j)),
        ],
        out_specs=pl.BlockSpec((None, MOD_ROWS, tn), lambda l, j: (l, 0, j)),
        out_shape=jax.ShapeDtypeStruct((DEPTH, MOD_ROWS, 6 * D_MODEL), F32),
        compiler_params=_cparams("parallel", "parallel"),
        name="ada_mod",
    )(cvec, w_ada, b_ada.reshape(DEPTH, 1, 6 * D_MODEL))
    return out.reshape(DEPTH, MOD_ROWS, 6, D_MODEL)


class _Seg(NamedTuple):
    col0: int
    width: int
    outs: tuple
    rope: bool = False
    scale: float = 1.0
    gelu: bool = False


def _proj_kernel(*refs, segs, rope):
    if rope:
        x_ref, mod_ref, g_ref, w_ref, cos_ref, sin_ref, *outs = refs
    else:
        x_ref, mod_ref, g_ref, w_ref, *outs = refs
    tm = x_ref.shape[0]
    rows = tm // PROJ_ROW_GROUPS
    groups = [slice(i * rows, (i + 1) * rows) for i in range(PROJ_ROW_GROUPS)]
    hbs = []
    for grp in groups:
        h = _rms(x_ref[grp, :], g_ref[...])
        hbs.append((h * (1.0 + mod_ref[1:2, :]) + mod_ref[0:1, :]).astype(BF16))
    if rope:
        first = (lax.broadcasted_iota(jnp.int32, (rows, LANES), 1) & 16) == 0
    outs = iter(outs)
    for seg in segs:
        ys = [_dot(hb, w_ref[:, seg.col0:seg.col0 + seg.width]) for hb in hbs]
        o_refs = [next(outs) for _ in seg.outs]
        for grp, y in zip(groups, ys):
            for t in range(seg.width // LANES):
                lanes = slice(t * LANES, (t + 1) * LANES)
                yt = y[:, lanes]
                if seg.rope:
                    sw = jnp.where(first, pltpu.roll(yt, LANES - 16, 1), pltpu.roll(yt, 16, 1))
                    yt = yt * cos_ref[grp, :] + sw * sin_ref[grp, :]
                if seg.scale != 1.0:
                    yt = yt * seg.scale
                if seg.gelu:
                    yt = _gelu_tanh(yt)
                for (layout, _), o_ref in zip(seg.outs, o_refs):
                    if layout == "cols":
                        o_ref[lanes, grp] = yt.T.astype(o_ref.dtype)
                    elif layout == "rows":
                        o_ref[grp, lanes] = yt.astype(o_ref.dtype)
                    else:
                        kind, seq_len = layout
                        assert rows % seq_len == 0
                        for r in range(rows // seq_len):
                            s = grp.start // seq_len + r
                            ys = yt[r * seq_len:(r + 1) * seq_len]
                            if kind == "heads":
                                o_ref[s, :, t, :] = ys.astype(o_ref.dtype)
                            else:
                                o_ref[s, lanes, :] = ys.T.astype(o_ref.dtype)


def _proj(x, mod, g, w, segs, *, tm, ctx, rope_tabs=None, name):
    B, T, D = x.shape

    def out_spec(width, layout):
        if layout == "cols":
            return pl.BlockSpec((None, width, tm), lambda b, i: (b, 0, i))
        if layout == "rows":
            return pl.BlockSpec((None, tm, width), lambda b, i: (b, i, 0))
        if layout[0] == "heads":
            return pl.BlockSpec((tm // layout[1], None, layout[1], width // LANES, LANES),
                                lambda b, i: (b * (T // tm) + i, 0, 0, 0, 0))
        return pl.BlockSpec((tm // layout[1], width, layout[1]), lambda b, i: (b * (T // tm) + i, 0, 0))

    def out_struct(width, layout, dt):
        if layout == "cols":
            return jax.ShapeDtypeStruct((B, width, T), dt)
        if layout == "rows":
            return jax.ShapeDtypeStruct((B, T, width), dt)
        if layout[0] == "heads":
            return jax.ShapeDtypeStruct((B * T // layout[1], 1, layout[1], width // LANES, LANES), dt)
        return jax.ShapeDtypeStruct((B * T // layout[1], width, layout[1]), dt)

    row = (lambda b, i: (CTX_ROW, 0, 0)) if ctx else (lambda b, i: (b, 0, 0))
    in_specs = [
        pl.BlockSpec((None, tm, D), lambda b, i: (b, i, 0)),
        pl.BlockSpec((None, 6, D), row),
        pl.BlockSpec((1, D), lambda b, i: (0, 0)),
        pl.BlockSpec(w.shape, lambda b, i: (0, 0)),
    ]
    args = [x, mod, g.reshape(1, D), w]
    if rope_tabs is not None:
        in_specs += [pl.BlockSpec((tm, LANES), lambda b, i: (i, 0))] * 2
        args += list(rope_tabs)
    out_specs = [out_spec(s.width, layout) for s in segs for layout, _ in s.outs]
    out_shape = [out_struct(s.width, layout, dt) for s in segs for layout, dt in s.outs]
    return pl.pallas_call(
        functools.partial(_proj_kernel, segs=tuple(segs), rope=rope_tabs is not None),
        grid=(B, T // tm),
        in_specs=in_specs,
        out_specs=out_specs,
        out_shape=out_shape,
        compiler_params=_cparams("parallel", "parallel"),
        name=name,
    )(*args)


def _diff_lambda(lq, lam_init):
    s1 = jnp.sum(lq[0:1, :] * lq[1:2, :], axis=1, keepdims=True)
    s2 = jnp.sum(lq[2:3, :] * lq[3:4, :], axis=1, keepdims=True)
    return jnp.exp(s1) - jnp.exp(s2) + lam_init


def _stack_group_queries(qa_ref, g, rows):
    return jnp.concatenate(
        [qa_ref[rows, (A_GROUP * g + hh) * LANES:(A_GROUP * g + hh + 1) * LANES] for hh in range(A_GROUP)], axis=0)


def _stack_pair_queries(q):
    lo = lax.broadcasted_iota(jnp.int32, q.shape, 1) < HEAD_DIM
    zero = jnp.zeros_like(q)
    return jnp.concatenate([jnp.where(lo, q, zero), jnp.where(lo, zero, q)], axis=0)


def _with_ones_rows(vt):
    return jnp.concatenate([vt, jnp.ones((ONES_ROWS, vt.shape[1]), BF16)], axis=0)


def _sink_row(sink_ref, g, tq):
    return jnp.concatenate(
        [jnp.full((1, tq), sink_ref[A_GROUP * g + hh] * LOG2E, F32) for hh in range(A_GROUP)], axis=1)


def _softmax_values(parts, sink=None):
    m = functools.reduce(jnp.maximum, [jnp.max(s, axis=0, keepdims=True) for s, _ in parts])
    if sink is not None:
        m = jnp.maximum(m, sink)
    ot = sum(_dot(vt1, jnp.exp2(s - m).astype(BF16)) for s, vt1 in parts)
    den = ot[LANES:LANES + 1]
    if sink is not None:
        den = den + jnp.exp2(sink - m)
    return ot[:LANES] / den


def _a_heads(ot, g, tq):
    return [ot[g * HEAD_DIM:(g + 1) * HEAD_DIM, hh * tq:(hh + 1) * tq] for hh in range(A_GROUP)]


def _subln(o, subln, lam_init):
    return _rms(o, subln) * (1.0 - lam_init)


def _diff_combine(ot, lam, subln, lam_init):
    tq = ot.shape[1] // 2
    return _subln((ot[:, :tq] - lam * ot[:, tq:]).T, subln, lam_init)


def _ctx_attn_kernel(sink_ref, qa_ref, ka_ref, va_ref, qb_ref, kb_ref, vb_ref, lamqk_ref, subln_ref, o_ref, *,
                     lam_init):
    T = qa_ref.shape[0]
    ka = ka_ref[...].astype(BF16)
    vat1 = _with_ones_rows(va_ref[...].T.astype(BF16))
    scores_a = [_dot_nt(ka, _stack_group_queries(qa_ref, g, slice(None))) for g in range(A_KV_HEADS)]
    scores_b, vbt1 = [], []
    for h in range(B_HEADS):
        sl = slice(h * LANES, (h + 1) * LANES)
        scores_b.append(_dot_nt(kb_ref[:, sl].astype(BF16), _stack_pair_queries(qb_ref[:, sl])))
        vbt1.append(_with_ones_rows(vb_ref[:, sl].T.astype(BF16)))
    heads = []
    for g in range(A_KV_HEADS):
        heads += _a_heads(_softmax_values([(scores_a[g], vat1)], _sink_row(sink_ref, g, T)), g, T)
    nqa = A_HEADS * HEAD_DIM
    o_ref[:, :nqa] = jnp.concatenate(heads, axis=0).T.astype(o_ref.dtype)
    lam = _diff_lambda(lamqk_ref[...], lam_init)
    for h in range(B_HEADS):
        ot = _softmax_values([(scores_b[h], vbt1[h])])
        o_ref[:, nqa + h * LANES:nqa + (h + 1) * LANES] = (
            _diff_combine(ot, lam, subln_ref[...], lam_init).astype(o_ref.dtype))


def _ctx_attention(qa, ka, va, qb, kb, vb, sink, lam_qk, subln, lam_init):
    B, T, _ = qa.shape
    blk = lambda w: pl.BlockSpec((None, T, w), lambda b: (b, 0, 0))
    full = lambda a: pl.BlockSpec(a.shape, lambda b: (0,) * a.ndim)
    return pl.pallas_call(
        functools.partial(_ctx_attn_kernel, lam_init=lam_init),
        grid=(B,),
        in_specs=[pl.BlockSpec(memory_space=pltpu.SMEM), blk(QA_W), blk(KA_W), blk(KA_W), blk(B_W), blk(B_W),
                  blk(B_W), full(lam_qk), full(subln)],
        out_specs=blk(A_HEADS * HEAD_DIM + B_W),
        out_shape=jax.ShapeDtypeStruct((B, T, A_HEADS * HEAD_DIM + B_W), BF16),
        compiler_params=_cparams("parallel"),
        name="ctx_attention",
    )(sink, qa, ka, va, qb, kb, vb, lam_qk, subln)


def _lat_a_kernel(sink_ref, qa_ref, k_ref, vt_ref, ck_ref, cvt_ref, o_ref, *, tq, nq, band):
    T = k_ref.shape[0]
    cols = A_GROUP * tq
    cvt1 = _with_ones_rows(cvt_ref[...])
    jobs = []
    for i in range(nq):
        qi = pl.program_id(1) * nq + i
        start = pl.multiple_of(jnp.clip(qi * tq - WINDOW, 0, T - band), WINDOW)
        kb = k_ref[pl.ds(start, band), :]
        vbt1 = _with_ones_rows(vt_ref[:, pl.ds(start, band)])
        kpos = start + lax.broadcasted_iota(jnp.int32, (band, cols), 0)
        qpos = qi * tq + lax.broadcasted_iota(jnp.int32, (band, cols), 1) % tq
        keep = jnp.abs(qpos - kpos) <= WINDOW
        for g in range(A_KV_HEADS):
            qg = _stack_group_queries(qa_ref, g, slice(i * tq, (i + 1) * tq))
            jobs.append((_dot_nt(ck_ref[...], qg), _dot_nt(kb, qg), keep, vbt1))
    for i in range(nq):
        heads = []
        for g in range(A_KV_HEADS):
            s_c, s_b, keep, vbt1 = jobs[i * A_KV_HEADS + g]
            ot = _softmax_values([(s_c, cvt1), (jnp.where(keep, s_b, NEG), vbt1)], _sink_row(sink_ref, g, tq))
            heads += _a_heads(ot, g, tq)
        o_ref[i * tq:(i + 1) * tq, :] = jnp.concatenate(heads, axis=0).T.astype(o_ref.dtype)


def _lat_a_attention(qa, ka, vat, cka, cvat, sink):
    B, T, _ = qa.shape
    L = cka.shape[1]
    tq, nq = WINDOW, 8
    band = 3 * WINDOW
    return pl.pallas_call(
        functools.partial(_lat_a_kernel, tq=tq, nq=nq, band=band),
        grid=(B, T // (tq * nq)),
        in_specs=[
            pl.BlockSpec(memory_space=pltpu.SMEM),
            pl.BlockSpec((None, tq * nq, QA_W), lambda b, i: (b, i, 0)),
            pl.BlockSpec((None, T, KA_W), lambda b, i: (b, 0, 0)),
            pl.BlockSpec((None, KA_W, T), lambda b, i: (b, 0, 0)),
            pl.BlockSpec((None, L, KA_W), lambda b, i: (b, 0, 0)),
            pl.BlockSpec((None, KA_W, L), lambda b, i: (b, 0, 0)),
        ],
        out_specs=pl.BlockSpec((None, tq * nq, A_HEADS * HEAD_DIM), lambda b, i: (b, i, 0)),
        out_shape=jax.ShapeDtypeStruct((B, T, A_HEADS * HEAD_DIM), BF16),
        compiler_params=_cparams("parallel", "parallel"),
        name="lat_a_attention",
    )(sink, qa, ka, vat, cka, cvat)


def _lat_b_kernel(qt_ref, k_ref, vt_ref, ck_ref, cvt_ref, lamqk_ref, subln_ref, o_ref, *, tk, ahead, lam_init):
    tq = qt_ref.shape[1]
    T = k_ref.shape[0]
    qt = qt_ref[...]
    zero = jnp.zeros((HEAD_DIM, tq), BF16)
    qst = jnp.concatenate([jnp.concatenate([qt[:HEAD_DIM], zero], axis=0),
                           jnp.concatenate([zero, qt[HEAD_DIM:]], axis=0)], axis=1)
    blocks = [(ck_ref[...], cvt_ref[...])]
    blocks += [(k_ref[j * tk:(j + 1) * tk, :], vt_ref[:, j * tk:(j + 1) * tk]) for j in range(T // tk)]

    def scores(k):
        return _dot(k, qst)

    def accumulate(s, vt, carry):
        m, acc = carry
        m_new = jnp.maximum(m, jnp.max(s, axis=0, keepdims=True))
        p = jnp.exp2(s - m_new).astype(BF16)
        acc = jnp.exp2(m - m_new) * acc + _dot(_with_ones_rows(vt), p)
        return m_new, acc

    carry = (jnp.full((1, 2 * tq), -jnp.inf, F32), jnp.zeros((LANES + ONES_ROWS, 2 * tq), F32))
    pending = [scores(blocks[j][0]) for j in range(ahead)]
    for j in range(len(blocks)):
        if j + ahead < len(blocks):
            pending.append(scores(blocks[j + ahead][0]))
        carry = accumulate(pending.pop(0), blocks[j][1], carry)
    _, acc = carry
    ot = acc[:LANES] / acc[LANES:LANES + 1]
    lam = _diff_lambda(lamqk_ref[...], lam_init)
    o_ref[...] = _diff_combine(ot, lam, subln_ref[...], lam_init).astype(o_ref.dtype)


def _lat_b_attention(qbt, kb, vbt, ckb, cvbt, lam_qk, subln, lam_init):
    B, T, _ = kb.shape
    L = ckb.shape[1]
    tq = 1024
    full = lambda a: pl.BlockSpec(a.shape, lambda b, h, i: (0,) * a.ndim)
    return pl.pallas_call(
        functools.partial(_lat_b_kernel, tk=512, ahead=2, lam_init=lam_init),
        grid=(B, B_HEADS, T // tq),
        in_specs=[
            pl.BlockSpec((None, LANES, tq), lambda b, h, i: (b, h, i)),
            pl.BlockSpec((None, T, LANES), lambda b, h, i: (b, 0, h)),
            pl.BlockSpec((None, LANES, T), lambda b, h, i: (b, h, 0)),
            pl.BlockSpec((None, L, LANES), lambda b, h, i: (b, 0, h)),
            pl.BlockSpec((None, LANES, L), lambda b, h, i: (b, h, 0)),
            full(lam_qk), full(subln),
        ],
        out_specs=pl.BlockSpec((None, tq, LANES), lambda b, h, i: (b, i, h)),
        out_shape=jax.ShapeDtypeStruct((B, T, B_W), BF16),
        compiler_params=_cparams("parallel", "parallel", "parallel"),
        name="lat_b_attention",
    )(qbt, kb, vbt, ckb, cvbt, lam_qk, subln)


def _scan_chunk_len(n_steps):
    chunk = -(-n_steps // SUBLANES)
    while chunk % 8 != 4:
        chunk += 1
    return chunk


class _RecBlock(NamedTuple):
    gate: object
    xr: object
    cw: object
    cb: object
    wbd: object
    bbd: object
    lam: object
    h0f: object
    h0b: object
    y: object
    sf: object
    sb: object
    xp: object
    af: object
    uf: object
    ab: object
    ub: object
    pf: object
    hf: object
    pb: object
    hb: object


def _rec_kernel(gate_ref, xr_ref, cw_ref, cb_ref, wbd_ref, bbd_ref, lam_ref, h0f_ref, h0b_ref,
                y_ref, sf_ref, sb_ref, *scratch, nblk, tc, gates_unroll, chunk, unroll):
    blocks = []
    for n in range(nblk):
        sl = slice(n * LANES, (n + 1) * LANES)
        lanes = [r.at[:, sl] for r in (gate_ref, xr_ref, cw_ref, cb_ref)]
        lanes += [wbd_ref.at[n], bbd_ref.at[n]]
        lanes += [r.at[:, sl] for r in (lam_ref, h0f_ref, h0b_ref, y_ref, sf_ref, sb_ref)]
        xp, af, uf, ab, ub = [r.at[n] for r in scratch]
        blocks.append(_RecBlock(*lanes, xp, af, uf, ab, ub, pf=af, hf=uf, pb=ab, hb=ub))
    for blk in blocks:
        _rec_gates(blk, tc=tc, gates_unroll=gates_unroll, chunk=chunk)
    _rec_scan(blocks, chunk=chunk, unroll=unroll)
    for blk in blocks:
        _rec_combine(blk, tc=tc)


def _rec_gates(blk, *, tc, gates_unroll, chunk):
    T = blk.xr.shape[0]
    pad = SUBLANES
    xp = blk.xp
    xp[0:pad, :] = jnp.zeros((pad, LANES), F32)
    xp[T + pad:T + 2 * pad, :] = jnp.zeros((pad, LANES), F32)
    xp[pad:T + pad, :] = blk.xr[...]
    tail = SUBLANES * chunk - T
    for a_s, u_s in ((blk.af, blk.uf), (blk.ab, blk.ub)):
        a_s[T:T + tail, :] = jnp.ones((tail, LANES), F32)
        u_s[T:T + tail, :] = jnp.zeros((tail, LANES), F32)
    cw = blk.cw[...]
    cb = blk.cb[...]
    nl = -blk.lam[...]
    softplus = jnp.maximum(nl, 0.0) + jnp.log1p(jnp.exp(-jnp.abs(nl)))
    cp = (0.5 * RGLRU_C) * softplus

    def gates(it, _):
        for q in range(gates_unroll):
            gate_chunk(it * gates_unroll + q)
        return 0

    def gate_chunk(ci):
        t0 = pl.multiple_of(ci * tc, tc)
        y = cb
        for j in range(CONV_W):
            y = y + xp[pl.ds(t0 + (pad - CONV_LEFT + j), tc), :] * cw[j:j + 1, :]
        t = jnp.tanh(_dot(y.astype(BF16), blk.wbd[...]) + blk.bbd[...])
        yh = 0.5 * y
        for d, (a_s, u_s) in enumerate(((blk.af, blk.uf), (blk.ab, blk.ub))):
            t_r = t[:, 2 * d * LANES:(2 * d + 1) * LANES]
            t_i = t[:, (2 * d + 1) * LANES:(2 * d + 2) * LANES]
            neg_log_a = cp[d:d + 1, :] * t_r + cp[d:d + 1, :]
            a = jnp.exp2(neg_log_a * (-LOG2E))
            w = jnp.tanh(neg_log_a) * (a * a + 1.0)
            sqrt_w = jnp.where(w > 0.0, w * lax.rsqrt(w), 0.0)
            a_s[pl.ds(t0, tc), :] = a
            u_s[pl.ds(t0, tc), :] = (t_i + 1.0) * (yh * sqrt_w)

    lax.fori_loop(0, T // (tc * gates_unroll), gates, 0)


def _rec_scan(blocks, *, chunk, unroll):
    T = blocks[0].xr.shape[0]

    def rows(k):
        return pl.ds(k, SUBLANES, stride=chunk)

    def local_scan(it, carry):
        carry = list(carry)
        for q in range(unroll):
            k = it * unroll + q
            kb = chunk - 1 - k
            for i, blk in enumerate(blocks):
                hf, pf, hb, pb = carry[4 * i:4 * i + 4]
                a = blk.af[rows(k), :]
                hf = a * hf + blk.uf[rows(k), :]
                pf = a * pf
                blk.hf[rows(k), :] = hf
                blk.pf[rows(k), :] = pf
                a = blk.ab[rows(kb), :]
                hb = a * hb + blk.ub[rows(kb), :]
                pb = a * pb
                blk.hb[rows(kb), :] = hb
                blk.pb[rows(kb), :] = pb
                carry[4 * i:4 * i + 4] = [hf, pf, hb, pb]
        return tuple(carry)

    zero = jnp.zeros((SUBLANES, LANES), F32)
    one = jnp.ones((SUBLANES, LANES), F32)
    ends = lax.fori_loop(0, chunk // unroll, local_scan, (zero, one, zero, one) * len(blocks))

    carries = []
    for i, blk in enumerate(blocks):
        hf, pf, hb, pb = ends[4 * i:4 * i + 4]
        cf = [blk.h0f[...]]
        for r in range(SUBLANES - 1):
            cf.append(hf[r:r + 1, :] + pf[r:r + 1, :] * cf[r])
        cb_rev = [blk.h0b[...]]
        for r in range(SUBLANES - 1, 0, -1):
            cb_rev.append(hb[r:r + 1, :] + pb[r:r + 1, :] * cb_rev[-1])
        carries.append((jnp.concatenate(cf, axis=0), jnp.concatenate(cb_rev[::-1], axis=0)))

    def apply_carry(it, _):
        for q in range(unroll):
            k = it * unroll + q
            for blk, (carry_f, carry_b) in zip(blocks, carries):
                blk.uf[rows(k), :] = blk.hf[rows(k), :] + blk.pf[rows(k), :] * carry_f
                blk.ub[rows(k), :] = blk.hb[rows(k), :] + blk.pb[rows(k), :] * carry_b
        return 0

    lax.fori_loop(0, chunk // unroll, apply_carry, 0)
    for blk in blocks:
        blk.sf[...] = blk.uf[T - 1:T, :]
        blk.sb[...] = blk.ub[0:1, :]


def _rec_combine(blk, *, tc):
    T = blk.xr.shape[0]

    def combine(ci, _):
        t0 = pl.multiple_of(ci * tc, tc)
        blk.y[pl.ds(t0, tc), :] = ((blk.uf[pl.ds(t0, tc), :] + blk.ub[pl.ds(t0, tc), :])
                                   * blk.gate[pl.ds(t0, tc), :]).astype(blk.y.dtype)
        return 0

    lax.fori_loop(0, T // tc, combine, 0)


def _rec_mixer(gate, xr, h0f, h0b, conv_w, conv_b, w_bd, b_bd, lam, *, nblk, tc, name):
    B, T, _ = xr.shape
    chunk = _scan_chunk_len(T)
    wid = nblk * LANES
    col = lambda rows: pl.BlockSpec((rows, wid), lambda b, n: (0, n))
    seq = pl.BlockSpec((None, T, wid), lambda b, n: (b, 0, n))
    st = pl.BlockSpec((None, 1, wid), lambda b, n: (b, 0, n))
    return pl.pallas_call(
        functools.partial(_rec_kernel, nblk=nblk, tc=tc, gates_unroll=2, chunk=chunk,
                          unroll=max(u for u in range(1, SCAN_MAX_UNROLL + 1) if chunk % u == 0)),
        grid=(B, RNN_BLOCKS // nblk),
        in_specs=[seq, seq, col(CONV_W), col(1),
                  pl.BlockSpec((nblk, LANES, 4 * LANES), lambda b, n: (n, 0, 0)),
                  pl.BlockSpec((nblk, 1, 4 * LANES), lambda b, n: (n, 0, 0)),
                  col(2), st, st],
        out_specs=[seq, st, st],
        out_shape=[jax.ShapeDtypeStruct((B, T, D_RNN), BF16),
                   jax.ShapeDtypeStruct((B, 1, D_RNN), F32),
                   jax.ShapeDtypeStruct((B, 1, D_RNN), F32)],
        scratch_shapes=[pltpu.VMEM((nblk, T + 2 * SUBLANES, LANES), F32)]
        + [pltpu.VMEM((nblk, SUBLANES * chunk, LANES), F32)] * 4,
        compiler_params=_cparams("parallel", "parallel"),
        name=name,
    )(gate, xr, conv_w, conv_b.reshape(1, D_RNN), w_bd, b_bd, lam, h0f, h0b)


def _post_kernel(*refs, n_mix, final, fc):
    x_ref, mod_ref, g2_ref = refs[:3]
    mix = refs[3:3 + 2 * n_mix]
    w1_ref, w2_ref = refs[3 + 2 * n_mix:5 + 2 * n_mix]
    rest = refs[5 + 2 * n_mix:]
    if final:
        gf_ref, o_ref = rest
    else:
        (o_ref,) = rest
    mixed = _dot(mix[0][...], mix[1][...])
    for i in range(1, n_mix):
        mixed = mixed + _dot(mix[2 * i][...], mix[2 * i + 1][...])
    x1 = x_ref[...] + mod_ref[2:3, :] * mixed
    h = _rms(x1, g2_ref[...])
    hb = (h * (1.0 + mod_ref[4:5, :]) + mod_ref[3:4, :]).astype(BF16)
    n_chunks = D_FF // fc
    up = lambda c: _dot(hb, w1_ref[:, c * fc:(c + 1) * fc])
    acc = None
    a = up(0)
    for c in range(n_chunks):
        a_next = up(c + 1) if c + 1 < n_chunks else None
        part = _dot(jnp.square(jnp.maximum(a, 0.0)).astype(BF16), w2_ref[c * fc:(c + 1) * fc, :])
        acc = part if acc is None else acc + part
        a = a_next
    x2 = x1 + mod_ref[5:6, :] * acc
    if final:
        x2 = _rms(x2, gf_ref[...])
    o_ref[...] = x2


def _post(x, mod, g2, mixes, w1, w2, *, tm, ctx, final_g=None, name):
    B, T, D = x.shape
    row = (lambda b, i: (CTX_ROW, 0, 0)) if ctx else (lambda b, i: (b, 0, 0))
    const = lambda a: pl.BlockSpec(a.shape, lambda b, i: (0,) * a.ndim, pipeline_mode=pl.Buffered(1))
    in_specs = [
        pl.BlockSpec((None, tm, D), lambda b, i: (b, i, 0)),
        pl.BlockSpec((None, 6, D), row),
        pl.BlockSpec((1, D), lambda b, i: (0, 0)),
    ]
    args = [x, mod, g2.reshape(1, D)]
    for o, w in mixes:
        in_specs += [pl.BlockSpec((None, tm, o.shape[-1]), lambda b, i: (b, i, 0)), const(w)]
        args += [o, w]
    in_specs += [const(w1), const(w2)]
    args += [w1, w2]
    if final_g is not None:
        in_specs.append(pl.BlockSpec((1, D), lambda b, i: (0, 0)))
        args.append(final_g.reshape(1, D))
    return pl.pallas_call(
        functools.partial(_post_kernel, n_mix=len(mixes), final=final_g is not None, fc=1024),
        grid=(B, T // tm),
        in_specs=in_specs,
        out_specs=pl.BlockSpec((None, tm, D), lambda b, i: (b, i, 0)),
        out_shape=jax.ShapeDtypeStruct((B, T, D), F32),
        compiler_params=_cparams("parallel", "parallel"),
        name=name,
    )(*args)


def _rope_tables(n_tokens):
    rows = n_tokens // GRID_W
    r, cl = jnp.meshgrid(jnp.arange(rows, dtype=F32), jnp.arange(GRID_W, dtype=F32), indexing='ij')
    quarter = HEAD_DIM // 4
    inv = ROPE_BASE ** (-jnp.arange(quarter, dtype=F32) / quarter)
    ang = jnp.stack([r.reshape(-1)[:, None] * inv, cl.reshape(-1)[:, None] * inv], axis=1)
    cos, sin = jnp.cos(ang), jnp.sin(ang)
    cos64 = jnp.concatenate([cos[:, 0], cos[:, 0], cos[:, 1], cos[:, 1]], axis=-1)
    sin64 = jnp.concatenate([-sin[:, 0], sin[:, 0], -sin[:, 1], sin[:, 1]], axis=-1)
    return jnp.tile(cos64, (1, LANES // HEAD_DIM)), jnp.tile(sin64, (1, LANES // HEAD_DIM))


def _att_in_weights(w_in):
    d = w_in.shape[0]
    nq = A_HEADS * HEAD_DIM
    wq = w_in[:, :nq].reshape(d, A_HEADS, HEAD_DIM)
    z = jnp.zeros_like(wq)
    in_first = (jnp.arange(A_HEADS) // A_GROUP == 0)[None, :, None]
    wq = jnp.where(in_first, jnp.concatenate([wq, z], axis=-1), jnp.concatenate([z, wq], axis=-1))
    return jnp.concatenate([wq.reshape(d, QA_W), w_in[:, nq:]], axis=1).astype(BF16)


def _block_diag_weights(w_a, b_a, w_x, b_x):
    w = jnp.concatenate([w_a[0], w_x[0], w_a[1], w_x[1]], axis=-1)
    b = jnp.concatenate([v.reshape(RNN_BLOCKS, 1, RNN_BW) for v in (b_a[0], b_x[0], b_a[1], b_x[1])], axis=-1)
    return (0.5 * w).astype(BF16), 0.5 * b


def kernel(x_prompt, x_sample, cache_a_k, cache_a_v, cache_b_k, cache_b_v, state_fwd, state_bwd, c, c_ctx, norm1, norm2, w_ada, b_ada, w_mlp1, w_mlp2, att_w_in, att_w_out, att_sink, att_lam_qk, att_subln, rec_w_in, rec_conv_w, rec_conv_b, rec_w_a, rec_b_a, rec_w_x, rec_b_x, rec_lam, rec_w_out, final_norm):
    nb, n_seq, _ = x_prompt.shape
    nd, d_seq, _ = x_sample.shape
    past = cache_a_k.shape[2]
    assert nd <= CTX_ROW and DEPTH == 2
    cvec = jnp.concatenate([c, jnp.zeros((CTX_ROW - nd, D_MODEL), F32), c_ctx[None],
                            jnp.zeros((MOD_ROWS - CTX_ROW - 1, D_MODEL), F32)], axis=0)
    mod = _ada_mod(cvec, w_ada, b_ada)
    w1 = w_mlp1.astype(BF16)
    w2 = w_mlp2.astype(BF16)
    tm_proj, tm_post, tm_ctx = 512, 1024, 512
    tok = lambda a, tm: a.reshape(-1, tm, a.shape[-1])
    seq = lambda a: a.reshape(nb, n_seq, a.shape[-1])

    lam_init = 0.8 - 0.6 * math.exp(-0.3 * 0)
    w_in = _att_in_weights(att_w_in[0])
    w_out = att_w_out[0].astype(BF16)
    nqa = A_HEADS * HEAD_DIM
    c0 = [0, QA_W, QA_W + KA_W, QA_W + 2 * KA_W, QA_W + 2 * KA_W + B_W, QA_W + 2 * KA_W + 2 * B_W]
    widths = [QA_W, KA_W, KA_W, B_W, B_W, B_W]
    scales = [SCALE * LOG2E, 1.0, 1.0, SCALE * LOG2E, 1.0, 1.0]
    roped = [True, True, False, True, True, False]
    sink = att_sink[0]
    lam_qk = att_lam_qk[0]
    subln = att_subln[0].reshape(1, 2 * HEAD_DIM)

    rows = lambda dt: (("rows", dt),)
    cache_a = ((("seqcols", n_seq), F32), ("rows", BF16))
    cache_b = ((("heads", n_seq), F32), ("rows", BF16))
    segs_ctx = [_Seg(c0[0], QA_W, rows(BF16), scale=scales[0]), _Seg(c0[1], KA_W, cache_a),
                _Seg(c0[2], KA_W, cache_a), _Seg(c0[3], B_W, rows(BF16), scale=scales[3]),
                _Seg(c0[4], B_W, cache_b), _Seg(c0[5], B_W, cache_b)]
    qa, kat, ka, vat, va, qb, new_b_k, kb, new_b_v, vb = _proj(tok(x_prompt, tm_proj), mod[0], norm1[0], w_in, segs_ctx,
                                                               tm=tm_proj, ctx=True, name="proj_att_ctx")
    qa, ka, va, qb, kb, vb = map(seq, (qa, ka, va, qb, kb, vb))
    uncol = lambda a: jnp.transpose(a.reshape(nb, A_KV_HEADS, HEAD_DIM, n_seq), (0, 3, 1, 2))[:, None]
    new_a_k, new_a_v = uncol(kat), uncol(vat)
    o_ctx = _ctx_attention(qa, ka, va, qb, kb, vb, sink, lam_qk, subln, lam_init)
    xp = _post(tok(x_prompt, tm_ctx), mod[0], norm2[0], [(tok(o_ctx, tm_ctx), w_out)], w1[0], w2[0], tm=tm_ctx, ctx=True,
               name="post_att_ctx")

    segs_lat = [_Seg(c0[i], widths[i], (("cols" if i in (2, 3, 5) else "rows", BF16),), rope=roped[i],
                     scale=scales[i]) for i in range(6)]
    qa, ka, vat, qbt, kb, vbt = _proj(x_sample, mod[0], norm1[0], w_in, segs_lat, tm=tm_proj, ctx=False,
                                      rope_tabs=_rope_tables(d_seq), name="proj_att_lat")
    cka = cache_a_k[:, 0].reshape(nd, past, KA_W).astype(BF16)
    cvat = jnp.swapaxes(cache_a_v[:, 0].reshape(nd, past, KA_W), 1, 2).astype(BF16)
    ckb = cache_b_k[:, 0].reshape(nd, past, B_W).astype(BF16)
    cvbt = jnp.swapaxes(cache_b_v[:, 0].reshape(nd, past, B_W), 1, 2).astype(BF16)
    oa = _lat_a_attention(qa, ka, vat, cka, cvat, sink)
    ob = _lat_b_attention(qbt, kb, vbt, ckb, cvbt, lam_qk, subln, lam_init)
    xs = _post(x_sample, mod[0], norm2[0], [(oa, w_out[:nqa]), (ob, w_out[nqa:])], w1[0], w2[0], tm=tm_post,
               ctx=False, name="post_att_lat")

    w_rin = rec_w_in[0].astype(BF16)
    w_rout = rec_w_out[0].astype(BF16)
    w_bd, b_bd = _block_diag_weights(rec_w_a[0], rec_b_a[0], rec_w_x[0], rec_b_x[0])
    segs_rec = [_Seg(0, D_RNN, rows(F32), gelu=True), _Seg(D_RNN, D_RNN, rows(F32))]
    zeros = jnp.zeros((nb, 1, D_RNN), F32)

    gate, xr = map(seq, _proj(tok(xp, tm_proj), mod[1], norm1[1], w_rin, segs_rec, tm=tm_proj, ctx=True,
                              name="proj_rec_ctx"))
    y, sf, sb = _rec_mixer(gate, xr, zeros, zeros, rec_conv_w[0], rec_conv_b[0], w_bd, b_bd, rec_lam[0],
                           nblk=RNN_BLOCKS, tc=128, name="rec_mixer_ctx")
    y_prompt = seq(_post(xp, mod[1], norm2[1], [(tok(y, tm_ctx), w_rout)], w1[1], w2[1], tm=tm_ctx, ctx=True,
                         final_g=final_norm, name="post_rec_ctx"))

    gate, xr = _proj(xs, mod[1], norm1[1], w_rin, segs_rec, tm=tm_proj, ctx=False, name="proj_rec_lat")
    y, _, _ = _rec_mixer(gate, xr, state_fwd[:, 0:1], state_bwd[:, 0:1], rec_conv_w[0], rec_conv_b[0], w_bd, b_bd,
                         rec_lam[0], nblk=2, tc=256, name="rec_mixer_lat")
    y_sample = _post(xs, mod[1], norm2[1], [(y, w_rout)], w1[1], w2[1], tm=tm_post, ctx=False, final_g=final_norm,
                     name="post_rec_lat")

    return (y_prompt, y_sample, new_a_k, new_a_v, new_b_k, new_b_v, sf, sb)
```

```python
import functools
import math
from typing import NamedTuple

import jax
import jax.numpy as jnp
import numpy as np
from jax import lax
from jax.experimental import pallas as pl
from jax.experimental.pallas import tpu as pltpu

F32 = jnp.float32
BF16 = jnp.bfloat16

LANES = 128
SUBLANES = 8
VMEM_LIMIT_BYTES = 56 * 1024 * 1024

D_MODEL = 1024
DEPTH = 2
GRID_W = 64
HEAD_DIM = 64
A_HEADS = 8
A_KV_HEADS = 2
A_GROUP = A_HEADS // A_KV_HEADS
B_HEADS = 4
WINDOW = 128
ROPE_BASE = 10000.0
D_RNN = 1280
RNN_BLOCKS = 10
RNN_BW = D_RNN // RNN_BLOCKS
CONV_W = 4
CONV_LEFT = (CONV_W - 1) // 2
RGLRU_C = 8.0
D_FF = 4 * D_MODEL
EPS = 1e-6
SCALE = HEAD_DIM ** -0.5
NEG = -1e30

QA_W = A_HEADS * LANES
KA_W = A_KV_HEADS * HEAD_DIM
B_W = B_HEADS * 2 * HEAD_DIM
MOD_ROWS = 8
CTX_ROW = 4
LOG2E = math.log2(math.e)
ONES_ROWS = 16
PROJ_ROW_GROUPS = 2
SCAN_MAX_UNROLL = 1024


def _cparams(*semantics):
    return pltpu.CompilerParams(dimension_semantics=semantics, vmem_limit_bytes=VMEM_LIMIT_BYTES)


def _dot(a, b):
    return jnp.dot(a, b, preferred_element_type=F32)


def _dot_nt(a, b):
    return lax.dot_general(a, b, (((1,), (1,)), ((), ())), preferred_element_type=F32)


def _rms(x, g):
    return x * lax.rsqrt(jnp.mean(x * x, axis=-1, keepdims=True) + EPS) * g


def _gelu_tanh(x):
    return x * (0.5 * (1.0 + jnp.tanh(math.sqrt(2.0 / math.pi) * (x + 0.044715 * (x * x * x)))))


def _ada_kernel(c_ref, w_ref, b_ref, o_ref):
    c = c_ref[...]
    s = c * jax.nn.sigmoid(c)
    o_ref[...] = _dot(s.astype(BF16), w_ref[...].astype(BF16)) + b_ref[...]


def _ada_mod(cvec, w_ada, b_ada):
    tn = 1536
    out = pl.pallas_call(
        _ada_kernel,
        grid=(DEPTH, 6 * D_MODEL // tn),
        in_specs=[
            pl.BlockSpec((MOD_ROWS, D_MODEL), lambda l, j: (0, 0)),
            pl.BlockSpec((None, D_MODEL, tn), lambda l, j: (l, 0, j)),
            pl.BlockSpec((None, 1, tn), lambda l, j: (l, 0, j)),
        ],
        out_specs=pl.BlockSpec((None, MOD_ROWS, tn), lambda l, j: (l, 0, j)),
        out_shape=jax.ShapeDtypeStruct((DEPTH, MOD_ROWS, 6 * D_MODEL), F32),
        compiler_params=_cparams("parallel", "parallel"),
        name="ada_mod",
    )(cvec, w_ada, b_ada.reshape(DEPTH, 1, 6 * D_MODEL))
    return out.reshape(DEPTH, MOD_ROWS, 6, D_MODEL)


class _Seg(NamedTuple):
    col0: int
    width: int
    outs: tuple
    rope: bool = False
    scale: float = 1.0
    gelu: bool = False


def _proj_kernel(*refs, segs, rope):
    if rope:
        x_ref, mod_ref, g_ref, w_ref, cos_ref, sin_ref, *outs = refs
    else:
        x_ref, mod_ref, g_ref, w_ref, *outs = refs
    tm = x_ref.shape[0]
    rows = tm // PROJ_ROW_GROUPS
    groups = [slice(i * rows, (i + 1) * rows) for i in range(PROJ_ROW_GROUPS)]
    hbs = []
    for grp in groups:
        h = _rms(x_ref[grp, :], g_ref[...])
        hbs.append((h * (1.0 + mod_ref[1:2, :]) + mod_ref[0:1, :]).astype(BF16))
    if rope:
        first = (lax.broadcasted_iota(jnp.int32, (rows, LANES), 1) & 16) == 0
    outs = iter(outs)
    for seg in segs:
        ys = [_dot(hb, w_ref[:, seg.col0:seg.col0 + seg.width]) for hb in hbs]
        o_refs = [next(outs) for _ in seg.outs]
        for grp, y in zip(groups, ys):
            for t in range(seg.width // LANES):
                lanes = slice(t * LANES, (t + 1) * LANES)
                yt = y[:, lanes]
                if seg.rope:
                    sw = jnp.where(first, pltpu.roll(yt, LANES - 16, 1), pltpu.roll(yt, 16, 1))
                    yt = yt * cos_ref[grp, :] + sw * sin_ref[grp, :]
                if seg.scale != 1.0:
                    yt = yt * seg.scale
                if seg.gelu:
                    yt = _gelu_tanh(yt)
                for (layout, _), o_ref in zip(seg.outs, o_refs):
                    if layout == "cols":
                        o_ref[lanes, grp] = yt.T.astype(o_ref.dtype)
                    elif layout == "rows":
                        o_ref[grp, lanes] = yt.astype(o_ref.dtype)
                    else:
                        kind, seq_len = layout
                        assert rows % seq_len == 0
                        for r in range(rows // seq_len):
                            s = grp.start // seq_len + r
                            ys = yt[r * seq_len:(r + 1) * seq_len]
                            if kind == "heads":
                                o_ref[s, :, t, :] = ys.astype(o_ref.dtype)
                            else:
                                o_ref[s, lanes, :] = ys.T.astype(o_ref.dtype)


def _proj(x, mod, g, w, segs, *, tm, ctx, rope_tabs=None, name):
    B, T, D = x.shape

    def out_spec(width, layout):
        if layout == "cols":
            return pl.BlockSpec((None, width, tm), lambda b, i: (b, 0, i))
        if layout == "rows":
            return pl.BlockSpec((None, tm, width), lambda b, i: (b, i, 0))
        if layout[0] == "heads":
            return pl.BlockSpec((tm // layout[1], None, layout[1], width // LANES, LANES),
                                lambda b, i: (b * (T // tm) + i, 0, 0, 0, 0))
        return pl.BlockSpec((tm // layout[1], width, layout[1]), lambda b, i: (b * (T // tm) + i, 0, 0))

    def out_struct(width, layout, dt):
        if layout == "cols":
            return jax.ShapeDtypeStruct((B, width, T), dt)
        if layout == "rows":
            return jax.ShapeDtypeStruct((B, T, width), dt)
        if layout[0] == "heads":
            return jax.ShapeDtypeStruct((B * T // layout[1], 1, layout[1], width // LANES, LANES), dt)
        return jax.ShapeDtypeStruct((B * T // layout[1], width, layout[1]), dt)

    row = (lambda b, i: (CTX_ROW, 0, 0)) if ctx else (lambda b, i: (b, 0, 0))
    in_specs = [
        pl.BlockSpec((None, tm, D), lambda b, i: (b, i, 0)),
        pl.BlockSpec((None, 6, D), row),
        pl.BlockSpec((1, D), lambda b, i: (0, 0)),
        pl.BlockSpec(w.shape, lambda b, i: (0, 0)),
    ]
    args = [x, mod, g.reshape(1, D), w]
    if rope_tabs is not None:
        in_specs += [pl.BlockSpec((tm, LANES), lambda b, i: (i, 0))] * 2
        args += list(rope_tabs)
    out_specs = [out_spec(s.width, layout) for s in segs for layout, _ in s.outs]
    out_shape = [out_struct(s.width, layout, dt) for s in segs for layout, dt in s.outs]
    return pl.pallas_call(
        functools.partial(_proj_kernel, segs=tuple(segs), rope=rope_tabs is not None),
        grid=(B, T // tm),
        in_specs=in_specs,
        out_specs=out_specs,
        out_shape=out_shape,
        compiler_params=_cparams("parallel", "parallel"),
        name=name,
    )(*args)


def _diff_lambda(lq, lam_init):
    s1 = jnp.sum(lq[0:1, :] * lq[1:2, :], axis=1, keepdims=True)
    s2 = jnp.sum(lq[2:3, :] * lq[3:4, :], axis=1, keepdims=True)
    return jnp.exp(s1) - jnp.exp(s2) + lam_init


def _stack_group_queries(qa_ref, g, rows):
    return jnp.concatenate(
        [qa_ref[rows, (A_GROUP * g + hh) * LANES:(A_GROUP * g + hh + 1) * LANES] for hh in range(A_GROUP)], axis=0)


def _stack_pair_queries(q):
    lo = lax.broadcasted_iota(jnp.int32, q.shape, 1) < HEAD_DIM
    zero = jnp.zeros_like(q)
    return jnp.concatenate([jnp.where(lo, q, zero), jnp.where(lo, zero, q)], axis=0)


def _with_ones_rows(vt):
    return jnp.concatenate([vt, jnp.ones((ONES_ROWS, vt.shape[1]), BF16)], axis=0)


def _sink_row(sink_ref, g, tq):
    return jnp.concatenate(
        [jnp.full((1, tq), sink_ref[A_GROUP * g + hh] * LOG2E, F32) for hh in range(A_GROUP)], axis=1)


def _softmax_values(parts, sink=None):
    m = functools.reduce(jnp.maximum, [jnp.max(s, axis=0, keepdims=True) for s, _ in parts])
    if sink is not None:
        m = jnp.maximum(m, sink)
    ot = sum(_dot(vt1, jnp.exp2(s - m).astype(BF16)) for s, vt1 in parts)
    den = ot[LANES:LANES + 1]
    if sink is not None:
        den = den + jnp.exp2(sink - m)
    return ot[:LANES] / den


def _a_heads(ot, g, tq):
    return [ot[g * HEAD_DIM:(g + 1) * HEAD_DIM, hh * tq:(hh + 1) * tq] for hh in range(A_GROUP)]


def _subln(o, subln, lam_init):
    return _rms(o, subln) * (1.0 - lam_init)


def _diff_combine(ot, lam, subln, lam_init):
    tq = ot.shape[1] // 2
    return _subln((ot[:, :tq] - lam * ot[:, tq:]).T, subln, lam_init)


def _ctx_attn_kernel(sink_ref, qa_ref, ka_ref, va_ref, qb_ref, kb_ref, vb_ref, lamqk_ref, subln_ref, o_ref, *,
                     lam_init):
    T = qa_ref.shape[0]
    ka = ka_ref[...].astype(BF16)
    vat1 = _with_ones_rows(va_ref[...].T.astype(BF16))
    scores_a = [_dot_nt(ka, _stack_group_queries(qa_ref, g, slice(None))) for g in range(A_KV_HEADS)]
    scores_b, vbt1 = [], []
    for h in range(B_HEADS):
        sl = slice(h * LANES, (h + 1) * LANES)
        scores_b.append(_dot_nt(kb_ref[:, sl].astype(BF16), _stack_pair_queries(qb_ref[:, sl])))
        vbt1.append(_with_ones_rows(vb_ref[:, sl].T.astype(BF16)))
    heads = []
    for g in range(A_KV_HEADS):
        heads += _a_heads(_softmax_values([(scores_a[g], vat1)], _sink_row(sink_ref, g, T)), g, T)
    nqa = A_HEADS * HEAD_DIM
    o_ref[:, :nqa] = jnp.concatenate(heads, axis=0).T.astype(o_ref.dtype)
    lam = _diff_lambda(lamqk_ref[...], lam_init)
    for h in range(B_HEADS):
        ot = _softmax_values([(scores_b[h], vbt1[h])])
        o_ref[:, nqa + h * LANES:nqa + (h + 1) * LANES] = (
            _diff_combine(ot, lam, subln_ref[...], lam_init).astype(o_ref.dtype))


def _ctx_attention(qa, ka, va, qb, kb, vb, sink, lam_qk, subln, lam_init):
    B, T, _ = qa.shape
    blk = lambda w: pl.BlockSpec((None, T, w), lambda b: (b, 0, 0))
    full = lambda a: pl.BlockSpec(a.shape, lambda b: (0,) * a.ndim)
    return pl.pallas_call(
        functools.partial(_ctx_attn_kernel, lam_init=lam_init),
        grid=(B,),
        in_specs=[pl.BlockSpec(memory_space=pltpu.SMEM), blk(QA_W), blk(KA_W), blk(KA_W), blk(B_W), blk(B_W),
                  blk(B_W), full(lam_qk), full(subln)],
        out_specs=blk(A_HEADS * HEAD_DIM + B_W),
        out_shape=jax.ShapeDtypeStruct((B, T, A_HEADS * HEAD_DIM + B_W), BF16),
        compiler_params=_cparams("parallel"),
        name="ctx_attention",
    )(sink, qa, ka, va, qb, kb, vb, lam_qk, subln)


def _lat_a_kernel(sink_ref, qa_ref, k_ref, vt_ref, ck_ref, cvt_ref, o_ref, *, tq, nq, band):
    T = k_ref.shape[0]
    cols = A_GROUP * tq
    cvt1 = _with_ones_rows(cvt_ref[...])
    jobs = []
    for i in range(nq):
        qi = pl.program_id(1) * nq + i
        start = pl.multiple_of(jnp.clip(qi * tq - WINDOW, 0, T - band), WINDOW)
        kb = k_ref[pl.ds(start, band), :]
        vbt1 = _with_ones_rows(vt_ref[:, pl.ds(start, band)])
        kpos = start + lax.broadcasted_iota(jnp.int32, (band, cols), 0)
        qpos = qi * tq + lax.broadcasted_iota(jnp.int32, (band, cols), 1) % tq
        keep = jnp.abs(qpos - kpos) <= WINDOW
        for g in range(A_KV_HEADS):
            qg = _stack_group_queries(qa_ref, g, slice(i * tq, (i + 1) * tq))
            jobs.append((_dot_nt(ck_ref[...], qg), _dot_nt(kb, qg), keep, vbt1))
    for i in range(nq):
        heads = []
        for g in range(A_KV_HEADS):
            s_c, s_b, keep, vbt1 = jobs[i * A_KV_HEADS + g]
            ot = _softmax_values([(s_c, cvt1), (jnp.where(keep, s_b, NEG), vbt1)], _sink_row(sink_ref, g, tq))
            heads += _a_heads(ot, g, tq)
        o_ref[i * tq:(i + 1) * tq, :] = jnp.concatenate(heads, axis=0).T.astype(o_ref.dtype)


def _lat_a_attention(qa, ka, vat, cka, cvat, sink):
    B, T, _ = qa.shape
    L = cka.shape[1]
    tq, nq = WINDOW, 8
    band = 3 * WINDOW
    return pl.pallas_call(
        functools.partial(_lat_a_kernel, tq=tq, nq=nq, band=band),
        grid=(B, T // (tq * nq)),
        in_specs=[
            pl.BlockSpec(memory_space=pltpu.SMEM),
            pl.BlockSpec((None, tq * nq, QA_W), lambda b, i: (b, i, 0)),
            pl.BlockSpec((None, T, KA_W), lambda b, i: (b, 0, 0)),
            pl.BlockSpec((None, KA_W, T), lambda b, i: (b, 0, 0)),
            pl.BlockSpec((None, L, KA_W), lambda b, i: (b, 0, 0)),
            pl.BlockSpec((None, KA_W, L), lambda b, i: (b, 0, 0)),
        ],
        out_specs=pl.BlockSpec((None, tq * nq, A_HEADS * HEAD_DIM), lambda b, i: (b, i, 0)),
        out_shape=jax.ShapeDtypeStruct((B, T, A_HEADS * HEAD_DIM), BF16),
        compiler_params=_cparams("parallel", "parallel"),
        name="lat_a_attention",
    )(sink, qa, ka, vat, cka, cvat)


def _lat_b_kernel(qt_ref, k_ref, vt_ref, ck_ref, cvt_ref, lamqk_ref, subln_ref, o_ref, *, tk, ahead, lam_init):
    tq = qt_ref.shape[1]
    T = k_ref.shape[0]
    qt = qt_ref[...]
    zero = jnp.zeros((HEAD_DIM, tq), BF16)
    qst = jnp.concatenate([jnp.concatenate([qt[:HEAD_DIM], zero], axis=0),
                           jnp.concatenate([zero, qt[HEAD_DIM:]], axis=0)], axis=1)
    blocks = [(ck_ref[...], cvt_ref[...])]
    blocks += [(k_ref[j * tk:(j + 1) * tk, :], vt_ref[:, j * tk:(j + 1) * tk]) for j in range(T // tk)]

    def scores(k):
        return _dot(k, qst)

    def accumulate(s, vt, carry):
        m, acc = carry
        m_new = jnp.maximum(m, jnp.max(s, axis=0, keepdims=True))
        p = jnp.exp2(s - m_new).astype(BF16)
        acc = jnp.exp2(m - m_new) * acc + _dot(_with_ones_rows(vt), p)
        return m_new, acc

    carry = (jnp.full((1, 2 * tq), -jnp.inf, F32), jnp.zeros((LANES + ONES_ROWS, 2 * tq), F32))
    pending = [scores(blocks[j][0]) for j in range(ahead)]
    for j in range(len(blocks)):
        if j + ahead < len(blocks):
            pending.append(scores(blocks[j + ahead][0]))
        carry = accumulate(pending.pop(0), blocks[j][1], carry)
    _, acc = carry
    ot = acc[:LANES] / acc[LANES:LANES + 1]
    lam = _diff_lambda(lamqk_ref[...], lam_init)
    o_ref[...] = _diff_combine(ot, lam, subln_ref[...], lam_init).astype(o_ref.dtype)


def _lat_b_attention(qbt, kb, vbt, ckb, cvbt, lam_qk, subln, lam_init):
    B, T, _ = kb.shape
    L = ckb.shape[1]
    tq = 1024
    full = lambda a: pl.BlockSpec(a.shape, lambda b, h, i: (0,) * a.ndim)
    return pl.pallas_call(
        functools.partial(_lat_b_kernel, tk=512, ahead=2, lam_init=lam_init),
        grid=(B, B_HEADS, T // tq),
        in_specs=[
            pl.BlockSpec((None, LANES, tq), lambda b, h, i: (b, h, i)),
            pl.BlockSpec((None, T, LANES), lambda b, h, i: (b, 0, h)),
            pl.BlockSpec((None, LANES, T), lambda b, h, i: (b, h, 0)),
            pl.BlockSpec((None, L, LANES), lambda b, h, i: (b, 0, h)),
            pl.BlockSpec((None, LANES, L), lambda b, h, i: (b, h, 0)),
            full(lam_qk), full(subln),
        ],
        out_specs=pl.BlockSpec((None, tq, LANES), lambda b, h, i: (b, i, h)),
        out_shape=jax.ShapeDtypeStruct((B, T, B_W), BF16),
        compiler_params=_cparams("parallel", "parallel", "parallel"),
        name="lat_b_attention",
    )(qbt, kb, vbt, ckb, cvbt, lam_qk, subln)


def _scan_chunk_len(n_steps):
    chunk = -(-n_steps // SUBLANES)
    while chunk % 8 != 4:
        chunk += 1
    return chunk


class _RecBlock(NamedTuple):
    gate: object
    xr: object
    cw: object
    cb: object
    wbd: object
    bbd: object
    lam: object
    h0f: object
    h0b: object
    y: object
    sf: object
    sb: object
    xp: object
    af: object
    uf: object
    ab: object
    ub: object
    pf: object
    hf: object
    pb: object
    hb: object


def _rec_kernel(gate_ref, xr_ref, cw_ref, cb_ref, wbd_ref, bbd_ref, lam_ref, h0f_ref, h0b_ref,
                y_ref, sf_ref, sb_ref, *scratch, nblk, tc, gates_unroll, chunk, unroll):
    blocks = []
    for n in range(nblk):
        sl = slice(n * LANES, (n + 1) * LANES)
        lanes = [r.at[:, sl] for r in (gate_ref, xr_ref, cw_ref, cb_ref)]
        lanes += [wbd_ref.at[n], bbd_ref.at[n]]
        lanes += [r.at[:, sl] for r in (lam_ref, h0f_ref, h0b_ref, y_ref, sf_ref, sb_ref)]
        xp, af, uf, ab, ub = [r.at[n] for r in scratch]
        blocks.append(_RecBlock(*lanes, xp, af, uf, ab, ub, pf=af, hf=uf, pb=ab, hb=ub))
    for blk in blocks:
        _rec_gates(blk, tc=tc, gates_unroll=gates_unroll, chunk=chunk)
    _rec_scan(blocks, chunk=chunk, unroll=unroll)
    for blk in blocks:
        _rec_combine(blk, tc=tc)


def _rec_gates(blk, *, tc, gates_unroll, chunk):
    T = blk.xr.shape[0]
    pad = SUBLANES
    xp = blk.xp
    xp[0:pad, :] = jnp.zeros((pad, LANES), F32)
    xp[T + pad:T + 2 * pad, :] = jnp.zeros((pad, LANES), F32)
    xp[pad:T + pad, :] = blk.xr[...]
    tail = SUBLANES * chunk - T
    for a_s, u_s in ((blk.af, blk.uf), (blk.ab, blk.ub)):
        a_s[T:T + tail, :] = jnp.ones((tail, LANES), F32)
        u_s[T:T + tail, :] = jnp.zeros((tail, LANES), F32)
    cw = blk.cw[...]
    cb = blk.cb[...]
    nl = -blk.lam[...]
    softplus = jnp.maximum(nl, 0.0) + jnp.log1p(jnp.exp(-jnp.abs(nl)))
    cp = (0.5 * RGLRU_C) * softplus

    def gates(it, _):
        for q in range(gates_unroll):
            gate_chunk(it * gates_unroll + q)
        return 0

    def gate_chunk(ci):
        t0 = pl.multiple_of(ci * tc, tc)
        y = cb
        for j in range(CONV_W):
            y = y + xp[pl.ds(t0 + (pad - CONV_LEFT + j), tc), :] * cw[j:j + 1, :]
        t = jnp.tanh(_dot(y.astype(BF16), blk.wbd[...]) + blk.bbd[...])
        yh = 0.5 * y
        for d, (a_s, u_s) in enumerate(((blk.af, blk.uf), (blk.ab, blk.ub))):
            t_r = t[:, 2 * d * LANES:(2 * d + 1) * LANES]
            t_i = t[:, (2 * d + 1) * LANES:(2 * d + 2) * LANES]
            neg_log_a = cp[d:d + 1, :] * t_r + cp[d:d + 1, :]
            a = jnp.exp2(neg_log_a * (-LOG2E))
            w = jnp.tanh(neg_log_a) * (a * a + 1.0)
            sqrt_w = jnp.where(w > 0.0, w * lax.rsqrt(w), 0.0)
            a_s[pl.ds(t0, tc), :] = a
            u_s[pl.ds(t0, tc), :] = (t_i + 1.0) * (yh * sqrt_w)

    lax.fori_loop(0, T // (tc * gates_unroll), gates, 0)


def _rec_scan(blocks, *, chunk, unroll):
    T = blocks[0].xr.shape[0]

    def rows(k):
        return pl.ds(k, SUBLANES, stride=chunk)

    def local_scan(it, carry):
        carry = list(carry)
        for q in range(unroll):
            k = it * unroll + q
            kb = chunk - 1 - k
            for i, blk in enumerate(blocks):
                hf, pf, hb, pb = carry[4 * i:4 * i + 4]
                a = blk.af[rows(k), :]
                hf = a * hf + blk.uf[rows(k), :]
                pf = a * pf
                blk.hf[rows(k), :] = hf
                blk.pf[rows(k), :] = pf
                a = blk.ab[rows(kb), :]
                hb = a * hb + blk.ub[rows(kb), :]
                pb = a * pb
                blk.hb[rows(kb), :] = hb
                blk.pb[rows(kb), :] = pb
                carry[4 * i:4 * i + 4] = [hf, pf, hb, pb]
        return tuple(carry)

    zero = jnp.zeros((SUBLANES, LANES), F32)
    one = jnp.ones((SUBLANES, LANES), F32)
    ends = lax.fori_loop(0, chunk // unroll, local_scan, (zero, one, zero, one) * len(blocks))

    carries = []
    for i, blk in enumerate(blocks):
        hf, pf, hb, pb = ends[4 * i:4 * i + 4]
        cf = [blk.h0f[...]]
        for r in range(SUBLANES - 1):
            cf.append(hf[r:r + 1, :] + pf[r:r + 1, :] * cf[r])
        cb_rev = [blk.h0b[...]]
        for r in range(SUBLANES - 1, 0, -1):
            cb_rev.append(hb[r:r + 1, :] + pb[r:r + 1, :] * cb_rev[-1])
        carries.append((jnp.concatenate(cf, axis=0), jnp.concatenate(cb_rev[::-1], axis=0)))

    def apply_carry(it, _):
        for q in range(unroll):
            k = it * unroll + q
            for blk, (carry_f, carry_b) in zip(blocks, carries):
                blk.uf[rows(k), :] = blk.hf[rows(k), :] + blk.pf[rows(k), :] * carry_f
                blk.ub[rows(k), :] = blk.hb[rows(k), :] + blk.pb[rows(k), :] * carry_b
        return 0

    lax.fori_loop(0, chunk // unroll, apply_carry, 0)
    for blk in blocks:
        blk.sf[...] = blk.uf[T - 1:T, :]
        blk.sb[...] = blk.ub[0:1, :]


def _rec_combine(blk, *, tc):
    T = blk.xr.shape[0]

    def combine(ci, _):
        t0 = pl.multiple_of(ci * tc, tc)
        blk.y[pl.ds(t0, tc), :] = ((blk.uf[pl.ds(t0, tc), :] + blk.ub[pl.ds(t0, tc), :])
                                   * blk.gate[pl.ds(t0, tc), :]).astype(blk.y.dtype)
        return 0

    lax.fori_loop(0, T // tc, combine, 0)


def _rec_mixer(gate, xr, h0f, h0b, conv_w, conv_b, w_bd, b_bd, lam, *, nblk, tc, name):
    B, T, _ = xr.shape
    chunk = _scan_chunk_len(T)
    wid = nblk * LANES
    col = lambda rows: pl.BlockSpec((rows, wid), lambda b, n: (0, n))
    seq = pl.BlockSpec((None, T, wid), lambda b, n: (b, 0, n))
    st = pl.BlockSpec((None, 1, wid), lambda b, n: (b, 0, n))
    return pl.pallas_call(
        functools.partial(_rec_kernel, nblk=nblk, tc=tc, gates_unroll=2, chunk=chunk,
                          unroll=max(u for u in range(1, SCAN_MAX_UNROLL + 1) if chunk % u == 0)),
        grid=(B, RNN_BLOCKS // nblk),
        in_specs=[seq, seq, col(CONV_W), col(1),
                  pl.BlockSpec((nblk, LANES, 4 * LANES), lambda b, n: (n, 0, 0)),
                  pl.BlockSpec((nblk, 1, 4 * LANES), lambda b, n: (n, 0, 0)),
                  col(2), st, st],
        out_specs=[seq, st, st],
        out_shape=[jax.ShapeDtypeStruct((B, T, D_RNN), BF16),
                   jax.ShapeDtypeStruct((B, 1, D_RNN), F32),
                   jax.ShapeDtypeStruct((B, 1, D_RNN), F32)],
        scratch_shapes=[pltpu.VMEM((nblk, T + 2 * SUBLANES, LANES), F32)]
        + [pltpu.VMEM((nblk, SUBLANES * chunk, LANES), F32)] * 4,
        compiler_params=_cparams("parallel", "parallel"),
        name=name,
    )(gate, xr, conv_w, conv_b.reshape(1, D_RNN), w_bd, b_bd, lam, h0f, h0b)


def _post_kernel(*refs, n_mix, final, fc):
    x_ref, mod_ref, g2_ref = refs[:3]
    mix = refs[3:3 + 2 * n_mix]
    w1_ref, w2_ref = refs[3 + 2 * n_mix:5 + 2 * n_mix]
    rest = refs[5 + 2 * n_mix:]
    if final:
        gf_ref, o_ref = rest
    else:
        (o_ref,) = rest
    mixed = _dot(mix[0][...], mix[1][...])
    for i in range(1, n_mix):
        mixed = mixed + _dot(mix[2 * i][...], mix[2 * i + 1][...])
    x1 = x_ref[...] + mod_ref[2:3, :] * mixed
    h = _rms(x1, g2_ref[...])
    hb = (h * (1.0 + mod_ref[4:5, :]) + mod_ref[3:4, :]).astype(BF16)
    n_chunks = D_FF // fc
    up = lambda c: _dot(hb, w1_ref[:, c * fc:(c + 1) * fc])
    acc = None
    a = up(0)
    for c in range(n_chunks):
        a_next = up(c + 1) if c + 1 < n_chunks else None
        part = _dot(jnp.square(jnp.maximum(a, 0.0)).astype(BF16), w2_ref[c * fc:(c + 1) * fc, :])
        acc = part if acc is None else acc + part
        a = a_next
    x2 = x1 + mod_ref[5:6, :] * acc
    if final:
        x2 = _rms(x2, gf_ref[...])
    o_ref[...] = x2


def _post(x, mod, g2, mixes, w1, w2, *, tm, ctx, final_g=None, name):
    B, T, D = x.shape
    row = (lambda b, i: (CTX_ROW, 0, 0)) if ctx else (lambda b, i: (b, 0, 0))
    const = lambda a: pl.BlockSpec(a.shape, lambda b, i: (0,) * a.ndim, pipeline_mode=pl.Buffered(1))
    in_specs = [
        pl.BlockSpec((None, tm, D), lambda b, i: (b, i, 0)),
        pl.BlockSpec((None, 6, D), row),
        pl.BlockSpec((1, D), lambda b, i: (0, 0)),
    ]
    args = [x, mod, g2.reshape(1, D)]
    for o, w in mixes:
        in_specs += [pl.BlockSpec((None, tm, o.shape[-1]), lambda b, i: (b, i, 0)), const(w)]
        args += [o, w]
    in_specs += [const(w1), const(w2)]
    args += [w1, w2]
    if final_g is not None:
        in_specs.append(pl.BlockSpec((1, D), lambda b, i: (0, 0)))
        args.append(final_g.reshape(1, D))
    return pl.pallas_call(
        functools.partial(_post_kernel, n_mix=len(mixes), final=final_g is not None, fc=1024),
        grid=(B, T // tm),
        in_specs=in_specs,
        out_specs=pl.BlockSpec((None, tm, D), lambda b, i: (b, i, 0)),
        out_shape=jax.ShapeDtypeStruct((B, T, D), F32),
        compiler_params=_cparams("parallel", "parallel"),
        name=name,
    )(*args)


def _rope_tables(n_tokens):
    rows = n_tokens // GRID_W
    r, cl = jnp.meshgrid(jnp.arange(rows, dtype=F32), jnp.arange(GRID_W, dtype=F32), indexing='ij')
    quarter = HEAD_DIM // 4
    inv = ROPE_BASE ** (-jnp.arange(quarter, dtype=F32) / quarter)
    ang = jnp.stack([r.reshape(-1)[:, None] * inv, cl.reshape(-1)[:, None] * inv], axis=1)
    cos, sin = jnp.cos(ang), jnp.sin(ang)
    cos64 = jnp.concatenate([cos[:, 0], cos[:, 0], cos[:, 1], cos[:, 1]], axis=-1)
    sin64 = jnp.concatenate([-sin[:, 0], sin[:, 0], -sin[:, 1], sin[:, 1]], axis=-1)
    return jnp.tile(cos64, (1, LANES // HEAD_DIM)), jnp.tile(sin64, (1, LANES // HEAD_DIM))


def _att_in_weights(w_in):
    d = w_in.shape[0]
    nq = A_HEADS * HEAD_DIM
    wq = w_in[:, :nq].reshape(d, A_HEADS, HEAD_DIM)
    z = jnp.zeros_like(wq)
    in_first = (jnp.arange(A_HEADS) // A_GROUP == 0)[None, :, None]
    wq = jnp.where(in_first, jnp.concatenate([wq, z], axis=-1), jnp.concatenate([z, wq], axis=-1))
    return jnp.concatenate([wq.reshape(d, QA_W), w_in[:, nq:]], axis=1).astype(BF16)


def _block_diag_weights(w_a, b_a, w_x, b_x):
    w = jnp.concatenate([w_a[0], w_x[0], w_a[1], w_x[1]], axis=-1)
    b = jnp.concatenate([v.reshape(RNN_BLOCKS, 1, RNN_BW) for v in (b_a[0], b_x[0], b_a[1], b_x[1])], axis=-1)
    return (0.5 * w).astype(BF16), 0.5 * b


def kernel(x_prompt, x_sample, cache_a_k, cache_a_v, cache_b_k, cache_b_v, state_fwd, state_bwd, c, c_ctx, norm1, norm2, w_ada, b_ada, w_mlp1, w_mlp2, att_w_in, att_w_out, att_sink, att_lam_qk, att_subln, rec_w_in, rec_conv_w, rec_conv_b, rec_w_a, rec_b_a, rec_w_x, rec_b_x, rec_lam, rec_w_out, final_norm):
    nb, n_seq, _ = x_prompt.shape
    nd, d_seq, _ = x_sample.shape
    past = cache_a_k.shape[2]
    assert nd <= CTX_ROW and DEPTH == 2
    cvec = jnp.concatenate([c, jnp.zeros((CTX_ROW - nd, D_MODEL), F32), c_ctx[None],
                            jnp.zeros((MOD_ROWS - CTX_ROW - 1, D_MODEL), F32)], axis=0)
    mod = _ada_mod(cvec, w_ada, b_ada)
    w1 = [w_mlp1[layer].astype(BF16) for layer in range(DEPTH)]
    w2 = [w_mlp2[layer].astype(BF16) for layer in range(DEPTH)]
    tm_proj, tm_post, tm_ctx = 512, 1024, 512
    tok = lambda a, tm: a.reshape(-1, tm, a.shape[-1])
    seq = lambda a: a.reshape(nb, n_seq, a.shape[-1])

    lam_init = 0.8 - 0.6 * math.exp(-0.3 * 0)
    w_in = _att_in_weights(att_w_in[0])
    w_out = att_w_out[0].astype(BF16)
    nqa = A_HEADS * HEAD_DIM
    c0 = [0, QA_W, QA_W + KA_W, QA_W + 2 * KA_W, QA_W + 2 * KA_W + B_W, QA_W + 2 * KA_W + 2 * B_W]
    widths = [QA_W, KA_W, KA_W, B_W, B_W, B_W]
    scales = [SCALE * LOG2E, 1.0, 1.0, SCALE * LOG2E, 1.0, 1.0]
    roped = [True, True, False, True, True, False]
    sink = att_sink[0]
    lam_qk = att_lam_qk[0]
    subln = att_subln[0].reshape(1, 2 * HEAD_DIM)

    rows = lambda dt: (("rows", dt),)
    cache_a = ((("seqcols", n_seq), F32), ("rows", BF16))
    cache_b = ((("heads", n_seq), F32), ("rows", BF16))
    segs_ctx = [_Seg(c0[0], QA_W, rows(BF16), scale=scales[0]), _Seg(c0[1], KA_W, cache_a),
                _Seg(c0[2], KA_W, cache_a), _Seg(c0[3], B_W, rows(BF16), scale=scales[3]),
                _Seg(c0[4], B_W, cache_b), _Seg(c0[5], B_W, cache_b)]
    qa, kat, ka, vat, va, qb, new_b_k, kb, new_b_v, vb = _proj(tok(x_prompt, tm_proj), mod[0], norm1[0], w_in, segs_ctx,
                                                               tm=tm_proj, ctx=True, name="proj_att_ctx")
    qa, ka, va, qb, kb, vb = map(seq, (qa, ka, va, qb, kb, vb))
    uncol = lambda a: jnp.transpose(a.reshape(nb, A_KV_HEADS, HEAD_DIM, n_seq), (0, 3, 1, 2))[:, None]
    new_a_k, new_a_v = uncol(kat), uncol(vat)
    o_ctx = _ctx_attention(qa, ka, va, qb, kb, vb, sink, lam_qk, subln, lam_init)
    xp = _post(tok(x_prompt, tm_ctx), mod[0], norm2[0], [(tok(o_ctx, tm_ctx), w_out)], w1[0], w2[0], tm=tm_ctx, ctx=True,
               name="post_att_ctx")

    segs_lat = [_Seg(c0[i], widths[i], (("cols" if i in (2, 3, 5) else "rows", BF16),), rope=roped[i],
                     scale=scales[i]) for i in range(6)]
    qa, ka, vat, qbt, kb, vbt = _proj(x_sample, mod[0], norm1[0], w_in, segs_lat, tm=tm_proj, ctx=False,
                                      rope_tabs=_rope_tables(d_seq), name="proj_att_lat")
    cka = cache_a_k[:, 0].reshape(nd, past, KA_W).astype(BF16)
    cvat = jnp.swapaxes(cache_a_v[:, 0].reshape(nd, past, KA_W), 1, 2).astype(BF16)
    ckb = cache_b_k[:, 0].reshape(nd, past, B_W).astype(BF16)
    cvbt = jnp.swapaxes(cache_b_v[:, 0].reshape(nd, past, B_W), 1, 2).astype(BF16)
    oa = _lat_a_attention(qa, ka, vat, cka, cvat, sink)
    ob = _lat_b_attention(qbt, kb, vbt, ckb, cvbt, lam_qk, subln, lam_init)
    xs = _post(x_sample, mod[0], norm2[0], [(oa, w_out[:nqa]), (ob, w_out[nqa:])], w1[0], w2[0], tm=tm_post,
               ctx=False, name="post_att_lat")

    w_rin = rec_w_in[0].astype(BF16)
    w_rout = rec_w_out[0].astype(BF16)
    w_bd, b_bd = _block_diag_weights(rec_w_a[0], rec_b_a[0], rec_w_x[0], rec_b_x[0])
    segs_rec = [_Seg(0, D_RNN, rows(F32), gelu=True), _Seg(D_RNN, D_RNN, rows(F32))]
    zeros = jnp.zeros((nb, 1, D_RNN), F32)

    gate, xr = map(seq, _proj(tok(xp, tm_proj), mod[1], norm1[1], w_rin, segs_rec, tm=tm_proj, ctx=True,
                              name="proj_rec_ctx"))
    y, sf, sb = _rec_mixer(gate, xr, zeros, zeros, rec_conv_w[0], rec_conv_b[0], w_bd, b_bd, rec_lam[0],
                           nblk=RNN_BLOCKS, tc=128, name="rec_mixer_ctx")
    y_prompt = seq(_post(xp, mod[1], norm2[1], [(tok(y, tm_ctx), w_rout)], w1[1], w2[1], tm=tm_ctx, ctx=True,
                         final_g=final_norm, name="post_rec_ctx"))

    gate, xr = _proj(xs, mod[1], norm1[1], w_rin, segs_rec, tm=tm_proj, ctx=False, name="proj_rec_lat")
    y, _, _ = _rec_mixer(gate, xr, state_fwd[:, 0:1], state_bwd[:, 0:1], rec_conv_w[0], rec_conv_b[0], w_bd, b_bd,
                         rec_lam[0], nblk=2, tc=256, name="rec_mixer_lat")
    y_sample = _post(xs, mod[1], norm2[1], [(y, w_rout)], w1[1], w2[1], tm=tm_post, ctx=False, final_g=final_norm,
                     name="post_rec_lat")

    return (y_prompt, y_sample, new_a_k, new_a_v, new_b_k, new_b_v, sf, sb)
```

```python
import functools
import math
from typing import NamedTuple

import jax
import jax.numpy as jnp
import numpy as np
from jax import lax
from jax.experimental import pallas as pl
from jax.experimental.pallas import tpu as pltpu

F32 = jnp.float32
BF16 = jnp.bfloat16

LANES = 128
SUBLANES = 8
VMEM_LIMIT_BYTES = 56 * 1024 * 1024

D_MODEL = 1024
DEPTH = 2
GRID_W = 64
HEAD_DIM = 64
A_HEADS = 8
A_KV_HEADS = 2
A_GROUP = A_HEADS // A_KV_HEADS
B_HEADS = 4
WINDOW = 128
ROPE_BASE = 10000.0
D_RNN = 1280
RNN_BLOCKS = 10
RNN_BW = D_RNN // RNN_BLOCKS
CONV_W = 4
CONV_LEFT = (CONV_W - 1) // 2
RGLRU_C = 8.0
D_FF = 4 * D_MODEL
EPS = 1e-6
SCALE = HEAD_DIM ** -0.5
NEG = -1e30

QA_W = A_HEADS * LANES
KA_W = A_KV_HEADS * HEAD_DIM
B_W = B_HEADS * 2 * HEAD_DIM
MOD_ROWS = 8
CTX_ROW = 4
LOG2E = math.log2(math.e)
ONES_ROWS = 16
PROJ_ROW_GROUPS = 2
SCAN_MAX_UNROLL = 1024


def _cparams(*semantics):
    return pltpu.CompilerParams(dimension_semantics=semantics, vmem_limit_bytes=VMEM_LIMIT_BYTES)


def _dot(a, b):
    return jnp.dot(a, b, preferred_element_type=F32)


def _dot_nt(a, b):
    return lax.dot_general(a, b, (((1,), (1,)), ((), ())), preferred_element_type=F32)


def _rms(x, g):
    return x * lax.rsqrt(jnp.mean(x * x, axis=-1, keepdims=True) + EPS) * g


def _gelu_tanh(x):
    return x * (0.5 * (1.0 + jnp.tanh(math.sqrt(2.0 / math.pi) * (x + 0.044715 * (x * x * x)))))


def _ada_kernel(c_ref, w_ref, b_ref, o_ref):
    c = c_ref[...]
    s = c * jax.nn.sigmoid(c)
    o_ref[...] = _dot(s.astype(BF16), w_ref[...].astype(BF16)) + b_ref[...]


def _ada_mod(cvec, w_ada, b_ada):
    tn = 1536
    out = pl.pallas_call(
        _ada_kernel,
        grid=(DEPTH, 6 * D_MODEL // tn),
        in_specs=[
            pl.BlockSpec((MOD_ROWS, D_MODEL), lambda l, j: (0, 0)),
            pl.BlockSpec((None, D_MODEL, tn), lambda l, j: (l, 0, j)),
            pl.BlockSpec((None, 1, tn), lambda l, j: (l, 0, j)),
        ],
        out_specs=pl.BlockSpec((None, MOD_ROWS, tn), lambda l, j: (l, 0, j)),
        out_shape=jax.ShapeDtypeStruct((DEPTH, MOD_ROWS, 6 * D_MODEL), F32),
        compiler_params=_cparams("parallel", "parallel"),
        name="ada_mod",
    )(cvec, w_ada, b_ada.reshape(DEPTH, 1, 6 * D_MODEL))
    return out.reshape(DEPTH, MOD_ROWS, 6, D_MODEL)


class _Seg(NamedTuple):
    col0: int
    width: int
    outs: tuple
    rope: bool = False
    scale: float = 1.0
    gelu: bool = False


def _proj_kernel(*refs, segs, rope):
    if rope:
        x_ref, mod_ref, g_ref, w_ref, cos_ref, sin_ref, *outs = refs
    else:
        x_ref, mod_ref, g_ref, w_ref, *outs = refs
    tm = x_ref.shape[0]
    rows = tm // PROJ_ROW_GROUPS
    groups = [slice(i * rows, (i + 1) * rows) for i in range(PROJ_ROW_GROUPS)]
    hbs = []
    for grp in groups:
        h = _rms(x_ref[grp, :], g_ref[...])
        hbs.append((h * (1.0 + mod_ref[1:2, :]) + mod_ref[0:1, :]).astype(BF16))
    if rope:
        first = (lax.broadcasted_iota(jnp.int32, (rows, LANES), 1) & 16) == 0
    outs = iter(outs)
    for seg in segs:
        ys = [_dot(hb, w_ref[:, seg.col0:seg.col0 + seg.width]) for hb in hbs]
        o_refs = [next(outs) for _ in seg.outs]
        for grp, y in zip(groups, ys):
            for t in range(seg.width // LANES):
                lanes = slice(t * LANES, (t + 1) * LANES)
                yt = y[:, lanes]
                if seg.rope:
                    sw = jnp.where(first, pltpu.roll(yt, LANES - 16, 1), pltpu.roll(yt, 16, 1))
                    yt = yt * cos_ref[grp, :] + sw * sin_ref[grp, :]
                if seg.scale != 1.0:
                    yt = yt * seg.scale
                if seg.gelu:
                    yt = _gelu_tanh(yt)
                for (layout, _), o_ref in zip(seg.outs, o_refs):
                    if layout == "cols":
                        o_ref[lanes, grp] = yt.T.astype(o_ref.dtype)
                    elif layout == "rows":
                        o_ref[grp, lanes] = yt.astype(o_ref.dtype)
                    else:
                        kind, seq_len = layout
                        assert rows % seq_len == 0
                        for r in range(rows // seq_len):
                            s = grp.start // seq_len + r
                            ys = yt[r * seq_len:(r + 1) * seq_len]
                            if kind == "heads":
                                o_ref[s, :, t, :] = ys.astype(o_ref.dtype)
                            else:
                                o_ref[s, lanes, :] = ys.T.astype(o_ref.dtype)


def _proj(x, mod, g, w, segs, *, tm, ctx, rope_tabs=None, name):
    B, T, D = x.shape

    def out_spec(width, layout):
        if layout == "cols":
            return pl.BlockSpec((None, width, tm), lambda b, i: (b, 0, i))
        if layout == "rows":
            return pl.BlockSpec((None, tm, width), lambda b, i: (b, i, 0))
        if layout[0] == "heads":
            return pl.BlockSpec((tm // layout[1], None, layout[1], width // LANES, LANES),
                                lambda b, i: (b * (T // tm) + i, 0, 0, 0, 0))
        return pl.BlockSpec((tm // layout[1], width, layout[1]), lambda b, i: (b * (T // tm) + i, 0, 0))

    def out_struct(width, layout, dt):
        if layout == "cols":
            return jax.ShapeDtypeStruct((B, width, T), dt)
        if layout == "rows":
            return jax.ShapeDtypeStruct((B, T, width), dt)
        if layout[0] == "heads":
            return jax.ShapeDtypeStruct((B * T // layout[1], 1, layout[1], width // LANES, LANES), dt)
        return jax.ShapeDtypeStruct((B * T // layout[1], width, layout[1]), dt)

    row = (lambda b, i: (CTX_ROW, 0, 0)) if ctx else (lambda b, i: (b, 0, 0))
    in_specs = [
        pl.BlockSpec((None, tm, D), lambda b, i: (b, i, 0)),
        pl.BlockSpec((None, 6, D), row),
        pl.BlockSpec((1, D), lambda b, i: (0, 0)),
        pl.BlockSpec(w.shape, lambda b, i: (0, 0)),
    ]
    args = [x, mod, g.reshape(1, D), w]
    if rope_tabs is not None:
        in_specs += [pl.BlockSpec((tm, LANES), lambda b, i: (i, 0))] * 2
        args += list(rope_tabs)
    out_specs = [out_spec(s.width, layout) for s in segs for layout, _ in s.outs]
    out_shape = [out_struct(s.width, layout, dt) for s in segs for layout, dt in s.outs]
    return pl.pallas_call(
        functools.partial(_proj_kernel, segs=tuple(segs), rope=rope_tabs is not None),
        grid=(B, T // tm),
        in_specs=in_specs,
        out_specs=out_specs,
        out_shape=out_shape,
        compiler_params=_cparams("parallel", "parallel"),
        name=name,
    )(*args)


def _diff_lambda(lq, lam_init):
    s1 = jnp.sum(lq[0:1, :] * lq[1:2, :], axis=1, keepdims=True)
    s2 = jnp.sum(lq[2:3, :] * lq[3:4, :], axis=1, keepdims=True)
    return jnp.exp(s1) - jnp.exp(s2) + lam_init


def _stack_group_queries(qa_ref, g, rows):
    return jnp.concatenate(
        [qa_ref[rows, (A_GROUP * g + hh) * LANES:(A_GROUP * g + hh + 1) * LANES] for hh in range(A_GROUP)], axis=0)


def _stack_pair_queries(q):
    lo = lax.broadcasted_iota(jnp.int32, q.shape, 1) < HEAD_DIM
    zero = jnp.zeros_like(q)
    return jnp.concatenate([jnp.where(lo, q, zero), jnp.where(lo, zero, q)], axis=0)


def _with_ones_rows(vt):
    return jnp.concatenate([vt, jnp.ones((ONES_ROWS, vt.shape[1]), BF16)], axis=0)


def _sink_row(sink_ref, g, tq):
    return jnp.concatenate(
        [jnp.full((1, tq), sink_ref[A_GROUP * g + hh] * LOG2E, F32) for hh in range(A_GROUP)], axis=1)


def _softmax_values(parts, sink=None):
    m = functools.reduce(jnp.maximum, [jnp.max(s, axis=0, keepdims=True) for s, _ in parts])
    if sink is not None:
        m = jnp.maximum(m, sink)
    ot = sum(_dot(vt1, jnp.exp2(s - m).astype(BF16)) for s, vt1 in parts)
    den = ot[LANES:LANES + 1]
    if sink is not None:
        den = den + jnp.exp2(sink - m)
    return ot[:LANES] / den


def _a_heads(ot, g, tq):
    return [ot[g * HEAD_DIM:(g + 1) * HEAD_DIM, hh * tq:(hh + 1) * tq] for hh in range(A_GROUP)]


def _subln(o, subln, lam_init):
    return _rms(o, subln) * (1.0 - lam_init)


def _diff_combine(ot, lam, subln, lam_init):
    tq = ot.shape[1] // 2
    return _subln((ot[:, :tq] - lam * ot[:, tq:]).T, subln, lam_init)


def _ctx_attn_kernel(sink_ref, qa_ref, ka_ref, va_ref, qb_ref, kb_ref, vb_ref, lamqk_ref, subln_ref, o_ref, *,
                     lam_init):
    T = qa_ref.shape[0]
    ka = ka_ref[...].astype(BF16)
    vat1 = _with_ones_rows(va_ref[...].T.astype(BF16))
    scores_a = [_dot_nt(ka, _stack_group_queries(qa_ref, g, slice(None))) for g in range(A_KV_HEADS)]
    scores_b, vbt1 = [], []
    for h in range(B_HEADS):
        sl = slice(h * LANES, (h + 1) * LANES)
        scores_b.append(_dot_nt(kb_ref[:, sl].astype(BF16), _stack_pair_queries(qb_ref[:, sl])))
        vbt1.append(_with_ones_rows(vb_ref[:, sl].T.astype(BF16)))
    heads = []
    for g in range(A_KV_HEADS):
        heads += _a_heads(_softmax_values([(scores_a[g], vat1)], _sink_row(sink_ref, g, T)), g, T)
    nqa = A_HEADS * HEAD_DIM
    o_ref[:, :nqa] = jnp.concatenate(heads, axis=0).T.astype(o_ref.dtype)
    lam = _diff_lambda(lamqk_ref[...], lam_init)
    for h in range(B_HEADS):
        ot = _softmax_values([(scores_b[h], vbt1[h])])
        o_ref[:, nqa + h * LANES:nqa + (h + 1) * LANES] = (
            _diff_combine(ot, lam, subln_ref[...], lam_init).astype(o_ref.dtype))


def _ctx_attention(qa, ka, va, qb, kb, vb, sink, lam_qk, subln, lam_init):
    B, T, _ = qa.shape
    blk = lambda w: pl.BlockSpec((None, T, w), lambda b: (b, 0, 0))
    full = lambda a: pl.BlockSpec(a.shape, lambda b: (0,) * a.ndim)
    return pl.pallas_call(
        functools.partial(_ctx_attn_kernel, lam_init=lam_init),
        grid=(B,),
        in_specs=[pl.BlockSpec(memory_space=pltpu.SMEM), blk(QA_W), blk(KA_W), blk(KA_W), blk(B_W), blk(B_W),
                  blk(B_W), full(lam_qk), full(subln)],
        out_specs=blk(A_HEADS * HEAD_DIM + B_W),
        out_shape=jax.ShapeDtypeStruct((B, T, A_HEADS * HEAD_DIM + B_W), BF16),
        compiler_params=_cparams("parallel"),
        name="ctx_attention",
    )(sink, qa, ka, va, qb, kb, vb, lam_qk, subln)


def _lat_a_kernel(sink_ref, qa_ref, k_ref, vt_ref, ck_ref, cvt_ref, o_ref, *, tq, nq, band):
    T = k_ref.shape[0]
    cols = A_GROUP * tq
    cvt1 = _with_ones_rows(cvt_ref[...])
    jobs = []
    for i in range(nq):
        qi = pl.program_id(1) * nq + i
        start = pl.multiple_of(jnp.clip(qi * tq - WINDOW, 0, T - band), WINDOW)
        kb = k_ref[pl.ds(start, band), :]
        vbt1 = _with_ones_rows(vt_ref[:, pl.ds(start, band)])
        kpos = start + lax.broadcasted_iota(jnp.int32, (band, cols), 0)
        qpos = qi * tq + lax.broadcasted_iota(jnp.int32, (band, cols), 1) % tq
        keep = jnp.abs(qpos - kpos) <= WINDOW
        for g in range(A_KV_HEADS):
            qg = _stack_group_queries(qa_ref, g, slice(i * tq, (i + 1) * tq))
            jobs.append((_dot_nt(ck_ref[...], qg), _dot_nt(kb, qg), keep, vbt1))
    for i in range(nq):
        heads = []
        for g in range(A_KV_HEADS):
            s_c, s_b, keep, vbt1 = jobs[i * A_KV_HEADS + g]
            ot = _softmax_values([(s_c, cvt1), (jnp.where(keep, s_b, NEG), vbt1)], _sink_row(sink_ref, g, tq))
            heads += _a_heads(ot, g, tq)
        o_ref[i * tq:(i + 1) * tq, :] = jnp.concatenate(heads, axis=0).T.astype(o_ref.dtype)


def _lat_a_attention(qa, ka, vat, cka, cvat, sink):
    B, T, _ = qa.shape
    L = cka.shape[1]
    tq, nq = WINDOW, 8
    band = 3 * WINDOW
    return pl.pallas_call(
        functools.partial(_lat_a_kernel, tq=tq, nq=nq, band=band),
        grid=(B, T // (tq * nq)),
        in_specs=[
            pl.BlockSpec(memory_space=pltpu.SMEM),
            pl.BlockSpec((None, tq * nq, QA_W), lambda b, i: (b, i, 0)),
            pl.BlockSpec((None, T, KA_W), lambda b, i: (b, 0, 0)),
            pl.BlockSpec((None, KA_W, T), lambda b, i: (b, 0, 0)),
            pl.BlockSpec((None, L, KA_W), lambda b, i: (b, 0, 0)),
            pl.BlockSpec((None, KA_W, L), lambda b, i: (b, 0, 0)),
        ],
        out_specs=pl.BlockSpec((None, tq * nq, A_HEADS * HEAD_DIM), lambda b, i: (b, i, 0)),
        out_shape=jax.ShapeDtypeStruct((B, T, A_HEADS * HEAD_DIM), BF16),
        compiler_params=_cparams("parallel", "parallel"),
        name="lat_a_attention",
    )(sink, qa, ka, vat, cka, cvat)


def _lat_b_kernel(qt_ref, k_ref, vt_ref, ck_ref, cvt_ref, lamqk_ref, subln_ref, o_ref, *, tk, ahead, lam_init):
    tq = qt_ref.shape[1]
    T = k_ref.shape[0]
    qt = qt_ref[...]
    zero = jnp.zeros((HEAD_DIM, tq), BF16)
    qst = jnp.concatenate([jnp.concatenate([qt[:HEAD_DIM], zero], axis=0),
                           jnp.concatenate([zero, qt[HEAD_DIM:]], axis=0)], axis=1)
    blocks = [(ck_ref[...], cvt_ref[...])]
    blocks += [(k_ref[j * tk:(j + 1) * tk, :], vt_ref[:, j * tk:(j + 1) * tk]) for j in range(T // tk)]

    def scores(k):
        return _dot(k, qst)

    def accumulate(s, vt, carry):
        m, acc = carry
        m_new = jnp.maximum(m, jnp.max(s, axis=0, keepdims=True))
        p = jnp.exp2(s - m_new).astype(BF16)
        acc = jnp.exp2(m - m_new) * acc + _dot(_with_ones_rows(vt), p)
        return m_new, acc

    carry = (jnp.full((1, 2 * tq), -jnp.inf, F32), jnp.zeros((LANES + ONES_ROWS, 2 * tq), F32))
    pending = [scores(blocks[j][0]) for j in range(ahead)]
    for j in range(len(blocks)):
        if j + ahead < len(blocks):
            pending.append(scores(blocks[j + ahead][0]))
        carry = accumulate(pending.pop(0), blocks[j][1], carry)
    _, acc = carry
    ot = acc[:LANES] / acc[LANES:LANES + 1]
    lam = _diff_lambda(lamqk_ref[...], lam_init)
    o_ref[...] = _diff_combine(ot, lam, subln_ref[...], lam_init).astype(o_ref.dtype)


def _lat_b_attention(qbt, kb, vbt, ckb, cvbt, lam_qk, subln, lam_init):
    B, T, _ = kb.shape
    L = ckb.shape[1]
    tq = 1024
    full = lambda a: pl.BlockSpec(a.shape, lambda b, h, i: (0,) * a.ndim)
    return pl.pallas_call(
        functools.partial(_lat_b_kernel, tk=512, ahead=2, lam_init=lam_init),
        grid=(B, B_HEADS, T // tq),
        in_specs=[
            pl.BlockSpec((None, LANES, tq), lambda b, h, i: (b, h, i)),
            pl.BlockSpec((None, T, LANES), lambda b, h, i: (b, 0, h)),
            pl.BlockSpec((None, LANES, T), lambda b, h, i: (b, h, 0)),
            pl.BlockSpec((None, L, LANES), lambda b, h, i: (b, 0, h)),
            pl.BlockSpec((None, LANES, L), lambda b, h, i: (b, h, 0)),
            full(lam_qk), full(subln),
        ],
        out_specs=pl.BlockSpec((None, tq, LANES), lambda b, h, i: (b, i, h)),
        out_shape=jax.ShapeDtypeStruct((B, T, B_W), BF16),
        compiler_params=_cparams("parallel", "parallel", "parallel"),
        name="lat_b_attention",
    )(qbt, kb, vbt, ckb, cvbt, lam_qk, subln)


def _scan_chunk_len(n_steps):
    chunk = -(-n_steps // SUBLANES)
    while chunk % 8 != 4:
        chunk += 1
    return chunk


class _RecBlock(NamedTuple):
    gate: object
    xr: object
    cw: object
    cb: object
    wbd: object
    bbd: object
    lam: object
    h0f: object
    h0b: object
    y: object
    sf: object
    sb: object
    xp: object
    af: object
    uf: object
    ab: object
    ub: object
    pf: object
    hf: object
    pb: object
    hb: object


def _rec_kernel(gate_ref, xr_ref, cw_ref, cb_ref, wbd_ref, bbd_ref, lam_ref, h0f_ref, h0b_ref,
                y_ref, sf_ref, sb_ref, *scratch, nblk, tc, gates_unroll, chunk, unroll):
    blocks = []
    for n in range(nblk):
        sl = slice(n * LANES, (n + 1) * LANES)
        lanes = [r.at[:, sl] for r in (gate_ref, xr_ref, cw_ref, cb_ref)]
        lanes += [wbd_ref.at[n], bbd_ref.at[n]]
        lanes += [r.at[:, sl] for r in (lam_ref, h0f_ref, h0b_ref, y_ref, sf_ref, sb_ref)]
        xp, af, uf, ab, ub = [r.at[n] for r in scratch]
        blocks.append(_RecBlock(*lanes, xp, af, uf, ab, ub, pf=af, hf=uf, pb=ab, hb=ub))
    for blk in blocks:
        _rec_gates(blk, tc=tc, gates_unroll=gates_unroll, chunk=chunk)
    _rec_scan(blocks, chunk=chunk, unroll=unroll)
    for blk in blocks:
        _rec_combine(blk, tc=tc)


def _rec_gates(blk, *, tc, gates_unroll, chunk):
    T = blk.xr.shape[0]
    pad = SUBLANES
    xp = blk.xp
    xp[0:pad, :] = jnp.zeros((pad, LANES), F32)
    xp[T + pad:T + 2 * pad, :] = jnp.zeros((pad, LANES), F32)
    xp[pad:T + pad, :] = blk.xr[...]
    tail = SUBLANES * chunk - T
    for a_s, u_s in ((blk.af, blk.uf), (blk.ab, blk.ub)):
        a_s[T:T + tail, :] = jnp.ones((tail, LANES), F32)
        u_s[T:T + tail, :] = jnp.zeros((tail, LANES), F32)
    cw = blk.cw[...]
    cb = blk.cb[...]
    nl = -blk.lam[...]
    softplus = jnp.maximum(nl, 0.0) + jnp.log1p(jnp.exp(-jnp.abs(nl)))
    cp = (0.5 * RGLRU_C) * softplus

    def gates(it, _):
        for q in range(gates_unroll):
            gate_chunk(it * gates_unroll + q)
        return 0

    def gate_chunk(ci):
        t0 = pl.multiple_of(ci * tc, tc)
        y = cb
        for j in range(CONV_W):
            y = y + xp[pl.ds(t0 + (pad - CONV_LEFT + j), tc), :] * cw[j:j + 1, :]
        t = jnp.tanh(_dot(y.astype(BF16), blk.wbd[...]) + blk.bbd[...])
        yh = 0.5 * y
        for d, (a_s, u_s) in enumerate(((blk.af, blk.uf), (blk.ab, blk.ub))):
            t_r = t[:, 2 * d * LANES:(2 * d + 1) * LANES]
            t_i = t[:, (2 * d + 1) * LANES:(2 * d + 2) * LANES]
            neg_log_a = cp[d:d + 1, :] * t_r + cp[d:d + 1, :]
            a = jnp.exp2(neg_log_a * (-LOG2E))
            w = jnp.tanh(neg_log_a) * (a * a + 1.0)
            sqrt_w = jnp.where(w > 0.0, w * lax.rsqrt(w), 0.0)
            a_s[pl.ds(t0, tc), :] = a
            u_s[pl.ds(t0, tc), :] = (t_i + 1.0) * (yh * sqrt_w)

    lax.fori_loop(0, T // (tc * gates_unroll), gates, 0)


def _rec_scan(blocks, *, chunk, unroll):
    T = blocks[0].xr.shape[0]

    def rows(k):
        return pl.ds(k, SUBLANES, stride=chunk)

    def local_scan(it, carry):
        carry = list(carry)
        for q in range(unroll):
            k = it * unroll + q
            kb = chunk - 1 - k
            for i, blk in enumerate(blocks):
                hf, pf, hb, pb = carry[4 * i:4 * i + 4]
                a = blk.af[rows(k), :]
                hf = a * hf + blk.uf[rows(k), :]
                pf = a * pf
                blk.hf[rows(k), :] = hf
                blk.pf[rows(k), :] = pf
                a = blk.ab[rows(kb), :]
                hb = a * hb + blk.ub[rows(kb), :]
                pb = a * pb
                blk.hb[rows(kb), :] = hb
                blk.pb[rows(kb), :] = pb
                carry[4 * i:4 * i + 4] = [hf, pf, hb, pb]
        return tuple(carry)

    zero = jnp.zeros((SUBLANES, LANES), F32)
    one = jnp.ones((SUBLANES, LANES), F32)
    ends = lax.fori_loop(0, chunk // unroll, local_scan, (zero, one, zero, one) * len(blocks))

    carries = []
    for i, blk in enumerate(blocks):
        hf, pf, hb, pb = ends[4 * i:4 * i + 4]
        cf = [blk.h0f[...]]
        for r in range(SUBLANES - 1):
            cf.append(hf[r:r + 1, :] + pf[r:r + 1, :] * cf[r])
        cb_rev = [blk.h0b[...]]
        for r in range(SUBLANES - 1, 0, -1):
            cb_rev.append(hb[r:r + 1, :] + pb[r:r + 1, :] * cb_rev[-1])
        carries.append((jnp.concatenate(cf, axis=0), jnp.concatenate(cb_rev[::-1], axis=0)))

    def apply_carry(it, _):
        for q in range(unroll):
            k = it * unroll + q
            for blk, (carry_f, carry_b) in zip(blocks, carries):
                blk.uf[rows(k), :] = blk.hf[rows(k), :] + blk.pf[rows(k), :] * carry_f
                blk.ub[rows(k), :] = blk.hb[rows(k), :] + blk.pb[rows(k), :] * carry_b
        return 0

    lax.fori_loop(0, chunk // unroll, apply_carry, 0)
    for blk in blocks:
        blk.sf[...] = blk.uf[T - 1:T, :]
        blk.sb[...] = blk.ub[0:1, :]


def _rec_combine(blk, *, tc):
    T = blk.xr.shape[0]

    def combine(ci, _):
        t0 = pl.multiple_of(ci * tc, tc)
        blk.y[pl.ds(t0, tc), :] = ((blk.uf[pl.ds(t0, tc), :] + blk.ub[pl.ds(t0, tc), :])
                                   * blk.gate[pl.ds(t0, tc), :]).astype(blk.y.dtype)
        return 0

    lax.fori_loop(0, T // tc, combine, 0)


def _rec_mixer(gate, xr, h0f, h0b, conv_w, conv_b, w_bd, b_bd, lam, *, nblk, tc, name):
    B, T, _ = xr.shape
    chunk = _scan_chunk_len(T)
    wid = nblk * LANES
    col = lambda rows: pl.BlockSpec((rows, wid), lambda b, n: (0, n))
    seq = pl.BlockSpec((None, T, wid), lambda b, n: (b, 0, n))
    st = pl.BlockSpec((None, 1, wid), lambda b, n: (b, 0, n))
    return pl.pallas_call(
        functools.partial(_rec_kernel, nblk=nblk, tc=tc, gates_unroll=2, chunk=chunk,
                          unroll=max(u for u in range(1, SCAN_MAX_UNROLL + 1) if chunk % u == 0)),
        grid=(B, RNN_BLOCKS // nblk),
        in_specs=[seq, seq, col(CONV_W), col(1),
                  pl.BlockSpec((nblk, LANES, 4 * LANES), lambda b, n: (n, 0, 0)),
                  pl.BlockSpec((nblk, 1, 4 * LANES), lambda b, n: (n, 0, 0)),
                  col(2), st, st],
        out_specs=[seq, st, st],
        out_shape=[jax.ShapeDtypeStruct((B, T, D_RNN), BF16),
                   jax.ShapeDtypeStruct((B, 1, D_RNN), F32),
                   jax.ShapeDtypeStruct((B, 1, D_RNN), F32)],
        scratch_shapes=[pltpu.VMEM((nblk, T + 2 * SUBLANES, LANES), F32)]
        + [pltpu.VMEM((nblk, SUBLANES * chunk, LANES), F32)] * 4,
        compiler_params=_cparams("parallel", "parallel"),
        name=name,
    )(gate, xr, conv_w, conv_b.reshape(1, D_RNN), w_bd, b_bd, lam, h0f, h0b)


def _post_kernel(*refs, n_mix, final, fc):
    x_ref, mod_ref, g2_ref = refs[:3]
    mix = refs[3:3 + 2 * n_mix]
    w1_ref, w2_ref = refs[3 + 2 * n_mix:5 + 2 * n_mix]
    rest = refs[5 + 2 * n_mix:]
    if final:
        gf_ref, o_ref = rest
    else:
        (o_ref,) = rest
    mixed = _dot(mix[0][...], mix[1][...])
    for i in range(1, n_mix):
        mixed = mixed + _dot(mix[2 * i][...], mix[2 * i + 1][...])
    x1 = x_ref[...] + mod_ref[2:3, :] * mixed
    h = _rms(x1, g2_ref[...])
    hb = (h * (1.0 + mod_ref[4:5, :]) + mod_ref[3:4, :]).astype(BF16)
    n_chunks = D_FF // fc
    up = lambda c: _dot(hb, w1_ref[:, c * fc:(c + 1) * fc])
    acc = None
    a = up(0)
    for c in range(n_chunks):
        a_next = up(c + 1) if c + 1 < n_chunks else None
        part = _dot(jnp.square(jnp.maximum(a, 0.0)).astype(BF16), w2_ref[c * fc:(c + 1) * fc, :])
        acc = part if acc is None else acc + part
        a = a_next
    x2 = x1 + mod_ref[5:6, :] * acc
    if final:
        x2 = _rms(x2, gf_ref[...])
    o_ref[...] = x2


def _post(x, mod, g2, mixes, w1, w2, *, layer, tm, ctx, final_g=None, name):
    B, T, D = x.shape
    row = (lambda b, i: (CTX_ROW, 0, 0)) if ctx else (lambda b, i: (b, 0, 0))
    const = lambda a: pl.BlockSpec(a.shape, lambda b, i: (0,) * a.ndim, pipeline_mode=pl.Buffered(1))
    slab = lambda a: pl.BlockSpec((None,) + a.shape[1:], lambda b, i: (layer, 0, 0), pipeline_mode=pl.Buffered(1))
    in_specs = [
        pl.BlockSpec((None, tm, D), lambda b, i: (b, i, 0)),
        pl.BlockSpec((None, 6, D), row),
        pl.BlockSpec((1, D), lambda b, i: (0, 0)),
    ]
    args = [x, mod, g2.reshape(1, D)]
    for o, w in mixes:
        in_specs += [pl.BlockSpec((None, tm, o.shape[-1]), lambda b, i: (b, i, 0)), const(w)]
        args += [o, w]
    in_specs += [slab(w1), slab(w2)]
    args += [w1, w2]
    if final_g is not None:
        in_specs.append(pl.BlockSpec((1, D), lambda b, i: (0, 0)))
        args.append(final_g.reshape(1, D))
    return pl.pallas_call(
        functools.partial(_post_kernel, n_mix=len(mixes), final=final_g is not None, fc=1024),
        grid=(B, T // tm),
        in_specs=in_specs,
        out_specs=pl.BlockSpec((None, tm, D), lambda b, i: (b, i, 0)),
        out_shape=jax.ShapeDtypeStruct((B, T, D), F32),
        compiler_params=_cparams("parallel", "parallel"),
        name=name,
    )(*args)


def _rope_tables(n_tokens):
    rows = n_tokens // GRID_W
    r, cl = jnp.meshgrid(jnp.arange(rows, dtype=F32), jnp.arange(GRID_W, dtype=F32), indexing='ij')
    quarter = HEAD_DIM // 4
    inv = ROPE_BASE ** (-jnp.arange(quarter, dtype=F32) / quarter)
    ang = jnp.stack([r.reshape(-1)[:, None] * inv, cl.reshape(-1)[:, None] * inv], axis=1)
    cos, sin = jnp.cos(ang), jnp.sin(ang)
    cos64 = jnp.concatenate([cos[:, 0], cos[:, 0], cos[:, 1], cos[:, 1]], axis=-1)
    sin64 = jnp.concatenate([-sin[:, 0], sin[:, 0], -sin[:, 1], sin[:, 1]], axis=-1)
    return jnp.tile(cos64, (1, LANES // HEAD_DIM)), jnp.tile(sin64, (1, LANES // HEAD_DIM))


def _att_in_weights(w_in):
    d = w_in.shape[0]
    nq = A_HEADS * HEAD_DIM
    wq = w_in[:, :nq].reshape(d, A_HEADS, HEAD_DIM)
    z = jnp.zeros_like(wq)
    in_first = (jnp.arange(A_HEADS) // A_GROUP == 0)[None, :, None]
    wq = jnp.where(in_first, jnp.concatenate([wq, z], axis=-1), jnp.concatenate([z, wq], axis=-1))
    return jnp.concatenate([wq.reshape(d, QA_W), w_in[:, nq:]], axis=1).astype(BF16)


def _block_diag_weights(w_a, b_a, w_x, b_x):
    w = jnp.concatenate([w_a[0], w_x[0], w_a[1], w_x[1]], axis=-1)
    b = jnp.concatenate([v.reshape(RNN_BLOCKS, 1, RNN_BW) for v in (b_a[0], b_x[0], b_a[1], b_x[1])], axis=-1)
    return (0.5 * w).astype(BF16), 0.5 * b


def kernel(x_prompt, x_sample, cache_a_k, cache_a_v, cache_b_k, cache_b_v, state_fwd, state_bwd, c, c_ctx, norm1, norm2, w_ada, b_ada, w_mlp1, w_mlp2, att_w_in, att_w_out, att_sink, att_lam_qk, att_subln, rec_w_in, rec_conv_w, rec_conv_b, rec_w_a, rec_b_a, rec_w_x, rec_b_x, rec_lam, rec_w_out, final_norm):
    nb, n_seq, _ = x_prompt.shape
    nd, d_seq, _ = x_sample.shape
    past = cache_a_k.shape[2]
    assert nd <= CTX_ROW and DEPTH == 2
    cvec = jnp.concatenate([c, jnp.zeros((CTX_ROW - nd, D_MODEL), F32), c_ctx[None],
                            jnp.zeros((MOD_ROWS - CTX_ROW - 1, D_MODEL), F32)], axis=0)
    mod = _ada_mod(cvec, w_ada, b_ada)
    w1 = w_mlp1.astype(BF16)
    w2 = w_mlp2.astype(BF16)
    tm_proj, tm_post, tm_ctx = 512, 1024, 512
    tok = lambda a, tm: a.reshape(-1, tm, a.shape[-1])
    seq = lambda a: a.reshape(nb, n_seq, a.shape[-1])

    lam_init = 0.8 - 0.6 * math.exp(-0.3 * 0)
    w_in = _att_in_weights(att_w_in[0])
    w_out = att_w_out[0].astype(BF16)
    nqa = A_HEADS * HEAD_DIM
    c0 = [0, QA_W, QA_W + KA_W, QA_W + 2 * KA_W, QA_W + 2 * KA_W + B_W, QA_W + 2 * KA_W + 2 * B_W]
    widths = [QA_W, KA_W, KA_W, B_W, B_W, B_W]
    scales = [SCALE * LOG2E, 1.0, 1.0, SCALE * LOG2E, 1.0, 1.0]
    roped = [True, True, False, True, True, False]
    sink = att_sink[0]
    lam_qk = att_lam_qk[0]
    subln = att_subln[0].reshape(1, 2 * HEAD_DIM)

    rows = lambda dt: (("rows", dt),)
    cache_a = ((("seqcols", n_seq), F32), ("rows", BF16))
    cache_b = ((("heads", n_seq), F32), ("rows", BF16))
    segs_ctx = [_Seg(c0[0], QA_W, rows(BF16), scale=scales[0]), _Seg(c0[1], KA_W, cache_a),
                _Seg(c0[2], KA_W, cache_a), _Seg(c0[3], B_W, rows(BF16), scale=scales[3]),
                _Seg(c0[4], B_W, cache_b), _Seg(c0[5], B_W, cache_b)]
    qa, kat, ka, vat, va, qb, new_b_k, kb, new_b_v, vb = _proj(tok(x_prompt, tm_proj), mod[0], norm1[0], w_in, segs_ctx,
                                                               tm=tm_proj, ctx=True, name="proj_att_ctx")
    qa, ka, va, qb, kb, vb = map(seq, (qa, ka, va, qb, kb, vb))
    uncol = lambda a: jnp.transpose(a.reshape(nb, A_KV_HEADS, HEAD_DIM, n_seq), (0, 3, 1, 2))[:, None]
    new_a_k, new_a_v = uncol(kat), uncol(vat)
    o_ctx = _ctx_attention(qa, ka, va, qb, kb, vb, sink, lam_qk, subln, lam_init)
    xp = _post(tok(x_prompt, tm_ctx), mod[0], norm2[0], [(tok(o_ctx, tm_ctx), w_out)], w1, w2, layer=0, tm=tm_ctx, ctx=True,
               name="post_att_ctx")

    segs_lat = [_Seg(c0[i], widths[i], (("cols" if i in (2, 3, 5) else "rows", BF16),), rope=roped[i],
                     scale=scales[i]) for i in range(6)]
    qa, ka, vat, qbt, kb, vbt = _proj(x_sample, mod[0], norm1[0], w_in, segs_lat, tm=tm_proj, ctx=False,
                                      rope_tabs=_rope_tables(d_seq), name="proj_att_lat")
    cka = cache_a_k[:, 0].reshape(nd, past, KA_W).astype(BF16)
    cvat = jnp.swapaxes(cache_a_v[:, 0].reshape(nd, past, KA_W), 1, 2).astype(BF16)
    ckb = cache_b_k[:, 0].reshape(nd, past, B_W).astype(BF16)
    cvbt = jnp.swapaxes(cache_b_v[:, 0].reshape(nd, past, B_W), 1, 2).astype(BF16)
    oa = _lat_a_attention(qa, ka, vat, cka, cvat, sink)
    ob = _lat_b_attention(qbt, kb, vbt, ckb, cvbt, lam_qk, subln, lam_init)
    xs = _post(x_sample, mod[0], norm2[0], [(oa, w_out[:nqa]), (ob, w_out[nqa:])], w1, w2, layer=0, tm=tm_post,
               ctx=False, name="post_att_lat")

    w_rin = rec_w_in[0].astype(BF16)
    w_rout = rec_w_out[0].astype(BF16)
    w_bd, b_bd = _block_diag_weights(rec_w_a[0], rec_b_a[0], rec_w_x[0], rec_b_x[0])
    segs_rec = [_Seg(0, D_RNN, rows(F32), gelu=True), _Seg(D_RNN, D_RNN, rows(F32))]
    zeros = jnp.zeros((nb, 1, D_RNN), F32)

    gate, xr = map(seq, _proj(tok(xp, tm_proj), mod[1], norm1[1], w_rin, segs_rec, tm=tm_proj, ctx=True,
                              name="proj_rec_ctx"))
    y, sf, sb = _rec_mixer(gate, xr, zeros, zeros, rec_conv_w[0], rec_conv_b[0], w_bd, b_bd, rec_lam[0],
                           nblk=RNN_BLOCKS, tc=128, name="rec_mixer_ctx")
    y_prompt = seq(_post(xp, mod[1], norm2[1], [(tok(y, tm_ctx), w_rout)], w1, w2, layer=1, tm=tm_ctx, ctx=True,
                         final_g=final_norm, name="post_rec_ctx"))

    gate, xr = _proj(xs, mod[1], norm1[1], w_rin, segs_rec, tm=tm_proj, ctx=False, name="proj_rec_lat")
    y, _, _ = _rec_mixer(gate, xr, state_fwd[:, 0:1], state_bwd[:, 0:1], rec_conv_w[0], rec_conv_b[0], w_bd, b_bd,
                         rec_lam[0], nblk=2, tc=256, name="rec_mixer_lat")
    y_sample = _post(xs, mod[1], norm2[1], [(y, w_rout)], w1, w2, layer=1, tm=tm_post, ctx=False, final_g=final_norm,
                     name="post_rec_lat")

    return (y_prompt, y_sample, new_a_k, new_a_v, new_b_k, new_b_v, sf, sb)
```

```python
import functools
import math
from typing import NamedTuple

import jax
import jax.numpy as jnp
import numpy as np
from jax import lax
from jax.experimental import pallas as pl
from jax.experimental.pallas import tpu as pltpu

F32 = jnp.float32
BF16 = jnp.bfloat16

LANES = 128
SUBLANES = 8
VMEM_LIMIT_BYTES = 56 * 1024 * 1024

D_MODEL = 1024
DEPTH = 2
GRID_W = 64
HEAD_DIM = 64
A_HEADS = 8
A_KV_HEADS = 2
A_GROUP = A_HEADS // A_KV_HEADS
B_HEADS = 4
WINDOW = 128
ROPE_BASE = 10000.0
D_RNN = 1280
RNN_BLOCKS = 10
RNN_BW = D_RNN // RNN_BLOCKS
CONV_W = 4
CONV_LEFT = (CONV_W - 1) // 2
RGLRU_C = 8.0
D_FF = 4 * D_MODEL
EPS = 1e-6
SCALE = HEAD_DIM ** -0.5
NEG = -1e30

QA_W = A_HEADS * LANES
KA_W = A_KV_HEADS * HEAD_DIM
B_W = B_HEADS * 2 * HEAD_DIM
MOD_ROWS = 8
CTX_ROW = 4
LOG2E = math.log2(math.e)
ONES_ROWS = 16
PROJ_ROW_GROUPS = 2
SCAN_MAX_UNROLL = 1024


def _cparams(*semantics):
    return pltpu.CompilerParams(dimension_semantics=semantics, vmem_limit_bytes=VMEM_LIMIT_BYTES)


def _dot(a, b):
    return jnp.dot(a, b, preferred_element_type=F32)


def _dot_nt(a, b):
    return lax.dot_general(a, b, (((1,), (1,)), ((), ())), preferred_element_type=F32)


def _rms(x, g):
    return x * lax.rsqrt(jnp.mean(x * x, axis=-1, keepdims=True) + EPS) * g


def _gelu_tanh(x):
    return x * (0.5 * (1.0 + jnp.tanh(math.sqrt(2.0 / math.pi) * (x + 0.044715 * (x * x * x)))))


def _ada_kernel(c_ref, w_ref, b_ref, o_ref):
    c = c_ref[...]
    s = c * jax.nn.sigmoid(c)
    o_ref[...] = _dot(s.astype(BF16), w_ref[...].astype(BF16)) + b_ref[...]


def _ada_mod(cvec, w_ada, b_ada):
    tn = 1536
    out = pl.pallas_call(
        _ada_kernel,
        grid=(DEPTH, 6 * D_MODEL // tn),
        in_specs=[
            pl.BlockSpec((MOD_ROWS, D_MODEL), lambda l, j: (0, 0)),
            pl.BlockSpec((None, D_MODEL, tn), lambda l, j: (l, 0, j)),
            pl.BlockSpec((None, 1, tn), lambda l, j: (l, 0, j)),
        ],
        out_specs=pl.BlockSpec((None, MOD_ROWS, tn), lambda l, j: (l, 0, j)),
        out_shape=jax.ShapeDtypeStruct((DEPTH, MOD_ROWS, 6 * D_MODEL), F32),
        compiler_params=_cparams("parallel", "parallel"),
        name="ada_mod",
    )(cvec, w_ada, b_ada.reshape(DEPTH, 1, 6 * D_MODEL))
    return out.reshape(DEPTH, MOD_ROWS, 6, D_MODEL)


class _Seg(NamedTuple):
    col0: int
    width: int
    outs: tuple
    rope: bool = False
    scale: float = 1.0
    gelu: bool = False


def _proj_kernel(*refs, segs, rope):
    if rope:
        x_ref, mod_ref, g_ref, w_ref, cos_ref, sin_ref, *outs = refs
    else:
        x_ref, mod_ref, g_ref, w_ref, *outs = refs
    tm = x_ref.shape[0]
    rows = tm // PROJ_ROW_GROUPS
    groups = [slice(i * rows, (i + 1) * rows) for i in range(PROJ_ROW_GROUPS)]
    hbs = []
    for grp in groups:
        h = _rms(x_ref[grp, :], g_ref[...])
        hbs.append((h * (1.0 + mod_ref[1:2, :]) + mod_ref[0:1, :]).astype(BF16))
    if rope:
        first = (lax.broadcasted_iota(jnp.int32, (rows, LANES), 1) & 16) == 0
    outs = iter(outs)
    for seg in segs:
        ys = [_dot(hb, w_ref[:, seg.col0:seg.col0 + seg.width]) for hb in hbs]
        o_refs = [next(outs) for _ in seg.outs]
        for grp, y in zip(groups, ys):
            for t in range(seg.width // LANES):
                lanes = slice(t * LANES, (t + 1) * LANES)
                yt = y[:, lanes]
                if seg.rope:
                    sw = jnp.where(first, pltpu.roll(yt, LANES - 16, 1), pltpu.roll(yt, 16, 1))
                    yt = yt * cos_ref[grp, :] + sw * sin_ref[grp, :]
                if seg.scale != 1.0:
                    yt = yt * seg.scale
                if seg.gelu:
                    yt = _gelu_tanh(yt)
                for (layout, _), o_ref in zip(seg.outs, o_refs):
                    if layout == "cols":
                        o_ref[lanes, grp] = yt.T.astype(o_ref.dtype)
                    elif layout == "rows":
                        o_ref[grp, lanes] = yt.astype(o_ref.dtype)
                    else:
                        kind, seq_len = layout
                        assert rows % seq_len == 0
                        for r in range(rows // seq_len):
                            s = grp.start // seq_len + r
                            ys = yt[r * seq_len:(r + 1) * seq_len]
                            if kind == "heads":
                                o_ref[s, :, t, :] = ys.astype(o_ref.dtype)
                            else:
                                o_ref[s, lanes, :] = ys.T.astype(o_ref.dtype)


def _proj(x, mod, g, w, segs, *, tm, ctx, rope_tabs=None, name):
    B, T, D = x.shape

    def out_spec(width, layout):
        if layout == "cols":
            return pl.BlockSpec((None, width, tm), lambda b, i: (b, 0, i))
        if layout == "rows":
            return pl.BlockSpec((None, tm, width), lambda b, i: (b, i, 0))
        if layout[0] == "heads":
            return pl.BlockSpec((tm // layout[1], None, layout[1], width // LANES, LANES),
                                lambda b, i: (b * (T // tm) + i, 0, 0, 0, 0))
        return pl.BlockSpec((tm // layout[1], width, layout[1]), lambda b, i: (b * (T // tm) + i, 0, 0))

    def out_struct(width, layout, dt):
        if layout == "cols":
            return jax.ShapeDtypeStruct((B, width, T), dt)
        if layout == "rows":
            return jax.ShapeDtypeStruct((B, T, width), dt)
        if layout[0] == "heads":
            return jax.ShapeDtypeStruct((B * T // layout[1], 1, layout[1], width // LANES, LANES), dt)
        return jax.ShapeDtypeStruct((B * T // layout[1], width, layout[1]), dt)

    row = (lambda b, i: (CTX_ROW, 0, 0)) if ctx else (lambda b, i: (b, 0, 0))
    in_specs = [
        pl.BlockSpec((None, tm, D), lambda b, i: (b, i, 0)),
        pl.BlockSpec((None, 6, D), row),
        pl.BlockSpec((1, D), lambda b, i: (0, 0)),
        pl.BlockSpec(w.shape, lambda b, i: (0, 0)),
    ]
    args = [x, mod, g.reshape(1, D), w]
    if rope_tabs is not None:
        in_specs += [pl.BlockSpec((tm, LANES), lambda b, i: (i, 0))] * 2
        args += list(rope_tabs)
    out_specs = [out_spec(s.width, layout) for s in segs for layout, _ in s.outs]
    out_shape = [out_struct(s.width, layout, dt) for s in segs for layout, dt in s.outs]
    return pl.pallas_call(
        functools.partial(_proj_kernel, segs=tuple(segs), rope=rope_tabs is not None),
        grid=(B, T // tm),
        in_specs=in_specs,
        out_specs=out_specs,
        out_shape=out_shape,
        compiler_params=_cparams("parallel", "parallel"),
        name=name,
    )(*args)


def _diff_lambda(lq, lam_init):
    s1 = jnp.sum(lq[0:1, :] * lq[1:2, :], axis=1, keepdims=True)
    s2 = jnp.sum(lq[2:3, :] * lq[3:4, :], axis=1, keepdims=True)
    return jnp.exp(s1) - jnp.exp(s2) + lam_init


def _stack_group_queries(qa_ref, g, rows):
    return jnp.concatenate(
        [qa_ref[rows, (A_GROUP * g + hh) * LANES:(A_GROUP * g + hh + 1) * LANES] for hh in range(A_GROUP)], axis=0)


def _stack_pair_queries(q):
    lo = lax.broadcasted_iota(jnp.int32, q.shape, 1) < HEAD_DIM
    zero = jnp.zeros_like(q)
    return jnp.concatenate([jnp.where(lo, q, zero), jnp.where(lo, zero, q)], axis=0)


def _with_ones_rows(vt):
    return jnp.concatenate([vt, jnp.ones((ONES_ROWS, vt.shape[1]), BF16)], axis=0)


def _sink_row(sink_ref, g, tq):
    return jnp.concatenate(
        [jnp.full((1, tq), sink_ref[A_GROUP * g + hh] * LOG2E, F32) for hh in range(A_GROUP)], axis=1)


def _softmax_values(parts, sink=None):
    m = functools.reduce(jnp.maximum, [jnp.max(s, axis=0, keepdims=True) for s, _ in parts])
    if sink is not None:
        m = jnp.maximum(m, sink)
    ot = sum(_dot(vt1, jnp.exp2(s - m).astype(BF16)) for s, vt1 in parts)
    den = ot[LANES:LANES + 1]
    if sink is not None:
        den = den + jnp.exp2(sink - m)
    return ot[:LANES] / den


def _a_heads(ot, g, tq):
    return [ot[g * HEAD_DIM:(g + 1) * HEAD_DIM, hh * tq:(hh + 1) * tq] for hh in range(A_GROUP)]


def _subln(o, subln, lam_init):
    return _rms(o, subln) * (1.0 - lam_init)


def _diff_combine(ot, lam, subln, lam_init):
    tq = ot.shape[1] // 2
    return _subln((ot[:, :tq] - lam * ot[:, tq:]).T, subln, lam_init)


def _ctx_attn_kernel(sink_ref, qa_ref, ka_ref, va_ref, qb_ref, kb_ref, vb_ref, lamqk_ref, subln_ref, o_ref, *,
                     lam_init):
    T = qa_ref.shape[0]
    ka = ka_ref[...].astype(BF16)
    vat1 = _with_ones_rows(va_ref[...].T.astype(BF16))
    scores_a = [_dot_nt(ka, _stack_group_queries(qa_ref, g, slice(None))) for g in range(A_KV_HEADS)]
    scores_b, vbt1 = [], []
    for h in range(B_HEADS):
        sl = slice(h * LANES, (h + 1) * LANES)
        scores_b.append(_dot_nt(kb_ref[:, sl].astype(BF16), _stack_pair_queries(qb_ref[:, sl])))
        vbt1.append(_with_ones_rows(vb_ref[:, sl].T.astype(BF16)))
    heads = []
    for g in range(A_KV_HEADS):
        heads += _a_heads(_softmax_values([(scores_a[g], vat1)], _sink_row(sink_ref, g, T)), g, T)
    nqa = A_HEADS * HEAD_DIM
    o_ref[:, :nqa] = jnp.concatenate(heads, axis=0).T.astype(o_ref.dtype)
    lam = _diff_lambda(lamqk_ref[...], lam_init)
    for h in range(B_HEADS):
        ot = _softmax_values([(scores_b[h], vbt1[h])])
        o_ref[:, nqa + h * LANES:nqa + (h + 1) * LANES] = (
            _diff_combine(ot, lam, subln_ref[...], lam_init).astype(o_ref.dtype))


def _ctx_attention(qa, ka, va, qb, kb, vb, sink, lam_qk, subln, lam_init):
    B, T, _ = qa.shape
    blk = lambda w: pl.BlockSpec((None, T, w), lambda b: (b, 0, 0))
    full = lambda a: pl.BlockSpec(a.shape, lambda b: (0,) * a.ndim)
    return pl.pallas_call(
        functools.partial(_ctx_attn_kernel, lam_init=lam_init),
        grid=(B,),
        in_specs=[pl.BlockSpec(memory_space=pltpu.SMEM), blk(QA_W), blk(KA_W), blk(KA_W), blk(B_W), blk(B_W),
                  blk(B_W), full(lam_qk), full(subln)],
        out_specs=blk(A_HEADS * HEAD_DIM + B_W),
        out_shape=jax.ShapeDtypeStruct((B, T, A_HEADS * HEAD_DIM + B_W), BF16),
        compiler_params=_cparams("parallel"),
        name="ctx_attention",
    )(sink, qa, ka, va, qb, kb, vb, lam_qk, subln)


def _lat_a_kernel(sink_ref, qa_ref, k_ref, vt_ref, ck_ref, cvt_ref, o_ref, *, tq, nq, band):
    T = k_ref.shape[0]
    cols = A_GROUP * tq
    cvt1 = _with_ones_rows(cvt_ref[...])
    jobs = []
    for i in range(nq):
        qi = pl.program_id(1) * nq + i
        start = pl.multiple_of(jnp.clip(qi * tq - WINDOW, 0, T - band), WINDOW)
        kb = k_ref[pl.ds(start, band), :]
        vbt1 = _with_ones_rows(vt_ref[:, pl.ds(start, band)])
        kpos = start + lax.broadcasted_iota(jnp.int32, (band, cols), 0)
        qpos = qi * tq + lax.broadcasted_iota(jnp.int32, (band, cols), 1) % tq
        keep = jnp.abs(qpos - kpos) <= WINDOW
        for g in range(A_KV_HEADS):
            qg = _stack_group_queries(qa_ref, g, slice(i * tq, (i + 1) * tq))
            jobs.append((_dot_nt(ck_ref[...], qg), _dot_nt(kb, qg), keep, vbt1))
    for i in range(nq):
        heads = []
        for g in range(A_KV_HEADS):
            s_c, s_b, keep, vbt1 = jobs[i * A_KV_HEADS + g]
            ot = _softmax_values([(s_c, cvt1), (jnp.where(keep, s_b, NEG), vbt1)], _sink_row(sink_ref, g, tq))
            heads += _a_heads(ot, g, tq)
        o_ref[i * tq:(i + 1) * tq, :] = jnp.concatenate(heads, axis=0).T.astype(o_ref.dtype)


def _lat_a_attention(qa, ka, vat, cka, cvat, sink):
    B, T, _ = qa.shape
    L = cka.shape[1]
    tq, nq = WINDOW, 8
    band = 3 * WINDOW
    return pl.pallas_call(
        functools.partial(_lat_a_kernel, tq=tq, nq=nq, band=band),
        grid=(B, T // (tq * nq)),
        in_specs=[
            pl.BlockSpec(memory_space=pltpu.SMEM),
            pl.BlockSpec((None, tq * nq, QA_W), lambda b, i: (b, i, 0)),
            pl.BlockSpec((None, T, KA_W), lambda b, i: (b, 0, 0)),
            pl.BlockSpec((None, KA_W, T), lambda b, i: (b, 0, 0)),
            pl.BlockSpec((None, L, KA_W), lambda b, i: (b, 0, 0)),
            pl.BlockSpec((None, KA_W, L), lambda b, i: (b, 0, 0)),
        ],
        out_specs=pl.BlockSpec((None, tq * nq, A_HEADS * HEAD_DIM), lambda b, i: (b, i, 0)),
        out_shape=jax.ShapeDtypeStruct((B, T, A_HEADS * HEAD_DIM), BF16),
        compiler_params=_cparams("parallel", "parallel"),
        name="lat_a_attention",
    )(sink, qa, ka, vat, cka, cvat)


def _lat_b_kernel(qt_ref, k_ref, vt_ref, ck_ref, cvt_ref, lamqk_ref, subln_ref, o_ref, *, tk, ahead, lam_init):
    tq = qt_ref.shape[1]
    T = k_ref.shape[0]
    qt = qt_ref[...]
    zero = jnp.zeros((HEAD_DIM, tq), BF16)
    qst = jnp.concatenate([jnp.concatenate([qt[:HEAD_DIM], zero], axis=0),
                           jnp.concatenate([zero, qt[HEAD_DIM:]], axis=0)], axis=1)
    blocks = [(ck_ref[...], cvt_ref[...])]
    blocks += [(k_ref[j * tk:(j + 1) * tk, :], vt_ref[:, j * tk:(j + 1) * tk]) for j in range(T // tk)]

    def scores(k):
        return _dot(k, qst)

    def accumulate(s, vt, carry):
        m, acc = carry
        m_new = jnp.maximum(m, jnp.max(s, axis=0, keepdims=True))
        p = jnp.exp2(s - m_new).astype(BF16)
        acc = jnp.exp2(m - m_new) * acc + _dot(_with_ones_rows(vt), p)
        return m_new, acc

    carry = (jnp.full((1, 2 * tq), -jnp.inf, F32), jnp.zeros((LANES + ONES_ROWS, 2 * tq), F32))
    pending = [scores(blocks[j][0]) for j in range(ahead)]
    for j in range(len(blocks)):
        if j + ahead < len(blocks):
            pending.append(scores(blocks[j + ahead][0]))
        carry = accumulate(pending.pop(0), blocks[j][1], carry)
    _, acc = carry
    ot = acc[:LANES] / acc[LANES:LANES + 1]
    lam = _diff_lambda(lamqk_ref[...], lam_init)
    o_ref[...] = _diff_combine(ot, lam, subln_ref[...], lam_init).astype(o_ref.dtype)


def _lat_b_attention(qbt, kb, vbt, ckb, cvbt, lam_qk, subln, lam_init):
    B, T, _ = kb.shape
    L = ckb.shape[1]
    tq = 1024
    full = lambda a: pl.BlockSpec(a.shape, lambda b, h, i: (0,) * a.ndim)
    return pl.pallas_call(
        functools.partial(_lat_b_kernel, tk=512, ahead=2, lam_init=lam_init),
        grid=(B, B_HEADS, T // tq),
        in_specs=[
            pl.BlockSpec((None, LANES, tq), lambda b, h, i: (b, h, i)),
            pl.BlockSpec((None, T, LANES), lambda b, h, i: (b, 0, h)),
            pl.BlockSpec((None, LANES, T), lambda b, h, i: (b, h, 0)),
            pl.BlockSpec((None, L, LANES), lambda b, h, i: (b, 0, h)),
            pl.BlockSpec((None, LANES, L), lambda b, h, i: (b, h, 0)),
            full(lam_qk), full(subln),
        ],
        out_specs=pl.BlockSpec((None, tq, LANES), lambda b, h, i: (b, i, h)),
        out_shape=jax.ShapeDtypeStruct((B, T, B_W), BF16),
        compiler_params=_cparams("parallel", "parallel", "parallel"),
        name="lat_b_attention",
    )(qbt, kb, vbt, ckb, cvbt, lam_qk, subln)


def _scan_chunk_len(n_steps):
    chunk = -(-n_steps // SUBLANES)
    while chunk % 8 != 4:
        chunk += 1
    return chunk


class _RecBlock(NamedTuple):
    gate: object
    xr: object
    cw: object
    cb: object
    wbd: object
    bbd: object
    lam: object
    h0f: object
    h0b: object
    y: object
    sf: object
    sb: object
    xp: object
    af: object
    uf: object
    ab: object
    ub: object
    pf: object
    hf: object
    pb: object
    hb: object


def _rec_kernel(gate_ref, xr_ref, cw_ref, cb_ref, wbd_ref, bbd_ref, lam_ref, h0f_ref, h0b_ref,
                y_ref, sf_ref, sb_ref, *scratch, nblk, tc, gates_unroll, chunk, unroll):
    blocks = []
    for n in range(nblk):
        sl = slice(n * LANES, (n + 1) * LANES)
        lanes = [r.at[:, sl] for r in (gate_ref, xr_ref, cw_ref, cb_ref)]
        lanes += [wbd_ref.at[n], bbd_ref.at[n]]
        lanes += [r.at[:, sl] for r in (lam_ref, h0f_ref, h0b_ref, y_ref, sf_ref, sb_ref)]
        xp, af, uf, ab, ub = [r.at[n] for r in scratch]
        blocks.append(_RecBlock(*lanes, xp, af, uf, ab, ub, pf=af, hf=uf, pb=ab, hb=ub))
    for blk in blocks:
        _rec_gates(blk, tc=tc, gates_unroll=gates_unroll, chunk=chunk)
    _rec_scan(blocks, chunk=chunk, unroll=unroll)
    for blk in blocks:
        _rec_combine(blk, tc=tc)


def _rec_gates(blk, *, tc, gates_unroll, chunk):
    T = blk.xr.shape[0]
    pad = SUBLANES
    xp = blk.xp
    xp[0:pad, :] = jnp.zeros((pad, LANES), F32)
    xp[T + pad:T + 2 * pad, :] = jnp.zeros((pad, LANES), F32)
    xp[pad:T + pad, :] = blk.xr[...]
    tail = SUBLANES * chunk - T
    for a_s, u_s in ((blk.af, blk.uf), (blk.ab, blk.ub)):
        a_s[T:T + tail, :] = jnp.ones((tail, LANES), F32)
        u_s[T:T + tail, :] = jnp.zeros((tail, LANES), F32)
    cw = blk.cw[...]
    cb = blk.cb[...]
    nl = -blk.lam[...]
    softplus = jnp.maximum(nl, 0.0) + jnp.log1p(jnp.exp(-jnp.abs(nl)))
    cp = (0.5 * RGLRU_C) * softplus

    def gates(it, _):
        for q in range(gates_unroll):
            gate_chunk(it * gates_unroll + q)
        return 0

    def gate_chunk(ci):
        t0 = pl.multiple_of(ci * tc, tc)
        y = cb
        for j in range(CONV_W):
            y = y + xp[pl.ds(t0 + (pad - CONV_LEFT + j), tc), :] * cw[j:j + 1, :]
        t = jnp.tanh(_dot(y.astype(BF16), blk.wbd[...]) + blk.bbd[...])
        yh = 0.5 * y
        for d, (a_s, u_s) in enumerate(((blk.af, blk.uf), (blk.ab, blk.ub))):
            t_r = t[:, 2 * d * LANES:(2 * d + 1) * LANES]
            t_i = t[:, (2 * d + 1) * LANES:(2 * d + 2) * LANES]
            neg_log_a = cp[d:d + 1, :] * t_r + cp[d:d + 1, :]
            a = jnp.exp2(neg_log_a * (-LOG2E))
            w = jnp.tanh(neg_log_a) * (a * a + 1.0)
            sqrt_w = jnp.where(w > 0.0, w * lax.rsqrt(w), 0.0)
            a_s[pl.ds(t0, tc), :] = a
            u_s[pl.ds(t0, tc), :] = (t_i + 1.0) * (yh * sqrt_w)

    lax.fori_loop(0, T // (tc * gates_unroll), gates, 0)


def _rec_scan(blocks, *, chunk, unroll):
    T = blocks[0].xr.shape[0]

    def rows(k):
        return pl.ds(k, SUBLANES, stride=chunk)

    def local_scan(it, carry):
        carry = list(carry)
        for q in range(unroll):
            k = it * unroll + q
            kb = chunk - 1 - k
            for i, blk in enumerate(blocks):
                hf, pf, hb, pb = carry[4 * i:4 * i + 4]
                a = blk.af[rows(k), :]
                hf = a * hf + blk.uf[rows(k), :]
                pf = a * pf
                blk.hf[rows(k), :] = hf
                blk.pf[rows(k), :] = pf
                a = blk.ab[rows(kb), :]
                hb = a * hb + blk.ub[rows(kb), :]
                pb = a * pb
                blk.hb[rows(kb), :] = hb
                blk.pb[rows(kb), :] = pb
                carry[4 * i:4 * i + 4] = [hf, pf, hb, pb]
        return tuple(carry)

    zero = jnp.zeros((SUBLANES, LANES), F32)
    one = jnp.ones((SUBLANES, LANES), F32)
    ends = lax.fori_loop(0, chunk // unroll, local_scan, (zero, one, zero, one) * len(blocks))

    carries = []
    for i, blk in enumerate(blocks):
        hf, pf, hb, pb = ends[4 * i:4 * i + 4]
        cf = [blk.h0f[...]]
        for r in range(SUBLANES - 1):
            cf.append(hf[r:r + 1, :] + pf[r:r + 1, :] * cf[r])
        cb_rev = [blk.h0b[...]]
        for r in range(SUBLANES - 1, 0, -1):
            cb_rev.append(hb[r:r + 1, :] + pb[r:r + 1, :] * cb_rev[-1])
        carries.append((jnp.concatenate(cf, axis=0), jnp.concatenate(cb_rev[::-1], axis=0)))

    def apply_carry(it, _):
        for q in range(unroll):
            k = it * unroll + q
            for blk, (carry_f, carry_b) in zip(blocks, carries):
                blk.uf[rows(k), :] = blk.hf[rows(k), :] + blk.pf[rows(k), :] * carry_f
                blk.ub[rows(k), :] = blk.hb[rows(k), :] + blk.pb[rows(k), :] * carry_b
        return 0

    lax.fori_loop(0, chunk // unroll, apply_carry, 0)
    for blk in blocks:
        blk.sf[...] = blk.uf[T - 1:T, :]
        blk.sb[...] = blk.ub[0:1, :]


def _rec_combine(blk, *, tc):
    T = blk.xr.shape[0]

    def combine(ci, _):
        t0 = pl.multiple_of(ci * tc, tc)
        blk.y[pl.ds(t0, tc), :] = ((blk.uf[pl.ds(t0, tc), :] + blk.ub[pl.ds(t0, tc), :])
                                   * blk.gate[pl.ds(t0, tc), :]).astype(blk.y.dtype)
        return 0

    lax.fori_loop(0, T // tc, combine, 0)


def _rec_mixer(gate, xr, h0f, h0b, conv_w, conv_b, w_bd, b_bd, lam, *, nblk, tc, name):
    B, T, _ = xr.shape
    chunk = _scan_chunk_len(T)
    wid = nblk * LANES
    col = lambda rows: pl.BlockSpec((rows, wid), lambda b, n: (0, n))
    seq = pl.BlockSpec((None, T, wid), lambda b, n: (b, 0, n))
    st = pl.BlockSpec((None, 1, wid), lambda b, n: (b, 0, n))
    return pl.pallas_call(
        functools.partial(_rec_kernel, nblk=nblk, tc=tc, gates_unroll=min(8, T // tc), chunk=chunk,
                          unroll=max(u for u in range(1, SCAN_MAX_UNROLL + 1) if chunk % u == 0)),
        grid=(B, RNN_BLOCKS // nblk),
        in_specs=[seq, seq, col(CONV_W), col(1),
                  pl.BlockSpec((nblk, LANES, 4 * LANES), lambda b, n: (n, 0, 0)),
                  pl.BlockSpec((nblk, 1, 4 * LANES), lambda b, n: (n, 0, 0)),
                  col(2), st, st],
        out_specs=[seq, st, st],
        out_shape=[jax.ShapeDtypeStruct((B, T, D_RNN), BF16),
                   jax.ShapeDtypeStruct((B, 1, D_RNN), F32),
                   jax.ShapeDtypeStruct((B, 1, D_RNN), F32)],
        scratch_shapes=[pltpu.VMEM((nblk, T + 2 * SUBLANES, LANES), F32)]
        + [pltpu.VMEM((nblk, SUBLANES * chunk, LANES), F32)] * 4,
        compiler_params=_cparams("parallel", "parallel"),
        name=name,
    )(gate, xr, conv_w, conv_b.reshape(1, D_RNN), w_bd, b_bd, lam, h0f, h0b)


def _post_kernel(*refs, n_mix, final, fc):
    x_ref, mod_ref, g2_ref = refs[:3]
    mix = refs[3:3 + 2 * n_mix]
    w1_ref, w2_ref = refs[3 + 2 * n_mix:5 + 2 * n_mix]
    rest = refs[5 + 2 * n_mix:]
    if final:
        gf_ref, o_ref = rest
    else:
        (o_ref,) = rest
    mixed = _dot(mix[0][...], mix[1][...])
    for i in range(1, n_mix):
        mixed = mixed + _dot(mix[2 * i][...], mix[2 * i + 1][...])
    x1 = x_ref[...] + mod_ref[2:3, :] * mixed
    h = _rms(x1, g2_ref[...])
    hb = (h * (1.0 + mod_ref[4:5, :]) + mod_ref[3:4, :]).astype(BF16)
    n_chunks = D_FF // fc
    up = lambda c: _dot(hb, w1_ref[:, c * fc:(c + 1) * fc])
    acc = None
    a = up(0)
    for c in range(n_chunks):
        a_next = up(c + 1) if c + 1 < n_chunks else None
        part = _dot(jnp.square(jnp.maximum(a, 0.0)).astype(BF16), w2_ref[c * fc:(c + 1) * fc, :])
        acc = part if acc is None else acc + part
        a = a_next
    x2 = x1 + mod_ref[5:6, :] * acc
    if final:
        x2 = _rms(x2, gf_ref[...])
    o_ref[...] = x2


def _post(x, mod, g2, mixes, w1, w2, *, layer, tm, ctx, final_g=None, name):
    B, T, D = x.shape
    row = (lambda b, i: (CTX_ROW, 0, 0)) if ctx else (lambda b, i: (b, 0, 0))
    const = lambda a: pl.BlockSpec(a.shape, lambda b, i: (0,) * a.ndim, pipeline_mode=pl.Buffered(1))
    slab = lambda a: pl.BlockSpec((None,) + a.shape[1:], lambda b, i: (layer, 0, 0), pipeline_mode=pl.Buffered(1))
    in_specs = [
        pl.BlockSpec((None, tm, D), lambda b, i: (b, i, 0)),
        pl.BlockSpec((None, 6, D), row),
        pl.BlockSpec((1, D), lambda b, i: (0, 0)),
    ]
    args = [x, mod, g2.reshape(1, D)]
    for o, w in mixes:
        in_specs += [pl.BlockSpec((None, tm, o.shape[-1]), lambda b, i: (b, i, 0)), const(w)]
        args += [o, w]
    in_specs += [slab(w1), slab(w2)]
    args += [w1, w2]
    if final_g is not None:
        in_specs.append(pl.BlockSpec((1, D), lambda b, i: (0, 0)))
        args.append(final_g.reshape(1, D))
    return pl.pallas_call(
        functools.partial(_post_kernel, n_mix=len(mixes), final=final_g is not None, fc=1024),
        grid=(B, T // tm),
        in_specs=in_specs,
        out_specs=pl.BlockSpec((None, tm, D), lambda b, i: (b, i, 0)),
        out_shape=jax.ShapeDtypeStruct((B, T, D), F32),
        compiler_params=_cparams("parallel", "parallel"),
        name=name,
    )(*args)


def _rope_tables(n_tokens):
    rows = n_tokens // GRID_W
    r, cl = jnp.meshgrid(jnp.arange(rows, dtype=F32), jnp.arange(GRID_W, dtype=F32), indexing='ij')
    quarter = HEAD_DIM // 4
    inv = ROPE_BASE ** (-jnp.arange(quarter, dtype=F32) / quarter)
    ang = jnp.stack([r.reshape(-1)[:, None] * inv, cl.reshape(-1)[:, None] * inv], axis=1)
    cos, sin = jnp.cos(ang), jnp.sin(ang)
    cos64 = jnp.concatenate([cos[:, 0], cos[:, 0], cos[:, 1], cos[:, 1]], axis=-1)
    sin64 = jnp.concatenate([-sin[:, 0], sin[:, 0], -sin[:, 1], sin[:, 1]], axis=-1)
    return jnp.tile(cos64, (1, LANES // HEAD_DIM)), jnp.tile(sin64, (1, LANES // HEAD_DIM))


def _att_in_weights(w_in):
    d = w_in.shape[0]
    nq = A_HEADS * HEAD_DIM
    wq = w_in[:, :nq].reshape(d, A_HEADS, HEAD_DIM)
    z = jnp.zeros_like(wq)
    in_first = (jnp.arange(A_HEADS) // A_GROUP == 0)[None, :, None]
    wq = jnp.where(in_first, jnp.concatenate([wq, z], axis=-1), jnp.concatenate([z, wq], axis=-1))
    return jnp.concatenate([wq.reshape(d, QA_W), w_in[:, nq:]], axis=1).astype(BF16)


def _block_diag_weights(w_a, b_a, w_x, b_x):
    w = jnp.concatenate([w_a[0], w_x[0], w_a[1], w_x[1]], axis=-1)
    b = jnp.concatenate([v.reshape(RNN_BLOCKS, 1, RNN_BW) for v in (b_a[0], b_x[0], b_a[1], b_x[1])], axis=-1)
    return (0.5 * w).astype(BF16), 0.5 * b


def kernel(x_prompt, x_sample, cache_a_k, cache_a_v, cache_b_k, cache_b_v, state_fwd, state_bwd, c, c_ctx, norm1, norm2, w_ada, b_ada, w_mlp1, w_mlp2, att_w_in, att_w_out, att_sink, att_lam_qk, att_subln, rec_w_in, rec_conv_w, rec_conv_b, rec_w_a, rec_b_a, rec_w_x, rec_b_x, rec_lam, rec_w_out, final_norm):
    nb, n_seq, _ = x_prompt.shape
    nd, d_seq, _ = x_sample.shape
    past = cache_a_k.shape[2]
    assert nd <= CTX_ROW and DEPTH == 2
    cvec = jnp.concatenate([c, jnp.zeros((CTX_ROW - nd, D_MODEL), F32), c_ctx[None],
                            jnp.zeros((MOD_ROWS - CTX_ROW - 1, D_MODEL), F32)], axis=0)
    mod = _ada_mod(cvec, w_ada, b_ada)
    w1 = w_mlp1.astype(BF16)
    w2 = w_mlp2.astype(BF16)
    tm_proj, tm_post, tm_ctx = 512, 1024, 512
    tok = lambda a, tm: a.reshape(-1, tm, a.shape[-1])
    seq = lambda a: a.reshape(nb, n_seq, a.shape[-1])

    lam_init = 0.8 - 0.6 * math.exp(-0.3 * 0)
    w_in = _att_in_weights(att_w_in[0])
    w_out = att_w_out[0].astype(BF16)
    nqa = A_HEADS * HEAD_DIM
    c0 = [0, QA_W, QA_W + KA_W, QA_W + 2 * KA_W, QA_W + 2 * KA_W + B_W, QA_W + 2 * KA_W + 2 * B_W]
    widths = [QA_W, KA_W, KA_W, B_W, B_W, B_W]
    scales = [SCALE * LOG2E, 1.0, 1.0, SCALE * LOG2E, 1.0, 1.0]
    roped = [True, True, False, True, True, False]
    sink = att_sink[0]
    lam_qk = att_lam_qk[0]
    subln = att_subln[0].reshape(1, 2 * HEAD_DIM)

    rows = lambda dt: (("rows", dt),)
    cache_a = ((("seqcols", n_seq), F32), ("rows", BF16))
    cache_b = ((("heads", n_seq), F32), ("rows", BF16))
    segs_ctx = [_Seg(c0[0], QA_W, rows(BF16), scale=scales[0]), _Seg(c0[1], KA_W, cache_a),
                _Seg(c0[2], KA_W, cache_a), _Seg(c0[3], B_W, rows(BF16), scale=scales[3]),
                _Seg(c0[4], B_W, cache_b), _Seg(c0[5], B_W, cache_b)]
    qa, kat, ka, vat, va, qb, new_b_k, kb, new_b_v, vb = _proj(tok(x_prompt, tm_proj), mod[0], norm1[0], w_in, segs_ctx,
                                                               tm=tm_proj, ctx=True, name="proj_att_ctx")
    qa, ka, va, qb, kb, vb = map(seq, (qa, ka, va, qb, kb, vb))
    uncol = lambda a: jnp.transpose(a.reshape(nb, A_KV_HEADS, HEAD_DIM, n_seq), (0, 3, 1, 2))[:, None]
    new_a_k, new_a_v = uncol(kat), uncol(vat)
    o_ctx = _ctx_attention(qa, ka, va, qb, kb, vb, sink, lam_qk, subln, lam_init)
    xp = _post(tok(x_prompt, tm_ctx), mod[0], norm2[0], [(tok(o_ctx, tm_ctx), w_out)], w1, w2, layer=0, tm=tm_ctx, ctx=True,
               name="post_att_ctx")

    segs_lat = [_Seg(c0[i], widths[i], (("cols" if i in (2, 3, 5) else "rows", BF16),), rope=roped[i],
                     scale=scales[i]) for i in range(6)]
    qa, ka, vat, qbt, kb, vbt = _proj(x_sample, mod[0], norm1[0], w_in, segs_lat, tm=tm_proj, ctx=False,
                                      rope_tabs=_rope_tables(d_seq), name="proj_att_lat")
    cka = cache_a_k[:, 0].reshape(nd, past, KA_W).astype(BF16)
    cvat = jnp.swapaxes(cache_a_v[:, 0].reshape(nd, past, KA_W), 1, 2).astype(BF16)
    ckb = cache_b_k[:, 0].reshape(nd, past, B_W).astype(BF16)
    cvbt = jnp.swapaxes(cache_b_v[:, 0].reshape(nd, past, B_W), 1, 2).astype(BF16)
    oa = _lat_a_attention(qa, ka, vat, cka, cvat, sink)
    ob = _lat_b_attention(qbt, kb, vbt, ckb, cvbt, lam_qk, subln, lam_init)
    xs = _post(x_sample, mod[0], norm2[0], [(oa, w_out[:nqa]), (ob, w_out[nqa:])], w1, w2, layer=0, tm=tm_post,
               ctx=False, name="post_att_lat")

    w_rin = rec_w_in[0].astype(BF16)
    w_rout = rec_w_out[0].astype(BF16)
    w_bd, b_bd = _block_diag_weights(rec_w_a[0], rec_b_a[0], rec_w_x[0], rec_b_x[0])
    segs_rec = [_Seg(0, D_RNN, rows(F32), gelu=True), _Seg(D_RNN, D_RNN, rows(F32))]
    zeros = jnp.zeros((nb, 1, D_RNN), F32)

    gate, xr = map(seq, _proj(tok(xp, tm_proj), mod[1], norm1[1], w_rin, segs_rec, tm=tm_proj, ctx=True,
                              name="proj_rec_ctx"))
    y, sf, sb = _rec_mixer(gate, xr, zeros, zeros, rec_conv_w[0], rec_conv_b[0], w_bd, b_bd, rec_lam[0],
                           nblk=RNN_BLOCKS, tc=128, name="rec_mixer_ctx")
    y_prompt = seq(_post(xp, mod[1], norm2[1], [(tok(y, tm_ctx), w_rout)], w1, w2, layer=1, tm=tm_ctx, ctx=True,
                         final_g=final_norm, name="post_rec_ctx"))

    gate, xr = _proj(xs, mod[1], norm1[1], w_rin, segs_rec, tm=tm_proj, ctx=False, name="proj_rec_lat")
    y, _, _ = _rec_mixer(gate, xr, state_fwd[:, 0:1], state_bwd[:, 0:1], rec_conv_w[0], rec_conv_b[0], w_bd, b_bd,
                         rec_lam[0], nblk=2, tc=256, name="rec_mixer_lat")
    y_sample = _post(xs, mod[1], norm2[1], [(y, w_rout)], w1, w2, layer=1, tm=tm_post, ctx=False, final_g=final_norm,
                     name="post_rec_lat")

    return (y_prompt, y_sample, new_a_k, new_a_v, new_b_k, new_b_v, sf, sb)
```

```python
import functools
import math
from typing import NamedTuple

import jax
import jax.numpy as jnp
import numpy as np
from jax import lax
from jax.experimental import pallas as pl
from jax.experimental.pallas import tpu as pltpu

F32 = jnp.float32
BF16 = jnp.bfloat16

LANES = 128
SUBLANES = 8
VMEM_LIMIT_BYTES = 56 * 1024 * 1024

D_MODEL = 1024
DEPTH = 2
GRID_W = 64
HEAD_DIM = 64
A_HEADS = 8
A_KV_HEADS = 2
A_GROUP = A_HEADS // A_KV_HEADS
B_HEADS = 4
WINDOW = 128
ROPE_BASE = 10000.0
D_RNN = 1280
RNN_BLOCKS = 10
RNN_BW = D_RNN // RNN_BLOCKS
CONV_W = 4
CONV_LEFT = (CONV_W - 1) // 2
RGLRU_C = 8.0
D_FF = 4 * D_MODEL
EPS = 1e-6
SCALE = HEAD_DIM ** -0.5
NEG = -1e30

QA_W = A_HEADS * LANES
KA_W = A_KV_HEADS * HEAD_DIM
B_W = B_HEADS * 2 * HEAD_DIM
MOD_ROWS = 8
CTX_ROW = 4
LOG2E = math.log2(math.e)
ONES_ROWS = 16
CTX_ATT_REQUESTS = 4
PROJ_ROW_GROUPS = 2
SCAN_MAX_UNROLL = 1024


def _cparams(*semantics):
    return pltpu.CompilerParams(dimension_semantics=semantics, vmem_limit_bytes=VMEM_LIMIT_BYTES)


def _dot(a, b):
    return jnp.dot(a, b, preferred_element_type=F32)


def _dot_nt(a, b):
    return lax.dot_general(a, b, (((1,), (1,)), ((), ())), preferred_element_type=F32)


def _rms(x, g):
    return x * lax.rsqrt(jnp.mean(x * x, axis=-1, keepdims=True) + EPS) * g


def _gelu_tanh(x):
    return x * (0.5 * (1.0 + jnp.tanh(math.sqrt(2.0 / math.pi) * (x + 0.044715 * (x * x * x)))))


def _ada_kernel(c_ref, w_ref, b_ref, o_ref):
    c = c_ref[...]
    s = c * jax.nn.sigmoid(c)
    o_ref[...] = _dot(s.astype(BF16), w_ref[...].astype(BF16)) + b_ref[...]


def _ada_mod(cvec, w_ada, b_ada):
    tn = 1536
    out = pl.pallas_call(
        _ada_kernel,
        grid=(DEPTH, 6 * D_MODEL // tn),
        in_specs=[
            pl.BlockSpec((MOD_ROWS, D_MODEL), lambda l, j: (0, 0)),
            pl.BlockSpec((None, D_MODEL, tn), lambda l, j: (l, 0, j)),
            pl.BlockSpec((None, 1, tn), lambda l, j: (l, 0, j)),
        ],
        out_specs=pl.BlockSpec((None, MOD_ROWS, tn), lambda l, j: (l, 0, j)),
        out_shape=jax.ShapeDtypeStruct((DEPTH, MOD_ROWS, 6 * D_MODEL), F32),
        compiler_params=_cparams("parallel", "parallel"),
        name="ada_mod",
    )(cvec, w_ada, b_ada.reshape(DEPTH, 1, 6 * D_MODEL))
    return out.reshape(DEPTH, MOD_ROWS, 6, D_MODEL)


class _Seg(NamedTuple):
    col0: int
    width: int
    outs: tuple
    rope: bool = False
    scale: float = 1.0
    gelu: bool = False


def _proj_kernel(*refs, segs, rope):
    if rope:
        x_ref, mod_ref, g_ref, w_ref, cos_ref, sin_ref, *outs = refs
    else:
        x_ref, mod_ref, g_ref, w_ref, *outs = refs
    tm = x_ref.shape[0]
    rows = tm // PROJ_ROW_GROUPS
    groups = [slice(i * rows, (i + 1) * rows) for i in range(PROJ_ROW_GROUPS)]
    hbs = []
    for grp in groups:
        h = _rms(x_ref[grp, :], g_ref[...])
        hbs.append((h * (1.0 + mod_ref[1:2, :]) + mod_ref[0:1, :]).astype(BF16))
    if rope:
        first = (lax.broadcasted_iota(jnp.int32, (rows, LANES), 1) & 16) == 0
    outs = iter(outs)
    for seg in segs:
        ys = [_dot(hb, w_ref[:, seg.col0:seg.col0 + seg.width]) for hb in hbs]
        o_refs = [next(outs) for _ in seg.outs]
        for grp, y in zip(groups, ys):
            for t in range(seg.width // LANES):
                lanes = slice(t * LANES, (t + 1) * LANES)
                yt = y[:, lanes]
                if seg.rope:
                    sw = jnp.where(first, pltpu.roll(yt, LANES - 16, 1), pltpu.roll(yt, 16, 1))
                    yt = yt * cos_ref[grp, :] + sw * sin_ref[grp, :]
                if seg.scale != 1.0:
                    yt = yt * seg.scale
                if seg.gelu:
                    yt = _gelu_tanh(yt)
                for (layout, _), o_ref in zip(seg.outs, o_refs):
                    if layout == "cols":
                        o_ref[lanes, grp] = yt.T.astype(o_ref.dtype)
                    elif layout == "rows":
                        o_ref[grp, lanes] = yt.astype(o_ref.dtype)
                    else:
                        kind, seq_len = layout
                        assert rows % seq_len == 0
                        for r in range(rows // seq_len):
                            s = grp.start // seq_len + r
                            ys = yt[r * seq_len:(r + 1) * seq_len]
                            if kind == "heads":
                                o_ref[s, :, t, :] = ys.astype(o_ref.dtype)
                            else:
                                o_ref[s, lanes, :] = ys.T.astype(o_ref.dtype)


def _proj(x, mod, g, w, segs, *, tm, ctx, rope_tabs=None, name):
    B, T, D = x.shape

    def out_spec(width, layout):
        if layout == "cols":
            return pl.BlockSpec((None, width, tm), lambda b, i: (b, 0, i))
        if layout == "rows":
            return pl.BlockSpec((None, tm, width), lambda b, i: (b, i, 0))
        if layout[0] == "heads":
            return pl.BlockSpec((tm // layout[1], None, layout[1], width // LANES, LANES),
                                lambda b, i: (b * (T // tm) + i, 0, 0, 0, 0))
        return pl.BlockSpec((tm // layout[1], width, layout[1]), lambda b, i: (b * (T // tm) + i, 0, 0))

    def out_struct(width, layout, dt):
        if layout == "cols":
            return jax.ShapeDtypeStruct((B, width, T), dt)
        if layout == "rows":
            return jax.ShapeDtypeStruct((B, T, width), dt)
        if layout[0] == "heads":
            return jax.ShapeDtypeStruct((B * T // layout[1], 1, layout[1], width // LANES, LANES), dt)
        return jax.ShapeDtypeStruct((B * T // layout[1], width, layout[1]), dt)

    row = (lambda b, i: (CTX_ROW, 0, 0)) if ctx else (lambda b, i: (b, 0, 0))
    in_specs = [
        pl.BlockSpec((None, tm, D), lambda b, i: (b, i, 0)),
        pl.BlockSpec((None, 6, D), row),
        pl.BlockSpec((1, D), lambda b, i: (0, 0)),
        pl.BlockSpec(w.shape, lambda b, i: (0, 0)),
    ]
    args = [x, mod, g.reshape(1, D), w]
    if rope_tabs is not None:
        in_specs += [pl.BlockSpec((tm, LANES), lambda b, i: (i, 0))] * 2
        args += list(rope_tabs)
    out_specs = [out_spec(s.width, layout) for s in segs for layout, _ in s.outs]
    out_shape = [out_struct(s.width, layout, dt) for s in segs for layout, dt in s.outs]
    return pl.pallas_call(
        functools.partial(_proj_kernel, segs=tuple(segs), rope=rope_tabs is not None),
        grid=(B, T // tm),
        in_specs=in_specs,
        out_specs=out_specs,
        out_shape=out_shape,
        compiler_params=_cparams("parallel", "parallel"),
        name=name,
    )(*args)


def _diff_lambda(lq, lam_init):
    s1 = jnp.sum(lq[0:1, :] * lq[1:2, :], axis=1, keepdims=True)
    s2 = jnp.sum(lq[2:3, :] * lq[3:4, :], axis=1, keepdims=True)
    return jnp.exp(s1) - jnp.exp(s2) + lam_init


def _stack_group_queries(qa_ref, g, rows):
    return jnp.concatenate(
        [qa_ref[rows, (A_GROUP * g + hh) * LANES:(A_GROUP * g + hh + 1) * LANES] for hh in range(A_GROUP)], axis=0)


def _stack_pair_queries(q):
    lo = lax.broadcasted_iota(jnp.int32, q.shape, 1) < HEAD_DIM
    zero = jnp.zeros_like(q)
    return jnp.concatenate([jnp.where(lo, q, zero), jnp.where(lo, zero, q)], axis=0)


def _with_ones_rows(vt):
    return jnp.concatenate([vt, jnp.ones((ONES_ROWS, vt.shape[1]), BF16)], axis=0)


def _sink_row(sink_ref, g, tq):
    return jnp.concatenate(
        [jnp.full((1, tq), sink_ref[A_GROUP * g + hh] * LOG2E, F32) for hh in range(A_GROUP)], axis=1)


def _softmax_values(parts, sink=None):
    m = functools.reduce(jnp.maximum, [jnp.max(s, axis=0, keepdims=True) for s, _ in parts])
    if sink is not None:
        m = jnp.maximum(m, sink)
    ot = sum(_dot(vt1, jnp.exp2(s - m).astype(BF16)) for s, vt1 in parts)
    den = ot[LANES:LANES + 1]
    if sink is not None:
        den = den + jnp.exp2(sink - m)
    return ot[:LANES] / den


def _a_heads(ot, g, tq):
    return [ot[g * HEAD_DIM:(g + 1) * HEAD_DIM, hh * tq:(hh + 1) * tq] for hh in range(A_GROUP)]


def _subln(o, subln, lam_init):
    return _rms(o, subln) * (1.0 - lam_init)


def _diff_combine(ot, lam, subln, lam_init):
    tq = ot.shape[1] // 2
    return _subln((ot[:, :tq] - lam * ot[:, tq:]).T, subln, lam_init)


def _ctx_attn_kernel(sink_ref, qa_ref, ka_ref, va_ref, qb_ref, kb_ref, vb_ref, lamqk_ref, subln_ref, o_ref, *,
                     lam_init):
    n_req, T = qa_ref.shape[0], qa_ref.shape[1]
    nqa = A_HEADS * HEAD_DIM
    lam = _diff_lambda(lamqk_ref[...], lam_init)
    jobs = []
    for r in range(n_req):
        qa, qb, kb, vb = qa_ref.at[r], qb_ref.at[r], kb_ref.at[r], vb_ref.at[r]
        ka = ka_ref[r].astype(BF16)
        vat1 = _with_ones_rows(va_ref[r].T.astype(BF16))
        scores_a = [_dot_nt(ka, _stack_group_queries(qa, g, slice(None))) for g in range(A_KV_HEADS)]
        scores_b, vbt1 = [], []
        for h in range(B_HEADS):
            sl = slice(h * LANES, (h + 1) * LANES)
            scores_b.append(_dot_nt(kb[:, sl].astype(BF16), _stack_pair_queries(qb[:, sl])))
            vbt1.append(_with_ones_rows(vb[:, sl].T.astype(BF16)))
        jobs.append((scores_a, vat1, scores_b, vbt1))
    for r, (scores_a, vat1, scores_b, vbt1) in enumerate(jobs):
        heads = []
        for g in range(A_KV_HEADS):
            heads += _a_heads(_softmax_values([(scores_a[g], vat1)], _sink_row(sink_ref, g, T)), g, T)
        o_ref[r, :, :nqa] = jnp.concatenate(heads, axis=0).T.astype(o_ref.dtype)
        for h in range(B_HEADS):
            ot = _softmax_values([(scores_b[h], vbt1[h])])
            o_ref[r, :, nqa + h * LANES:nqa + (h + 1) * LANES] = (
                _diff_combine(ot, lam, subln_ref[...], lam_init).astype(o_ref.dtype))


def _ctx_attention(qa, ka, va, qb, kb, vb, sink, lam_qk, subln, lam_init):
    B, T, _ = qa.shape
    blk = lambda w: pl.BlockSpec((CTX_ATT_REQUESTS, T, w), lambda b: (b, 0, 0))
    full = lambda a: pl.BlockSpec(a.shape, lambda b: (0,) * a.ndim)
    return pl.pallas_call(
        functools.partial(_ctx_attn_kernel, lam_init=lam_init),
        grid=(B // CTX_ATT_REQUESTS,),
        in_specs=[pl.BlockSpec(memory_space=pltpu.SMEM), blk(QA_W), blk(KA_W), blk(KA_W), blk(B_W), blk(B_W),
                  blk(B_W), full(lam_qk), full(subln)],
        out_specs=blk(A_HEADS * HEAD_DIM + B_W),
        out_shape=jax.ShapeDtypeStruct((B, T, A_HEADS * HEAD_DIM + B_W), BF16),
        compiler_params=_cparams("parallel"),
        name="ctx_attention",
    )(sink, qa, ka, va, qb, kb, vb, lam_qk, subln)


def _lat_a_kernel(sink_ref, qa_ref, k_ref, vt_ref, ck_ref, cvt_ref, o_ref, *, tq, nq, band):
    T = k_ref.shape[0]
    cols = A_GROUP * tq
    cvt1 = _with_ones_rows(cvt_ref[...])
    jobs = []
    for i in range(nq):
        qi = pl.program_id(1) * nq + i
        start = pl.multiple_of(jnp.clip(qi * tq - WINDOW, 0, T - band), WINDOW)
        kb = k_ref[pl.ds(start, band), :]
        vbt1 = _with_ones_rows(vt_ref[:, pl.ds(start, band)])
        kpos = start + lax.broadcasted_iota(jnp.int32, (band, cols), 0)
        qpos = qi * tq + lax.broadcasted_iota(jnp.int32, (band, cols), 1) % tq
        keep = jnp.abs(qpos - kpos) <= WINDOW
        for g in range(A_KV_HEADS):
            qg = _stack_group_queries(qa_ref, g, slice(i * tq, (i + 1) * tq))
            jobs.append((_dot_nt(ck_ref[...], qg), _dot_nt(kb, qg), keep, vbt1))
    for i in range(nq):
        heads = []
        for g in range(A_KV_HEADS):
            s_c, s_b, keep, vbt1 = jobs[i * A_KV_HEADS + g]
            ot = _softmax_values([(s_c, cvt1), (jnp.where(keep, s_b, NEG), vbt1)], _sink_row(sink_ref, g, tq))
            heads += _a_heads(ot, g, tq)
        o_ref[i * tq:(i + 1) * tq, :] = jnp.concatenate(heads, axis=0).T.astype(o_ref.dtype)


def _lat_a_attention(qa, ka, vat, cka, cvat, sink):
    B, T, _ = qa.shape
    L = cka.shape[1]
    tq, nq = WINDOW, 8
    band = 3 * WINDOW
    return pl.pallas_call(
        functools.partial(_lat_a_kernel, tq=tq, nq=nq, band=band),
        grid=(B, T // (tq * nq)),
        in_specs=[
            pl.BlockSpec(memory_space=pltpu.SMEM),
            pl.BlockSpec((None, tq * nq, QA_W), lambda b, i: (b, i, 0)),
            pl.BlockSpec((None, T, KA_W), lambda b, i: (b, 0, 0)),
            pl.BlockSpec((None, KA_W, T), lambda b, i: (b, 0, 0)),
            pl.BlockSpec((None, L, KA_W), lambda b, i: (b, 0, 0)),
            pl.BlockSpec((None, KA_W, L), lambda b, i: (b, 0, 0)),
        ],
        out_specs=pl.BlockSpec((None, tq * nq, A_HEADS * HEAD_DIM), lambda b, i: (b, i, 0)),
        out_shape=jax.ShapeDtypeStruct((B, T, A_HEADS * HEAD_DIM), BF16),
        compiler_params=_cparams("parallel", "parallel"),
        name="lat_a_attention",
    )(sink, qa, ka, vat, cka, cvat)


def _lat_b_kernel(qt_ref, k_ref, vt_ref, ck_ref, cvt_ref, lamqk_ref, subln_ref, o_ref, *, tk, ahead, lam_init):
    tq = qt_ref.shape[1]
    T = k_ref.shape[0]
    qt = qt_ref[...]
    zero = jnp.zeros((HEAD_DIM, tq), BF16)
    qst = jnp.concatenate([jnp.concatenate([qt[:HEAD_DIM], zero], axis=0),
                           jnp.concatenate([zero, qt[HEAD_DIM:]], axis=0)], axis=1)
    blocks = [(ck_ref[...], cvt_ref[...])]
    blocks += [(k_ref[j * tk:(j + 1) * tk, :], vt_ref[:, j * tk:(j + 1) * tk]) for j in range(T // tk)]

    def scores(k):
        return _dot(k, qst)

    def accumulate(s, vt, carry):
        m, acc = carry
        m_new = jnp.maximum(m, jnp.max(s, axis=0, keepdims=True))
        p = jnp.exp2(s - m_new).astype(BF16)
        acc = jnp.exp2(m - m_new) * acc + _dot(_with_ones_rows(vt), p)
        return m_new, acc

    carry = (jnp.full((1, 2 * tq), -jnp.inf, F32), jnp.zeros((LANES + ONES_ROWS, 2 * tq), F32))
    pending = [scores(blocks[j][0]) for j in range(ahead)]
    for j in range(len(blocks)):
        if j + ahead < len(blocks):
            pending.append(scores(blocks[j + ahead][0]))
        carry = accumulate(pending.pop(0), blocks[j][1], carry)
    _, acc = carry
    ot = acc[:LANES] / acc[LANES:LANES + 1]
    lam = _diff_lambda(lamqk_ref[...], lam_init)
    o_ref[...] = _diff_combine(ot, lam, subln_ref[...], lam_init).astype(o_ref.dtype)


def _lat_b_attention(qbt, kb, vbt, ckb, cvbt, lam_qk, subln, lam_init):
    B, T, _ = kb.shape
    L = ckb.shape[1]
    tq = 1024
    full = lambda a: pl.BlockSpec(a.shape, lambda b, h, i: (0,) * a.ndim)
    return pl.pallas_call(
        functools.partial(_lat_b_kernel, tk=512, ahead=2, lam_init=lam_init),
        grid=(B, B_HEADS, T // tq),
        in_specs=[
            pl.BlockSpec((None, LANES, tq), lambda b, h, i: (b, h, i)),
            pl.BlockSpec((None, T, LANES), lambda b, h, i: (b, 0, h)),
            pl.BlockSpec((None, LANES, T), lambda b, h, i: (b, h, 0)),
            pl.BlockSpec((None, L, LANES), lambda b, h, i: (b, 0, h)),
            pl.BlockSpec((None, LANES, L), lambda b, h, i: (b, h, 0)),
            full(lam_qk), full(subln),
        ],
        out_specs=pl.BlockSpec((None, tq, LANES), lambda b, h, i: (b, i, h)),
        out_shape=jax.ShapeDtypeStruct((B, T, B_W), BF16),
        compiler_params=_cparams("parallel", "parallel", "parallel"),
        name="lat_b_attention",
    )(qbt, kb, vbt, ckb, cvbt, lam_qk, subln)


def _scan_chunk_len(n_steps):
    chunk = -(-n_steps // SUBLANES)
    while chunk % 8 != 4:
        chunk += 1
    return chunk


class _RecBlock(NamedTuple):
    gate: object
    xr: object
    cw: object
    cb: object
    wbd: object
    bbd: object
    lam: object
    h0f: object
    h0b: object
    y: object
    sf: object
    sb: object
    xp: object
    af: object
    uf: object
    ab: object
    ub: object
    pf: object
    hf: object
    pb: object
    hb: object


def _rec_kernel(gate_ref, xr_ref, cw_ref, cb_ref, wbd_ref, bbd_ref, lam_ref, h0f_ref, h0b_ref,
                y_ref, sf_ref, sb_ref, *scratch, nblk, tc, gates_unroll, chunk, unroll):
    blocks = []
    for n in range(nblk):
        sl = slice(n * LANES, (n + 1) * LANES)
        lanes = [r.at[:, sl] for r in (gate_ref, xr_ref, cw_ref, cb_ref)]
        lanes += [wbd_ref.at[n], bbd_ref.at[n]]
        lanes += [r.at[:, sl] for r in (lam_ref, h0f_ref, h0b_ref, y_ref, sf_ref, sb_ref)]
        xp, af, uf, ab, ub = [r.at[n] for r in scratch]
        blocks.append(_RecBlock(*lanes, xp, af, uf, ab, ub, pf=af, hf=uf, pb=ab, hb=ub))
    for blk in blocks:
        _rec_gates(blk, tc=tc, gates_unroll=gates_unroll, chunk=chunk)
    _rec_scan(blocks, chunk=chunk, unroll=unroll)
    for blk in blocks:
        _rec_combine(blk, tc=tc)


def _rec_gates(blk, *, tc, gates_unroll, chunk):
    T = blk.xr.shape[0]
    pad = SUBLANES
    xp = blk.xp
    xp[0:pad, :] = jnp.zeros((pad, LANES), F32)
    xp[T + pad:T + 2 * pad, :] = jnp.zeros((pad, LANES), F32)
    xp[pad:T + pad, :] = blk.xr[...]
    tail = SUBLANES * chunk - T
    for a_s, u_s in ((blk.af, blk.uf), (blk.ab, blk.ub)):
        a_s[T:T + tail, :] = jnp.ones((tail, LANES), F32)
        u_s[T:T + tail, :] = jnp.zeros((tail, LANES), F32)
    cw = blk.cw[...]
    cb = blk.cb[...]
    nl = -blk.lam[...]
    softplus = jnp.maximum(nl, 0.0) + jnp.log1p(jnp.exp(-jnp.abs(nl)))
    cp = (0.5 * RGLRU_C) * softplus

    def gates(it, _):
        for q in range(gates_unroll):
            gate_chunk(it * gates_unroll + q)
        return 0

    def gate_chunk(ci):
        t0 = pl.multiple_of(ci * tc, tc)
        y = cb
        for j in range(CONV_W):
            y = y + xp[pl.ds(t0 + (pad - CONV_LEFT + j), tc), :] * cw[j:j + 1, :]
        t = jnp.tanh(_dot(y.astype(BF16), blk.wbd[...]) + blk.bbd[...])
        yh = 0.5 * y
        for d, (a_s, u_s) in enumerate(((blk.af, blk.uf), (blk.ab, blk.ub))):
            t_r = t[:, 2 * d * LANES:(2 * d + 1) * LANES]
            t_i = t[:, (2 * d + 1) * LANES:(2 * d + 2) * LANES]
            neg_log_a = cp[d:d + 1, :] * t_r + cp[d:d + 1, :]
            a = jnp.exp2(neg_log_a * (-LOG2E))
            w = jnp.tanh(neg_log_a) * (a * a + 1.0)
            sqrt_w = jnp.where(w > 0.0, w * lax.rsqrt(w), 0.0)
            a_s[pl.ds(t0, tc), :] = a
            u_s[pl.ds(t0, tc), :] = (t_i + 1.0) * (yh * sqrt_w)

    lax.fori_loop(0, T // (tc * gates_unroll), gates, 0)


def _rec_scan(blocks, *, chunk, unroll):
    T = blocks[0].xr.shape[0]

    def rows(k):
        return pl.ds(k, SUBLANES, stride=chunk)

    def local_scan(it, carry):
        carry = list(carry)
        for q in range(unroll):
            k = it * unroll + q
            kb = chunk - 1 - k
            for i, blk in enumerate(blocks):
                hf, pf, hb, pb = carry[4 * i:4 * i + 4]
                a = blk.af[rows(k), :]
                hf = a * hf + blk.uf[rows(k), :]
                pf = a * pf
                blk.hf[rows(k), :] = hf
                blk.pf[rows(k), :] = pf
                a = blk.ab[rows(kb), :]
                hb = a * hb + blk.ub[rows(kb), :]
                pb = a * pb
                blk.hb[rows(kb), :] = hb
                blk.pb[rows(kb), :] = pb
                carry[4 * i:4 * i + 4] = [hf, pf, hb, pb]
        return tuple(carry)

    zero = jnp.zeros((SUBLANES, LANES), F32)
    one = jnp.ones((SUBLANES, LANES), F32)
    ends = lax.fori_loop(0, chunk // unroll, local_scan, (zero, one, zero, one) * len(blocks))

    carries = []
    for i, blk in enumerate(blocks):
        hf, pf, hb, pb = ends[4 * i:4 * i + 4]
        cf = [blk.h0f[...]]
        for r in range(SUBLANES - 1):
            cf.append(hf[r:r + 1, :] + pf[r:r + 1, :] * cf[r])
        cb_rev = [blk.h0b[...]]
        for r in range(SUBLANES - 1, 0, -1):
            cb_rev.append(hb[r:r + 1, :] + pb[r:r + 1, :] * cb_rev[-1])
        carries.append((jnp.concatenate(cf, axis=0), jnp.concatenate(cb_rev[::-1], axis=0)))

    def apply_carry(it, _):
        for q in range(unroll):
            k = it * unroll + q
            for blk, (carry_f, carry_b) in zip(blocks, carries):
                blk.uf[rows(k), :] = blk.hf[rows(k), :] + blk.pf[rows(k), :] * carry_f
                blk.ub[rows(k), :] = blk.hb[rows(k), :] + blk.pb[rows(k), :] * carry_b
        return 0

    lax.fori_loop(0, chunk // unroll, apply_carry, 0)
    for blk in blocks:
        blk.sf[...] = blk.uf[T - 1:T, :]
        blk.sb[...] = blk.ub[0:1, :]


def _rec_combine(blk, *, tc):
    T = blk.xr.shape[0]

    def combine(ci, _):
        t0 = pl.multiple_of(ci * tc, tc)
        blk.y[pl.ds(t0, tc), :] = ((blk.uf[pl.ds(t0, tc), :] + blk.ub[pl.ds(t0, tc), :])
                                   * blk.gate[pl.ds(t0, tc), :]).astype(blk.y.dtype)
        return 0

    lax.fori_loop(0, T // tc, combine, 0)


def _rec_mixer(gate, xr, h0f, h0b, conv_w, conv_b, w_bd, b_bd, lam, *, nblk, tc, name):
    B, T, _ = xr.shape
    chunk = _scan_chunk_len(T)
    wid = nblk * LANES
    col = lambda rows: pl.BlockSpec((rows, wid), lambda b, n: (0, n))
    seq = pl.BlockSpec((None, T, wid), lambda b, n: (b, 0, n))
    st = pl.BlockSpec((None, 1, wid), lambda b, n: (b, 0, n))
    return pl.pallas_call(
        functools.partial(_rec_kernel, nblk=nblk, tc=tc, gates_unroll=min(8, T // tc), chunk=chunk,
                          unroll=max(u for u in range(1, SCAN_MAX_UNROLL + 1) if chunk % u == 0)),
        grid=(B, RNN_BLOCKS // nblk),
        in_specs=[seq, seq, col(CONV_W), col(1),
                  pl.BlockSpec((nblk, LANES, 4 * LANES), lambda b, n: (n, 0, 0)),
                  pl.BlockSpec((nblk, 1, 4 * LANES), lambda b, n: (n, 0, 0)),
                  col(2), st, st],
        out_specs=[seq, st, st],
        out_shape=[jax.ShapeDtypeStruct((B, T, D_RNN), BF16),
                   jax.ShapeDtypeStruct((B, 1, D_RNN), F32),
                   jax.ShapeDtypeStruct((B, 1, D_RNN), F32)],
        scratch_shapes=[pltpu.VMEM((nblk, T + 2 * SUBLANES, LANES), F32)]
        + [pltpu.VMEM((nblk, SUBLANES * chunk, LANES), F32)] * 4,
        compiler_params=_cparams("parallel", "parallel"),
        name=name,
    )(gate, xr, conv_w, conv_b.reshape(1, D_RNN), w_bd, b_bd, lam, h0f, h0b)


def _post_kernel(*refs, n_mix, final, fc):
    x_ref, mod_ref, g2_ref = refs[:3]
    mix = refs[3:3 + 2 * n_mix]
    w1_ref, w2_ref = refs[3 + 2 * n_mix:5 + 2 * n_mix]
    rest = refs[5 + 2 * n_mix:]
    if final:
        gf_ref, o_ref = rest
    else:
        (o_ref,) = rest
    mixed = _dot(mix[0][...], mix[1][...])
    for i in range(1, n_mix):
        mixed = mixed + _dot(mix[2 * i][...], mix[2 * i + 1][...])
    x1 = x_ref[...] + mod_ref[2:3, :] * mixed
    h = _rms(x1, g2_ref[...])
    hb = (h * (1.0 + mod_ref[4:5, :]) + mod_ref[3:4, :]).astype(BF16)
    n_chunks = D_FF // fc
    up = lambda c: _dot(hb, w1_ref[:, c * fc:(c + 1) * fc])
    acc = None
    a = up(0)
    for c in range(n_chunks):
        a_next = up(c + 1) if c + 1 < n_chunks else None
        part = _dot(jnp.square(jnp.maximum(a, 0.0)).astype(BF16), w2_ref[c * fc:(c + 1) * fc, :])
        acc = part if acc is None else acc + part
        a = a_next
    x2 = x1 + mod_ref[5:6, :] * acc
    if final:
        x2 = _rms(x2, gf_ref[...])
    o_ref[...] = x2


def _post(x, mod, g2, mixes, w1, w2, *, layer, tm, ctx, final_g=None, name):
    B, T, D = x.shape
    row = (lambda b, i: (CTX_ROW, 0, 0)) if ctx else (lambda b, i: (b, 0, 0))
    const = lambda a: pl.BlockSpec(a.shape, lambda b, i: (0,) * a.ndim, pipeline_mode=pl.Buffered(1))
    slab = lambda a: pl.BlockSpec((None,) + a.shape[1:], lambda b, i: (layer, 0, 0), pipeline_mode=pl.Buffered(1))
    in_specs = [
        pl.BlockSpec((None, tm, D), lambda b, i: (b, i, 0)),
        pl.BlockSpec((None, 6, D), row),
        pl.BlockSpec((1, D), lambda b, i: (0, 0)),
    ]
    args = [x, mod, g2.reshape(1, D)]
    for o, w in mixes:
        in_specs += [pl.BlockSpec((None, tm, o.shape[-1]), lambda b, i: (b, i, 0)), const(w)]
        args += [o, w]
    in_specs += [slab(w1), slab(w2)]
    args += [w1, w2]
    if final_g is not None:
        in_specs.append(pl.BlockSpec((1, D), lambda b, i: (0, 0)))
        args.append(final_g.reshape(1, D))
    return pl.pallas_call(
        functools.partial(_post_kernel, n_mix=len(mixes), final=final_g is not None, fc=1024),
        grid=(B, T // tm),
        in_specs=in_specs,
        out_specs=pl.BlockSpec((None, tm, D), lambda b, i: (b, i, 0)),
        out_shape=jax.ShapeDtypeStruct((B, T, D), F32),
        compiler_params=_cparams("parallel", "parallel"),
        name=name,
    )(*args)


def _rope_tables(n_tokens):
    rows = n_tokens // GRID_W
    r, cl = jnp.meshgrid(jnp.arange(rows, dtype=F32), jnp.arange(GRID_W, dtype=F32), indexing='ij')
    quarter = HEAD_DIM // 4
    inv = ROPE_BASE ** (-jnp.arange(quarter, dtype=F32) / quarter)
    ang = jnp.stack([r.reshape(-1)[:, None] * inv, cl.reshape(-1)[:, None] * inv], axis=1)
    cos, sin = jnp.cos(ang), jnp.sin(ang)
    cos64 = jnp.concatenate([cos[:, 0], cos[:, 0], cos[:, 1], cos[:, 1]], axis=-1)
    sin64 = jnp.concatenate([-sin[:, 0], sin[:, 0], -sin[:, 1], sin[:, 1]], axis=-1)
    return jnp.tile(cos64, (1, LANES // HEAD_DIM)), jnp.tile(sin64, (1, LANES // HEAD_DIM))


def _att_in_weights(w_in):
    d = w_in.shape[0]
    nq = A_HEADS * HEAD_DIM
    wq = w_in[:, :nq].reshape(d, A_HEADS, HEAD_DIM)
    z = jnp.zeros_like(wq)
    in_first = (jnp.arange(A_HEADS) // A_GROUP == 0)[None, :, None]
    wq = jnp.where(in_first, jnp.concatenate([wq, z], axis=-1), jnp.concatenate([z, wq], axis=-1))
    return jnp.concatenate([wq.reshape(d, QA_W), w_in[:, nq:]], axis=1).astype(BF16)


def _block_diag_weights(w_a, b_a, w_x, b_x):
    w = jnp.concatenate([w_a[0], w_x[0], w_a[1], w_x[1]], axis=-1)
    b = jnp.concatenate([v.reshape(RNN_BLOCKS, 1, RNN_BW) for v in (b_a[0], b_x[0], b_a[1], b_x[1])], axis=-1)
    return (0.5 * w).astype(BF16), 0.5 * b


def kernel(x_prompt, x_sample, cache_a_k, cache_a_v, cache_b_k, cache_b_v, state_fwd, state_bwd, c, c_ctx, norm1, norm2, w_ada, b_ada, w_mlp1, w_mlp2, att_w_in, att_w_out, att_sink, att_lam_qk, att_subln, rec_w_in, rec_conv_w, rec_conv_b, rec_w_a, rec_b_a, rec_w_x, rec_b_x, rec_lam, rec_w_out, final_norm):
    nb, n_seq, _ = x_prompt.shape
    nd, d_seq, _ = x_sample.shape
    past = cache_a_k.shape[2]
    assert nd <= CTX_ROW and DEPTH == 2
    cvec = jnp.concatenate([c, jnp.zeros((CTX_ROW - nd, D_MODEL), F32), c_ctx[None],
                            jnp.zeros((MOD_ROWS - CTX_ROW - 1, D_MODEL), F32)], axis=0)
    mod = _ada_mod(cvec, w_ada, b_ada)
    w1 = w_mlp1.astype(BF16)
    w2 = w_mlp2.astype(BF16)
    tm_proj, tm_post, tm_ctx = 512, 1024, 512
    tok = lambda a, tm: a.reshape(-1, tm, a.shape[-1])
    seq = lambda a: a.reshape(nb, n_seq, a.shape[-1])

    lam_init = 0.8 - 0.6 * math.exp(-0.3 * 0)
    w_in = _att_in_weights(att_w_in[0])
    w_out = att_w_out[0].astype(BF16)
    nqa = A_HEADS * HEAD_DIM
    c0 = [0, QA_W, QA_W + KA_W, QA_W + 2 * KA_W, QA_W + 2 * KA_W + B_W, QA_W + 2 * KA_W + 2 * B_W]
    widths = [QA_W, KA_W, KA_W, B_W, B_W, B_W]
    scales = [SCALE * LOG2E, 1.0, 1.0, SCALE * LOG2E, 1.0, 1.0]
    roped = [True, True, False, True, True, False]
    sink = att_sink[0]
    lam_qk = att_lam_qk[0]
    subln = att_subln[0].reshape(1, 2 * HEAD_DIM)

    rows = lambda dt: (("rows", dt),)
    cache_a = ((("seqcols", n_seq), F32), ("rows", BF16))
    cache_b = ((("heads", n_seq), F32), ("rows", BF16))
    segs_ctx = [_Seg(c0[0], QA_W, rows(BF16), scale=scales[0]), _Seg(c0[1], KA_W, cache_a),
                _Seg(c0[2], KA_W, cache_a), _Seg(c0[3], B_W, rows(BF16), scale=scales[3]),
                _Seg(c0[4], B_W, cache_b), _Seg(c0[5], B_W, cache_b)]
    qa, kat, ka, vat, va, qb, new_b_k, kb, new_b_v, vb = _proj(tok(x_prompt, tm_proj), mod[0], norm1[0], w_in, segs_ctx,
                                                               tm=tm_proj, ctx=True, name="proj_att_ctx")
    qa, ka, va, qb, kb, vb = map(seq, (qa, ka, va, qb, kb, vb))
    uncol = lambda a: jnp.transpose(a.reshape(nb, A_KV_HEADS, HEAD_DIM, n_seq), (0, 3, 1, 2))[:, None]
    new_a_k, new_a_v = uncol(kat), uncol(vat)
    o_ctx = _ctx_attention(qa, ka, va, qb, kb, vb, sink, lam_qk, subln, lam_init)
    xp = _post(tok(x_prompt, tm_ctx), mod[0], norm2[0], [(tok(o_ctx, tm_ctx), w_out)], w1, w2, layer=0, tm=tm_ctx, ctx=True,
               name="post_att_ctx")

    segs_lat = [_Seg(c0[i], widths[i], (("cols" if i in (2, 3, 5) else "rows", BF16),), rope=roped[i],
                     scale=scales[i]) for i in range(6)]
    qa, ka, vat, qbt, kb, vbt = _proj(x_sample, mod[0], norm1[0], w_in, segs_lat, tm=tm_proj, ctx=False,
                                      rope_tabs=_rope_tables(d_seq), name="proj_att_lat")
    cka = cache_a_k[:, 0].reshape(nd, past, KA_W).astype(BF16)
    cvat = jnp.swapaxes(cache_a_v[:, 0].reshape(nd, past, KA_W), 1, 2).astype(BF16)
    ckb = cache_b_k[:, 0].reshape(nd, past, B_W).astype(BF16)
    cvbt = jnp.swapaxes(cache_b_v[:, 0].reshape(nd, past, B_W), 1, 2).astype(BF16)
    oa = _lat_a_attention(qa, ka, vat, cka, cvat, sink)
    ob = _lat_b_attention(qbt, kb, vbt, ckb, cvbt, lam_qk, subln, lam_init)
    xs = _post(x_sample, mod[0], norm2[0], [(oa, w_out[:nqa]), (ob, w_out[nqa:])], w1, w2, layer=0, tm=tm_post,
               ctx=False, name="post_att_lat")

    w_rin = rec_w_in[0].astype(BF16)
    w_rout = rec_w_out[0].astype(BF16)
    w_bd, b_bd = _block_diag_weights(rec_w_a[0], rec_b_a[0], rec_w_x[0], rec_b_x[0])
    segs_rec = [_Seg(0, D_RNN, rows(F32), gelu=True), _Seg(D_RNN, D_RNN, rows(F32))]
    zeros = jnp.zeros((nb, 1, D_RNN), F32)

    gate, xr = map(seq, _proj(tok(xp, tm_proj), mod[1], norm1[1], w_rin, segs_rec, tm=tm_proj, ctx=True,
                              name="proj_rec_ctx"))
    y, sf, sb = _rec_mixer(gate, xr, zeros, zeros, rec_conv_w[0], rec_conv_b[0], w_bd, b_bd, rec_lam[0],
                           nblk=RNN_BLOCKS, tc=128, name="rec_mixer_ctx")
    y_prompt = seq(_post(xp, mod[1], norm2[1], [(tok(y, tm_ctx), w_rout)], w1, w2, layer=1, tm=tm_ctx, ctx=True,
                         final_g=final_norm, name="post_rec_ctx"))

    gate, xr = _proj(xs, mod[1], norm1[1], w_rin, segs_rec, tm=tm_proj, ctx=False, name="proj_rec_lat")
    y, _, _ = _rec_mixer(gate, xr, state_fwd[:, 0:1], state_bwd[:, 0:1], rec_conv_w[0], rec_conv_b[0], w_bd, b_bd,
                         rec_lam[0], nblk=2, tc=256, name="rec_mixer_lat")
    y_sample = _post(xs, mod[1], norm2[1], [(y, w_rout)], w1, w2, layer=1, tm=tm_post, ctx=False, final_g=final_norm,
                     name="post_rec_lat")

    return (y_prompt, y_sample, new_a_k, new_a_v, new_b_k, new_b_v, sf, sb)
```

```python
import functools
import math
from typing import NamedTuple

import jax
import jax.numpy as jnp
import numpy as np
from jax import lax
from jax.experimental import pallas as pl
from jax.experimental.pallas import tpu as pltpu

F32 = jnp.float32
BF16 = jnp.bfloat16

LANES = 128
SUBLANES = 8
VMEM_LIMIT_BYTES = 56 * 1024 * 1024

D_MODEL = 1024
DEPTH = 2
GRID_W = 64
HEAD_DIM = 64
A_HEADS = 8
A_KV_HEADS = 2
A_GROUP = A_HEADS // A_KV_HEADS
B_HEADS = 4
WINDOW = 128
ROPE_BASE = 10000.0
D_RNN = 1280
RNN_BLOCKS = 10
RNN_BW = D_RNN // RNN_BLOCKS
CONV_W = 4
CONV_LEFT = (CONV_W - 1) // 2
RGLRU_C = 8.0
D_FF = 4 * D_MODEL
EPS = 1e-6
SCALE = HEAD_DIM ** -0.5
NEG = -1e30

QA_W = A_HEADS * LANES
KA_W = A_KV_HEADS * HEAD_DIM
B_W = B_HEADS * 2 * HEAD_DIM
MOD_ROWS = 8
CTX_ROW = 4
LOG2E = math.log2(math.e)
ONES_ROWS = 16
CTX_ATT_REQUESTS = 4
PROJ_ROW_GROUPS = 2
SCAN_MAX_UNROLL = 1024


def _cparams(*semantics):
    return pltpu.CompilerParams(dimension_semantics=semantics, vmem_limit_bytes=VMEM_LIMIT_BYTES)


def _dot(a, b):
    return jnp.dot(a, b, preferred_element_type=F32)


def _dot_nt(a, b):
    return lax.dot_general(a, b, (((1,), (1,)), ((), ())), preferred_element_type=F32)


def _rms(x, g):
    return x * lax.rsqrt(jnp.mean(x * x, axis=-1, keepdims=True) + EPS) * g


def _gelu_tanh(x):
    return x * (0.5 * (1.0 + jnp.tanh(math.sqrt(2.0 / math.pi) * (x + 0.044715 * (x * x * x)))))


def _ada_kernel(c_ref, w_ref, b_ref, o_ref):
    c = c_ref[...]
    s = c * jax.nn.sigmoid(c)
    o_ref[...] = _dot(s.astype(BF16), w_ref[...].astype(BF16)) + b_ref[...]


def _ada_mod(cvec, w_ada, b_ada):
    tn = 1536
    out = pl.pallas_call(
        _ada_kernel,
        grid=(DEPTH, 6 * D_MODEL // tn),
        in_specs=[
            pl.BlockSpec((MOD_ROWS, D_MODEL), lambda l, j: (0, 0)),
            pl.BlockSpec((None, D_MODEL, tn), lambda l, j: (l, 0, j)),
            pl.BlockSpec((None, 1, tn), lambda l, j: (l, 0, j)),
        ],
        out_specs=pl.BlockSpec((None, MOD_ROWS, tn), lambda l, j: (l, 0, j)),
        out_shape=jax.ShapeDtypeStruct((DEPTH, MOD_ROWS, 6 * D_MODEL), F32),
        compiler_params=_cparams("parallel", "parallel"),
        name="ada_mod",
    )(cvec, w_ada, b_ada.reshape(DEPTH, 1, 6 * D_MODEL))
    return out.reshape(DEPTH, MOD_ROWS, 6, D_MODEL)


class _Seg(NamedTuple):
    col0: int
    width: int
    outs: tuple
    rope: bool = False
    scale: float = 1.0
    gelu: bool = False


def _proj_kernel(*refs, segs, rope):
    if rope:
        x_ref, mod_ref, g_ref, w_ref, cos_ref, sin_ref, *outs = refs
    else:
        x_ref, mod_ref, g_ref, w_ref, *outs = refs
    tm = x_ref.shape[0]
    rows = tm // PROJ_ROW_GROUPS
    groups = [slice(i * rows, (i + 1) * rows) for i in range(PROJ_ROW_GROUPS)]
    hbs = []
    for grp in groups:
        h = _rms(x_ref[grp, :], g_ref[...])
        hbs.append((h * (1.0 + mod_ref[1:2, :]) + mod_ref[0:1, :]).astype(BF16))
    if rope:
        first = (lax.broadcasted_iota(jnp.int32, (rows, LANES), 1) & 16) == 0
    outs = iter(outs)
    for seg in segs:
        ys = [_dot(hb, w_ref[:, seg.col0:seg.col0 + seg.width]) for hb in hbs]
        o_refs = [next(outs) for _ in seg.outs]
        for grp, y in zip(groups, ys):
            for t in range(seg.width // LANES):
                lanes = slice(t * LANES, (t + 1) * LANES)
                yt = y[:, lanes]
                if seg.rope:
                    sw = jnp.where(first, pltpu.roll(yt, LANES - 16, 1), pltpu.roll(yt, 16, 1))
                    yt = yt * cos_ref[grp, :] + sw * sin_ref[grp, :]
                if seg.scale != 1.0:
                    yt = yt * seg.scale
                if seg.gelu:
                    yt = _gelu_tanh(yt)
                for (layout, _), o_ref in zip(seg.outs, o_refs):
                    if layout == "cols":
                        o_ref[lanes, grp] = yt.T.astype(o_ref.dtype)
                    elif layout == "rows":
                        o_ref[grp, lanes] = yt.astype(o_ref.dtype)
                    else:
                        kind, seq_len = layout
                        assert rows % seq_len == 0
                        for r in range(rows // seq_len):
                            s = grp.start // seq_len + r
                            ys = yt[r * seq_len:(r + 1) * seq_len]
                            if kind == "heads":
                                o_ref[s, :, t, :] = ys.astype(o_ref.dtype)
                            else:
                                o_ref[s, lanes, :] = ys.T.astype(o_ref.dtype)


def _proj(x, mod, g, w, segs, *, tm, ctx, rope_tabs=None, name):
    B, T, D = x.shape

    def out_spec(width, layout):
        if layout == "cols":
            return pl.BlockSpec((None, width, tm), lambda b, i: (b, 0, i))
        if layout == "rows":
            return pl.BlockSpec((None, tm, width), lambda b, i: (b, i, 0))
        if layout[0] == "heads":
            return pl.BlockSpec((tm // layout[1], None, layout[1], width // LANES, LANES),
                                lambda b, i: (b * (T // tm) + i, 0, 0, 0, 0))
        return pl.BlockSpec((tm // layout[1], width, layout[1]), lambda b, i: (b * (T // tm) + i, 0, 0))

    def out_struct(width, layout, dt):
        if layout == "cols":
            return jax.ShapeDtypeStruct((B, width, T), dt)
        if layout == "rows":
            return jax.ShapeDtypeStruct((B, T, width), dt)
        if layout[0] == "heads":
            return jax.ShapeDtypeStruct((B * T // layout[1], 1, layout[1], width // LANES, LANES), dt)
        return jax.ShapeDtypeStruct((B * T // layout[1], width, layout[1]), dt)

    row = (lambda b, i: (CTX_ROW, 0, 0)) if ctx else (lambda b, i: (b, 0, 0))
    in_specs = [
        pl.BlockSpec((None, tm, D), lambda b, i: (b, i, 0)),
        pl.BlockSpec((None, 6, D), row),
        pl.BlockSpec((1, D), lambda b, i: (0, 0)),
        pl.BlockSpec(w.shape, lambda b, i: (0, 0)),
    ]
    args = [x, mod, g.reshape(1, D), w]
    if rope_tabs is not None:
        in_specs += [pl.BlockSpec((tm, LANES), lambda b, i: (i, 0))] * 2
        args += list(rope_tabs)
    out_specs = [out_spec(s.width, layout) for s in segs for layout, _ in s.outs]
    out_shape = [out_struct(s.width, layout, dt) for s in segs for layout, dt in s.outs]
    return pl.pallas_call(
        functools.partial(_proj_kernel, segs=tuple(segs), rope=rope_tabs is not None),
        grid=(B, T // tm),
        in_specs=in_specs,
        out_specs=out_specs,
        out_shape=out_shape,
        compiler_params=_cparams("parallel", "parallel"),
        name=name,
    )(*args)


def _diff_lambda(lq, lam_init):
    s1 = jnp.sum(lq[0:1, :] * lq[1:2, :], axis=1, keepdims=True)
    s2 = jnp.sum(lq[2:3, :] * lq[3:4, :], axis=1, keepdims=True)
    return jnp.exp(s1) - jnp.exp(s2) + lam_init


def _stack_group_queries(qa_ref, g, rows):
    return jnp.concatenate(
        [qa_ref[rows, (A_GROUP * g + hh) * LANES:(A_GROUP * g + hh + 1) * LANES] for hh in range(A_GROUP)], axis=0)


def _stack_pair_queries(q):
    lo = lax.broadcasted_iota(jnp.int32, q.shape, 1) < HEAD_DIM
    zero = jnp.zeros_like(q)
    return jnp.concatenate([jnp.where(lo, q, zero), jnp.where(lo, zero, q)], axis=0)


def _with_ones_rows(vt):
    return jnp.concatenate([vt, jnp.ones((ONES_ROWS, vt.shape[1]), BF16)], axis=0)


def _sink_row(sink_ref, g, tq):
    return jnp.concatenate(
        [jnp.full((1, tq), sink_ref[A_GROUP * g + hh] * LOG2E, F32) for hh in range(A_GROUP)], axis=1)


def _softmax_values(parts, sink=None):
    m = functools.reduce(jnp.maximum, [jnp.max(s, axis=0, keepdims=True) for s, _ in parts])
    if sink is not None:
        m = jnp.maximum(m, sink)
    ot = sum(_dot(vt1, jnp.exp2(s - m).astype(BF16)) for s, vt1 in parts)
    den = ot[LANES:LANES + 1]
    if sink is not None:
        den = den + jnp.exp2(sink - m)
    return ot[:LANES] / den


def _a_heads(ot, g, tq):
    return [ot[g * HEAD_DIM:(g + 1) * HEAD_DIM, hh * tq:(hh + 1) * tq] for hh in range(A_GROUP)]


def _subln(o, subln, lam_init):
    return _rms(o, subln) * (1.0 - lam_init)


def _diff_combine(ot, lam, subln, lam_init):
    tq = ot.shape[1] // 2
    return _subln((ot[:, :tq] - lam * ot[:, tq:]).T, subln, lam_init)


def _ctx_attn_kernel(sink_ref, qa_ref, ka_ref, va_ref, qb_ref, kb_ref, vb_ref, lamqk_ref, subln_ref, o_ref, *,
                     lam_init):
    n_req, T = qa_ref.shape[0], qa_ref.shape[1]
    nqa = A_HEADS * HEAD_DIM
    lam = _diff_lambda(lamqk_ref[...], lam_init)
    jobs = []
    for r in range(n_req):
        qa, qb, kb, vb = qa_ref.at[r], qb_ref.at[r], kb_ref.at[r], vb_ref.at[r]
        ka = ka_ref[r].astype(BF16)
        vat1 = _with_ones_rows(va_ref[r].T.astype(BF16))
        scores_a = [_dot_nt(ka, _stack_group_queries(qa, g, slice(None))) for g in range(A_KV_HEADS)]
        scores_b, vbt1 = [], []
        for h in range(B_HEADS):
            sl = slice(h * LANES, (h + 1) * LANES)
            scores_b.append(_dot_nt(kb[:, sl].astype(BF16), _stack_pair_queries(qb[:, sl])))
            vbt1.append(_with_ones_rows(vb[:, sl].T.astype(BF16)))
        jobs.append((scores_a, vat1, scores_b, vbt1))
    for r, (scores_a, vat1, scores_b, vbt1) in enumerate(jobs):
        heads = []
        for g in range(A_KV_HEADS):
            heads += _a_heads(_softmax_values([(scores_a[g], vat1)], _sink_row(sink_ref, g, T)), g, T)
        o_ref[r, :, :nqa] = jnp.concatenate(heads, axis=0).T.astype(o_ref.dtype)
        for h in range(B_HEADS):
            ot = _softmax_values([(scores_b[h], vbt1[h])])
            o_ref[r, :, nqa + h * LANES:nqa + (h + 1) * LANES] = (
                _diff_combine(ot, lam, subln_ref[...], lam_init).astype(o_ref.dtype))


def _ctx_attention(qa, ka, va, qb, kb, vb, sink, lam_qk, subln, lam_init):
    B, T, _ = qa.shape
    blk = lambda w: pl.BlockSpec((CTX_ATT_REQUESTS, T, w), lambda b: (b, 0, 0))
    full = lambda a: pl.BlockSpec(a.shape, lambda b: (0,) * a.ndim)
    return pl.pallas_call(
        functools.partial(_ctx_attn_kernel, lam_init=lam_init),
        grid=(B // CTX_ATT_REQUESTS,),
        in_specs=[pl.BlockSpec(memory_space=pltpu.SMEM), blk(QA_W), blk(KA_W), blk(KA_W), blk(B_W), blk(B_W),
                  blk(B_W), full(lam_qk), full(subln)],
        out_specs=blk(A_HEADS * HEAD_DIM + B_W),
        out_shape=jax.ShapeDtypeStruct((B, T, A_HEADS * HEAD_DIM + B_W), BF16),
        compiler_params=_cparams("parallel"),
        name="ctx_attention",
    )(sink, qa, ka, va, qb, kb, vb, lam_qk, subln)


def _lat_a_kernel(sink_ref, qa_ref, k_ref, vt_ref, ck_ref, cvt_ref, o_ref, *, tq, nq, band):
    T = k_ref.shape[0]
    cols = A_GROUP * tq
    cvt1 = _with_ones_rows(cvt_ref[...])
    jobs = []
    for i in range(nq):
        qi = pl.program_id(1) * nq + i
        start = pl.multiple_of(jnp.clip(qi * tq - WINDOW, 0, T - band), WINDOW)
        kb = k_ref[pl.ds(start, band), :]
        vbt1 = _with_ones_rows(vt_ref[:, pl.ds(start, band)])
        kpos = start + lax.broadcasted_iota(jnp.int32, (band, cols), 0)
        qpos = qi * tq + lax.broadcasted_iota(jnp.int32, (band, cols), 1) % tq
        keep = jnp.abs(qpos - kpos) <= WINDOW
        for g in range(A_KV_HEADS):
            qg = _stack_group_queries(qa_ref, g, slice(i * tq, (i + 1) * tq))
            jobs.append((_dot_nt(ck_ref[...], qg), _dot_nt(kb, qg), keep, vbt1))
    for i in range(nq):
        heads = []
        for g in range(A_KV_HEADS):
            s_c, s_b, keep, vbt1 = jobs[i * A_KV_HEADS + g]
            ot = _softmax_values([(s_c, cvt1), (jnp.where(keep, s_b, NEG), vbt1)], _sink_row(sink_ref, g, tq))
            heads += _a_heads(ot, g, tq)
        o_ref[i * tq:(i + 1) * tq, :] = jnp.concatenate(heads, axis=0).T.astype(o_ref.dtype)


def _lat_a_attention(qa, ka, vat, cka, cvat, sink):
    B, T, _ = qa.shape
    L = cka.shape[1]
    tq, nq = WINDOW, 8
    band = 3 * WINDOW
    return pl.pallas_call(
        functools.partial(_lat_a_kernel, tq=tq, nq=nq, band=band),
        grid=(B, T // (tq * nq)),
        in_specs=[
            pl.BlockSpec(memory_space=pltpu.SMEM),
            pl.BlockSpec((None, tq * nq, QA_W), lambda b, i: (b, i, 0)),
            pl.BlockSpec((None, T, KA_W), lambda b, i: (b, 0, 0)),
            pl.BlockSpec((None, KA_W, T), lambda b, i: (b, 0, 0)),
            pl.BlockSpec((None, L, KA_W), lambda b, i: (b, 0, 0)),
            pl.BlockSpec((None, KA_W, L), lambda b, i: (b, 0, 0)),
        ],
        out_specs=pl.BlockSpec((None, tq * nq, A_HEADS * HEAD_DIM), lambda b, i: (b, i, 0)),
        out_shape=jax.ShapeDtypeStruct((B, T, A_HEADS * HEAD_DIM), BF16),
        compiler_params=_cparams("parallel", "parallel"),
        name="lat_a_attention",
    )(sink, qa, ka, vat, cka, cvat)


def _lat_b_kernel(qt_ref, k_ref, vt_ref, ck_ref, cvt_ref, lamqk_ref, subln_ref, o_ref, *, tk, ahead, lam_init):
    tq = qt_ref.shape[1]
    T = k_ref.shape[0]
    qt = qt_ref[...]
    zero = jnp.zeros((HEAD_DIM, tq), BF16)
    qst = jnp.concatenate([jnp.concatenate([qt[:HEAD_DIM], zero], axis=0),
                           jnp.concatenate([zero, qt[HEAD_DIM:]], axis=0)], axis=1)
    blocks = [(ck_ref[...], cvt_ref[...])]
    blocks += [(k_ref[j * tk:(j + 1) * tk, :], vt_ref[:, j * tk:(j + 1) * tk]) for j in range(T // tk)]

    def scores(k):
        return _dot(k, qst)

    def accumulate(s, vt, carry):
        m, acc = carry
        m_new = jnp.maximum(m, jnp.max(s, axis=0, keepdims=True))
        p = jnp.exp2(s - m_new).astype(BF16)
        acc = jnp.exp2(m - m_new) * acc + _dot(_with_ones_rows(vt), p)
        return m_new, acc

    carry = (jnp.full((1, 2 * tq), -jnp.inf, F32), jnp.zeros((LANES + ONES_ROWS, 2 * tq), F32))
    pending = [scores(blocks[j][0]) for j in range(ahead)]
    for j in range(len(blocks)):
        if j + ahead < len(blocks):
            pending.append(scores(blocks[j + ahead][0]))
        carry = accumulate(pending.pop(0), blocks[j][1], carry)
    _, acc = carry
    ot = acc[:LANES] / acc[LANES:LANES + 1]
    lam = _diff_lambda(lamqk_ref[...], lam_init)
    o_ref[...] = _diff_combine(ot, lam, subln_ref[...], lam_init).astype(o_ref.dtype)


def _lat_b_attention(qbt, kb, vbt, ckb, cvbt, lam_qk, subln, lam_init):
    B, T, _ = kb.shape
    L = ckb.shape[1]
    tq = 1024
    full = lambda a: pl.BlockSpec(a.shape, lambda b, h, i: (0,) * a.ndim)
    return pl.pallas_call(
        functools.partial(_lat_b_kernel, tk=512, ahead=2, lam_init=lam_init),
        grid=(B, B_HEADS, T // tq),
        in_specs=[
            pl.BlockSpec((None, LANES, tq), lambda b, h, i: (b, h, i)),
            pl.BlockSpec((None, T, LANES), lambda b, h, i: (b, 0, h)),
            pl.BlockSpec((None, LANES, T), lambda b, h, i: (b, h, 0)),
            pl.BlockSpec((None, L, LANES), lambda b, h, i: (b, 0, h)),
            pl.BlockSpec((None, LANES, L), lambda b, h, i: (b, h, 0)),
            full(lam_qk), full(subln),
        ],
        out_specs=pl.BlockSpec((None, tq, LANES), lambda b, h, i: (b, i, h)),
        out_shape=jax.ShapeDtypeStruct((B, T, B_W), BF16),
        compiler_params=_cparams("parallel", "parallel", "parallel"),
        name="lat_b_attention",
    )(qbt, kb, vbt, ckb, cvbt, lam_qk, subln)


def _scan_chunk_len(n_steps):
    chunk = -(-n_steps // SUBLANES)
    while chunk % 8 != 4:
        chunk += 1
    return chunk


class _RecBlock(NamedTuple):
    gate: object
    xr: object
    cw: object
    cb: object
    wbd: object
    bbd: object
    lam: object
    h0f: object
    h0b: object
    y: object
    sf: object
    sb: object
    xp: object
    af: object
    uf: object
    ab: object
    ub: object
    pf: object
    hf: object
    pb: object
    hb: object


def _rec_kernel(gate_ref, xr_ref, cw_ref, cb_ref, wbd_ref, bbd_ref, lam_ref, h0f_ref, h0b_ref,
                y_ref, sf_ref, sb_ref, *scratch, nreq, nblk, tc, gates_unroll, chunk, unroll):
    blocks = []
    for q in range(nreq):
        for n in range(nblk):
            sl = slice(n * LANES, (n + 1) * LANES)
            lanes = [r.at[q].at[:, sl] for r in (gate_ref, xr_ref)]
            lanes += [r.at[:, sl] for r in (cw_ref, cb_ref)]
            lanes += [wbd_ref.at[n], bbd_ref.at[n], lam_ref.at[:, sl]]
            lanes += [r.at[q].at[:, sl] for r in (h0f_ref, h0b_ref, y_ref, sf_ref, sb_ref)]
            xp, af, uf, ab, ub = [r.at[q * nblk + n] for r in scratch]
            blocks.append(_RecBlock(*lanes, xp, af, uf, ab, ub, pf=af, hf=uf, pb=ab, hb=ub))
    for blk in blocks:
        _rec_gates(blk, tc=tc, gates_unroll=gates_unroll, chunk=chunk)
    _rec_scan(blocks, chunk=chunk, unroll=unroll)
    for blk in blocks:
        _rec_combine(blk, tc=tc)


def _rec_gates(blk, *, tc, gates_unroll, chunk):
    T = blk.xr.shape[0]
    pad = SUBLANES
    xp = blk.xp
    xp[0:pad, :] = jnp.zeros((pad, LANES), F32)
    xp[T + pad:T + 2 * pad, :] = jnp.zeros((pad, LANES), F32)
    xp[pad:T + pad, :] = blk.xr[...]
    tail = SUBLANES * chunk - T
    for a_s, u_s in ((blk.af, blk.uf), (blk.ab, blk.ub)):
        a_s[T:T + tail, :] = jnp.ones((tail, LANES), F32)
        u_s[T:T + tail, :] = jnp.zeros((tail, LANES), F32)
    cw = blk.cw[...]
    cb = blk.cb[...]
    nl = -blk.lam[...]
    softplus = jnp.maximum(nl, 0.0) + jnp.log1p(jnp.exp(-jnp.abs(nl)))
    cp = (0.5 * RGLRU_C) * softplus

    def gates(it, _):
        for q in range(gates_unroll):
            gate_chunk(it * gates_unroll + q)
        return 0

    def gate_chunk(ci):
        t0 = pl.multiple_of(ci * tc, tc)
        y = cb
        for j in range(CONV_W):
            y = y + xp[pl.ds(t0 + (pad - CONV_LEFT + j), tc), :] * cw[j:j + 1, :]
        t = jnp.tanh(_dot(y.astype(BF16), blk.wbd[...]) + blk.bbd[...])
        yh = 0.5 * y
        for d, (a_s, u_s) in enumerate(((blk.af, blk.uf), (blk.ab, blk.ub))):
            t_r = t[:, 2 * d * LANES:(2 * d + 1) * LANES]
            t_i = t[:, (2 * d + 1) * LANES:(2 * d + 2) * LANES]
            neg_log_a = cp[d:d + 1, :] * t_r + cp[d:d + 1, :]
            a = jnp.exp2(neg_log_a * (-LOG2E))
            w = jnp.tanh(neg_log_a) * (a * a + 1.0)
            sqrt_w = jnp.where(w > 0.0, w * lax.rsqrt(w), 0.0)
            a_s[pl.ds(t0, tc), :] = a
            u_s[pl.ds(t0, tc), :] = (t_i + 1.0) * (yh * sqrt_w)

    lax.fori_loop(0, T // (tc * gates_unroll), gates, 0)


def _rec_scan(blocks, *, chunk, unroll):
    T = blocks[0].xr.shape[0]

    def rows(k):
        return pl.ds(k, SUBLANES, stride=chunk)

    def local_scan(it, carry):
        carry = list(carry)
        for q in range(unroll):
            k = it * unroll + q
            kb = chunk - 1 - k
            for i, blk in enumerate(blocks):
                hf, pf, hb, pb = carry[4 * i:4 * i + 4]
                a = blk.af[rows(k), :]
                hf = a * hf + blk.uf[rows(k), :]
                pf = a * pf
                blk.hf[rows(k), :] = hf
                blk.pf[rows(k), :] = pf
                a = blk.ab[rows(kb), :]
                hb = a * hb + blk.ub[rows(kb), :]
                pb = a * pb
                blk.hb[rows(kb), :] = hb
                blk.pb[rows(kb), :] = pb
                carry[4 * i:4 * i + 4] = [hf, pf, hb, pb]
        return tuple(carry)

    zero = jnp.zeros((SUBLANES, LANES), F32)
    one = jnp.ones((SUBLANES, LANES), F32)
    ends = lax.fori_loop(0, chunk // unroll, local_scan, (zero, one, zero, one) * len(blocks))

    carries = []
    for i, blk in enumerate(blocks):
        hf, pf, hb, pb = ends[4 * i:4 * i + 4]
        cf = [blk.h0f[...]]
        for r in range(SUBLANES - 1):
            cf.append(hf[r:r + 1, :] + pf[r:r + 1, :] * cf[r])
        cb_rev = [blk.h0b[...]]
        for r in range(SUBLANES - 1, 0, -1):
            cb_rev.append(hb[r:r + 1, :] + pb[r:r + 1, :] * cb_rev[-1])
        carries.append((jnp.concatenate(cf, axis=0), jnp.concatenate(cb_rev[::-1], axis=0)))

    def apply_carry(it, _):
        for q in range(unroll):
            k = it * unroll + q
            for blk, (carry_f, carry_b) in zip(blocks, carries):
                blk.uf[rows(k), :] = blk.hf[rows(k), :] + blk.pf[rows(k), :] * carry_f
                blk.ub[rows(k), :] = blk.hb[rows(k), :] + blk.pb[rows(k), :] * carry_b
        return 0

    lax.fori_loop(0, chunk // unroll, apply_carry, 0)
    for blk in blocks:
        blk.sf[...] = blk.uf[T - 1:T, :]
        blk.sb[...] = blk.ub[0:1, :]


def _rec_combine(blk, *, tc):
    T = blk.xr.shape[0]

    def combine(ci, _):
        t0 = pl.multiple_of(ci * tc, tc)
        blk.y[pl.ds(t0, tc), :] = ((blk.uf[pl.ds(t0, tc), :] + blk.ub[pl.ds(t0, tc), :])
                                   * blk.gate[pl.ds(t0, tc), :]).astype(blk.y.dtype)
        return 0

    lax.fori_loop(0, T // tc, combine, 0)


def _rec_mixer(gate, xr, h0f, h0b, conv_w, conv_b, w_bd, b_bd, lam, *, nreq, nblk, tc, name):
    B, T, _ = xr.shape
    chunk = _scan_chunk_len(T)
    wid = nblk * LANES
    col = lambda rows: pl.BlockSpec((rows, wid), lambda b, n: (0, n))
    seq = pl.BlockSpec((nreq, T, wid), lambda b, n: (b, 0, n))
    st = pl.BlockSpec((nreq, 1, wid), lambda b, n: (b, 0, n))
    return pl.pallas_call(
        functools.partial(_rec_kernel, nreq=nreq, nblk=nblk, tc=tc, gates_unroll=min(8, T // tc), chunk=chunk,
                          unroll=max(u for u in range(1, SCAN_MAX_UNROLL + 1) if chunk % u == 0)),
        grid=(B // nreq, RNN_BLOCKS // nblk),
        in_specs=[seq, seq, col(CONV_W), col(1),
                  pl.BlockSpec((nblk, LANES, 4 * LANES), lambda b, n: (n, 0, 0)),
                  pl.BlockSpec((nblk, 1, 4 * LANES), lambda b, n: (n, 0, 0)),
                  col(2), st, st],
        out_specs=[seq, st, st],
        out_shape=[jax.ShapeDtypeStruct((B, T, D_RNN), BF16),
                   jax.ShapeDtypeStruct((B, 1, D_RNN), F32),
                   jax.ShapeDtypeStruct((B, 1, D_RNN), F32)],
        scratch_shapes=[pltpu.VMEM((nreq * nblk, T + 2 * SUBLANES, LANES), F32)]
        + [pltpu.VMEM((nreq * nblk, SUBLANES * chunk, LANES), F32)] * 4,
        compiler_params=_cparams("parallel", "parallel"),
        name=name,
    )(gate, xr, conv_w, conv_b.reshape(1, D_RNN), w_bd, b_bd, lam, h0f, h0b)


def _post_kernel(*refs, n_mix, final, fc):
    x_ref, mod_ref, g2_ref = refs[:3]
    mix = refs[3:3 + 2 * n_mix]
    w1_ref, w2_ref = refs[3 + 2 * n_mix:5 + 2 * n_mix]
    rest = refs[5 + 2 * n_mix:]
    if final:
        gf_ref, o_ref = rest
    else:
        (o_ref,) = rest
    mixed = _dot(mix[0][...], mix[1][...])
    for i in range(1, n_mix):
        mixed = mixed + _dot(mix[2 * i][...], mix[2 * i + 1][...])
    x1 = x_ref[...] + mod_ref[2:3, :] * mixed
    h = _rms(x1, g2_ref[...])
    hb = (h * (1.0 + mod_ref[4:5, :]) + mod_ref[3:4, :]).astype(BF16)
    n_chunks = D_FF // fc
    up = lambda c: _dot(hb, w1_ref[:, c * fc:(c + 1) * fc])
    acc = None
    a = up(0)
    for c in range(n_chunks):
        a_next = up(c + 1) if c + 1 < n_chunks else None
        part = _dot(jnp.square(jnp.maximum(a, 0.0)).astype(BF16), w2_ref[c * fc:(c + 1) * fc, :])
        acc = part if acc is None else acc + part
        a = a_next
    x2 = x1 + mod_ref[5:6, :] * acc
    if final:
        x2 = _rms(x2, gf_ref[...])
    o_ref[...] = x2


def _post(x, mod, g2, mixes, w1, w2, *, layer, tm, ctx, final_g=None, name):
    B, T, D = x.shape
    row = (lambda b, i: (CTX_ROW, 0, 0)) if ctx else (lambda b, i: (b, 0, 0))
    const = lambda a: pl.BlockSpec(a.shape, lambda b, i: (0,) * a.ndim, pipeline_mode=pl.Buffered(1))
    slab = lambda a: pl.BlockSpec((None,) + a.shape[1:], lambda b, i: (layer, 0, 0), pipeline_mode=pl.Buffered(1))
    in_specs = [
        pl.BlockSpec((None, tm, D), lambda b, i: (b, i, 0)),
        pl.BlockSpec((None, 6, D), row),
        pl.BlockSpec((1, D), lambda b, i: (0, 0)),
    ]
    args = [x, mod, g2.reshape(1, D)]
    for o, w in mixes:
        in_specs += [pl.BlockSpec((None, tm, o.shape[-1]), lambda b, i: (b, i, 0)), const(w)]
        args += [o, w]
    in_specs += [slab(w1), slab(w2)]
    args += [w1, w2]
    if final_g is not None:
        in_specs.append(pl.BlockSpec((1, D), lambda b, i: (0, 0)))
        args.append(final_g.reshape(1, D))
    return pl.pallas_call(
        functools.partial(_post_kernel, n_mix=len(mixes), final=final_g is not None, fc=1024),
        grid=(B, T // tm),
        in_specs=in_specs,
        out_specs=pl.BlockSpec((None, tm, D), lambda b, i: (b, i, 0)),
        out_shape=jax.ShapeDtypeStruct((B, T, D), F32),
        compiler_params=_cparams("parallel", "parallel"),
        name=name,
    )(*args)


def _rope_tables(n_tokens):
    rows = n_tokens // GRID_W
    r, cl = jnp.meshgrid(jnp.arange(rows, dtype=F32), jnp.arange(GRID_W, dtype=F32), indexing='ij')
    quarter = HEAD_DIM // 4
    inv = ROPE_BASE ** (-jnp.arange(quarter, dtype=F32) / quarter)
    ang = jnp.stack([r.reshape(-1)[:, None] * inv, cl.reshape(-1)[:, None] * inv], axis=1)
    cos, sin = jnp.cos(ang), jnp.sin(ang)
    cos64 = jnp.concatenate([cos[:, 0], cos[:, 0], cos[:, 1], cos[:, 1]], axis=-1)
    sin64 = jnp.concatenate([-sin[:, 0], sin[:, 0], -sin[:, 1], sin[:, 1]], axis=-1)
    return jnp.tile(cos64, (1, LANES // HEAD_DIM)), jnp.tile(sin64, (1, LANES // HEAD_DIM))


def _att_in_weights(w_in):
    d = w_in.shape[0]
    nq = A_HEADS * HEAD_DIM
    wq = w_in[:, :nq].reshape(d, A_HEADS, HEAD_DIM)
    z = jnp.zeros_like(wq)
    in_first = (jnp.arange(A_HEADS) // A_GROUP == 0)[None, :, None]
    wq = jnp.where(in_first, jnp.concatenate([wq, z], axis=-1), jnp.concatenate([z, wq], axis=-1))
    return jnp.concatenate([wq.reshape(d, QA_W), w_in[:, nq:]], axis=1).astype(BF16)


def _block_diag_weights(w_a, b_a, w_x, b_x):
    w = jnp.concatenate([w_a[0], w_x[0], w_a[1], w_x[1]], axis=-1)
    b = jnp.concatenate([v.reshape(RNN_BLOCKS, 1, RNN_BW) for v in (b_a[0], b_x[0], b_a[1], b_x[1])], axis=-1)
    return (0.5 * w).astype(BF16), 0.5 * b


def kernel(x_prompt, x_sample, cache_a_k, cache_a_v, cache_b_k, cache_b_v, state_fwd, state_bwd, c, c_ctx, norm1, norm2, w_ada, b_ada, w_mlp1, w_mlp2, att_w_in, att_w_out, att_sink, att_lam_qk, att_subln, rec_w_in, rec_conv_w, rec_conv_b, rec_w_a, rec_b_a, rec_w_x, rec_b_x, rec_lam, rec_w_out, final_norm):
    nb, n_seq, _ = x_prompt.shape
    nd, d_seq, _ = x_sample.shape
    past = cache_a_k.shape[2]
    assert nd <= CTX_ROW and DEPTH == 2
    cvec = jnp.concatenate([c, jnp.zeros((CTX_ROW - nd, D_MODEL), F32), c_ctx[None],
                            jnp.zeros((MOD_ROWS - CTX_ROW - 1, D_MODEL), F32)], axis=0)
    mod = _ada_mod(cvec, w_ada, b_ada)
    w1 = w_mlp1.astype(BF16)
    w2 = w_mlp2.astype(BF16)
    tm_proj, tm_post, tm_ctx = 512, 1024, 512
    tok = lambda a, tm: a.reshape(-1, tm, a.shape[-1])
    seq = lambda a: a.reshape(nb, n_seq, a.shape[-1])

    lam_init = 0.8 - 0.6 * math.exp(-0.3 * 0)
    w_in = _att_in_weights(att_w_in[0])
    w_out = att_w_out[0].astype(BF16)
    nqa = A_HEADS * HEAD_DIM
    c0 = [0, QA_W, QA_W + KA_W, QA_W + 2 * KA_W, QA_W + 2 * KA_W + B_W, QA_W + 2 * KA_W + 2 * B_W]
    widths = [QA_W, KA_W, KA_W, B_W, B_W, B_W]
    scales = [SCALE * LOG2E, 1.0, 1.0, SCALE * LOG2E, 1.0, 1.0]
    roped = [True, True, False, True, True, False]
    sink = att_sink[0]
    lam_qk = att_lam_qk[0]
    subln = att_subln[0].reshape(1, 2 * HEAD_DIM)

    rows = lambda dt: (("rows", dt),)
    cache_a = ((("seqcols", n_seq), F32), ("rows", BF16))
    cache_b = ((("heads", n_seq), F32), ("rows", BF16))
    segs_ctx = [_Seg(c0[0], QA_W, rows(BF16), scale=scales[0]), _Seg(c0[1], KA_W, cache_a),
                _Seg(c0[2], KA_W, cache_a), _Seg(c0[3], B_W, rows(BF16), scale=scales[3]),
                _Seg(c0[4], B_W, cache_b), _Seg(c0[5], B_W, cache_b)]
    qa, kat, ka, vat, va, qb, new_b_k, kb, new_b_v, vb = _proj(tok(x_prompt, tm_proj), mod[0], norm1[0], w_in, segs_ctx,
                                                               tm=tm_proj, ctx=True, name="proj_att_ctx")
    qa, ka, va, qb, kb, vb = map(seq, (qa, ka, va, qb, kb, vb))
    uncol = lambda a: jnp.transpose(a.reshape(nb, A_KV_HEADS, HEAD_DIM, n_seq), (0, 3, 1, 2))[:, None]
    new_a_k, new_a_v = uncol(kat), uncol(vat)
    o_ctx = _ctx_attention(qa, ka, va, qb, kb, vb, sink, lam_qk, subln, lam_init)
    xp = _post(tok(x_prompt, tm_ctx), mod[0], norm2[0], [(tok(o_ctx, tm_ctx), w_out)], w1, w2, layer=0, tm=tm_ctx, ctx=True,
               name="post_att_ctx")

    segs_lat = [_Seg(c0[i], widths[i], (("cols" if i in (2, 3, 5) else "rows", BF16),), rope=roped[i],
                     scale=scales[i]) for i in range(6)]
    qa, ka, vat, qbt, kb, vbt = _proj(x_sample, mod[0], norm1[0], w_in, segs_lat, tm=tm_proj, ctx=False,
                                      rope_tabs=_rope_tables(d_seq), name="proj_att_lat")
    cka = cache_a_k[:, 0].reshape(nd, past, KA_W).astype(BF16)
    cvat = jnp.swapaxes(cache_a_v[:, 0].reshape(nd, past, KA_W), 1, 2).astype(BF16)
    ckb = cache_b_k[:, 0].reshape(nd, past, B_W).astype(BF16)
    cvbt = jnp.swapaxes(cache_b_v[:, 0].reshape(nd, past, B_W), 1, 2).astype(BF16)
    oa = _lat_a_attention(qa, ka, vat, cka, cvat, sink)
    ob = _lat_b_attention(qbt, kb, vbt, ckb, cvbt, lam_qk, subln, lam_init)
    xs = _post(x_sample, mod[0], norm2[0], [(oa, w_out[:nqa]), (ob, w_out[nqa:])], w1, w2, layer=0, tm=tm_post,
               ctx=False, name="post_att_lat")

    w_rin = rec_w_in[0].astype(BF16)
    w_rout = rec_w_out[0].astype(BF16)
    w_bd, b_bd = _block_diag_weights(rec_w_a[0], rec_b_a[0], rec_w_x[0], rec_b_x[0])
    segs_rec = [_Seg(0, D_RNN, rows(F32), gelu=True), _Seg(D_RNN, D_RNN, rows(F32))]
    zeros = jnp.zeros((nb, 1, D_RNN), F32)

    gate, xr = map(seq, _proj(tok(xp, tm_proj), mod[1], norm1[1], w_rin, segs_rec, tm=tm_proj, ctx=True,
                              name="proj_rec_ctx"))
    y, sf, sb = _rec_mixer(gate, xr, zeros, zeros, rec_conv_w[0], rec_conv_b[0], w_bd, b_bd, rec_lam[0],
                           nreq=2, nblk=RNN_BLOCKS, tc=128, name="rec_mixer_ctx")
    y_prompt = seq(_post(xp, mod[1], norm2[1], [(tok(y, tm_ctx), w_rout)], w1, w2, layer=1, tm=tm_ctx, ctx=True,
                         final_g=final_norm, name="post_rec_ctx"))

    gate, xr = _proj(xs, mod[1], norm1[1], w_rin, segs_rec, tm=tm_proj, ctx=False, name="proj_rec_lat")
    y, _, _ = _rec_mixer(gate, xr, state_fwd[:, 0:1], state_bwd[:, 0:1], rec_conv_w[0], rec_conv_b[0], w_bd, b_bd,
                         rec_lam[0], nreq=1, nblk=2, tc=256, name="rec_mixer_lat")
    y_sample = _post(xs, mod[1], norm2[1], [(y, w_rout)], w1, w2, layer=1, tm=tm_post, ctx=False, final_g=final_norm,
                     name="post_rec_lat")

    return (y_prompt, y_sample, new_a_k, new_a_v, new_b_k, new_b_v, sf, sb)
```
